```python
import numpy as np
import jax
import jax.numpy as jnp
from jax import lax

D_MODEL = 1024
BATCH = 8
SEQ = 8192
DEPTH = 2

HEAD_DIM = 64
N_MIXERS = 4
GROUP_WIDTH = D_MODEL // N_MIXERS
N_GROUP_HEADS = GROUP_WIDTH // HEAD_DIM
BLOCK = 128
RMS_EPS = 1e-6

SWA_WINDOW = 128
SWA_KV_HEADS = N_GROUP_HEADS // 2

POOL_WINDOWS = (2, 4, 8, 16)
POOL_GROUP_CH = GROUP_WIDTH // len(POOL_WINDOWS)

CMP_LEN = 32
CMP_STRIDE = 16
SEL_BLOCK = 64
NSA_TOP_N = 16
NSA_WINDOW = 512
NSA_FORCE = 1e4

DN_CONV = 4
DN_CHUNK = 64

N_EXPERT_GROUPS = 4
EXPERTS_PER_GROUP = 4
EXPERT_TOP_K = 2
D_EXPERT = D_MODEL // 2

IN_SIZES = (GROUP_WIDTH, SWA_KV_HEADS * HEAD_DIM, SWA_KV_HEADS * HEAD_DIM,
            GROUP_WIDTH,
            GROUP_WIDTH, 6 * HEAD_DIM, 3 * N_GROUP_HEADS,
            3 * GROUP_WIDTH, GROUP_WIDTH, N_GROUP_HEADS, N_GROUP_HEADS)
IN_WIDTH = sum(IN_SIZES)
IN_OFFSETS = tuple(int(o) for o in np.cumsum(IN_SIZES)[:-1])

kernel_name = 'hybrid_parallel_heads_hier_moe'

F32 = jnp.float32


def rms_norm(x, gain):
    x32 = x.astype(F32)
    y = x32 * lax.rsqrt(jnp.mean(x32 * x32, axis=-1, keepdims=True) + RMS_EPS)
    return (y * gain.astype(F32)).astype(x.dtype)


def l2_normalize(t):
    return t * lax.rsqrt(jnp.sum(t * t, axis=-1, keepdims=True) + 1e-6)


def masked_softmax(s, mask):
    s = jnp.where(mask, s, -jnp.inf)
    m = jnp.max(s, axis=-1, keepdims=True)
    m = jnp.where(jnp.isfinite(m), m, 0.0)
    p = jnp.where(mask, jnp.exp(s - m), 0.0)
    d = jnp.sum(p, axis=-1, keepdims=True)
    return p / jnp.where(d > 0, d, 1.0)


def banded_attention(q, k, v, window, sink=None):
    b, s, hq, dh = q.shape
    hkv = k.shape[2]
    grp = hq // hkv
    nb = s // BLOCK
    span = window + BLOCK
    qb = q.reshape(b, nb, BLOCK, hkv, grp, dh)
    pad = ((0, 0), (window, 0), (0, 0), (0, 0))
    idx = jnp.arange(nb)[:, None] * BLOCK + jnp.arange(span)[None, :]
    kb = jnp.pad(k, pad)[:, idx]
    vb = jnp.pad(v, pad)[:, idx]
    tq = (jnp.arange(nb)[:, None] * BLOCK + jnp.arange(BLOCK)[None, :])[:, :, None]
    tk = (idx - window)[:, None, :]
    mask = (tk <= tq) & (tk > tq - window) & (tk >= 0)
    sc = jnp.einsum('bnqhgd,bnkhd->bnhgqk', qb, kb).astype(F32) * dh ** -0.5
    sc = jnp.where(mask[None, :, None, None], sc, -jnp.inf)
    m = jnp.max(sc, axis=-1, keepdims=True)
    if sink is not None:
        sink_l = sink.astype(F32).reshape(1, 1, hkv, grp, 1, 1)
        m = jnp.maximum(m, sink_l)
        p = jnp.exp(sc - m)
        denom = jnp.sum(p, axis=-1, keepdims=True) + jnp.exp(sink_l - m)
    else:
        p = jnp.exp(sc - m)
        denom = jnp.sum(p, axis=-1, keepdims=True)
    o = jnp.einsum('bnhgqk,bnkhd->bnqhgd', p / denom, vb.astype(F32))
    return o.reshape(b, s, hq, dh).astype(q.dtype)


def multiscale_pool(u, w_pool, pool_scale):
    b, s, _ = u.shape
    u32 = u.astype(F32)
    cs = jnp.cumsum(u32, axis=1)
    pos = jnp.arange(s)
    outs = []
    for gi, w in enumerate(POOL_WINDOWS):
        sl = slice(gi * POOL_GROUP_CH, (gi + 1) * POOL_GROUP_CH)
        c_g = cs[..., sl]
        lower = jnp.pad(c_g[:, :s - w], ((0, 0), (w, 0), (0, 0)))
        cnt = jnp.minimum(pos + 1, w).astype(F32)[None, :, None]
        outs.append((c_g - lower) / cnt - u32[..., sl])
    d = jnp.stack(outs, axis=2)
    y = jnp.einsum('bsgc,gcd->bsgd', d, w_pool.astype(F32)).reshape(b, s, -1) * pool_scale.astype(F32)
    return y.astype(u.dtype)


def nsa_attention(q, kv, gate_raw, cmp_pos, cmp_w1, cmp_w2):
    b, s, h, dh = q.shape
    k_cmp, v_cmp, k_sel, v_sel, k_win, v_win = jnp.split(kv, 6, axis=-1)
    n_cmp = (s - CMP_LEN) // CMP_STRIDE + 1
    n_sel = s // SEL_BLOCK
    n_top = min(NSA_TOP_N, n_sel)
    cmp_start = jnp.arange(n_cmp) * CMP_STRIDE
    win_idx = cmp_start[:, None] + jnp.arange(CMP_LEN)[None, :]

    def compress(t, pos, w1, w2):
        blocks = (t[:, win_idx] + pos).reshape(b, n_cmp, CMP_LEN * dh)
        return jax.nn.silu(blocks @ w1) @ w2

    kc = compress(k_cmp, cmp_pos[0], cmp_w1[0], cmp_w2[0]).astype(F32)
    vc = compress(v_cmp, cmp_pos[1], cmp_w1[1], cmp_w2[1]).astype(F32)
    sel_start = jnp.arange(n_sel) * SEL_BLOCK
    cover = jnp.maximum(jnp.minimum(cmp_start[:, None] + CMP_LEN, sel_start[None, :] + SEL_BLOCK)
                        - jnp.maximum(cmp_start[:, None], sel_start[None, :]), 0).astype(F32) / CMP_LEN
    cmp_end = cmp_start + CMP_LEN - 1
    kb = k_sel.reshape(b, n_sel, SEL_BLOCK, dh)
    vb = v_sel.reshape(b, n_sel, SEL_BLOCK, dh)
    blk = jnp.arange(n_sel)
    scale = dh ** -0.5

    def query_block(i):
        t = i * BLOCK + jnp.arange(BLOCK)
        qi = lax.dynamic_slice_in_dim(q, i * BLOCK, BLOCK, axis=1).astype(F32)
        p_c = masked_softmax(jnp.einsum('bqhd,bnd->bhqn', qi, kc) * scale, cmp_end[None, :] <= t[:, None])
        o_c = jnp.einsum('bhqn,bnd->bqhd', p_c, vc)
        importance = jnp.einsum('bhqn,nj->bqj', p_c, cover)
        cur = (t // SEL_BLOCK)[:, None]
        forced = (blk[None, :] == 0) | (blk[None, :] == cur) | (blk[None, :] == cur - 1)
        causal = blk[None, :] * SEL_BLOCK <= t[:, None]
        score = jnp.where(causal, jnp.where(forced, NSA_FORCE, importance), -jnp.inf)
        _, idx = lax.top_k(score, n_top)
        kg = jax.vmap(lambda a, ix: a[ix])(kb, idx).astype(F32)
        vg = jax.vmap(lambda a, ix: a[ix])(vb, idx).astype(F32)
        key_pos = idx[..., None] * SEL_BLOCK + jnp.arange(SEL_BLOCK)
        m_s = (key_pos <= t[None, :, None, None]).reshape(b, BLOCK, 1, n_top * SEL_BLOCK)
        s_s = jnp.einsum('bqhd,bqnkd->bqhnk', qi, kg).reshape(b, BLOCK, h, n_top * SEL_BLOCK) * scale
        p_s = masked_softmax(s_s, m_s)
        o_s = jnp.einsum('bqhm,bqmd->bqhd', p_s, vg.reshape(b, BLOCK, n_top * SEL_BLOCK, dh))
        return o_c, o_s

    o_c, o_s = lax.map(query_block, jnp.arange(s // BLOCK))
    to_seq = lambda o: jnp.moveaxis(o, 0, 1).reshape(b, s, h, dh)
    o_w = banded_attention(q, k_win[:, :, None], v_win[:, :, None], NSA_WINDOW).astype(F32)
    gates = jax.nn.sigmoid(gate_raw.astype(F32)).reshape(b, s, h, 3, 1)
    out = gates[:, :, :, 0] * to_seq(o_c) + gates[:, :, :, 1] * to_seq(o_s) + gates[:, :, :, 2] * o_w
    return out.astype(q.dtype)


def gated_delta_rule(q, k, v, g, beta):
    b, s, h, dk = q.shape
    dv = v.shape[-1]
    n = s // DN_CHUNK
    cl = DN_CHUNK

    def chunks(t):
        return jnp.moveaxis(t.reshape(b, n, cl, h, -1), 3, 1)

    q, k, v = chunks(q), chunks(k), chunks(v)
    g = jnp.cumsum(jnp.moveaxis(g.reshape(b, n, cl, h), 3, 1), axis=-1)
    beta = jnp.moveaxis(beta.reshape(b, n, cl, h), 3, 1)[..., None]
    k_beta = k * beta
    v_beta = v * beta
    causal = jnp.tril(jnp.ones((cl, cl), bool))
    strict = jnp.tril(jnp.ones((cl, cl), bool), -1)
    decay = jnp.exp(jnp.where(causal, g[..., :, None] - g[..., None, :], -jnp.inf))
    lower = jnp.where(strict, jnp.einsum('bhnid,bhnjd->bhnij', k_beta, k) * decay, 0.0)
    eye = jnp.eye(cl, dtype=q.dtype)
    t_inv = lax.linalg.triangular_solve(eye + lower, jnp.broadcast_to(eye, lower.shape),
                                        left_side=True, lower=True, unit_diagonal=True)
    u = t_inv @ v_beta
    w = t_inv @ (k_beta * jnp.exp(g)[..., None])
    intra = jnp.where(causal, jnp.einsum('bhnid,bhnjd->bhnij', q, k) * decay, 0.0)
    q_dec = q * jnp.exp(g)[..., None]
    k_dec = k * jnp.exp(g[..., -1:] - g)[..., None]
    chunk_decay = jnp.exp(g[..., -1])

    def step(state, xs):
        u_i, w_i, qd_i, kd_i, a_i, cd_i = xs
        v_new = u_i - w_i @ state
        o_i = qd_i @ state + a_i @ v_new
        state = state * cd_i[..., None, None] + jnp.swapaxes(kd_i, -1, -2) @ v_new
        return state, o_i

    xs = tuple(jnp.moveaxis(t, 2, 0) for t in (u, w, q_dec, k_dec, intra, chunk_decay))
    state0 = jnp.zeros((b, h, dk, dv), q.dtype)
    _, o = lax.scan(step, state0, xs)
    return jnp.moveaxis(o, 0, 2).transpose(0, 2, 3, 1, 4).reshape(b, s, h, dv)


def gated_deltanet(qkv, z, beta_raw, a_raw, conv_w, a_log, dt_bias, norm_w):
    b, s, ch = qkv.shape
    qkv = jax.nn.silu(lax.conv_general_dilated(qkv, conv_w[:, None, :], (1,), [(DN_CONV - 1, 0)],
                                               dimension_numbers=('NWC', 'WIO', 'NWC'),
                                               feature_group_count=ch))
    q, k, v = (t.reshape(b, s, N_GROUP_HEADS, HEAD_DIM).astype(F32) for t in jnp.split(qkv, 3, axis=-1))
    q = l2_normalize(q) * HEAD_DIM ** -0.5
    k = l2_normalize(k)
    beta = jax.nn.sigmoid(beta_raw.astype(F32))
    g = -jnp.exp(a_log.astype(F32)) * jax.nn.softplus(a_raw.astype(F32) + dt_bias.astype(F32))
    o = gated_delta_rule(q, k, v, g, beta)
    o = o * lax.rsqrt(jnp.mean(o * o, axis=-1, keepdims=True) + RMS_EPS) * norm_w.astype(F32)
    o = o * jax.nn.silu(z.astype(F32).reshape(b, s, N_GROUP_HEADS, HEAD_DIM))
    return o.reshape(b, s, -1).astype(qkv.dtype)


def hybrid_mixer(h, w_in, w_out, attn_sink, w_pool, pool_scale, cmp_pos, cmp_w1, cmp_w2,
                 conv_w, a_log, dt_bias, dn_norm):
    b, s, _ = h.shape
    (a_q, a_k, a_v, pool_in, c_q, c_kv, c_gate,
     d_qkv, d_z, d_beta, d_a) = jnp.split(h @ w_in, IN_OFFSETS, axis=-1)
    heads = lambda t: t.reshape(b, s, -1, HEAD_DIM)
    y_a = banded_attention(heads(a_q), heads(a_k), heads(a_v), SWA_WINDOW, attn_sink)
    y_b = multiscale_pool(pool_in, w_pool, pool_scale)
    y_c = nsa_attention(heads(c_q), c_kv, c_gate, cmp_pos, cmp_w1, cmp_w2)
    y_d = gated_deltanet(d_qkv, d_z, d_beta, d_a, conv_w, a_log, dt_bias, dn_norm)
    y = jnp.concatenate([y_a.reshape(b, s, -1), y_b, y_c.reshape(b, s, -1), y_d], axis=-1)
    return y @ w_out


def hier_moe(h, w_rg, b_rg, w_re, b_re, w_gate, w_up, w_down):
    b, s, d = h.shape
    t = h.reshape(b * s, d)
    p_group = jax.nn.softmax((t @ w_rg + b_rg).astype(F32), axis=-1)
    pg_top, g_top = lax.top_k(p_group, 1)
    g_onehot = jax.nn.one_hot(g_top[:, 0], N_EXPERT_GROUPS, dtype=F32)
    logits_e = (t @ w_re + b_re).astype(F32).reshape(-1, N_EXPERT_GROUPS, EXPERTS_PER_GROUP)
    p_exp = jax.nn.softmax(jnp.einsum('nge,ng->ne', logits_e, g_onehot), axis=-1)
    pe_top, e_top = lax.top_k(p_exp, EXPERT_TOP_K)
    pe_top = pe_top / jnp.sum(pe_top, axis=-1, keepdims=True)
    w_e = jnp.einsum('nke,nk->ne', jax.nn.one_hot(e_top, EXPERTS_PER_GROUP, dtype=F32), pe_top) * pg_top
    combine = (g_onehot[:, :, None] * w_e[:, None, :]).astype(h.dtype)
    y = jnp.zeros_like(t)
    for gi in range(N_EXPERT_GROUPS):
        hid = jax.nn.silu(jnp.einsum('nd,edf->nef', t, w_gate[gi])) * jnp.einsum('nd,edf->nef', t, w_up[gi])
        y = y + jnp.einsum('nef,efd->nd', hid * combine[:, gi, :, None], w_down[gi])
    return y.reshape(b, s, d)


def setup_inputs(seed: int = 0) -> dict:
    key = jax.random.key(seed)
    keys = iter(jax.random.split(key, 40))

    def normal(shape, scale):
        return jax.random.normal(next(keys), shape, F32) * scale

    def gain(shape):
        return 1.0 + normal(shape, 0.02)

    L, D, H = DEPTH, D_MODEL, N_GROUP_HEADS
    G, EG, F = N_EXPERT_GROUPS, EXPERTS_PER_GROUP, D_EXPERT
    x = normal((BATCH, SEQ, D), 1.0)
    c = normal((BATCH, D), 1.0)
    dt = jax.random.uniform(next(keys), (L, H), F32, 1e-3, 1e-1)
    a_log = jnp.log(jax.random.uniform(next(keys), (L, H), F32, 1.0, 16.0))
    return {
        'x': x,
        'c': c,
        'norm_mix': gain((L, D)),
        'norm_ffn': gain((L, D)),
        'final_norm': gain((D,)),
        'w_ada': normal((L, D, 6 * D), 0.25 * D ** -0.5),
        'b_ada': normal((L, 6 * D), 0.02),
        'w_in': normal((L, D, IN_WIDTH), D ** -0.5),
        'w_out': normal((L, D, D), D ** -0.5),
        'attn_sink': normal((L, H), 0.5),
        'w_pool': normal((L, len(POOL_WINDOWS), POOL_GROUP_CH, POOL_GROUP_CH), POOL_GROUP_CH ** -0.5),
        'pool_scale': gain((L, GROUP_WIDTH)),
        'cmp_pos': normal((L, 2, CMP_LEN, HEAD_DIM), 0.02),
        'cmp_w1': normal((L, 2, CMP_LEN * HEAD_DIM, HEAD_DIM), (CMP_LEN * HEAD_DIM) ** -0.5),
        'cmp_w2': normal((L, 2, HEAD_DIM, HEAD_DIM), HEAD_DIM ** -0.5),
        'conv_w': normal((L, DN_CONV, 3 * GROUP_WIDTH), DN_CONV ** -0.5),
        'a_log': a_log,
        'dt_bias': jnp.log(jnp.expm1(dt)),
        'dn_norm': gain((L, HEAD_DIM)),
        'w_route_group': normal((L, D, G), D ** -0.5),
        'b_route_group': normal((L, G), 0.01),
        'w_route_expert': normal((L, D, G * EG), D ** -0.5),
        'b_route_expert': normal((L, G * EG), 0.01),
        'w_gate': normal((L, G, EG, D, F), D ** -0.5),
        'w_up': normal((L, G, EG, D, F), D ** -0.5),
        'w_down': normal((L, G, EG, F, D), F ** -0.5),
    }


def reference(x, c, norm_mix, norm_ffn, final_norm, w_ada, b_ada, w_in, w_out, attn_sink, w_pool,
              pool_scale, cmp_pos, cmp_w1, cmp_w2, conv_w, a_log, dt_bias, dn_norm, w_route_group,
              b_route_group, w_route_expert, b_route_expert, w_gate, w_up, w_down):
    cond = jax.nn.silu(c)
    for l in range(DEPTH):
        mod = (cond @ w_ada[l] + b_ada[l])[:, None, :]
        sh1, sc1, g1, sh2, sc2, g2 = jnp.split(mod, 6, axis=-1)
        h = rms_norm(x, norm_mix[l]) * (1 + sc1) + sh1
        y = hybrid_mixer(h, w_in[l], w_out[l], attn_sink[l], w_pool[l], pool_scale[l], cmp_pos[l],
                         cmp_w1[l], cmp_w2[l], conv_w[l], a_log[l], dt_bias[l], dn_norm[l])
        x = x + g1 * y
        h = rms_norm(x, norm_ffn[l]) * (1 + sc2) + sh2
        x = x + g2 * hier_moe(h, w_route_group[l], b_route_group[l], w_route_expert[l],
                              b_route_expert[l], w_gate[l], w_up[l], w_down[l])
    return rms_norm(x, final_norm)
```

```python
import functools

import numpy as np
import jax
import jax.numpy as jnp
from jax import lax
from jax.experimental import pallas as pl
from jax.experimental.pallas import tpu as pltpu

F32 = jnp.float32
BF16 = jnp.bfloat16
HI = lax.Precision.HIGHEST

HEAD_DIM = 64
N_HEADS = 4
GROUP_WIDTH = N_HEADS * HEAD_DIM
BLOCK = 128
RMS_EPS = 1e-6
SWA_WINDOW = 128
POOL_WINDOWS = (2, 4, 8, 16)
POOL_HALO = 16
CMP_LEN = 32
CMP_STRIDE = 16
SEL_BLOCK = 64
NSA_TOP_N = 16
NSA_WINDOW = 512
NSA_FORCE = 1e4
SEL_CHUNK = 512
DN_CONV = 4
DN_CHUNK = 64
N_EXPERT_GROUPS = 4
EXPERTS_PER_GROUP = 4
N_EXPERTS = N_EXPERT_GROUPS * EXPERTS_PER_GROUP
LANES = 128
MASKED = -1e30
VMEM_LIMIT = 56 * 1024 * 1024

SEG_WIDTHS = (
    ("a_q", 256), ("a_k", 256), ("a_v", 256), ("c_q", 256),
    ("c_ksel", 128), ("c_vsel", 128), ("c_kwin", 128), ("c_vwin", 128),
    ("b_u", 256), ("c_cmp", 128), ("c_gate", 128), ("d_qkv", 768), ("d_z", 256), ("d_ba", 128),
)
SEG_BF16 = ("a_q", "a_k", "a_v", "c_q", "c_ksel", "c_vsel", "c_kwin", "c_vwin")
SEG_OFFSETS = {}
_off = 0
for _name, _w in SEG_WIDTHS:
    SEG_OFFSETS[_name] = (_off, _w)
    _off += _w
PACKED_WIDTH = _off


def _sigmoid(x):
    return 1.0 / (1.0 + jnp.exp(-x))


def _silu(x):
    return x * _sigmoid(x)


def _dot(a, b, precision=None):
    return jnp.dot(a, b, precision=precision, preferred_element_type=F32)


def _dot_nt(a, b, precision=None):
    return lax.dot_general(a, b, (((1,), (1,)), ((), ())), precision=precision,
                           preferred_element_type=F32)


def _params(*semantics):
    return pltpu.CompilerParams(dimension_semantics=semantics, vmem_limit_bytes=VMEM_LIMIT)


def _ada_kernel(c_ref, w_ref, b_ref, o_ref):
    cond = _silu(c_ref[...])
    o_ref[...] = _dot(cond, w_ref[...], HI) + b_ref[...]


def ada_modulation(c, w_ada, b_ada, col_tile=1536):
    depth, d, width = w_ada.shape
    b = c.shape[0]
    return pl.pallas_call(
        _ada_kernel,
        grid=(depth, width // col_tile),
        in_specs=[
            pl.BlockSpec((b, d), lambda l, j: (0, 0)),
            pl.BlockSpec((None, d, col_tile), lambda l, j: (l, 0, j)),
            pl.BlockSpec((None, 1, col_tile), lambda l, j: (l, 0, j)),
        ],
        out_specs=pl.BlockSpec((None, b, col_tile), lambda l, j: (l, 0, j)),
        out_shape=jax.ShapeDtypeStruct((depth, b, width), F32),
        compiler_params=_params("parallel", "parallel"),
        name="ada_modulation",
    )(c, w_ada, b_ada.reshape(depth, 1, width))


def _modulated_norm(x, gain, scale, shift):
    y = x * lax.rsqrt(jnp.mean(x * x, axis=-1, keepdims=True) + RMS_EPS)
    return y * gain * (1.0 + scale) + shift


def _in_proj_kernel(x_ref, gain_ref, sc_ref, sh_ref, w_ref, *out_refs):
    h = _modulated_norm(x_ref[...], gain_ref[...], sc_ref[...], sh_ref[...]).astype(BF16)
    for (name, _), o_ref in zip(SEG_WIDTHS, out_refs):
        off, width = SEG_OFFSETS[name]
        o_ref[...] = _dot(h, w_ref[:, off:off + width]).astype(o_ref.dtype)


def in_projection(x, gain, scale, shift, w_packed, seq, row_tile=512):
    n, d = x.shape
    tiles_per_seq = seq // row_tile
    row = lambda i: (i, 0)
    per_batch = lambda i: (i // tiles_per_seq, 0, 0)
    out_shape = tuple(
        jax.ShapeDtypeStruct((n, w), BF16 if name in SEG_BF16 else F32) for name, w in SEG_WIDTHS)
    outs = pl.pallas_call(
        _in_proj_kernel,
        grid=(n // row_tile,),
        in_specs=[
            pl.BlockSpec((row_tile, d), row),
            pl.BlockSpec((1, d), lambda i: (0, 0)),
            pl.BlockSpec((None, 1, d), per_batch),
            pl.BlockSpec((None, 1, d), per_batch),
            pl.BlockSpec((d, PACKED_WIDTH), lambda i: (0, 0)),
        ],
        out_specs=tuple(pl.BlockSpec((row_tile, w), row) for _, w in SEG_WIDTHS),
        out_shape=out_shape,
        compiler_params=_params("parallel"),
        name="in_projection",
    )(x, gain, scale, shift, w_packed)
    return dict(zip((name for name, _ in SEG_WIDTHS), outs))


def pack_in_weights(w_in):
    gw, hd = GROUP_WIDTH, HEAD_DIM
    sizes = (gw, 2 * hd, 2 * hd, gw, gw, 6 * hd, 3 * N_HEADS, 3 * gw, gw, N_HEADS, N_HEADS)
    offs = np.concatenate([[0], np.cumsum(sizes)])
    (a_q, a_k, a_v, b_u, c_q, c_kv, c_gate, d_qkv, d_z, d_beta, d_a) = (
        w_in[:, offs[i]:offs[i + 1]] for i in range(len(sizes)))
    d = w_in.shape[0]
    dup = lambda t: jnp.concatenate([t, t], axis=1)
    pad = lambda t: jnp.pad(t, ((0, 0), (0, LANES - t.shape[1])))
    k_cmp, v_cmp, k_sel, v_sel, k_win, v_win = (c_kv[:, i * hd:(i + 1) * hd] for i in range(6))
    segs = {
        "a_q": a_q,
        "a_k": jnp.concatenate([dup(a_k[:, :hd]), dup(a_k[:, hd:])], axis=1),
        "a_v": jnp.concatenate([dup(a_v[:, :hd]), dup(a_v[:, hd:])], axis=1),
        "c_q": c_q,
        "c_ksel": dup(k_sel), "c_vsel": dup(v_sel), "c_kwin": dup(k_win), "c_vwin": dup(v_win),
        "b_u": b_u,
        "c_cmp": jnp.concatenate([k_cmp, v_cmp], axis=1),
        "c_gate": pad(c_gate),
        "d_qkv": d_qkv, "d_z": d_z,
        "d_ba": pad(jnp.concatenate([d_beta, d_a], axis=1)),
    }
    packed = jnp.concatenate([segs[name] for name, _ in SEG_WIDTHS], axis=1)
    assert packed.shape == (d, PACKED_WIDTH)
    return packed.astype(BF16)


def _stack_heads(slab):
    lane = lax.broadcasted_iota(jnp.int32, slab.shape, 1)
    zero = jnp.zeros_like(slab)
    return jnp.concatenate([jnp.where(lane < HEAD_DIM, slab, zero),
                            jnp.where(lane >= HEAD_DIM, slab, zero)], axis=0)


def _unstack_heads(o, rows):
    lane = lax.broadcasted_iota(jnp.int32, (rows, LANES), 1)
    return jnp.where(lane < HEAD_DIM, o[0:rows], o[rows:2 * rows])


def _swa_kernel(sink_ref, q_ref, kp_ref, kc_ref, vp_ref, vc_ref, o_ref):
    i = pl.program_id(1)
    q = q_ref[...]
    row = lax.broadcasted_iota(jnp.int32, (2 * BLOCK, 2 * BLOCK), 0)
    col = lax.broadcasted_iota(jnp.int32, (2 * BLOCK, 2 * BLOCK), 1)
    tq = row & (BLOCK - 1)
    tk = col - BLOCK
    valid = (tk <= tq) & (tk > tq - SWA_WINDOW) & ((col >= BLOCK) | (i > 0))
    slabs = []
    for j in range(N_HEADS // 2):
        lanes = slice(j * LANES, (j + 1) * LANES)
        q2 = _stack_heads(q[:, lanes]) * jnp.asarray(HEAD_DIM ** -0.5, BF16)
        k = jnp.concatenate([kp_ref[:, lanes], kc_ref[:, lanes]], axis=0)
        v = jnp.concatenate([vp_ref[:, lanes], vc_ref[:, lanes]], axis=0)
        s = jnp.where(valid, _dot_nt(q2, k), -jnp.inf)
        rowh = lax.broadcasted_iota(jnp.int32, (2 * BLOCK, 1), 0)
        sink = jnp.where(rowh < BLOCK, sink_ref[2 * j], sink_ref[2 * j + 1])
        m = jnp.maximum(jnp.max(s, axis=-1, keepdims=True), sink)
        p = jnp.exp(s - m)
        denom = jnp.sum(p, axis=-1, keepdims=True) + jnp.exp(sink - m)
        o = _dot(p.astype(BF16), v) / denom
        slabs.append(_unstack_heads(o, BLOCK))
    o_ref[...] = jnp.concatenate(slabs, axis=1).astype(o_ref.dtype)


def swa_attention(q, k, v, sink):
    b, s, w = q.shape
    nb = s // BLOCK
    cur = lambda bi, i: (bi, i, 0)
    prev = lambda bi, i: (bi, jnp.maximum(i - 1, 0), 0)
    blk = lambda im: pl.BlockSpec((None, BLOCK, w), im)
    return pl.pallas_call(
        _swa_kernel,
        grid=(b, nb),
        in_specs=[pl.BlockSpec(memory_space=pltpu.SMEM),
                  blk(cur), blk(prev), blk(cur), blk(prev), blk(cur)],
        out_specs=blk(cur),
        out_shape=jax.ShapeDtypeStruct((b, s, w), BF16),
        compiler_params=_params("parallel", "parallel"),
        name="swa_attention",
    )(sink, q, k, k, v, v)


def _pool_kernel(up_ref, u_ref, w_ref, scale_ref, o_ref, ext_ref):
    i = pl.program_id(1)
    rows = u_ref.shape[0]
    u = u_ref[...]
    halo = up_ref[...]
    ext_ref[0:POOL_HALO, :] = jnp.where(i > 0, halo, jnp.zeros_like(halo))
    ext_ref[POOL_HALO:POOL_HALO + rows, :] = u
    lane = lax.broadcasted_iota(jnp.int32, u.shape, 1)
    pos = i * rows + lax.broadcasted_iota(jnp.int32, u.shape, 0)
    group_ch = GROUP_WIDTH // len(POOL_WINDOWS)
    total = u
    d = jnp.zeros_like(u)
    width = 1
    for gi, w in enumerate(POOL_WINDOWS):
        while width < w:
            total = total + ext_ref[pl.ds(POOL_HALO - width, rows), :]
            width += 1
        cnt = jnp.minimum(pos + 1, w).astype(F32)
        in_group = (lane >= gi * group_ch) & (lane < (gi + 1) * group_ch)
        d = jnp.where(in_group, total / cnt - u, d)
    o_ref[...] = (_dot(d.astype(BF16), w_ref[...]) * scale_ref[...]).astype(o_ref.dtype)


def multiscale_pool(u, w_blockdiag, pool_scale, row_tile=512):
    b, s, w = u.shape
    halo_per_tile = row_tile // POOL_HALO
    return pl.pallas_call(
        _pool_kernel,
        grid=(b, s // row_tile),
        in_specs=[
            pl.BlockSpec((None, POOL_HALO, w), lambda bi, i: (bi, jnp.maximum(i * halo_per_tile - 1, 0), 0)),
            pl.BlockSpec((None, row_tile, w), lambda bi, i: (bi, i, 0)),
            pl.BlockSpec((w, w), lambda bi, i: (0, 0)),
            pl.BlockSpec((1, w), lambda bi, i: (0, 0)),
        ],
        out_specs=pl.BlockSpec((None, row_tile, w), lambda bi, i: (bi, i, 0)),
        out_shape=jax.ShapeDtypeStruct((b, s, w), BF16),
        scratch_shapes=[pltpu.VMEM((POOL_HALO + row_tile, w), F32)],
        compiler_params=_params("parallel", "parallel"),
        name="multiscale_pool",
    )(u, u, w_blockdiag, pool_scale)


def _compress_kernel(x_ref, w1_ref, pos_ref, w2k_ref, w2v_ref, kc_ref, vc_ref):
    n_chunks = x_ref.shape[0]
    both = _dot(x_ref[...], w1_ref[...], HI)
    pre = both[:, 0:LANES] + pltpu.roll(both[:, LANES:2 * LANES], n_chunks - 1, 0) + pos_ref[...]
    hid = _silu(pre)
    kc_ref[...] = _dot(hid, w2k_ref[...], HI).astype(kc_ref.dtype)
    vc_ref[...] = _dot(hid, w2v_ref[...], HI).astype(vc_ref.dtype)


def nsa_compress(cmp_in, w1_packed, pos_term, w2k, w2v):
    b, s, w = cmp_in.shape
    n_chunks = s // CMP_STRIDE
    flat = cmp_in.reshape(b, n_chunks, CMP_STRIDE * w)
    const = lambda shape: pl.BlockSpec(shape, lambda bi: tuple(0 for _ in shape))
    out = jax.ShapeDtypeStruct((b, n_chunks, LANES), BF16)
    return pl.pallas_call(
        _compress_kernel,
        grid=(b,),
        in_specs=[pl.BlockSpec((None, n_chunks, CMP_STRIDE * w), lambda bi: (bi, 0, 0)),
                  const(w1_packed.shape), const(pos_term.shape), const(w2k.shape), const(w2v.shape)],
        out_specs=(pl.BlockSpec((None, n_chunks, LANES), lambda bi: (bi, 0, 0)),) * 2,
        out_shape=(out, out),
        compiler_params=_params("parallel"),
        name="nsa_compress",
    )(flat, w1_packed, pos_term, w2k, w2v)


def pack_compress_weights(cmp_pos, cmp_w1, cmp_w2):
    hd, half = HEAD_DIM, CMP_LEN // 2
    w1 = cmp_w1.reshape(2, 2, half, hd, hd)
    zeros = jnp.zeros((half, hd, hd), F32)
    halves = []
    for part in range(2):
        wk = jnp.concatenate([w1[0, part], zeros], axis=-1)
        wv = jnp.concatenate([zeros, w1[1, part]], axis=-1)
        halves.append(jnp.concatenate([wk, wv], axis=1).reshape(half * 2 * hd, 2 * hd))
    w1_packed = jnp.concatenate(halves, axis=1)
    pos_flat = cmp_pos.reshape(2, 1, CMP_LEN * hd)
    pos_term = jnp.concatenate([jnp.matmul(pos_flat[0], cmp_w1[0], precision=HI),
                                jnp.matmul(pos_flat[1], cmp_w1[1], precision=HI)], axis=1)
    zero2 = jnp.zeros((hd, 2 * hd), F32)
    w2k = jnp.concatenate([jnp.concatenate([cmp_w2[0], cmp_w2[0]], axis=1), zero2], axis=0)
    w2v = jnp.concatenate([zero2, jnp.concatenate([cmp_w2[1], cmp_w2[1]], axis=1)], axis=0)
    return w1_packed, pos_term, w2k, w2v


def _nsa_kernel(q_ref, gate_ref, kc_ref, vc_ref, ksel_ref, vsel_ref, kwin_ref, vwin_ref,
                cover_ref, gexp_ref, o_ref, score_ref, m_ref, l_ref, acc_ref):
    i = pl.program_id(1)
    rows = N_HEADS * BLOCK
    q = q_ref[...]
    q4 = jnp.concatenate([_stack_heads(q[:, 0:LANES]), _stack_heads(q[:, LANES:2 * LANES])], axis=0)
    q4 = q4 * jnp.asarray(HEAD_DIM ** -0.5, BF16)
    tq = i * BLOCK + (lax.broadcasted_iota(jnp.int32, (rows, 1), 0) & (BLOCK - 1))

    n_cmp = kc_ref.shape[0]
    n_idx = lax.broadcasted_iota(jnp.int32, (1, n_cmp), 1)
    valid_c = (n_idx * CMP_STRIDE + (CMP_LEN - 1) <= tq) & (n_idx < n_cmp - 1)
    s_c = jnp.where(valid_c, _dot_nt(q4, kc_ref[...]), -jnp.inf)
    m_c = jnp.max(s_c, axis=-1, keepdims=True)
    m_c = jnp.where(m_c == -jnp.inf, 0.0, m_c)
    p_c = jnp.where(valid_c, jnp.exp(s_c - m_c), 0.0)
    d_c = jnp.sum(p_c, axis=-1, keepdims=True)
    p_c = p_c / jnp.where(d_c > 0, d_c, 1.0)
    o_c = _dot(p_c.astype(BF16), vc_ref[...])

    p_heads = p_c[0:BLOCK] + p_c[BLOCK:2 * BLOCK] + p_c[2 * BLOCK:3 * BLOCK] + p_c[3 * BLOCK:4 * BLOCK]
    importance = _dot_nt(cover_ref[...], p_heads, HI)
    n_sel = cover_ref.shape[0]
    blk = lax.broadcasted_iota(jnp.int32, (n_sel, BLOCK), 0)
    t_lane = i * BLOCK + lax.broadcasted_iota(jnp.int32, (n_sel, BLOCK), 1)
    cur = t_lane // SEL_BLOCK
    causal = blk * SEL_BLOCK <= t_lane
    forced = (blk == 0) | (blk == cur) | (blk == cur - 1)
    score = jnp.where(causal, jnp.where(forced, NSA_FORCE, importance), -jnp.inf)
    score_ref[...] = score

    def rank_step(j, rank):
        other = score_ref[pl.ds(j, 1), :]
        ahead = (other > score) | ((other == score) & (blk > j))
        return rank + jnp.where(ahead, 1, 0)

    n_causal_blocks = (i * BLOCK + BLOCK - 1) // SEL_BLOCK + 1
    rank = lax.fori_loop(0, jnp.minimum(n_causal_blocks, n_sel), rank_step,
                         jnp.zeros((n_sel, BLOCK), jnp.int32))
    chosen = jnp.where((rank < NSA_TOP_N) & causal, 1.0, 0.0)
    chosen_q = chosen.T.astype(BF16)

    m_ref[...] = jnp.full(m_ref.shape, MASKED, F32)
    l_ref[...] = jnp.zeros(l_ref.shape, F32)
    acc_ref[...] = jnp.zeros(acc_ref.shape, F32)
    tq_blk = i * BLOCK + lax.broadcasted_iota(jnp.int32, (BLOCK, 1), 0)

    def sel_step(c, carry):
        start = pl.multiple_of(c * SEL_CHUNK, SEL_CHUNK)
        key = start + lax.broadcasted_iota(jnp.int32, (1, SEL_CHUNK), 1)
        expand = jnp.where(
            lax.broadcasted_iota(jnp.int32, (n_sel, SEL_CHUNK), 0)
            == (start + lax.broadcasted_iota(jnp.int32, (n_sel, SEL_CHUNK), 1)) // SEL_BLOCK,
            1.0, 0.0).astype(BF16)
        keep = jnp.where(key <= tq_blk, _dot(chosen_q, expand), 0.0)
        keep = jnp.concatenate([keep] * N_HEADS, axis=0) > 0.5
        s = jnp.where(keep, _dot_nt(q4, ksel_ref[pl.ds(start, SEL_CHUNK), :]), MASKED)
        m_old = m_ref[...]
        m_new = jnp.maximum(m_old, jnp.max(s, axis=-1, keepdims=True))
        alpha = jnp.exp(m_old - m_new)
        p = jnp.where(keep, jnp.exp(s - m_new), 0.0)
        l_ref[...] = alpha * l_ref[...] + jnp.sum(p, axis=-1, keepdims=True)
        acc_ref[...] = alpha * acc_ref[...] + _dot(p.astype(BF16), vsel_ref[pl.ds(start, SEL_CHUNK), :])
        m_ref[...] = m_new
        return carry

    lax.fori_loop(0, (i * BLOCK + BLOCK - 1) // SEL_CHUNK + 1, sel_step, 0)
    o_s = acc_ref[...] / l_ref[...]

    span = NSA_WINDOW + BLOCK
    start_w = pl.multiple_of(jnp.maximum(i * BLOCK - NSA_WINDOW, 0), BLOCK)
    tk = start_w + lax.broadcasted_iota(jnp.int32, (1, span), 1)
    valid_w = (tk <= tq) & (tk > tq - NSA_WINDOW)
    s_w = jnp.where(valid_w, _dot_nt(q4, kwin_ref[pl.ds(start_w, span), :]), -jnp.inf)
    p_w = jnp.exp(s_w - jnp.max(s_w, axis=-1, keepdims=True))
    o_w = _dot(p_w.astype(BF16), vwin_ref[pl.ds(start_w, span), :]) / jnp.sum(p_w, axis=-1, keepdims=True)

    def heads_to_lanes(o):
        return jnp.concatenate([_unstack_heads(o[0:2 * BLOCK], BLOCK),
                                _unstack_heads(o[2 * BLOCK:4 * BLOCK], BLOCK)], axis=1)

    gates = _dot(_sigmoid(gate_ref[...]), gexp_ref[...], HI)
    gw = GROUP_WIDTH
    out = (gates[:, 0:gw] * heads_to_lanes(o_c) + gates[:, gw:2 * gw] * heads_to_lanes(o_s)
           + gates[:, 2 * gw:3 * gw] * heads_to_lanes(o_w))
    o_ref[...] = out.astype(o_ref.dtype)


def nsa_constants(seq):
    n_cmp_rows = seq // CMP_STRIDE
    n_sel = seq // SEL_BLOCK
    cmp_start = np.arange(n_cmp_rows) * CMP_STRIDE
    sel_start = np.arange(n_sel) * SEL_BLOCK
    cover = np.maximum(np.minimum(cmp_start[None, :] + CMP_LEN, sel_start[:, None] + SEL_BLOCK)
                       - np.maximum(cmp_start[None, :], sel_start[:, None]), 0).astype(np.float32) / CMP_LEN
    gexp = np.zeros((LANES, 3 * GROUP_WIDTH), np.float32)
    for h in range(N_HEADS):
        for br in range(3):
            gexp[h * 3 + br, br * GROUP_WIDTH + h * HEAD_DIM: br * GROUP_WIDTH + (h + 1) * HEAD_DIM] = 1.0
    return cover, gexp


def nsa_attention(q, gate, kc, vc, ksel, vsel, kwin, vwin, cover, gexp):
    b, s, w = q.shape
    assert s >= NSA_WINDOW + BLOCK and s % SEL_CHUNK == 0
    n_sel = s // SEL_BLOCK
    blk = lambda width: pl.BlockSpec((None, BLOCK, width), lambda bi, i: (bi, i, 0))
    per_batch = lambda rows: pl.BlockSpec((None, rows, LANES), lambda bi, i: (bi, 0, 0))
    const = lambda shape: pl.BlockSpec(shape, lambda bi, i: (0, 0))
    rows = N_HEADS * BLOCK
    return pl.pallas_call(
        _nsa_kernel,
        grid=(b, s // BLOCK),
        in_specs=[blk(w), blk(LANES), per_batch(kc.shape[1]), per_batch(vc.shape[1]),
                  per_batch(s), per_batch(s), per_batch(s), per_batch(s),
                  const(cover.shape), const(gexp.shape)],
        out_specs=blk(w),
        out_shape=jax.ShapeDtypeStruct((b, s, w), BF16),
        scratch_shapes=[pltpu.VMEM((n_sel, BLOCK), F32), pltpu.VMEM((rows, 1), F32),
                        pltpu.VMEM((rows, 1), F32), pltpu.VMEM((rows, LANES), F32)],
        compiler_params=_params("parallel", "arbitrary"),
        name="nsa_attention",
    )(q, gate, kc, vc, ksel, vsel, kwin, vwin, cover, gexp)


CONV_TAIL = 8


def _gdn_kernel(qkv_ref, z_ref, ba_ref, convw_ref, alog_ref, dtb_ref, nw_ref, bexp_ref, bd_ref,
                o_ref, ext_ref, state_ref):
    c = pl.program_id(1)
    cl, gw = DN_CHUNK, GROUP_WIDTH

    @pl.when(c == 0)
    def _():
        ext_ref[0:CONV_TAIL, :] = jnp.zeros((CONV_TAIL, 3 * gw), F32)
        state_ref[...] = jnp.zeros(state_ref.shape, F32)

    ext_ref[CONV_TAIL:CONV_TAIL + cl, :] = qkv_ref[...]
    cw = convw_ref[...]
    acc = ext_ref[CONV_TAIL:CONV_TAIL + cl, :] * cw[DN_CONV - 1:DN_CONV, :]
    for j in range(DN_CONV - 1):
        acc = acc + ext_ref[pl.ds(CONV_TAIL - (DN_CONV - 1) + j, cl), :] * cw[j:j + 1, :]
    ext_ref[0:CONV_TAIL, :] = ext_ref[cl:cl + CONV_TAIL, :]
    act = _silu(acc)

    bd = bd_ref[...]
    per_head_sum = lambda t: _dot(t, bd, HI)
    q = act[:, 0:gw]
    k = act[:, gw:2 * gw]
    v = act[:, 2 * gw:3 * gw]
    q = q * lax.rsqrt(per_head_sum(q * q) + 1e-6) * (HEAD_DIM ** -0.5)
    k = k * lax.rsqrt(per_head_sum(k * k) + 1e-6)

    ba = _dot(ba_ref[...], bexp_ref[...], HI)
    beta = _sigmoid(ba[:, 0:gw])
    a_in = ba[:, gw:2 * gw] + dtb_ref[...]
    softplus = jnp.maximum(a_in, 0.0) + jnp.log(1.0 + jnp.exp(-jnp.abs(a_in)))
    g = -jnp.exp(alog_ref[...]) * softplus

    ri = lax.broadcasted_iota(jnp.int32, (cl, cl), 0)
    ci = lax.broadcasted_iota(jnp.int32, (cl, cl), 1)
    causal = ci <= ri
    strict = ci < ri
    eye = jnp.where(ci == ri, 1.0, 0.0)
    gc = _dot(jnp.where(causal, 1.0, 0.0), g, HI)
    g_last = gc[cl - 1:cl, :]
    eg = jnp.exp(gc)
    k_beta = k * beta
    v_beta = v * beta
    kbg = k_beta * eg
    q_dec = q * eg
    k_dec = k * jnp.exp(g_last - gc)
    chunk_decay = jnp.exp(g_last)

    lane = lax.broadcasted_iota(jnp.int32, (1, gw), 1)
    u = jnp.zeros((cl, gw), F32)
    w = jnp.zeros((cl, gw), F32)
    intras, head_masks = [], []
    for h in range(N_HEADS):
        mh = jnp.where((lane >= h * HEAD_DIM) & (lane < (h + 1) * HEAD_DIM), 1.0, 0.0)
        pick = jnp.broadcast_to(mh * (1.0 / HEAD_DIM), (cl, gw))
        diff = _dot_nt(gc, pick, HI) - _dot_nt(pick, gc, HI)
        decay = jnp.exp(jnp.where(causal, diff, -jnp.inf))
        lower = jnp.where(strict, _dot_nt(k_beta * mh, k, HI) * decay, 0.0)
        intras.append(jnp.where(causal, _dot_nt(q * mh, k, HI) * decay, 0.0))
        head_masks.append(mh)
        power = -lower
        t_inv = eye + power
        for _ in range(5):
            power = _dot(power, power, HI)
            t_inv = t_inv + _dot(t_inv, power, HI)
        u = u + mh * _dot(t_inv, v_beta, HI)
        w = w + mh * _dot(t_inv, kbg, HI)

    state = state_ref[...]
    v_new = u - _dot(w, state, HI)
    o = _dot(q_dec, state, HI)
    for h in range(N_HEADS):
        o = o + head_masks[h] * _dot(intras[h], v_new, HI)
    outer = lax.dot_general(k_dec, v_new, (((0,), (0,)), ((), ())), precision=HI,
                            preferred_element_type=F32)
    state_ref[...] = state * chunk_decay + bd * outer

    o = o * lax.rsqrt(per_head_sum(o * o) * (1.0 / HEAD_DIM) + RMS_EPS) * nw_ref[...]
    o_ref[...] = (o * _silu(z_ref[...])).astype(o_ref.dtype)


def gdn_constants():
    lane_head = np.arange(GROUP_WIDTH) // HEAD_DIM
    bd = (lane_head[:, None] == lane_head[None, :]).astype(np.float32)
    bexp = np.zeros((LANES, 2 * GROUP_WIDTH), np.float32)
    for h in range(N_HEADS):
        bexp[h, h * HEAD_DIM:(h + 1) * HEAD_DIM] = 1.0
        bexp[N_HEADS + h, GROUP_WIDTH + h * HEAD_DIM:GROUP_WIDTH + (h + 1) * HEAD_DIM] = 1.0
    return bexp, bd


def gated_deltanet(qkv, z, ba, conv_w, a_log_rep, dt_bias_rep, norm_w_rep, bexp, bd):
    b, s, w3 = qkv.shape
    gw = GROUP_WIDTH
    blk = lambda width: pl.BlockSpec((None, DN_CHUNK, width), lambda bi, c: (bi, c, 0))
    const = lambda shape: pl.BlockSpec(shape, lambda bi, c: (0, 0))
    return pl.pallas_call(
        _gdn_kernel,
        grid=(b, s // DN_CHUNK),
        in_specs=[blk(w3), blk(gw), blk(LANES), const(conv_w.shape), const((1, gw)), const((1, gw)),
                  const((1, gw)), const(bexp.shape), const(bd.shape)],
        out_specs=blk(gw),
        out_shape=jax.ShapeDtypeStruct((b, s, gw), BF16),
        scratch_shapes=[pltpu.VMEM((CONV_TAIL + DN_CHUNK, w3), F32), pltpu.VMEM((gw, gw), F32)],
        compiler_params=_params("parallel", "arbitrary"),
        name="gated_deltanet",
    )(qkv, z, ba, conv_w, a_log_rep, dt_bias_rep, norm_w_rep, bexp, bd)


def _first_max(values):
    best = values[0]
    for v in values[1:]:
        best = jnp.maximum(best, v)
    taken = jnp.zeros_like(best)
    hot = []
    for v in values:
        h = jnp.where((v == best) & (taken < 0.5), 1.0, 0.0)
        taken = taken + h
        hot.append(h)
    return best, hot


def _softmax_rows(rows):
    m = rows[0]
    for r in rows[1:]:
        m = jnp.maximum(m, r)
    e = [jnp.exp(r - m) for r in rows]
    z = e[0]
    for t in e[1:]:
        z = z + t
    return [t / z for t in e]


def _out_proj_kernel(ya_ref, yb_ref, yc_ref, yd_ref, wo_ref, x_ref, g1_ref, gain_ref, sc_ref, sh_ref,
                     wr_ref, br_ref, xo_ref, h_ref, comb_ref, combt_ref):
    gw = GROUP_WIDTH
    y = _dot(ya_ref[...], wo_ref[0:gw, :])
    y = y + _dot(yb_ref[...], wo_ref[gw:2 * gw, :])
    y = y + _dot(yc_ref[...], wo_ref[2 * gw:3 * gw, :])
    y = y + _dot(yd_ref[...], wo_ref[3 * gw:4 * gw, :])
    x = x_ref[...] + g1_ref[...] * y
    xo_ref[...] = x
    h = _modulated_norm(x, gain_ref[...], sc_ref[...], sh_ref[...])
    h_ref[...] = h.astype(h_ref.dtype)

    logits = _dot_nt(wr_ref[...], h, HI) + br_ref[...]
    ng, ne = N_EXPERT_GROUPS, EXPERTS_PER_GROUP
    p_group = _softmax_rows([logits[r:r + 1, :] for r in range(ng)])
    pg_top, g_hot = _first_max(p_group)
    e_logits = []
    for e in range(ne):
        t = g_hot[0] * logits[ng + e:ng + e + 1, :]
        for gi in range(1, ng):
            t = t + g_hot[gi] * logits[ng + gi * ne + e:ng + gi * ne + e + 1, :]
        e_logits.append(t)
    p_exp = _softmax_rows(e_logits)
    p1, hot1 = _first_max(p_exp)
    rest = [jnp.where(h1 > 0.5, -1.0, p) for p, h1 in zip(p_exp, hot1)]
    p2, hot2 = _first_max(rest)
    total = p1 + p2
    w_exp = [(h1 * (p1 / total) + h2 * (p2 / total)) * pg_top for h1, h2 in zip(hot1, hot2)]
    combt_ref[...] = jnp.zeros(combt_ref.shape, F32)
    for gi in range(ng):
        for e in range(ne):
            combt_ref[gi * ne + e:gi * ne + e + 1, :] = g_hot[gi] * w_exp[e]
    comb_ref[...] = combt_ref[...].T


def out_projection(ys, w_out, x, g1, gain, scale, shift, w_router_t, b_router, seq, row_tile=512):
    n, d = x.shape
    tiles_per_seq = seq // row_tile
    row = lambda i: (i, 0)
    per_batch = lambda i: (i // tiles_per_seq, 0, 0)
    const = lambda shape: pl.BlockSpec(shape, lambda i: (0, 0))
    mod = pl.BlockSpec((None, 1, d), per_batch)
    return pl.pallas_call(
        _out_proj_kernel,
        grid=(n // row_tile,),
        in_specs=[pl.BlockSpec((row_tile, GROUP_WIDTH), row)] * 4
        + [const(w_out.shape), pl.BlockSpec((row_tile, d), row), mod, const((1, d)), mod, mod,
           const(w_router_t.shape), const(b_router.shape)],
        out_specs=(pl.BlockSpec((row_tile, d), row), pl.BlockSpec((row_tile, d), row),
                   pl.BlockSpec((row_tile, LANES), row)),
        out_shape=(jax.ShapeDtypeStruct((n, d), F32), jax.ShapeDtypeStruct((n, d), BF16),
                   jax.ShapeDtypeStruct((n, LANES), F32)),
        scratch_shapes=[pltpu.VMEM((LANES, row_tile), F32)],
        compiler_params=_params("parallel"),
        name="out_projection",
    )(*ys, w_out, x, g1, gain, scale, shift, w_router_t, b_router)


def _moe_kernel(h_ref, comb_ref, wg_ref, wu_ref, wd_ref, x_ref, g2_ref, fin_ref, o_ref, acc_ref, *,
                final_norm):
    e = pl.program_id(1)

    @pl.when(e == 0)
    def _():
        acc_ref[...] = jnp.zeros(acc_ref.shape, F32)

    h = h_ref[...]
    comb = comb_ref[...]
    lane = lax.broadcasted_iota(jnp.int32, comb.shape, 1)
    weight = jnp.sum(jnp.where(lane == e, comb, 0.0), axis=-1, keepdims=True)
    hid = _silu(_dot(h, wg_ref[...])) * _dot(h, wu_ref[...]) * weight
    acc_ref[...] += _dot(hid.astype(BF16), wd_ref[...])

    @pl.when(e == N_EXPERTS - 1)
    def _():
        x = x_ref[...] + g2_ref[...] * acc_ref[...]
        if final_norm:
            x = x * lax.rsqrt(jnp.mean(x * x, axis=-1, keepdims=True) + RMS_EPS) * fin_ref[...]
        o_ref[...] = x


def moe_experts(h, comb, w_gate, w_up, w_down, x, g2, fin_gain, seq, final_norm, row_tile=1024):
    n, d = x.shape
    f = w_gate.shape[-1]
    tiles_per_seq = seq // row_tile
    row = lambda i, e: (i, 0)
    return pl.pallas_call(
        functools.partial(_moe_kernel, final_norm=final_norm),
        grid=(n // row_tile, N_EXPERTS),
        in_specs=[pl.BlockSpec((row_tile, d), row), pl.BlockSpec((row_tile, LANES), row),
                  pl.BlockSpec((None, d, f), lambda i, e: (e, 0, 0)),
                  pl.BlockSpec((None, d, f), lambda i, e: (e, 0, 0)),
                  pl.BlockSpec((None, f, d), lambda i, e: (e, 0, 0)),
                  pl.BlockSpec((row_tile, d), row),
                  pl.BlockSpec((None, 1, d), lambda i, e: (i // tiles_per_seq, 0, 0)),
                  pl.BlockSpec((1, d), lambda i, e: (0, 0))],
        out_specs=pl.BlockSpec((row_tile, d), row),
        out_shape=jax.ShapeDtypeStruct((n, d), F32),
        scratch_shapes=[pltpu.VMEM((row_tile, d), F32)],
        compiler_params=_params("parallel", "arbitrary"),
        name="moe_experts",
    )(h, comb, w_gate, w_up, w_down, x, g2, fin_gain)


def kernel(x, c, norm_mix, norm_ffn, final_norm, w_ada, b_ada, w_in, w_out, attn_sink, w_pool, pool_scale, cmp_pos, cmp_w1, cmp_w2, conv_w, a_log, dt_bias, dn_norm, w_route_group, b_route_group, w_route_expert, b_route_expert, w_gate, w_up, w_down):
    batch, seq, d = x.shape
    depth = w_in.shape[0]
    n = batch * seq
    cover, gexp = nsa_constants(seq)
    bexp, bd = gdn_constants()
    mod = ada_modulation(c, w_ada, b_ada)
    spread = lambda t: jnp.repeat(t, HEAD_DIM).reshape(1, GROUP_WIDTH)
    seq3 = lambda t: t.reshape(batch, seq, t.shape[-1])
    xf = x.reshape(n, d)
    for l in range(depth):
        sh1, sc1, g1, sh2, sc2, g2 = (mod[l, :, i * d:(i + 1) * d].reshape(batch, 1, d) for i in range(6))
        p = in_projection(xf, norm_mix[l].reshape(1, d), sc1, sh1, pack_in_weights(w_in[l]), seq)
        y_a = swa_attention(seq3(p["a_q"]), seq3(p["a_k"]), seq3(p["a_v"]), attn_sink[l])
        w_pool_bd = jax.scipy.linalg.block_diag(*[w_pool[l, gi] for gi in range(len(POOL_WINDOWS))])
        y_b = multiscale_pool(seq3(p["b_u"]), w_pool_bd.astype(BF16), pool_scale[l].reshape(1, GROUP_WIDTH))
        kc, vc = nsa_compress(seq3(p["c_cmp"]), *pack_compress_weights(cmp_pos[l], cmp_w1[l], cmp_w2[l]))
        y_c = nsa_attention(seq3(p["c_q"]), seq3(p["c_gate"]), kc, vc, seq3(p["c_ksel"]), seq3(p["c_vsel"]),
                            seq3(p["c_kwin"]), seq3(p["c_vwin"]), cover, gexp)
        y_d = gated_deltanet(seq3(p["d_qkv"]), seq3(p["d_z"]), seq3(p["d_ba"]), conv_w[l], spread(a_log[l]),
                             spread(dt_bias[l]), jnp.tile(dn_norm[l], N_HEADS).reshape(1, GROUP_WIDTH), bexp, bd)
        ys = [t.reshape(n, GROUP_WIDTH) for t in (y_a, y_b, y_c, y_d)]
        n_logits = N_EXPERT_GROUPS + N_EXPERTS
        w_router_t = jnp.pad(jnp.concatenate([w_route_group[l], w_route_expert[l]], axis=1).T,
                             ((0, LANES - n_logits), (0, 0)))
        b_router = jnp.pad(jnp.concatenate([b_route_group[l], b_route_expert[l]]),
                           (0, LANES - n_logits)).reshape(LANES, 1)
        xf, h2, comb = out_projection(ys, w_out[l].astype(BF16), xf, g1, norm_ffn[l].reshape(1, d), sc2, sh2,
                                      w_router_t, b_router, seq)
        f = w_gate.shape[-1]
        xf = moe_experts(h2, comb, w_gate[l].reshape(N_EXPERTS, d, f).astype(BF16),
                         w_up[l].reshape(N_EXPERTS, d, f).astype(BF16),
                         w_down[l].reshape(N_EXPERTS, f, d).astype(BF16),
                         xf, g2, final_norm.reshape(1, d), seq, final_norm=(l == depth - 1))
    return xf.reshape(batch, seq, d)
```

```python
import functools

import numpy as np
import jax
import jax.numpy as jnp
from jax import lax
from jax.experimental import pallas as pl
from jax.experimental.pallas import tpu as pltpu

F32 = jnp.float32
BF16 = jnp.bfloat16
HI = lax.Precision.HIGHEST

HEAD_DIM = 64
N_HEADS = 4
GROUP_WIDTH = N_HEADS * HEAD_DIM
BLOCK = 128
RMS_EPS = 1e-6
SWA_WINDOW = 128
POOL_WINDOWS = (2, 4, 8, 16)
POOL_HALO = 16
CMP_LEN = 32
CMP_STRIDE = 16
SEL_BLOCK = 64
NSA_TOP_N = 16
NSA_WINDOW = 512
NSA_FORCE = 1e4
SEL_CHUNK = 512
DN_CONV = 4
DN_CHUNK = 64
N_EXPERT_GROUPS = 4
EXPERTS_PER_GROUP = 4
N_EXPERTS = N_EXPERT_GROUPS * EXPERTS_PER_GROUP
LANES = 128
MASKED = -1e30
VMEM_LIMIT = 56 * 1024 * 1024

SEG_WIDTHS = (
    ("a_q", 256), ("a_k", 256), ("a_v", 256), ("c_q", 256),
    ("c_ksel", 128), ("c_vsel", 128), ("c_kwin", 128), ("c_vwin", 128),
    ("b_u", 256), ("c_cmp", 128), ("c_gate", 128), ("d_qkv", 768), ("d_z", 256), ("d_ba", 128),
)
SEG_BF16 = ("a_q", "a_k", "a_v", "c_q", "c_ksel", "c_vsel", "c_kwin", "c_vwin")
SEG_OFFSETS = {}
_off = 0
for _name, _w in SEG_WIDTHS:
    SEG_OFFSETS[_name] = (_off, _w)
    _off += _w
PACKED_WIDTH = _off


def _sigmoid(x):
    return 1.0 / (1.0 + jnp.exp(-x))


def _silu(x):
    return x * _sigmoid(x)


def _dot(a, b, precision=None):
    return jnp.dot(a, b, precision=precision, preferred_element_type=F32)


def _dot_nt(a, b, precision=None):
    return lax.dot_general(a, b, (((1,), (1,)), ((), ())), precision=precision,
                           preferred_element_type=F32)


def _params(*semantics):
    return pltpu.CompilerParams(dimension_semantics=semantics, vmem_limit_bytes=VMEM_LIMIT)


def _ada_kernel(c_ref, w_ref, b_ref, o_ref):
    cond = _silu(c_ref[...])
    o_ref[...] = _dot(cond, w_ref[...], HI) + b_ref[...]


def ada_modulation(c, w_ada, b_ada, col_tile=1536):
    depth, d, width = w_ada.shape
    b = c.shape[0]
    return pl.pallas_call(
        _ada_kernel,
        grid=(depth, width // col_tile),
        in_specs=[
            pl.BlockSpec((b, d), lambda l, j: (0, 0)),
            pl.BlockSpec((None, d, col_tile), lambda l, j: (l, 0, j)),
            pl.BlockSpec((None, 1, col_tile), lambda l, j: (l, 0, j)),
        ],
        out_specs=pl.BlockSpec((None, b, col_tile), lambda l, j: (l, 0, j)),
        out_shape=jax.ShapeDtypeStruct((depth, b, width), F32),
        compiler_params=_params("parallel", "parallel"),
        name="ada_modulation",
    )(c, w_ada, b_ada.reshape(depth, 1, width))


def _modulated_norm(x, gain, scale, shift):
    y = x * lax.rsqrt(jnp.mean(x * x, axis=-1, keepdims=True) + RMS_EPS)
    return y * gain * (1.0 + scale) + shift


def _in_proj_kernel(x_ref, gain_ref, sc_ref, sh_ref, w_ref, *out_refs):
    h = _modulated_norm(x_ref[...], gain_ref[...], sc_ref[...], sh_ref[...]).astype(BF16)
    for (name, _), o_ref in zip(SEG_WIDTHS, out_refs):
        off, width = SEG_OFFSETS[name]
        o_ref[...] = _dot(h, w_ref[:, off:off + width]).astype(o_ref.dtype)


def in_projection(x, gain, scale, shift, w_packed, seq, row_tile=512):
    n, d = x.shape
    tiles_per_seq = seq // row_tile
    row = lambda i: (i, 0)
    per_batch = lambda i: (i // tiles_per_seq, 0, 0)
    out_shape = tuple(
        jax.ShapeDtypeStruct((n, w), BF16 if name in SEG_BF16 else F32) for name, w in SEG_WIDTHS)
    outs = pl.pallas_call(
        _in_proj_kernel,
        grid=(n // row_tile,),
        in_specs=[
            pl.BlockSpec((row_tile, d), row),
            pl.BlockSpec((1, d), lambda i: (0, 0)),
            pl.BlockSpec((None, 1, d), per_batch),
            pl.BlockSpec((None, 1, d), per_batch),
            pl.BlockSpec((d, PACKED_WIDTH), lambda i: (0, 0)),
        ],
        out_specs=tuple(pl.BlockSpec((row_tile, w), row) for _, w in SEG_WIDTHS),
        out_shape=out_shape,
        compiler_params=_params("parallel"),
        name="in_projection",
    )(x, gain, scale, shift, w_packed)
    return dict(zip((name for name, _ in SEG_WIDTHS), outs))


def pack_in_weights(w_in):
    gw, hd = GROUP_WIDTH, HEAD_DIM
    sizes = (gw, 2 * hd, 2 * hd, gw, gw, 6 * hd, 3 * N_HEADS, 3 * gw, gw, N_HEADS, N_HEADS)
    offs = np.concatenate([[0], np.cumsum(sizes)])
    (a_q, a_k, a_v, b_u, c_q, c_kv, c_gate, d_qkv, d_z, d_beta, d_a) = (
        w_in[:, offs[i]:offs[i + 1]] for i in range(len(sizes)))
    d = w_in.shape[0]
    dup = lambda t: jnp.concatenate([t, t], axis=1)
    pad = lambda t: jnp.pad(t, ((0, 0), (0, LANES - t.shape[1])))
    k_cmp, v_cmp, k_sel, v_sel, k_win, v_win = (c_kv[:, i * hd:(i + 1) * hd] for i in range(6))
    segs = {
        "a_q": a_q,
        "a_k": jnp.concatenate([dup(a_k[:, :hd]), dup(a_k[:, hd:])], axis=1),
        "a_v": jnp.concatenate([dup(a_v[:, :hd]), dup(a_v[:, hd:])], axis=1),
        "c_q": c_q,
        "c_ksel": dup(k_sel), "c_vsel": dup(v_sel), "c_kwin": dup(k_win), "c_vwin": dup(v_win),
        "b_u": b_u,
        "c_cmp": jnp.concatenate([k_cmp, v_cmp], axis=1),
        "c_gate": pad(c_gate),
        "d_qkv": d_qkv, "d_z": d_z,
        "d_ba": pad(jnp.concatenate([d_beta, d_a], axis=1)),
    }
    packed = jnp.concatenate([segs[name] for name, _ in SEG_WIDTHS], axis=1)
    assert packed.shape == (d, PACKED_WIDTH)
    return packed.astype(BF16)


def _stack_heads(slab):
    lane = lax.broadcasted_iota(jnp.int32, slab.shape, 1)
    zero = jnp.zeros_like(slab)
    return jnp.concatenate([jnp.where(lane < HEAD_DIM, slab, zero),
                            jnp.where(lane >= HEAD_DIM, slab, zero)], axis=0)


def _unstack_heads(o, rows):
    lane = lax.broadcasted_iota(jnp.int32, (rows, LANES), 1)
    return jnp.where(lane < HEAD_DIM, o[0:rows], o[rows:2 * rows])


def _swa_kernel(sink_ref, q_ref, kp_ref, kc_ref, vp_ref, vc_ref, o_ref):
    i = pl.program_id(1)
    q = q_ref[...]
    row = lax.broadcasted_iota(jnp.int32, (2 * BLOCK, 2 * BLOCK), 0)
    col = lax.broadcasted_iota(jnp.int32, (2 * BLOCK, 2 * BLOCK), 1)
    tq = row & (BLOCK - 1)
    tk = col - BLOCK
    valid = (tk <= tq) & (tk > tq - SWA_WINDOW) & ((col >= BLOCK) | (i > 0))
    slabs = []
    for j in range(N_HEADS // 2):
        lanes = slice(j * LANES, (j + 1) * LANES)
        q2 = _stack_heads(q[:, lanes]) * jnp.asarray(HEAD_DIM ** -0.5, BF16)
        k = jnp.concatenate([kp_ref[:, lanes], kc_ref[:, lanes]], axis=0)
        v = jnp.concatenate([vp_ref[:, lanes], vc_ref[:, lanes]], axis=0)
        s = jnp.where(valid, _dot_nt(q2, k), -jnp.inf)
        rowh = lax.broadcasted_iota(jnp.int32, (2 * BLOCK, 1), 0)
        sink = jnp.where(rowh < BLOCK, sink_ref[2 * j], sink_ref[2 * j + 1])
        m = jnp.maximum(jnp.max(s, axis=-1, keepdims=True), sink)
        p = jnp.exp(s - m)
        denom = jnp.sum(p, axis=-1, keepdims=True) + jnp.exp(sink - m)
        o = _dot(p.astype(BF16), v) / denom
        slabs.append(_unstack_heads(o, BLOCK))
    o_ref[...] = jnp.concatenate(slabs, axis=1).astype(o_ref.dtype)


def swa_attention(q, k, v, sink):
    b, s, w = q.shape
    nb = s // BLOCK
    cur = lambda bi, i: (bi, i, 0)
    prev = lambda bi, i: (bi, jnp.maximum(i - 1, 0), 0)
    blk = lambda im: pl.BlockSpec((None, BLOCK, w), im)
    return pl.pallas_call(
        _swa_kernel,
        grid=(b, nb),
        in_specs=[pl.BlockSpec(memory_space=pltpu.SMEM),
                  blk(cur), blk(prev), blk(cur), blk(prev), blk(cur)],
        out_specs=blk(cur),
        out_shape=jax.ShapeDtypeStruct((b, s, w), BF16),
        compiler_params=_params("parallel", "parallel"),
        name="swa_attention",
    )(sink, q, k, k, v, v)


def _pool_kernel(up_ref, u_ref, w_ref, scale_ref, o_ref, ext_ref):
    i = pl.program_id(1)
    rows = u_ref.shape[0]
    u = u_ref[...]
    halo = up_ref[...]
    ext_ref[0:POOL_HALO, :] = jnp.where(i > 0, halo, jnp.zeros_like(halo))
    ext_ref[POOL_HALO:POOL_HALO + rows, :] = u
    lane = lax.broadcasted_iota(jnp.int32, u.shape, 1)
    pos = i * rows + lax.broadcasted_iota(jnp.int32, u.shape, 0)
    group_ch = GROUP_WIDTH // len(POOL_WINDOWS)
    total = u
    d = jnp.zeros_like(u)
    width = 1
    for gi, w in enumerate(POOL_WINDOWS):
        while width < w:
            total = total + ext_ref[pl.ds(POOL_HALO - width, rows), :]
            width += 1
        cnt = jnp.minimum(pos + 1, w).astype(F32)
        in_group = (lane >= gi * group_ch) & (lane < (gi + 1) * group_ch)
        d = jnp.where(in_group, total / cnt - u, d)
    o_ref[...] = (_dot(d.astype(BF16), w_ref[...]) * scale_ref[...]).astype(o_ref.dtype)


def multiscale_pool(u, w_blockdiag, pool_scale, row_tile=512):
    b, s, w = u.shape
    halo_per_tile = row_tile // POOL_HALO
    return pl.pallas_call(
        _pool_kernel,
        grid=(b, s // row_tile),
        in_specs=[
            pl.BlockSpec((None, POOL_HALO, w), lambda bi, i: (bi, jnp.maximum(i * halo_per_tile - 1, 0), 0)),
            pl.BlockSpec((None, row_tile, w), lambda bi, i: (bi, i, 0)),
            pl.BlockSpec((w, w), lambda bi, i: (0, 0)),
            pl.BlockSpec((1, w), lambda bi, i: (0, 0)),
        ],
        out_specs=pl.BlockSpec((None, row_tile, w), lambda bi, i: (bi, i, 0)),
        out_shape=jax.ShapeDtypeStruct((b, s, w), BF16),
        scratch_shapes=[pltpu.VMEM((POOL_HALO + row_tile, w), F32)],
        compiler_params=_params("parallel", "parallel"),
        name="multiscale_pool",
    )(u, u, w_blockdiag, pool_scale)


def _compress_kernel(x_ref, w1_ref, pos_ref, w2k_ref, w2v_ref, kc_ref, vc_ref):
    n_chunks = x_ref.shape[0]
    both = _dot(x_ref[...], w1_ref[...], HI)
    pre = both[:, 0:LANES] + pltpu.roll(both[:, LANES:2 * LANES], n_chunks - 1, 0) + pos_ref[...]
    hid = _silu(pre)
    kc_ref[...] = _dot(hid, w2k_ref[...], HI).astype(kc_ref.dtype)
    vc_ref[...] = _dot(hid, w2v_ref[...], HI).astype(vc_ref.dtype)


def nsa_compress(cmp_in, w1_packed, pos_term, w2k, w2v):
    b, s, w = cmp_in.shape
    n_chunks = s // CMP_STRIDE
    flat = cmp_in.reshape(b, n_chunks, CMP_STRIDE * w)
    const = lambda shape: pl.BlockSpec(shape, lambda bi: tuple(0 for _ in shape))
    out = jax.ShapeDtypeStruct((b, n_chunks, LANES), BF16)
    return pl.pallas_call(
        _compress_kernel,
        grid=(b,),
        in_specs=[pl.BlockSpec((None, n_chunks, CMP_STRIDE * w), lambda bi: (bi, 0, 0)),
                  const(w1_packed.shape), const(pos_term.shape), const(w2k.shape), const(w2v.shape)],
        out_specs=(pl.BlockSpec((None, n_chunks, LANES), lambda bi: (bi, 0, 0)),) * 2,
        out_shape=(out, out),
        compiler_params=_params("parallel"),
        name="nsa_compress",
    )(flat, w1_packed, pos_term, w2k, w2v)


def pack_compress_weights(cmp_pos, cmp_w1, cmp_w2):
    hd, half = HEAD_DIM, CMP_LEN // 2
    w1 = cmp_w1.reshape(2, 2, half, hd, hd)
    zeros = jnp.zeros((half, hd, hd), F32)
    halves = []
    for part in range(2):
        wk = jnp.concatenate([w1[0, part], zeros], axis=-1)
        wv = jnp.concatenate([zeros, w1[1, part]], axis=-1)
        halves.append(jnp.concatenate([wk, wv], axis=1).reshape(half * 2 * hd, 2 * hd))
    w1_packed = jnp.concatenate(halves, axis=1)
    pos_flat = cmp_pos.reshape(2, 1, CMP_LEN * hd)
    pos_term = jnp.concatenate([jnp.matmul(pos_flat[0], cmp_w1[0], precision=HI),
                                jnp.matmul(pos_flat[1], cmp_w1[1], precision=HI)], axis=1)
    zero2 = jnp.zeros((hd, 2 * hd), F32)
    w2k = jnp.concatenate([jnp.concatenate([cmp_w2[0], cmp_w2[0]], axis=1), zero2], axis=0)
    w2v = jnp.concatenate([zero2, jnp.concatenate([cmp_w2[1], cmp_w2[1]], axis=1)], axis=0)
    return w1_packed, pos_term, w2k, w2v


def _nsa_kernel(q_ref, gate_ref, kc_ref, vc_ref, ksel_ref, vsel_ref, kwin_ref, vwin_ref,
                cover_ref, gexp_ref, o_ref, score_ref, q4_ref, m_ref, acc_ref):
    i = pl.program_id(1)
    rows = N_HEADS * BLOCK
    q = q_ref[...]
    q4 = jnp.concatenate([_stack_heads(q[:, 0:LANES]), _stack_heads(q[:, LANES:2 * LANES])], axis=0)
    q4 = q4 * jnp.asarray(HEAD_DIM ** -0.5, BF16)
    tq = i * BLOCK + (lax.broadcasted_iota(jnp.int32, (rows, 1), 0) & (BLOCK - 1))

    n_cmp = kc_ref.shape[0]
    n_idx = lax.broadcasted_iota(jnp.int32, (1, n_cmp), 1)
    valid_c = (n_idx * CMP_STRIDE + (CMP_LEN - 1) <= tq) & (n_idx < n_cmp - 1)
    s_c = jnp.where(valid_c, _dot_nt(q4, kc_ref[...]), -jnp.inf)
    m_c = jnp.max(s_c, axis=-1, keepdims=True)
    m_c = jnp.where(m_c == -jnp.inf, 0.0, m_c)
    p_c = jnp.where(valid_c, jnp.exp(s_c - m_c), 0.0)
    d_c = jnp.sum(p_c, axis=-1, keepdims=True)
    p_c = p_c / jnp.where(d_c > 0, d_c, 1.0)
    o_c = _dot(p_c.astype(BF16), vc_ref[...])

    p_heads = p_c[0:BLOCK] + p_c[BLOCK:2 * BLOCK] + p_c[2 * BLOCK:3 * BLOCK] + p_c[3 * BLOCK:4 * BLOCK]
    importance = _dot_nt(cover_ref[...], p_heads, HI)
    n_sel = cover_ref.shape[0]
    blk = lax.broadcasted_iota(jnp.int32, (n_sel, BLOCK), 0)
    t_lane = i * BLOCK + lax.broadcasted_iota(jnp.int32, (n_sel, BLOCK), 1)
    cur = t_lane // SEL_BLOCK
    causal = blk * SEL_BLOCK <= t_lane
    forced = (blk == 0) | (blk == cur) | (blk == cur - 1)
    score = jnp.where(causal, jnp.where(forced, NSA_FORCE, importance), -jnp.inf)
    score_ref[...] = score

    def rank_step(j, rank):
        other = score_ref[pl.ds(j, 1), :]
        ahead = (other > score) | ((other == score) & (blk > j))
        return rank + jnp.where(ahead, 1, 0)

    n_causal_blocks = (i * BLOCK + BLOCK - 1) // SEL_BLOCK + 1
    rank = lax.fori_loop(0, jnp.minimum(n_causal_blocks, n_sel), rank_step,
                         jnp.zeros((n_sel, BLOCK), jnp.int32))
    chosen = jnp.where((rank < NSA_TOP_N) & causal, 1.0, 0.0)
    chosen_q = chosen.T.astype(BF16)

    q4_ref[...] = q4
    m_ref[...] = jnp.full(m_ref.shape, MASKED, F32)
    acc_ref[...] = jnp.zeros(acc_ref.shape, F32)
    tq_blk = i * BLOCK + lax.broadcasted_iota(jnp.int32, (BLOCK, 1), 0)
    value_lane = lax.broadcasted_iota(jnp.int32, (1, LANES), 1) < HEAD_DIM

    def with_ones(v):
        return jnp.where(value_lane, v, jnp.ones_like(v))

    def sel_step(c, carry):
        start = pl.multiple_of(c * SEL_CHUNK, SEL_CHUNK)
        key = start + lax.broadcasted_iota(jnp.int32, (1, SEL_CHUNK), 1)
        expand = jnp.where(
            lax.broadcasted_iota(jnp.int32, (n_sel, SEL_CHUNK), 0)
            == (start + lax.broadcasted_iota(jnp.int32, (n_sel, SEL_CHUNK), 1)) // SEL_BLOCK,
            1.0, 0.0).astype(BF16)
        keep = jnp.where(key <= tq_blk, _dot(chosen_q, expand), 0.0) > 0.5
        k_t = ksel_ref[pl.ds(start, SEL_CHUNK), :]
        v_aug = with_ones(vsel_ref[pl.ds(start, SEL_CHUNK), :])
        head_rows = [slice(h * BLOCK, (h + 1) * BLOCK) for h in range(N_HEADS)]

        def scores(h):
            return jnp.where(keep, _dot_nt(q4_ref[head_rows[h]], k_t), MASKED)

        def probs(h, s):
            m_old = m_ref[head_rows[h]]
            m_new = jnp.maximum(m_old, jnp.max(s, axis=-1, keepdims=True))
            m_ref[head_rows[h]] = m_new
            return jnp.exp(s - m_new).astype(BF16), jnp.exp(m_old - m_new)

        def accumulate(h, p, alpha):
            acc_ref[head_rows[h]] = alpha * acc_ref[head_rows[h]] + _dot(p, v_aug)

        s0 = scores(0)
        s1 = scores(1)
        p0 = probs(0, s0)
        s2 = scores(2)
        accumulate(0, *p0)
        p1 = probs(1, s1)
        s3 = scores(3)
        accumulate(1, *p1)
        p2 = probs(2, s2)
        accumulate(2, *p2)
        p3 = probs(3, s3)
        accumulate(3, *p3)
        return carry

    lax.fori_loop(0, (i * BLOCK + BLOCK - 1) // SEL_CHUNK + 1, sel_step, 0)

    span = NSA_WINDOW + BLOCK
    start_w = pl.multiple_of(jnp.maximum(i * BLOCK - NSA_WINDOW, 0), BLOCK)
    tk = start_w + lax.broadcasted_iota(jnp.int32, (1, span), 1)
    valid_w = (tk <= tq) & (tk > tq - NSA_WINDOW)
    s_w = jnp.where(valid_w, _dot_nt(q4, kwin_ref[pl.ds(start_w, span), :]), -jnp.inf)
    p_w = jnp.exp(s_w - jnp.max(s_w, axis=-1, keepdims=True))
    acc_w = _dot(p_w.astype(BF16), with_ones(vwin_ref[pl.ds(start_w, span), :]))

    def heads_to_lanes(o):
        return jnp.concatenate([_unstack_heads(o[0:2 * BLOCK], BLOCK),
                                _unstack_heads(o[2 * BLOCK:4 * BLOCK], BLOCK)], axis=1)

    def normalized_heads_to_lanes(acc):
        slabs = []
        for j in range(N_HEADS // 2):
            even = acc[2 * j * BLOCK:(2 * j + 1) * BLOCK]
            odd = acc[(2 * j + 1) * BLOCK:(2 * j + 2) * BLOCK]
            numer = jnp.where(value_lane, even, pltpu.roll(odd, HEAD_DIM, 1))
            denom = jnp.where(value_lane, pltpu.roll(even, HEAD_DIM, 1), odd)
            slabs.append(numer / denom)
        return jnp.concatenate(slabs, axis=1)

    gates = _dot(_sigmoid(gate_ref[...]), gexp_ref[...], HI)
    gw = GROUP_WIDTH
    out = (gates[:, 0:gw] * heads_to_lanes(o_c) + gates[:, gw:2 * gw] * normalized_heads_to_lanes(acc_ref[...])
           + gates[:, 2 * gw:3 * gw] * normalized_heads_to_lanes(acc_w))
    o_ref[...] = out.astype(o_ref.dtype)


def nsa_constants(seq):
    n_cmp_rows = seq // CMP_STRIDE
    n_sel = seq // SEL_BLOCK
    cmp_start = np.arange(n_cmp_rows) * CMP_STRIDE
    sel_start = np.arange(n_sel) * SEL_BLOCK
    cover = np.maximum(np.minimum(cmp_start[None, :] + CMP_LEN, sel_start[:, None] + SEL_BLOCK)
                       - np.maximum(cmp_start[None, :], sel_start[:, None]), 0).astype(np.float32) / CMP_LEN
    gexp = np.zeros((LANES, 3 * GROUP_WIDTH), np.float32)
    for h in range(N_HEADS):
        for br in range(3):
            gexp[h * 3 + br, br * GROUP_WIDTH + h * HEAD_DIM: br * GROUP_WIDTH + (h + 1) * HEAD_DIM] = 1.0
    return cover, gexp


def nsa_attention(q, gate, kc, vc, ksel, vsel, kwin, vwin, cover, gexp):
    b, s, w = q.shape
    assert s >= NSA_WINDOW + BLOCK and s % SEL_CHUNK == 0
    n_sel = s // SEL_BLOCK
    blk = lambda width: pl.BlockSpec((None, BLOCK, width), lambda bi, i: (bi, i, 0))
    per_batch = lambda rows: pl.BlockSpec((None, rows, LANES), lambda bi, i: (bi, 0, 0))
    const = lambda shape: pl.BlockSpec(shape, lambda bi, i: (0, 0))
    rows = N_HEADS * BLOCK
    return pl.pallas_call(
        _nsa_kernel,
        grid=(b, s // BLOCK),
        in_specs=[blk(w), blk(LANES), per_batch(kc.shape[1]), per_batch(vc.shape[1]),
                  per_batch(s), per_batch(s), per_batch(s), per_batch(s),
                  const(cover.shape), const(gexp.shape)],
        out_specs=blk(w),
        out_shape=jax.ShapeDtypeStruct((b, s, w), BF16),
        scratch_shapes=[pltpu.VMEM((n_sel, BLOCK), F32), pltpu.VMEM((rows, LANES), BF16),
                        pltpu.VMEM((rows, 1), F32), pltpu.VMEM((rows, LANES), F32)],
        compiler_params=_params("parallel", "arbitrary"),
        name="nsa_attention",
    )(q, gate, kc, vc, ksel, vsel, kwin, vwin, cover, gexp)


CONV_TAIL = 8


GDN_ROWS = 256


def _split(a):
    hi = a.astype(BF16)
    return hi, (a - hi.astype(F32)).astype(BF16)


def _dot_split_lhs(a, b):
    hi, lo = _split(a)
    return _dot(hi, b) + _dot(lo, b)


def _gdn_kernel(qkv_ref, z_ref, ba_ref, convw_ref, alog_ref, dtb_ref, nw_ref, bexp_ref, bd_ref, tril_ref,
                o_ref, ext_ref, state_ref):
    step = pl.program_id(1)
    cl, gw = DN_CHUNK, GROUP_WIDTH
    rows = qkv_ref.shape[0]

    @pl.when(step == 0)
    def _():
        ext_ref[0:CONV_TAIL, :] = jnp.zeros((CONV_TAIL, 3 * gw), F32)
        state_ref[...] = jnp.zeros(state_ref.shape, F32)

    ext_ref[CONV_TAIL:CONV_TAIL + rows, :] = qkv_ref[...]
    cw = convw_ref[...]
    acc = ext_ref[CONV_TAIL:CONV_TAIL + rows, :] * cw[DN_CONV - 1:DN_CONV, :]
    for j in range(DN_CONV - 1):
        acc = acc + ext_ref[pl.ds(CONV_TAIL - (DN_CONV - 1) + j, rows), :] * cw[j:j + 1, :]
    ext_ref[0:CONV_TAIL, :] = ext_ref[rows:rows + CONV_TAIL, :]
    act = _silu(acc)

    bd = bd_ref[...]
    per_head_sum = lambda t: _dot_split_lhs(t, bd)
    q = act[:, 0:gw]
    k = act[:, gw:2 * gw]
    v = act[:, 2 * gw:3 * gw]
    q = q * lax.rsqrt(per_head_sum(q * q) + 1e-6) * (HEAD_DIM ** -0.5)
    k = k * lax.rsqrt(per_head_sum(k * k) + 1e-6)

    ba = _dot_split_lhs(ba_ref[...], bexp_ref[...])
    beta = _sigmoid(ba[:, 0:gw])
    a_in = ba[:, gw:2 * gw] + dtb_ref[...]
    softplus = jnp.maximum(a_in, 0.0) + jnp.log(1.0 + jnp.exp(-jnp.abs(a_in)))
    g = -jnp.exp(alog_ref[...]) * softplus
    tril = tril_ref[...]
    g_hi = g.astype(BF16)
    g_mid, g_lo = _split(g - g_hi.astype(F32))
    gc_all = _dot(tril, g_hi) + (_dot(tril, g_mid) + _dot(tril, g_lo))

    ri = lax.broadcasted_iota(jnp.int32, (cl, cl), 0)
    ci = lax.broadcasted_iota(jnp.int32, (cl, cl), 1)
    causal = ci <= ri
    strict = ci < ri
    eye = jnp.where(ci == ri, 1.0, 0.0)
    lane = lax.broadcasted_iota(jnp.int32, (1, gw), 1)
    head_lanes = [(lane >= h * HEAD_DIM) & (lane < (h + 1) * HEAD_DIM) for h in range(N_HEADS)]

    n_chunks = rows // cl
    pairs = [(c, h) for c in range(n_chunks) for h in range(N_HEADS)]
    chunk = lambda t, c: t[c * cl:(c + 1) * cl]
    gcs = [chunk(gc_all, c) for c in range(n_chunks)]
    gc_ts = [gc.T for gc in gcs]
    g_lasts = [gc[cl - 1:cl, :] for gc in gcs]
    egs = [jnp.exp(gc) for gc in gcs]
    ks = [chunk(k, c) for c in range(n_chunks)]
    k_bfs = [t.astype(BF16) for t in ks]
    k_betas = [chunk(k, c) * chunk(beta, c) for c in range(n_chunks)]
    v_betas = [(chunk(v, c) * chunk(beta, c)).astype(BF16) for c in range(n_chunks)]
    kbgs = [(k_betas[c] * egs[c]).astype(BF16) for c in range(n_chunks)]
    q_decs = [(chunk(q, c) * egs[c]).astype(BF16) for c in range(n_chunks)]
    k_decs = [(ks[c] * jnp.exp(g_lasts[c] - gcs[c])).astype(BF16) for c in range(n_chunks)]
    head = lambda h: slice(h * HEAD_DIM, (h + 1) * HEAD_DIM)
    decays = [jnp.exp(jnp.where(causal, gcs[c][:, head(h)] - gc_ts[c][head(h), :], -jnp.inf)) for c, h in pairs]
    per_head = lambda t: [jnp.where(head_lanes[h], t, 0.0) for h in range(N_HEADS)]
    kq = [_dot_nt(jnp.concatenate(per_head(k_betas[c]) + per_head(chunk(q, c)), axis=0).astype(BF16), k_bfs[c])
          for c in range(n_chunks)]
    kks = [kq[c][h * cl:(h + 1) * cl] for c, h in pairs]
    qks = [kq[c][(N_HEADS + h) * cl:(N_HEADS + h + 1) * cl] for c, h in pairs]
    intras = [jnp.where(causal, qk * d, 0.0) for qk, d in zip(qks, decays)]
    powers = [jnp.where(strict, -(kk * d), 0.0) for kk, d in zip(kks, decays)]
    t_invs = [eye + p for p in powers]
    for _ in range(5):
        p_bfs = [p.astype(BF16) for p in powers]
        powers = [_dot(p, p) for p in p_bfs]
        t_invs = [t + _dot(t.astype(BF16), p.astype(BF16)) for t, p in zip(t_invs, powers)]
    tv = [_dot(jnp.concatenate(t_invs[c * N_HEADS:(c + 1) * N_HEADS], axis=0).astype(BF16),
               jnp.concatenate([v_betas[c], kbgs[c]], axis=1)) for c in range(n_chunks)]

    def merge_heads(stacked, lanes):
        out = jnp.zeros((cl, gw), F32)
        for h in range(N_HEADS):
            out = jnp.where(head_lanes[h], stacked[h * cl:(h + 1) * cl, lanes], out)
        return out

    us = [merge_heads(tv[c], slice(0, gw)) for c in range(n_chunks)]
    ws = [merge_heads(tv[c], slice(gw, 2 * gw)).astype(BF16) for c in range(n_chunks)]
    bd_f = bd.astype(F32)
    tn = (((0,), (0,)), ((), ()))
    kws = [(bd_f * lax.dot_general(k_decs[c], ws[c], tn, preferred_element_type=F32)).astype(BF16)
           for c in range(n_chunks)]
    kus = [bd_f * lax.dot_general(k_decs[c], us[c].astype(BF16), tn, preferred_element_type=F32)
           for c in range(n_chunks)]
    intra_cat = [jnp.concatenate([intras[c * N_HEADS + h] for h in range(N_HEADS)], axis=1).astype(BF16)
                 for c in range(n_chunks)]

    state = state_ref[...]
    outs = []
    for c in range(n_chunks):
        on_state = _dot(jnp.concatenate([ws[c], q_decs[c], kws[c]], axis=0), state.astype(BF16))
        v_new = us[c] - on_state[0:cl]
        v_stack = jnp.concatenate(per_head(v_new), axis=0).astype(BF16)
        outs.append(on_state[cl:2 * cl] + _dot(intra_cat[c], v_stack))
        state = state * jnp.exp(g_lasts[c]) - on_state[2 * cl:] + kus[c]
    state_ref[...] = state

    o = jnp.concatenate(outs, axis=0)
    o = o * lax.rsqrt(per_head_sum(o * o) * (1.0 / HEAD_DIM) + RMS_EPS) * nw_ref[...]
    o_ref[...] = (o * _silu(z_ref[...])).astype(o_ref.dtype)


def gdn_constants():
    lane_head = np.arange(GROUP_WIDTH) // HEAD_DIM
    bd = (lane_head[:, None] == lane_head[None, :]).astype(np.float32)
    bexp = np.zeros((LANES, 2 * GROUP_WIDTH), np.float32)
    for h in range(N_HEADS):
        bexp[h, h * HEAD_DIM:(h + 1) * HEAD_DIM] = 1.0
        bexp[N_HEADS + h, GROUP_WIDTH + h * HEAD_DIM:GROUP_WIDTH + (h + 1) * HEAD_DIM] = 1.0
    r = np.arange(GDN_ROWS)
    tril = ((r[:, None] // DN_CHUNK == r[None, :] // DN_CHUNK) & (r[None, :] <= r[:, None])).astype(np.float32)
    return bexp, bd, tril


def gated_deltanet(qkv, z, ba, conv_w, a_log_rep, dt_bias_rep, norm_w_rep, bexp, bd, tril):
    b, s, w3 = qkv.shape
    gw = GROUP_WIDTH
    blk = lambda width: pl.BlockSpec((None, GDN_ROWS, width), lambda bi, c: (bi, c, 0))
    const = lambda shape: pl.BlockSpec(shape, lambda bi, c: (0, 0))
    as_bf16 = lambda t: jnp.asarray(t, BF16)
    return pl.pallas_call(
        _gdn_kernel,
        grid=(b, s // GDN_ROWS),
        in_specs=[blk(w3), blk(gw), blk(LANES), const(conv_w.shape), const((1, gw)), const((1, gw)),
                  const((1, gw)), const(bexp.shape), const(bd.shape), const(tril.shape)],
        out_specs=blk(gw),
        out_shape=jax.ShapeDtypeStruct((b, s, gw), BF16),
        scratch_shapes=[pltpu.VMEM((CONV_TAIL + GDN_ROWS, w3), F32), pltpu.VMEM((gw, gw), F32)],
        compiler_params=_params("parallel", "arbitrary"),
        name="gated_deltanet",
    )(qkv, z, ba, conv_w, a_log_rep, dt_bias_rep, norm_w_rep, as_bf16(bexp), as_bf16(bd), as_bf16(tril))


def _first_max(values):
    best = values[0]
    for v in values[1:]:
        best = jnp.maximum(best, v)
    taken = jnp.zeros_like(best)
    hot = []
    for v in values:
        h = jnp.where((v == best) & (taken < 0.5), 1.0, 0.0)
        taken = taken + h
        hot.append(h)
    return best, hot


def _softmax_rows(rows):
    m = rows[0]
    for r in rows[1:]:
        m = jnp.maximum(m, r)
    e = [jnp.exp(r - m) for r in rows]
    z = e[0]
    for t in e[1:]:
        z = z + t
    return [t / z for t in e]


def _out_proj_kernel(ya_ref, yb_ref, yc_ref, yd_ref, wo_ref, x_ref, g1_ref, gain_ref, sc_ref, sh_ref,
                     wr_ref, br_ref, xo_ref, h_ref, comb_ref, combt_ref):
    gw = GROUP_WIDTH
    y = _dot(ya_ref[...], wo_ref[0:gw, :])
    y = y + _dot(yb_ref[...], wo_ref[gw:2 * gw, :])
    y = y + _dot(yc_ref[...], wo_ref[2 * gw:3 * gw, :])
    y = y + _dot(yd_ref[...], wo_ref[3 * gw:4 * gw, :])
    x = x_ref[...] + g1_ref[...] * y
    xo_ref[...] = x
    h = _modulated_norm(x, gain_ref[...], sc_ref[...], sh_ref[...])
    h_ref[...] = h.astype(h_ref.dtype)

    logits = _dot_nt(wr_ref[...], h, HI) + br_ref[...]
    ng, ne = N_EXPERT_GROUPS, EXPERTS_PER_GROUP
    p_group = _softmax_rows([logits[r:r + 1, :] for r in range(ng)])
    pg_top, g_hot = _first_max(p_group)
    e_logits = []
    for e in range(ne):
        t = g_hot[0] * logits[ng + e:ng + e + 1, :]
        for gi in range(1, ng):
            t = t + g_hot[gi] * logits[ng + gi * ne + e:ng + gi * ne + e + 1, :]
        e_logits.append(t)
    p_exp = _softmax_rows(e_logits)
    p1, hot1 = _first_max(p_exp)
    rest = [jnp.where(h1 > 0.5, -1.0, p) for p, h1 in zip(p_exp, hot1)]
    p2, hot2 = _first_max(rest)
    total = p1 + p2
    w_exp = [(h1 * (p1 / total) + h2 * (p2 / total)) * pg_top for h1, h2 in zip(hot1, hot2)]
    combt_ref[...] = jnp.zeros(combt_ref.shape, F32)
    for gi in range(ng):
        for e in range(ne):
            combt_ref[gi * ne + e:gi * ne + e + 1, :] = g_hot[gi] * w_exp[e]
    comb_ref[...] = combt_ref[...].T


def out_projection(ys, w_out, x, g1, gain, scale, shift, w_router_t, b_router, seq, row_tile=512):
    n, d = x.shape
    tiles_per_seq = seq // row_tile
    row = lambda i: (i, 0)
    per_batch = lambda i: (i // tiles_per_seq, 0, 0)
    const = lambda shape: pl.BlockSpec(shape, lambda i: (0, 0))
    mod = pl.BlockSpec((None, 1, d), per_batch)
    return pl.pallas_call(
        _out_proj_kernel,
        grid=(n // row_tile,),
        in_specs=[pl.BlockSpec((row_tile, GROUP_WIDTH), row)] * 4
        + [const(w_out.shape), pl.BlockSpec((row_tile, d), row), mod, const((1, d)), mod, mod,
           const(w_router_t.shape), const(b_router.shape)],
        out_specs=(pl.BlockSpec((row_tile, d), row), pl.BlockSpec((row_tile, d), row),
                   pl.BlockSpec((row_tile, LANES), row)),
        out_shape=(jax.ShapeDtypeStruct((n, d), F32), jax.ShapeDtypeStruct((n, d), BF16),
                   jax.ShapeDtypeStruct((n, LANES), F32)),
        scratch_shapes=[pltpu.VMEM((LANES, row_tile), F32)],
        compiler_params=_params("parallel"),
        name="out_projection",
    )(*ys, w_out, x, g1, gain, scale, shift, w_router_t, b_router)


def _moe_kernel(h_ref, comb_ref, wg_ref, wu_ref, wd_ref, x_ref, g2_ref, fin_ref, o_ref, acc_ref, *,
                final_norm):
    e = pl.program_id(1)

    @pl.when(e == 0)
    def _():
        acc_ref[...] = jnp.zeros(acc_ref.shape, F32)

    h = h_ref[...]
    comb = comb_ref[...]
    lane = lax.broadcasted_iota(jnp.int32, comb.shape, 1)
    weight = jnp.sum(jnp.where(lane == e, comb, 0.0), axis=-1, keepdims=True)
    hid = _silu(_dot(h, wg_ref[...])) * _dot(h, wu_ref[...]) * weight
    acc_ref[...] += _dot(hid.astype(BF16), wd_ref[...])

    @pl.when(e == N_EXPERTS - 1)
    def _():
        x = x_ref[...] + g2_ref[...] * acc_ref[...]
        if final_norm:
            x = x * lax.rsqrt(jnp.mean(x * x, axis=-1, keepdims=True) + RMS_EPS) * fin_ref[...]
        o_ref[...] = x


def moe_experts(h, comb, w_gate, w_up, w_down, x, g2, fin_gain, seq, final_norm, row_tile=1024):
    n, d = x.shape
    f = w_gate.shape[-1]
    tiles_per_seq = seq // row_tile
    row = lambda i, e: (i, 0)
    return pl.pallas_call(
        functools.partial(_moe_kernel, final_norm=final_norm),
        grid=(n // row_tile, N_EXPERTS),
        in_specs=[pl.BlockSpec((row_tile, d), row), pl.BlockSpec((row_tile, LANES), row),
                  pl.BlockSpec((None, d, f), lambda i, e: (e, 0, 0)),
                  pl.BlockSpec((None, d, f), lambda i, e: (e, 0, 0)),
                  pl.BlockSpec((None, f, d), lambda i, e: (e, 0, 0)),
                  pl.BlockSpec((row_tile, d), row),
                  pl.BlockSpec((None, 1, d), lambda i, e: (i // tiles_per_seq, 0, 0)),
                  pl.BlockSpec((1, d), lambda i, e: (0, 0))],
        out_specs=pl.BlockSpec((row_tile, d), row),
        out_shape=jax.ShapeDtypeStruct((n, d), F32),
        scratch_shapes=[pltpu.VMEM((row_tile, d), F32)],
        compiler_params=_params("parallel", "arbitrary"),
        name="moe_experts",
    )(h, comb, w_gate, w_up, w_down, x, g2, fin_gain)


def kernel(x, c, norm_mix, norm_ffn, final_norm, w_ada, b_ada, w_in, w_out, attn_sink, w_pool, pool_scale, cmp_pos, cmp_w1, cmp_w2, conv_w, a_log, dt_bias, dn_norm, w_route_group, b_route_group, w_route_expert, b_route_expert, w_gate, w_up, w_down):
    batch, seq, d = x.shape
    depth = w_in.shape[0]
    n = batch * seq
    cover, gexp = nsa_constants(seq)
    bexp, bd, tril = gdn_constants()
    mod = ada_modulation(c, w_ada, b_ada)
    spread = lambda t: jnp.repeat(t, HEAD_DIM).reshape(1, GROUP_WIDTH)
    seq3 = lambda t: t.reshape(batch, seq, t.shape[-1])
    xf = x.reshape(n, d)
    for l in range(depth):
        sh1, sc1, g1, sh2, sc2, g2 = (mod[l, :, i * d:(i + 1) * d].reshape(batch, 1, d) for i in range(6))
        p = in_projection(xf, norm_mix[l].reshape(1, d), sc1, sh1, pack_in_weights(w_in[l]), seq)
        y_a = swa_attention(seq3(p["a_q"]), seq3(p["a_k"]), seq3(p["a_v"]), attn_sink[l])
        w_pool_bd = jax.scipy.linalg.block_diag(*[w_pool[l, gi] for gi in range(len(POOL_WINDOWS))])
        y_b = multiscale_pool(seq3(p["b_u"]), w_pool_bd.astype(BF16), pool_scale[l].reshape(1, GROUP_WIDTH))
        kc, vc = nsa_compress(seq3(p["c_cmp"]), *pack_compress_weights(cmp_pos[l], cmp_w1[l], cmp_w2[l]))
        y_c = nsa_attention(seq3(p["c_q"]), seq3(p["c_gate"]), kc, vc, seq3(p["c_ksel"]), seq3(p["c_vsel"]),
                            seq3(p["c_kwin"]), seq3(p["c_vwin"]), cover, gexp)
        y_d = gated_deltanet(seq3(p["d_qkv"]), seq3(p["d_z"]), seq3(p["d_ba"]), conv_w[l], spread(a_log[l]),
                             spread(dt_bias[l]), jnp.tile(dn_norm[l], N_HEADS).reshape(1, GROUP_WIDTH), bexp, bd, tril)
        ys = [t.reshape(n, GROUP_WIDTH) for t in (y_a, y_b, y_c, y_d)]
        n_logits = N_EXPERT_GROUPS + N_EXPERTS
        w_router_t = jnp.pad(jnp.concatenate([w_route_group[l], w_route_expert[l]], axis=1).T,
                             ((0, LANES - n_logits), (0, 0)))
        b_router = jnp.pad(jnp.concatenate([b_route_group[l], b_route_expert[l]]),
                           (0, LANES - n_logits)).reshape(LANES, 1)
        xf, h2, comb = out_projection(ys, w_out[l].astype(BF16), xf, g1, norm_ffn[l].reshape(1, d), sc2, sh2,
                                      w_router_t, b_router, seq)
        f = w_gate.shape[-1]
        xf = moe_experts(h2, comb, w_gate[l].reshape(N_EXPERTS, d, f).astype(BF16),
                         w_up[l].reshape(N_EXPERTS, d, f).astype(BF16),
                         w_down[l].reshape(N_EXPERTS, f, d).astype(BF16),
                         xf, g2, final_norm.reshape(1, d), seq, final_norm=(l == depth - 1))
    return xf.reshape(batch, seq, d)
```

```python
import functools

import numpy as np
import jax
import jax.numpy as jnp
from jax import lax
from jax.experimental import pallas as pl
from jax.experimental.pallas import tpu as pltpu

F32 = jnp.float32
BF16 = jnp.bfloat16
HI = lax.Precision.HIGHEST

HEAD_DIM = 64
N_HEADS = 4
GROUP_WIDTH = N_HEADS * HEAD_DIM
BLOCK = 128
RMS_EPS = 1e-6
SWA_WINDOW = 128
POOL_WINDOWS = (2, 4, 8, 16)
POOL_HALO = 16
CMP_LEN = 32
CMP_STRIDE = 16
SEL_BLOCK = 64
NSA_TOP_N = 16
NSA_WINDOW = 512
NSA_FORCE = 1e4
SEL_CHUNK = 1024
DN_CONV = 4
DN_CHUNK = 64
N_EXPERT_GROUPS = 4
EXPERTS_PER_GROUP = 4
N_EXPERTS = N_EXPERT_GROUPS * EXPERTS_PER_GROUP
LANES = 128
MASKED = -1e30
QK_SCALE = HEAD_DIM ** -0.5 * float(np.log2(np.e))
VMEM_LIMIT = 56 * 1024 * 1024

SEG_WIDTHS = (
    ("a_q", 256), ("a_k", 256), ("a_v", 256), ("c_q", 256),
    ("c_ksel", 128), ("c_vsel", 128), ("c_kwin", 128), ("c_vwin", 128),
    ("b_u", 256), ("c_cmp", 128), ("c_gate", 128), ("d_qkv", 768), ("d_z", 256), ("d_ba", 128),
)
SEG_BF16 = ("a_q", "a_k", "a_v", "c_q", "c_ksel", "c_vsel", "c_kwin", "c_vwin")
SEG_OFFSETS = {}
_off = 0
for _name, _w in SEG_WIDTHS:
    SEG_OFFSETS[_name] = (_off, _w)
    _off += _w
PACKED_WIDTH = _off


def _sigmoid(x):
    return 1.0 / (1.0 + jnp.exp(-x))


def _silu(x):
    return x * _sigmoid(x)


def _dot(a, b, precision=None):
    return jnp.dot(a, b, precision=precision, preferred_element_type=F32)


def _dot_nt(a, b, precision=None):
    return lax.dot_general(a, b, (((1,), (1,)), ((), ())), precision=precision,
                           preferred_element_type=F32)


def _split(a):
    hi = a.astype(BF16)
    return hi, (a - hi.astype(F32)).astype(BF16)


def _params(*semantics):
    return pltpu.CompilerParams(dimension_semantics=semantics, vmem_limit_bytes=VMEM_LIMIT)


def _ada_kernel(c_ref, w_ref, b_ref, o_ref):
    cond = _silu(c_ref[...])
    o_ref[...] = _dot(cond, w_ref[...], HI) + b_ref[...]


def ada_modulation(c, w_ada, b_ada, col_tile=1536):
    depth, d, width = w_ada.shape
    b = c.shape[0]
    return pl.pallas_call(
        _ada_kernel,
        grid=(depth, width // col_tile),
        in_specs=[
            pl.BlockSpec((b, d), lambda l, j: (0, 0)),
            pl.BlockSpec((None, d, col_tile), lambda l, j: (l, 0, j)),
            pl.BlockSpec((None, 1, col_tile), lambda l, j: (l, 0, j)),
        ],
        out_specs=pl.BlockSpec((None, b, col_tile), lambda l, j: (l, 0, j)),
        out_shape=jax.ShapeDtypeStruct((depth, b, width), F32),
        compiler_params=_params("parallel", "parallel"),
        name="ada_modulation",
    )(c, w_ada, b_ada.reshape(depth, 1, width))


def _modulated_norm(x, gain, scale, shift):
    y = x * lax.rsqrt(jnp.mean(x * x, axis=-1, keepdims=True) + RMS_EPS)
    return y * gain * (1.0 + scale) + shift


def _in_proj_kernel(x_ref, gain_ref, sc_ref, sh_ref, w_ref, *out_refs):
    h = _modulated_norm(x_ref[...], gain_ref[...], sc_ref[...], sh_ref[...]).astype(BF16)
    for (name, _), o_ref in zip(SEG_WIDTHS, out_refs):
        off, width = SEG_OFFSETS[name]
        o_ref[...] = _dot(h, w_ref[:, off:off + width]).astype(o_ref.dtype)


def in_projection(x, gain, scale, shift, w_packed, seq, row_tile=512):
    n, d = x.shape
    tiles_per_seq = seq // row_tile
    row = lambda i: (i, 0)
    per_batch = lambda i: (i // tiles_per_seq, 0, 0)
    out_shape = tuple(
        jax.ShapeDtypeStruct((n, w), BF16 if name in SEG_BF16 else F32) for name, w in SEG_WIDTHS)
    outs = pl.pallas_call(
        _in_proj_kernel,
        grid=(n // row_tile,),
        in_specs=[
            pl.BlockSpec((row_tile, d), row),
            pl.BlockSpec((1, d), lambda i: (0, 0)),
            pl.BlockSpec((None, 1, d), per_batch),
            pl.BlockSpec((None, 1, d), per_batch),
            pl.BlockSpec((d, PACKED_WIDTH), lambda i: (0, 0)),
        ],
        out_specs=tuple(pl.BlockSpec((row_tile, w), row) for _, w in SEG_WIDTHS),
        out_shape=out_shape,
        compiler_params=_params("parallel"),
        name="in_projection",
    )(x, gain, scale, shift, w_packed)
    return dict(zip((name for name, _ in SEG_WIDTHS), outs))


def pack_in_weights(w_in):
    gw, hd = GROUP_WIDTH, HEAD_DIM
    sizes = (gw, 2 * hd, 2 * hd, gw, gw, 6 * hd, 3 * N_HEADS, 3 * gw, gw, N_HEADS, N_HEADS)
    offs = np.concatenate([[0], np.cumsum(sizes)])
    (a_q, a_k, a_v, b_u, c_q, c_kv, c_gate, d_qkv, d_z, d_beta, d_a) = (
        w_in[:, offs[i]:offs[i + 1]] for i in range(len(sizes)))
    d = w_in.shape[0]
    dup = lambda t: jnp.concatenate([t, t], axis=1)
    pad = lambda t: jnp.pad(t, ((0, 0), (0, LANES - t.shape[1])))
    k_cmp, v_cmp, k_sel, v_sel, k_win, v_win = (c_kv[:, i * hd:(i + 1) * hd] for i in range(6))
    a_q = a_q * QK_SCALE
    c_q = c_q * QK_SCALE
    segs = {
        "a_q": a_q,
        "a_k": jnp.concatenate([dup(a_k[:, :hd]), dup(a_k[:, hd:])], axis=1),
        "a_v": jnp.concatenate([dup(a_v[:, :hd]), dup(a_v[:, hd:])], axis=1),
        "c_q": c_q,
        "c_ksel": dup(k_sel), "c_vsel": dup(v_sel), "c_kwin": dup(k_win), "c_vwin": dup(v_win),
        "b_u": b_u,
        "c_cmp": jnp.concatenate([k_cmp, v_cmp], axis=1),
        "c_gate": pad(c_gate),
        "d_qkv": d_qkv, "d_z": d_z,
        "d_ba": pad(jnp.concatenate([d_beta, d_a], axis=1)),
    }
    packed = jnp.concatenate([segs[name] for name, _ in SEG_WIDTHS], axis=1)
    assert packed.shape == (d, PACKED_WIDTH)
    return packed.astype(BF16)


def _stack_heads(slab):
    lane = lax.broadcasted_iota(jnp.int32, slab.shape, 1)
    zero = jnp.zeros_like(slab)
    return jnp.concatenate([jnp.where(lane < HEAD_DIM, slab, zero),
                            jnp.where(lane >= HEAD_DIM, slab, zero)], axis=0)


def _unstack_heads(o, rows):
    lane = lax.broadcasted_iota(jnp.int32, (rows, LANES), 1)
    return jnp.where(lane < HEAD_DIM, o[0:rows], o[rows:2 * rows])


def _swa_kernel(sink_ref, q_ref, kp_ref, kc_ref, vp_ref, vc_ref, o_ref):
    i = pl.program_id(1)
    q = q_ref[...]
    row = lax.broadcasted_iota(jnp.int32, (2 * BLOCK, 2 * BLOCK), 0)
    col = lax.broadcasted_iota(jnp.int32, (2 * BLOCK, 2 * BLOCK), 1)
    tq = row & (BLOCK - 1)
    tk = col - BLOCK
    valid = (tk <= tq) & (tk > tq - SWA_WINDOW) & ((col >= BLOCK) | (i > 0))
    slabs = []
    for j in range(N_HEADS // 2):
        lanes = slice(j * LANES, (j + 1) * LANES)
        q2 = _stack_heads(q[:, lanes])
        k = jnp.concatenate([kp_ref[:, lanes], kc_ref[:, lanes]], axis=0)
        v = jnp.concatenate([vp_ref[:, lanes], vc_ref[:, lanes]], axis=0)
        s = jnp.where(valid, _dot_nt(q2, k), -jnp.inf)
        rowh = lax.broadcasted_iota(jnp.int32, (2 * BLOCK, 1), 0)
        sink = jnp.where(rowh < BLOCK, sink_ref[2 * j], sink_ref[2 * j + 1]) * float(np.log2(np.e))
        m = jnp.maximum(jnp.max(s, axis=-1, keepdims=True), sink)
        p = jnp.exp2(s - m)
        denom = jnp.sum(p, axis=-1, keepdims=True) + jnp.exp2(sink - m)
        o = _dot(p.astype(BF16), v) / denom
        slabs.append(_unstack_heads(o, BLOCK))
    o_ref[...] = jnp.concatenate(slabs, axis=1).astype(o_ref.dtype)


def swa_attention(q, k, v, sink):
    b, s, w = q.shape
    nb = s // BLOCK
    cur = lambda bi, i: (bi, i, 0)
    prev = lambda bi, i: (bi, jnp.maximum(i - 1, 0), 0)
    blk = lambda im: pl.BlockSpec((None, BLOCK, w), im)
    return pl.pallas_call(
        _swa_kernel,
        grid=(b, nb),
        in_specs=[pl.BlockSpec(memory_space=pltpu.SMEM),
                  blk(cur), blk(prev), blk(cur), blk(prev), blk(cur)],
        out_specs=blk(cur),
        out_shape=jax.ShapeDtypeStruct((b, s, w), BF16),
        compiler_params=_params("parallel", "parallel"),
        name="swa_attention",
    )(sink, q, k, k, v, v)


def _pool_kernel(up_ref, u_ref, w_ref, scale_ref, o_ref, ext_ref):
    i = pl.program_id(1)
    rows = u_ref.shape[0]
    u = u_ref[...]
    halo = up_ref[...]
    ext_ref[0:POOL_HALO, :] = jnp.where(i > 0, halo, jnp.zeros_like(halo))
    ext_ref[POOL_HALO:POOL_HALO + rows, :] = u
    lane = lax.broadcasted_iota(jnp.int32, u.shape, 1)
    pos = i * rows + lax.broadcasted_iota(jnp.int32, u.shape, 0)
    group_ch = GROUP_WIDTH // len(POOL_WINDOWS)
    total = u
    d = jnp.zeros_like(u)
    width = 1
    for gi, w in enumerate(POOL_WINDOWS):
        while width < w:
            total = total + ext_ref[pl.ds(POOL_HALO - width, rows), :]
            width += 1
        cnt = jnp.minimum(pos + 1, w).astype(F32)
        in_group = (lane >= gi * group_ch) & (lane < (gi + 1) * group_ch)
        d = jnp.where(in_group, total / cnt - u, d)
    o_ref[...] = (_dot(d.astype(BF16), w_ref[...]) * scale_ref[...]).astype(o_ref.dtype)


def multiscale_pool(u, w_blockdiag, pool_scale, row_tile=512):
    b, s, w = u.shape
    halo_per_tile = row_tile // POOL_HALO
    return pl.pallas_call(
        _pool_kernel,
        grid=(b, s // row_tile),
        in_specs=[
            pl.BlockSpec((None, POOL_HALO, w), lambda bi, i: (bi, jnp.maximum(i * halo_per_tile - 1, 0), 0)),
            pl.BlockSpec((None, row_tile, w), lambda bi, i: (bi, i, 0)),
            pl.BlockSpec((w, w), lambda bi, i: (0, 0)),
            pl.BlockSpec((1, w), lambda bi, i: (0, 0)),
        ],
        out_specs=pl.BlockSpec((None, row_tile, w), lambda bi, i: (bi, i, 0)),
        out_shape=jax.ShapeDtypeStruct((b, s, w), BF16),
        scratch_shapes=[pltpu.VMEM((POOL_HALO + row_tile, w), F32)],
        compiler_params=_params("parallel", "parallel"),
        name="multiscale_pool",
    )(u, u, w_blockdiag, pool_scale)


def _compress_kernel(x_ref, w1_ref, pos_ref, w2k_ref, w2v_ref, kc_ref, vc_ref):
    n_chunks = x_ref.shape[0]
    both = _dot(x_ref[...], w1_ref[...], HI)
    pre = both[:, 0:LANES] + pltpu.roll(both[:, LANES:2 * LANES], n_chunks - 1, 0) + pos_ref[...]
    hid = _silu(pre)
    kc_ref[...] = _dot(hid, w2k_ref[...], HI).astype(kc_ref.dtype)
    vc_ref[...] = _dot(hid, w2v_ref[...], HI).astype(vc_ref.dtype)


def nsa_compress(cmp_in, w1_packed, pos_term, w2k, w2v):
    b, s, w = cmp_in.shape
    n_chunks = s // CMP_STRIDE
    flat = cmp_in.reshape(b, n_chunks, CMP_STRIDE * w)
    const = lambda shape: pl.BlockSpec(shape, lambda bi: tuple(0 for _ in shape))
    out = jax.ShapeDtypeStruct((b, n_chunks, LANES), BF16)
    return pl.pallas_call(
        _compress_kernel,
        grid=(b,),
        in_specs=[pl.BlockSpec((None, n_chunks, CMP_STRIDE * w), lambda bi: (bi, 0, 0)),
                  const(w1_packed.shape), const(pos_term.shape), const(w2k.shape), const(w2v.shape)],
        out_specs=(pl.BlockSpec((None, n_chunks, LANES), lambda bi: (bi, 0, 0)),) * 2,
        out_shape=(out, out),
        compiler_params=_params("parallel"),
        name="nsa_compress",
    )(flat, w1_packed, pos_term, w2k, w2v)


def pack_compress_weights(cmp_pos, cmp_w1, cmp_w2):
    hd, half = HEAD_DIM, CMP_LEN // 2
    w1 = cmp_w1.reshape(2, 2, half, hd, hd)
    zeros = jnp.zeros((half, hd, hd), F32)
    halves = []
    for part in range(2):
        wk = jnp.concatenate([w1[0, part], zeros], axis=-1)
        wv = jnp.concatenate([zeros, w1[1, part]], axis=-1)
        halves.append(jnp.concatenate([wk, wv], axis=1).reshape(half * 2 * hd, 2 * hd))
    w1_packed = jnp.concatenate(halves, axis=1)
    pos_flat = cmp_pos.reshape(2, 1, CMP_LEN * hd)
    pos_term = jnp.concatenate([jnp.matmul(pos_flat[0], cmp_w1[0], precision=HI),
                                jnp.matmul(pos_flat[1], cmp_w1[1], precision=HI)], axis=1)
    zero2 = jnp.zeros((hd, 2 * hd), F32)
    w2k = jnp.concatenate([jnp.concatenate([cmp_w2[0], cmp_w2[0]], axis=1), zero2], axis=0)
    w2v = jnp.concatenate([zero2, jnp.concatenate([cmp_w2[1], cmp_w2[1]], axis=1)], axis=0)
    return w1_packed, pos_term, w2k, w2v


def _nsa_kernel(q_ref, gate_ref, kc_ref, vc_ref, ksel_ref, vsel_ref, kwin_ref, vwin_ref,
                cover_ref, gexp_ref, expand_ref, o_ref, q4_ref, m_ref, acc_ref, sa_ref, sb_ref, ma_ref, mb_ref):
    i = pl.program_id(1)
    rows = N_HEADS * BLOCK
    q = q_ref[...]
    q4 = jnp.concatenate([_stack_heads(q[:, 0:LANES]), _stack_heads(q[:, LANES:2 * LANES])], axis=0)
    tq = i * BLOCK + (lax.broadcasted_iota(jnp.int32, (rows, 1), 0) & (BLOCK - 1))

    n_cmp = kc_ref.shape[0]
    n_idx = lax.broadcasted_iota(jnp.int32, (1, n_cmp), 1)
    valid_c = (n_idx * CMP_STRIDE + (CMP_LEN - 1) <= tq) & (n_idx < n_cmp - 1)
    s_c = jnp.where(valid_c, _dot_nt(q4, kc_ref[...]), -jnp.inf)
    m_c = jnp.max(s_c, axis=-1, keepdims=True)
    m_c = jnp.where(m_c == -jnp.inf, 0.0, m_c)
    p_c = jnp.where(valid_c, jnp.exp2(s_c - m_c), 0.0)
    d_c = jnp.sum(p_c, axis=-1, keepdims=True)
    p_c = p_c / jnp.where(d_c > 0, d_c, 1.0)
    o_c = _dot(p_c.astype(BF16), vc_ref[...])

    head_rows = [slice(h * BLOCK, (h + 1) * BLOCK) for h in range(N_HEADS)]
    value_lane = lax.broadcasted_iota(jnp.int32, (1, LANES), 1) < HEAD_DIM

    def with_ones(v):
        return jnp.where(value_lane, v, jnp.ones_like(v))

    span = NSA_WINDOW + BLOCK
    start_w = pl.multiple_of(jnp.maximum(i * BLOCK - NSA_WINDOW, 0), BLOCK)
    tk = start_w + lax.broadcasted_iota(jnp.int32, (1, span), 1)
    tq_blk = i * BLOCK + lax.broadcasted_iota(jnp.int32, (BLOCK, 1), 0)
    valid_w = (tk <= tq_blk) & (tk > tq_blk - NSA_WINDOW)
    k_w = kwin_ref[pl.ds(start_w, span), :]
    v_w = with_ones(vwin_ref[pl.ds(start_w, span), :])
    acc_w = []

    def window_head(h):
        s_w = jnp.where(valid_w, _dot_nt(q4[head_rows[h]], k_w), -jnp.inf)
        p_w = jnp.exp2(s_w - jnp.max(s_w, axis=-1, keepdims=True))
        acc_w.append(_dot(p_w.astype(BF16), v_w))

    p_heads = p_c[0:BLOCK] + p_c[BLOCK:2 * BLOCK] + p_c[2 * BLOCK:3 * BLOCK] + p_c[3 * BLOCK:4 * BLOCK]
    p_hi, p_lo = _split(p_heads)
    cover = cover_ref[...]
    importance = _dot_nt(cover, p_hi) + _dot_nt(cover, p_lo)
    n_sel = cover_ref.shape[0]
    blk = lax.broadcasted_iota(jnp.int32, (n_sel, BLOCK), 0)
    t_lane = i * BLOCK + lax.broadcasted_iota(jnp.int32, (n_sel, BLOCK), 1)
    cur = t_lane // SEL_BLOCK
    causal = blk * SEL_BLOCK <= t_lane
    forced = (blk == 0) | (blk == cur) | (blk == cur - 1)
    score = jnp.where(causal, jnp.where(forced, NSA_FORCE, importance), -jnp.inf)
    blk_f = blk.astype(F32)
    chosen = jnp.zeros((n_sel, BLOCK), F32)
    for r in range(NSA_TOP_N):
        if r % (NSA_TOP_N // N_HEADS) == 0:
            window_head(r // (NSA_TOP_N // N_HEADS))
        top = jnp.max(score, axis=0, keepdims=True)
        first = jnp.min(jnp.where(score == top, blk_f, float(n_sel)), axis=0, keepdims=True)
        pick = blk_f == first
        score = jnp.where(pick, -jnp.inf, score)
        chosen = jnp.where(pick, 1.0, chosen)
    chosen = jnp.where(causal, chosen, 0.0)
    chosen_q = chosen.T.astype(BF16)

    q4_ref[...] = q4
    m_ref[...] = jnp.full(m_ref.shape, MASKED, F32)
    acc_ref[...] = jnp.zeros(acc_ref.shape, F32)
    n_chunks = (i * BLOCK + BLOCK - 1) // SEL_CHUNK + 1

    def chunk_start(c):
        return pl.multiple_of(jnp.minimum(c, n_chunks - 1) * SEL_CHUNK, SEL_CHUNK)

    def chunk_keep(c):
        start = chunk_start(c)
        key = start + lax.broadcasted_iota(jnp.int32, (1, SEL_CHUNK), 1)
        on_keys = _dot(chosen_q, expand_ref[:, pl.ds(start, SEL_CHUNK)])
        return jnp.where((key <= tq_blk) & (c < n_chunks), on_keys, 0.0) > 0.5

    def stage_scores(h, bufs, c, keep):
        s = jnp.where(keep, _dot_nt(q4_ref[head_rows[h]], ksel_ref[pl.ds(chunk_start(c), SEL_CHUNK), :]), MASKED)
        bufs[0][head_rows[h]] = s
        bufs[1][head_rows[h]] = jnp.max(s, axis=-1, keepdims=True)

    def consume_scores(h, bufs, v_aug):
        m_old = m_ref[head_rows[h]]
        m_new = jnp.maximum(m_old, bufs[1][head_rows[h]])
        m_ref[head_rows[h]] = m_new
        p = jnp.exp2(bufs[0][head_rows[h]] - m_new).astype(BF16)
        acc_ref[head_rows[h]] = jnp.exp2(m_old - m_new) * acc_ref[head_rows[h]] + _dot(p, v_aug)

    def half_step(c, cur_bufs, next_bufs):
        keep_next = chunk_keep(c + 1)
        v_aug = with_ones(vsel_ref[pl.ds(chunk_start(c), SEL_CHUNK), :])
        for h in range(N_HEADS):
            stage_scores(h, next_bufs, c + 1, keep_next)
            consume_scores(h, cur_bufs, v_aug)

    bufs_a = (sa_ref, ma_ref)
    bufs_b = (sb_ref, mb_ref)
    keep0 = chunk_keep(0)
    for h in range(N_HEADS):
        stage_scores(h, bufs_a, 0, keep0)

    def sel_step(t, carry):
        half_step(2 * t, bufs_a, bufs_b)
        half_step(2 * t + 1, bufs_b, bufs_a)
        return carry

    lax.fori_loop(0, (n_chunks + 1) // 2, sel_step, 0)

    def heads_to_lanes(o):
        return jnp.concatenate([_unstack_heads(o[0:2 * BLOCK], BLOCK),
                                _unstack_heads(o[2 * BLOCK:4 * BLOCK], BLOCK)], axis=1)

    def normalized_heads_to_lanes(acc):
        slabs = []
        for j in range(N_HEADS // 2):
            even = acc[2 * j * BLOCK:(2 * j + 1) * BLOCK]
            odd = acc[(2 * j + 1) * BLOCK:(2 * j + 2) * BLOCK]
            numer = jnp.where(value_lane, even, pltpu.roll(odd, HEAD_DIM, 1))
            denom = jnp.where(value_lane, pltpu.roll(even, HEAD_DIM, 1), odd)
            slabs.append(numer / denom)
        return jnp.concatenate(slabs, axis=1)

    gates = _dot(_sigmoid(gate_ref[...]), gexp_ref[...], HI)
    gw = GROUP_WIDTH
    out = (gates[:, 0:gw] * heads_to_lanes(o_c) + gates[:, gw:2 * gw] * normalized_heads_to_lanes(acc_ref[...])
           + gates[:, 2 * gw:3 * gw] * normalized_heads_to_lanes(jnp.concatenate(acc_w, axis=0)))
    o_ref[...] = out.astype(o_ref.dtype)


def nsa_constants(seq):
    n_cmp_rows = seq // CMP_STRIDE
    n_sel = seq // SEL_BLOCK
    cmp_start = np.arange(n_cmp_rows) * CMP_STRIDE
    sel_start = np.arange(n_sel) * SEL_BLOCK
    cover = np.maximum(np.minimum(cmp_start[None, :] + CMP_LEN, sel_start[:, None] + SEL_BLOCK)
                       - np.maximum(cmp_start[None, :], sel_start[:, None]), 0).astype(np.float32) / CMP_LEN
    gexp = np.zeros((LANES, 3 * GROUP_WIDTH), np.float32)
    for h in range(N_HEADS):
        for br in range(3):
            gexp[h * 3 + br, br * GROUP_WIDTH + h * HEAD_DIM: br * GROUP_WIDTH + (h + 1) * HEAD_DIM] = 1.0
    expand = (np.arange(n_sel)[:, None] == np.arange(seq)[None, :] // SEL_BLOCK).astype(np.float32)
    return cover, gexp, expand


def nsa_attention(q, gate, kc, vc, ksel, vsel, kwin, vwin, cover, gexp, expand):
    b, s, w = q.shape
    assert s >= NSA_WINDOW + BLOCK and s % SEL_CHUNK == 0
    blk = lambda width: pl.BlockSpec((None, BLOCK, width), lambda bi, i: (bi, i, 0))
    per_batch = lambda rows: pl.BlockSpec((None, rows, LANES), lambda bi, i: (bi, 0, 0))
    const = lambda shape: pl.BlockSpec(shape, lambda bi, i: (0, 0))
    rows = N_HEADS * BLOCK
    return pl.pallas_call(
        _nsa_kernel,
        grid=(b, s // BLOCK),
        in_specs=[blk(w), blk(LANES), per_batch(kc.shape[1]), per_batch(vc.shape[1]),
                  per_batch(s), per_batch(s), per_batch(s), per_batch(s),
                  const(cover.shape), const(gexp.shape), const(expand.shape)],
        out_specs=blk(w),
        out_shape=jax.ShapeDtypeStruct((b, s, w), BF16),
        scratch_shapes=[pltpu.VMEM((rows, LANES), BF16), pltpu.VMEM((rows, 1), F32),
                        pltpu.VMEM((rows, LANES), F32), pltpu.VMEM((rows, SEL_CHUNK), F32),
                        pltpu.VMEM((rows, SEL_CHUNK), F32), pltpu.VMEM((rows, 1), F32),
                        pltpu.VMEM((rows, 1), F32)],
        compiler_params=_params("parallel", "arbitrary"),
        name="nsa_attention",
    )(q, gate, kc, vc, ksel, vsel, kwin, vwin, jnp.asarray(cover, BF16), gexp, jnp.asarray(expand, BF16))


CONV_TAIL = 8


GDN_ROWS = 256


def _dot_split_lhs(a, b):
    hi, lo = _split(a)
    return _dot(hi, b) + _dot(lo, b)


def _gdn_kernel(qkv_ref, z_ref, ba_ref, convw_ref, alog_ref, dtb_ref, nw_ref, bexp_ref, bd_ref, tril_ref,
                o_ref, ext_ref, state_ref):
    step = pl.program_id(1)
    cl, gw = DN_CHUNK, GROUP_WIDTH
    rows = qkv_ref.shape[0]

    @pl.when(step == 0)
    def _():
        ext_ref[0:CONV_TAIL, :] = jnp.zeros((CONV_TAIL, 3 * gw), F32)
        state_ref[...] = jnp.zeros(state_ref.shape, F32)

    ext_ref[CONV_TAIL:CONV_TAIL + rows, :] = qkv_ref[...]
    cw = convw_ref[...]
    acc = ext_ref[CONV_TAIL:CONV_TAIL + rows, :] * cw[DN_CONV - 1:DN_CONV, :]
    for j in range(DN_CONV - 1):
        acc = acc + ext_ref[pl.ds(CONV_TAIL - (DN_CONV - 1) + j, rows), :] * cw[j:j + 1, :]
    ext_ref[0:CONV_TAIL, :] = ext_ref[rows:rows + CONV_TAIL, :]
    act = _silu(acc)

    bd = bd_ref[...]
    per_head_sum = lambda t: _dot_split_lhs(t, bd)
    q = act[:, 0:gw]
    k = act[:, gw:2 * gw]
    v = act[:, 2 * gw:3 * gw]
    q = q * lax.rsqrt(per_head_sum(q * q) + 1e-6) * (HEAD_DIM ** -0.5)
    k = k * lax.rsqrt(per_head_sum(k * k) + 1e-6)

    ba = _dot_split_lhs(ba_ref[...], bexp_ref[...])
    beta = _sigmoid(ba[:, 0:gw])
    a_in = ba[:, gw:2 * gw] + dtb_ref[...]
    softplus = jnp.maximum(a_in, 0.0) + jnp.log(1.0 + jnp.exp(-jnp.abs(a_in)))
    g = -jnp.exp(alog_ref[...]) * softplus
    tril = tril_ref[...]
    g_hi = g.astype(BF16)
    g_mid, g_lo = _split(g - g_hi.astype(F32))
    gc_all = _dot(tril, g_hi) + (_dot(tril, g_mid) + _dot(tril, g_lo))

    ri = lax.broadcasted_iota(jnp.int32, (cl, cl), 0)
    ci = lax.broadcasted_iota(jnp.int32, (cl, cl), 1)
    causal = ci <= ri
    strict = ci < ri
    eye = jnp.where(ci == ri, 1.0, 0.0)
    lane = lax.broadcasted_iota(jnp.int32, (1, gw), 1)
    head_lanes = [(lane >= h * HEAD_DIM) & (lane < (h + 1) * HEAD_DIM) for h in range(N_HEADS)]

    n_chunks = rows // cl
    pairs = [(c, h) for c in range(n_chunks) for h in range(N_HEADS)]
    chunk = lambda t, c: t[c * cl:(c + 1) * cl]
    gcs = [chunk(gc_all, c) for c in range(n_chunks)]
    gc_ts = [gc.T for gc in gcs]
    g_lasts = [gc[cl - 1:cl, :] for gc in gcs]
    egs = [jnp.exp(gc) for gc in gcs]
    ks = [chunk(k, c) for c in range(n_chunks)]
    k_bfs = [t.astype(BF16) for t in ks]
    k_betas = [chunk(k, c) * chunk(beta, c) for c in range(n_chunks)]
    v_betas = [(chunk(v, c) * chunk(beta, c)).astype(BF16) for c in range(n_chunks)]
    kbgs = [(k_betas[c] * egs[c]).astype(BF16) for c in range(n_chunks)]
    q_decs = [(chunk(q, c) * egs[c]).astype(BF16) for c in range(n_chunks)]
    k_decs = [(ks[c] * jnp.exp(g_lasts[c] - gcs[c])).astype(BF16) for c in range(n_chunks)]
    head = lambda h: slice(h * HEAD_DIM, (h + 1) * HEAD_DIM)
    decays = [jnp.exp(jnp.where(causal, gcs[c][:, head(h)] - gc_ts[c][head(h), :], -jnp.inf)) for c, h in pairs]
    per_head = lambda t: [jnp.where(head_lanes[h], t, 0.0) for h in range(N_HEADS)]
    kq = [_dot_nt(jnp.concatenate(per_head(k_betas[c]) + per_head(chunk(q, c)), axis=0).astype(BF16), k_bfs[c])
          for c in range(n_chunks)]
    kks = [kq[c][h * cl:(h + 1) * cl] for c, h in pairs]
    qks = [kq[c][(N_HEADS + h) * cl:(N_HEADS + h + 1) * cl] for c, h in pairs]
    intras = [jnp.where(causal, qk * d, 0.0) for qk, d in zip(qks, decays)]
    powers = [jnp.where(strict, -(kk * d), 0.0) for kk, d in zip(kks, decays)]
    t_invs = [eye + p for p in powers]
    for _ in range(5):
        p_bfs = [p.astype(BF16) for p in powers]
        powers = [_dot(p, p) for p in p_bfs]
        t_invs = [t + _dot(t.astype(BF16), p.astype(BF16)) for t, p in zip(t_invs, powers)]
    tv = [_dot(jnp.concatenate(t_invs[c * N_HEADS:(c + 1) * N_HEADS], axis=0).astype(BF16),
               jnp.concatenate([v_betas[c], kbgs[c]], axis=1)) for c in range(n_chunks)]

    def merge_heads(stacked, lanes):
        out = jnp.zeros((cl, gw), F32)
        for h in range(N_HEADS):
            out = jnp.where(head_lanes[h], stacked[h * cl:(h + 1) * cl, lanes], out)
        return out

    us = [merge_heads(tv[c], slice(0, gw)) for c in range(n_chunks)]
    ws = [merge_heads(tv[c], slice(gw, 2 * gw)).astype(BF16) for c in range(n_chunks)]
    bd_f = bd.astype(F32)
    tn = (((0,), (0,)), ((), ()))
    kws = [(bd_f * lax.dot_general(k_decs[c], ws[c], tn, preferred_element_type=F32)).astype(BF16)
           for c in range(n_chunks)]
    kus = [bd_f * lax.dot_general(k_decs[c], us[c].astype(BF16), tn, preferred_element_type=F32)
           for c in range(n_chunks)]
    intra_cat = [jnp.concatenate([intras[c * N_HEADS + h] for h in range(N_HEADS)], axis=1).astype(BF16)
                 for c in range(n_chunks)]

    state = state_ref[...]
    outs = []
    for c in range(n_chunks):
        on_state = _dot(jnp.concatenate([ws[c], q_decs[c], kws[c]], axis=0), state.astype(BF16))
        v_new = us[c] - on_state[0:cl]
        v_stack = jnp.concatenate(per_head(v_new), axis=0).astype(BF16)
        outs.append(on_state[cl:2 * cl] + _dot(intra_cat[c], v_stack))
        state = state * jnp.exp(g_lasts[c]) - on_state[2 * cl:] + kus[c]
    state_ref[...] = state

    o = jnp.concatenate(outs, axis=0)
    o = o * lax.rsqrt(per_head_sum(o * o) * (1.0 / HEAD_DIM) + RMS_EPS) * nw_ref[...]
    o_ref[...] = (o * _silu(z_ref[...])).astype(o_ref.dtype)


def gdn_constants():
    lane_head = np.arange(GROUP_WIDTH) // HEAD_DIM
    bd = (lane_head[:, None] == lane_head[None, :]).astype(np.float32)
    bexp = np.zeros((LANES, 2 * GROUP_WIDTH), np.float32)
    for h in range(N_HEADS):
        bexp[h, h * HEAD_DIM:(h + 1) * HEAD_DIM] = 1.0
        bexp[N_HEADS + h, GROUP_WIDTH + h * HEAD_DIM:GROUP_WIDTH + (h + 1) * HEAD_DIM] = 1.0
    r = np.arange(GDN_ROWS)
    tril = ((r[:, None] // DN_CHUNK == r[None, :] // DN_CHUNK) & (r[None, :] <= r[:, None])).astype(np.float32)
    return bexp, bd, tril


def gated_deltanet(qkv, z, ba, conv_w, a_log_rep, dt_bias_rep, norm_w_rep, bexp, bd, tril):
    b, s, w3 = qkv.shape
    gw = GROUP_WIDTH
    blk = lambda width: pl.BlockSpec((None, GDN_ROWS, width), lambda bi, c: (bi, c, 0))
    const = lambda shape: pl.BlockSpec(shape, lambda bi, c: (0, 0))
    as_bf16 = lambda t: jnp.asarray(t, BF16)
    return pl.pallas_call(
        _gdn_kernel,
        grid=(b, s // GDN_ROWS),
        in_specs=[blk(w3), blk(gw), blk(LANES), const(conv_w.shape), const((1, gw)), const((1, gw)),
                  const((1, gw)), const(bexp.shape), const(bd.shape), const(tril.shape)],
        out_specs=blk(gw),
        out_shape=jax.ShapeDtypeStruct((b, s, gw), BF16),
        scratch_shapes=[pltpu.VMEM((CONV_TAIL + GDN_ROWS, w3), F32), pltpu.VMEM((gw, gw), F32)],
        compiler_params=_params("parallel", "arbitrary"),
        name="gated_deltanet",
    )(qkv, z, ba, conv_w, a_log_rep, dt_bias_rep, norm_w_rep, as_bf16(bexp), as_bf16(bd), as_bf16(tril))


def _first_max(values):
    best = values[0]
    for v in values[1:]:
        best = jnp.maximum(best, v)
    taken = jnp.zeros_like(best)
    hot = []
    for v in values:
        h = jnp.where((v == best) & (taken < 0.5), 1.0, 0.0)
        taken = taken + h
        hot.append(h)
    return best, hot


def _softmax_rows(rows):
    m = rows[0]
    for r in rows[1:]:
        m = jnp.maximum(m, r)
    e = [jnp.exp(r - m) for r in rows]
    z = e[0]
    for t in e[1:]:
        z = z + t
    return [t / z for t in e]


def _out_proj_kernel(ya_ref, yb_ref, yc_ref, yd_ref, wo_ref, x_ref, g1_ref, gain_ref, sc_ref, sh_ref,
                     wr_ref, br_ref, xo_ref, h_ref, comb_ref, combt_ref):
    gw = GROUP_WIDTH
    y = _dot(ya_ref[...], wo_ref[0:gw, :])
    y = y + _dot(yb_ref[...], wo_ref[gw:2 * gw, :])
    y = y + _dot(yc_ref[...], wo_ref[2 * gw:3 * gw, :])
    y = y + _dot(yd_ref[...], wo_ref[3 * gw:4 * gw, :])
    x = x_ref[...] + g1_ref[...] * y
    xo_ref[...] = x
    h = _modulated_norm(x, gain_ref[...], sc_ref[...], sh_ref[...])
    h_hi, h_lo = _split(h)
    h_ref[...] = h_hi

    w_hi, w_lo = _split(wr_ref[...])
    on_h_hi = _dot_nt(jnp.concatenate([w_hi, w_lo], axis=0), h_hi)
    logits = on_h_hi[0:LANES] + (on_h_hi[LANES:2 * LANES] + _dot_nt(w_hi, h_lo)) + br_ref[...]
    ng, ne = N_EXPERT_GROUPS, EXPERTS_PER_GROUP
    p_group = _softmax_rows([logits[r:r + 1, :] for r in range(ng)])
    pg_top, g_hot = _first_max(p_group)
    e_logits = []
    for e in range(ne):
        t = g_hot[0] * logits[ng + e:ng + e + 1, :]
        for gi in range(1, ng):
            t = t + g_hot[gi] * logits[ng + gi * ne + e:ng + gi * ne + e + 1, :]
        e_logits.append(t)
    p_exp = _softmax_rows(e_logits)
    p1, hot1 = _first_max(p_exp)
    rest = [jnp.where(h1 > 0.5, -1.0, p) for p, h1 in zip(p_exp, hot1)]
    p2, hot2 = _first_max(rest)
    total = p1 + p2
    w_exp = [(h1 * (p1 / total) + h2 * (p2 / total)) * pg_top for h1, h2 in zip(hot1, hot2)]
    combt_ref[...] = jnp.zeros(combt_ref.shape, F32)
    for gi in range(ng):
        for e in range(ne):
            combt_ref[gi * ne + e:gi * ne + e + 1, :] = g_hot[gi] * w_exp[e]
    comb_ref[...] = combt_ref[...].T


def out_projection(ys, w_out, x, g1, gain, scale, shift, w_router, b_router, seq, row_tile=512):
    n, d = x.shape
    w_router_t = w_router.T
    tiles_per_seq = seq // row_tile
    row = lambda i: (i, 0)
    per_batch = lambda i: (i // tiles_per_seq, 0, 0)
    const = lambda shape: pl.BlockSpec(shape, lambda i: (0, 0))
    mod = pl.BlockSpec((None, 1, d), per_batch)
    return pl.pallas_call(
        _out_proj_kernel,
        grid=(n // row_tile,),
        in_specs=[pl.BlockSpec((row_tile, GROUP_WIDTH), row)] * 4
        + [const(w_out.shape), pl.BlockSpec((row_tile, d), row), mod, const((1, d)), mod, mod,
           const(w_router_t.shape), const(b_router.shape)],
        out_specs=(pl.BlockSpec((row_tile, d), row), pl.BlockSpec((row_tile, d), row),
                   pl.BlockSpec((row_tile, LANES), row)),
        out_shape=(jax.ShapeDtypeStruct((n, d), F32), jax.ShapeDtypeStruct((n, d), BF16),
                   jax.ShapeDtypeStruct((n, LANES), F32)),
        scratch_shapes=[pltpu.VMEM((LANES, row_tile), F32)],
        compiler_params=_params("parallel"),
        name="out_projection",
    )(*ys, w_out, x, g1, gain, scale, shift, w_router_t, b_router)


def _moe_kernel(h_ref, comb_ref, wg_ref, wu_ref, wd_ref, x_ref, g2_ref, fin_ref, o_ref, acc_ref, *,
                final_norm):
    e = pl.program_id(1)

    @pl.when(e == 0)
    def _():
        acc_ref[...] = jnp.zeros(acc_ref.shape, F32)

    h = h_ref[...]
    comb = comb_ref[...]
    lane = lax.broadcasted_iota(jnp.int32, comb.shape, 1)
    weight = jnp.sum(jnp.where(lane == e, comb, 0.0), axis=-1, keepdims=True)
    hid = _silu(_dot(h, wg_ref[...])) * _dot(h, wu_ref[...]) * weight
    acc_ref[...] += _dot(hid.astype(BF16), wd_ref[...])

    @pl.when(e == N_EXPERTS - 1)
    def _():
        x = x_ref[...] + g2_ref[...] * acc_ref[...]
        if final_norm:
            x = x * lax.rsqrt(jnp.mean(x * x, axis=-1, keepdims=True) + RMS_EPS) * fin_ref[...]
        o_ref[...] = x


def moe_experts(h, comb, w_gate, w_up, w_down, x, g2, fin_gain, seq, final_norm, row_tile=1024):
    n, d = x.shape
    f = w_gate.shape[-1]
    tiles_per_seq = seq // row_tile
    row = lambda i, e: (i, 0)
    return pl.pallas_call(
        functools.partial(_moe_kernel, final_norm=final_norm),
        grid=(n // row_tile, N_EXPERTS),
        in_specs=[pl.BlockSpec((row_tile, d), row), pl.BlockSpec((row_tile, LANES), row),
                  pl.BlockSpec((None, d, f), lambda i, e: (e, 0, 0)),
                  pl.BlockSpec((None, d, f), lambda i, e: (e, 0, 0)),
                  pl.BlockSpec((None, f, d), lambda i, e: (e, 0, 0)),
                  pl.BlockSpec((row_tile, d), row),
                  pl.BlockSpec((None, 1, d), lambda i, e: (i // tiles_per_seq, 0, 0)),
                  pl.BlockSpec((1, d), lambda i, e: (0, 0))],
        out_specs=pl.BlockSpec((row_tile, d), row),
        out_shape=jax.ShapeDtypeStruct((n, d), F32),
        scratch_shapes=[pltpu.VMEM((row_tile, d), F32)],
        compiler_params=_params("parallel", "arbitrary"),
        name="moe_experts",
    )(h, comb, w_gate, w_up, w_down, x, g2, fin_gain)


def kernel(x, c, norm_mix, norm_ffn, final_norm, w_ada, b_ada, w_in, w_out, attn_sink, w_pool, pool_scale, cmp_pos, cmp_w1, cmp_w2, conv_w, a_log, dt_bias, dn_norm, w_route_group, b_route_group, w_route_expert, b_route_expert, w_gate, w_up, w_down):
    batch, seq, d = x.shape
    depth = w_in.shape[0]
    n = batch * seq
    cover, gexp, expand = nsa_constants(seq)
    bexp, bd, tril = gdn_constants()
    mod = ada_modulation(c, w_ada, b_ada)
    spread = lambda t: jnp.repeat(t, HEAD_DIM).reshape(1, GROUP_WIDTH)
    seq3 = lambda t: t.reshape(batch, seq, t.shape[-1])
    xf = x.reshape(n, d)
    for l in range(depth):
        sh1, sc1, g1, sh2, sc2, g2 = (mod[l, :, i * d:(i + 1) * d].reshape(batch, 1, d) for i in range(6))
        p = in_projection(xf, norm_mix[l].reshape(1, d), sc1, sh1, pack_in_weights(w_in[l]), seq)
        y_a = swa_attention(seq3(p["a_q"]), seq3(p["a_k"]), seq3(p["a_v"]), attn_sink[l])
        w_pool_bd = jax.scipy.linalg.block_diag(*[w_pool[l, gi] for gi in range(len(POOL_WINDOWS))])
        y_b = multiscale_pool(seq3(p["b_u"]), w_pool_bd.astype(BF16), pool_scale[l].reshape(1, GROUP_WIDTH))
        kc, vc = nsa_compress(seq3(p["c_cmp"]), *pack_compress_weights(cmp_pos[l], cmp_w1[l], cmp_w2[l]))
        y_c = nsa_attention(seq3(p["c_q"]), seq3(p["c_gate"]), kc, vc, seq3(p["c_ksel"]), seq3(p["c_vsel"]),
                            seq3(p["c_kwin"]), seq3(p["c_vwin"]), cover, gexp, expand)
        y_d = gated_deltanet(seq3(p["d_qkv"]), seq3(p["d_z"]), seq3(p["d_ba"]), conv_w[l], spread(a_log[l]),
                             spread(dt_bias[l]), jnp.tile(dn_norm[l], N_HEADS).reshape(1, GROUP_WIDTH), bexp, bd, tril)
        ys = [t.reshape(n, GROUP_WIDTH) for t in (y_a, y_b, y_c, y_d)]
        n_logits = N_EXPERT_GROUPS + N_EXPERTS
        w_router = jnp.pad(jnp.concatenate([w_route_group[l], w_route_expert[l]], axis=1),
                           ((0, 0), (0, LANES - n_logits)))
        b_router = jnp.pad(jnp.concatenate([b_route_group[l], b_route_expert[l]]),
                           (0, LANES - n_logits)).reshape(LANES, 1)
        xf, h2, comb = out_projection(ys, w_out[l].astype(BF16), xf, g1, norm_ffn[l].reshape(1, d), sc2, sh2,
                                      w_router, b_router, seq)
        f = w_gate.shape[-1]
        xf = moe_experts(h2, comb, w_gate[l].reshape(N_EXPERTS, d, f).astype(BF16),
                         w_up[l].reshape(N_EXPERTS, d, f).astype(BF16),
                         w_down[l].reshape(N_EXPERTS, f, d).astype(BF16),
                         xf, g2, final_norm.reshape(1, d), seq, final_norm=(l == depth - 1))
    return xf.reshape(batch, seq, d)
```

```python
import functools

import numpy as np
import jax
import jax.numpy as jnp
from jax import lax
from jax.experimental import pallas as pl
from jax.experimental.pallas import tpu as pltpu

F32 = jnp.float32
BF16 = jnp.bfloat16
HI = lax.Precision.HIGHEST

HEAD_DIM = 64
N_HEADS = 4
GROUP_WIDTH = N_HEADS * HEAD_DIM
BLOCK = 128
RMS_EPS = 1e-6
SWA_WINDOW = 128
SWA_BLOCKS = 4
POOL_WINDOWS = (2, 4, 8, 16)
POOL_HALO = 16
CMP_LEN = 32
CMP_STRIDE = 16
SEL_BLOCK = 64
NSA_TOP_N = 16
NSA_WINDOW = 512
NSA_FORCE = 1e4
SEL_CHUNK = 1024
DN_CONV = 4
DN_CHUNK = 64
N_EXPERT_GROUPS = 4
EXPERTS_PER_GROUP = 4
N_EXPERTS = N_EXPERT_GROUPS * EXPERTS_PER_GROUP
LANES = 128
MASKED = -1e30
QK_SCALE = HEAD_DIM ** -0.5 * float(np.log2(np.e))
VMEM_LIMIT = 56 * 1024 * 1024

SEG_WIDTHS = (
    ("a_q", 256), ("a_k", 256), ("a_v", 256), ("c_q", 256),
    ("c_ksel", 128), ("c_vsel", 128), ("c_kwin", 128), ("c_vwin", 128),
    ("b_u", 256), ("c_cmp", 128), ("c_gate", 128), ("d_qkv", 768), ("d_z", 256), ("d_ba", 128),
)
SEG_BF16 = ("a_q", "a_k", "a_v", "c_q", "c_ksel", "c_vsel", "c_kwin", "c_vwin")
SEG_OFFSETS = {}
_off = 0
for _name, _w in SEG_WIDTHS:
    SEG_OFFSETS[_name] = (_off, _w)
    _off += _w
PACKED_WIDTH = _off


def _sigmoid(x):
    return 1.0 / (1.0 + jnp.exp(-x))


def _silu(x):
    return x * _sigmoid(x)


def _dot(a, b, precision=None):
    return jnp.dot(a, b, precision=precision, preferred_element_type=F32)


def _dot_nt(a, b, precision=None):
    return lax.dot_general(a, b, (((1,), (1,)), ((), ())), precision=precision,
                           preferred_element_type=F32)


def _split(a):
    hi = a.astype(BF16)
    return hi, (a - hi.astype(F32)).astype(BF16)


def _params(*semantics):
    return pltpu.CompilerParams(dimension_semantics=semantics, vmem_limit_bytes=VMEM_LIMIT)


def _ada_kernel(c_ref, w_ref, b_ref, o_ref):
    cond = _silu(c_ref[...])
    o_ref[...] = _dot(cond, w_ref[...], HI) + b_ref[...]


def ada_modulation(c, w_ada, b_ada, col_tile=1536):
    depth, d, width = w_ada.shape
    b = c.shape[0]
    return pl.pallas_call(
        _ada_kernel,
        grid=(depth, width // col_tile),
        in_specs=[
            pl.BlockSpec((b, d), lambda l, j: (0, 0)),
            pl.BlockSpec((None, d, col_tile), lambda l, j: (l, 0, j)),
            pl.BlockSpec((None, 1, col_tile), lambda l, j: (l, 0, j)),
        ],
        out_specs=pl.BlockSpec((None, b, col_tile), lambda l, j: (l, 0, j)),
        out_shape=jax.ShapeDtypeStruct((depth, b, width), F32),
        compiler_params=_params("parallel", "parallel"),
        name="ada_modulation",
    )(c, w_ada, b_ada.reshape(depth, 1, width))


def _modulated_norm(x, gain, scale, shift):
    y = x * lax.rsqrt(jnp.mean(x * x, axis=-1, keepdims=True) + RMS_EPS)
    return y * gain * (1.0 + scale) + shift


def _in_proj_kernel(x_ref, gain_ref, sc_ref, sh_ref, w_ref, *out_refs):
    h = _modulated_norm(x_ref[...], gain_ref[...], sc_ref[...], sh_ref[...]).astype(BF16)
    for (name, _), o_ref in zip(SEG_WIDTHS, out_refs):
        off, width = SEG_OFFSETS[name]
        o_ref[...] = _dot(h, w_ref[:, off:off + width]).astype(o_ref.dtype)


def in_projection(x, gain, scale, shift, w_packed, seq, row_tile=512):
    n, d = x.shape
    tiles_per_seq = seq // row_tile
    row = lambda i: (i, 0)
    per_batch = lambda i: (i // tiles_per_seq, 0, 0)
    out_shape = tuple(
        jax.ShapeDtypeStruct((n, w), BF16 if name in SEG_BF16 else F32) for name, w in SEG_WIDTHS)
    outs = pl.pallas_call(
        _in_proj_kernel,
        grid=(n // row_tile,),
        in_specs=[
            pl.BlockSpec((row_tile, d), row),
            pl.BlockSpec((1, d), lambda i: (0, 0)),
            pl.BlockSpec((None, 1, d), per_batch),
            pl.BlockSpec((None, 1, d), per_batch),
            pl.BlockSpec((d, PACKED_WIDTH), lambda i: (0, 0)),
        ],
        out_specs=tuple(pl.BlockSpec((row_tile, w), row) for _, w in SEG_WIDTHS),
        out_shape=out_shape,
        compiler_params=_params("parallel"),
        name="in_projection",
    )(x, gain, scale, shift, w_packed)
    return dict(zip((name for name, _ in SEG_WIDTHS), outs))


def pack_in_weights(w_in):
    gw, hd = GROUP_WIDTH, HEAD_DIM
    sizes = (gw, 2 * hd, 2 * hd, gw, gw, 6 * hd, 3 * N_HEADS, 3 * gw, gw, N_HEADS, N_HEADS)
    offs = np.concatenate([[0], np.cumsum(sizes)])
    (a_q, a_k, a_v, b_u, c_q, c_kv, c_gate, d_qkv, d_z, d_beta, d_a) = (
        w_in[:, offs[i]:offs[i + 1]] for i in range(len(sizes)))
    d = w_in.shape[0]
    dup = lambda t: jnp.concatenate([t, t], axis=1)
    pad = lambda t: jnp.pad(t, ((0, 0), (0, LANES - t.shape[1])))
    k_cmp, v_cmp, k_sel, v_sel, k_win, v_win = (c_kv[:, i * hd:(i + 1) * hd] for i in range(6))
    a_q = a_q * QK_SCALE
    c_q = c_q * QK_SCALE
    segs = {
        "a_q": a_q,
        "a_k": jnp.concatenate([dup(a_k[:, :hd]), dup(a_k[:, hd:])], axis=1),
        "a_v": jnp.concatenate([dup(a_v[:, :hd]), dup(a_v[:, hd:])], axis=1),
        "c_q": c_q,
        "c_ksel": dup(k_sel), "c_vsel": dup(v_sel), "c_kwin": dup(k_win), "c_vwin": dup(v_win),
        "b_u": b_u,
        "c_cmp": jnp.concatenate([k_cmp, v_cmp], axis=1),
        "c_gate": pad(c_gate),
        "d_qkv": d_qkv, "d_z": d_z,
        "d_ba": pad(jnp.concatenate([d_beta, d_a], axis=1)),
    }
    packed = jnp.concatenate([segs[name] for name, _ in SEG_WIDTHS], axis=1)
    assert packed.shape == (d, PACKED_WIDTH)
    return packed.astype(BF16)


def _stack_heads(slab):
    lane = lax.broadcasted_iota(jnp.int32, slab.shape, 1)
    zero = jnp.zeros_like(slab)
    return jnp.concatenate([jnp.where(lane < HEAD_DIM, slab, zero),
                            jnp.where(lane >= HEAD_DIM, slab, zero)], axis=0)


def _unstack_heads(o, rows):
    lane = lax.broadcasted_iota(jnp.int32, (rows, LANES), 1)
    return jnp.where(lane < HEAD_DIM, o[0:rows], o[rows:2 * rows])


def _swa_kernel(sink_ref, q_ref, kp_ref, kc_ref, vp_ref, vc_ref, o_ref):
    i = pl.program_id(1)
    row = lax.broadcasted_iota(jnp.int32, (2 * BLOCK, 2 * BLOCK), 0)
    col = lax.broadcasted_iota(jnp.int32, (2 * BLOCK, 2 * BLOCK), 1)
    tq = row & (BLOCK - 1)
    tk = col - BLOCK
    banded = (tk <= tq) & (tk > tq - SWA_WINDOW)
    first_valid = banded & ((col >= BLOCK) | (i > 0))
    rowh = lax.broadcasted_iota(jnp.int32, (2 * BLOCK, 1), 0)
    log2e = float(np.log2(np.e))
    sinks = [jnp.where(rowh < BLOCK, sink_ref[2 * j], sink_ref[2 * j + 1]) * log2e for j in range(N_HEADS // 2)]
    pairs = [(sb, j) for sb in range(SWA_BLOCKS) for j in range(N_HEADS // 2)]

    def keys(cur_ref, prev_ref, sb, lanes):
        if sb == 0:
            return jnp.concatenate([prev_ref[:, lanes], cur_ref[0:BLOCK, lanes]], axis=0)
        return cur_ref[(sb - 1) * BLOCK:(sb + 1) * BLOCK, lanes]

    lanes_of = lambda j: slice(j * LANES, (j + 1) * LANES)
    s = [jnp.where(first_valid if sb == 0 else banded,
                   _dot_nt(_stack_heads(q_ref[sb * BLOCK:(sb + 1) * BLOCK, lanes_of(j)]),
                           keys(kc_ref, kp_ref, sb, lanes_of(j))), -jnp.inf) for sb, j in pairs]
    m = [jnp.maximum(jnp.max(t, axis=-1, keepdims=True), sinks[j]) for t, (sb, j) in zip(s, pairs)]
    p = [jnp.exp2(t - mm) for t, mm in zip(s, m)]
    denom = [jnp.sum(t, axis=-1, keepdims=True) + jnp.exp2(sinks[j] - mm) for t, mm, (sb, j) in zip(p, m, pairs)]
    o = [_dot(t.astype(BF16), keys(vc_ref, vp_ref, sb, lanes_of(j))) / d for t, d, (sb, j) in zip(p, denom, pairs)]
    for sb in range(SWA_BLOCKS):
        slabs = [_unstack_heads(o[sb * (N_HEADS // 2) + j], BLOCK) for j in range(N_HEADS // 2)]
        o_ref[sb * BLOCK:(sb + 1) * BLOCK, :] = jnp.concatenate(slabs, axis=1).astype(o_ref.dtype)


def swa_attention(q, k, v, sink):
    b, s, w = q.shape
    rows = SWA_BLOCKS * BLOCK
    cur = pl.BlockSpec((None, rows, w), lambda bi, i: (bi, i, 0))
    prev = pl.BlockSpec((None, BLOCK, w), lambda bi, i: (bi, jnp.maximum(i * SWA_BLOCKS - 1, 0), 0))
    return pl.pallas_call(
        _swa_kernel,
        grid=(b, s // rows),
        in_specs=[pl.BlockSpec(memory_space=pltpu.SMEM), cur, prev, cur, prev, cur],
        out_specs=cur,
        out_shape=jax.ShapeDtypeStruct((b, s, w), BF16),
        compiler_params=_params("parallel", "parallel"),
        name="swa_attention",
    )(sink, q, k, k, v, v)


def _pool_kernel(up_ref, u_ref, w_ref, scale_ref, o_ref, ext_ref):
    i = pl.program_id(1)
    rows = u_ref.shape[0]
    u = u_ref[...]
    halo = up_ref[...]
    ext_ref[0:POOL_HALO, :] = jnp.where(i > 0, halo, jnp.zeros_like(halo))
    ext_ref[POOL_HALO:POOL_HALO + rows, :] = u
    lane = lax.broadcasted_iota(jnp.int32, u.shape, 1)
    pos = i * rows + lax.broadcasted_iota(jnp.int32, u.shape, 0)
    group_ch = GROUP_WIDTH // len(POOL_WINDOWS)
    total = u
    d = jnp.zeros_like(u)
    width = 1
    for gi, w in enumerate(POOL_WINDOWS):
        while width < w:
            total = total + ext_ref[pl.ds(POOL_HALO - width, rows), :]
            width += 1
        cnt = jnp.minimum(pos + 1, w).astype(F32)
        in_group = (lane >= gi * group_ch) & (lane < (gi + 1) * group_ch)
        d = jnp.where(in_group, total / cnt - u, d)
    o_ref[...] = (_dot(d.astype(BF16), w_ref[...]) * scale_ref[...]).astype(o_ref.dtype)


def multiscale_pool(u, w_blockdiag, pool_scale, row_tile=512):
    b, s, w = u.shape
    halo_per_tile = row_tile // POOL_HALO
    return pl.pallas_call(
        _pool_kernel,
        grid=(b, s // row_tile),
        in_specs=[
            pl.BlockSpec((None, POOL_HALO, w), lambda bi, i: (bi, jnp.maximum(i * halo_per_tile - 1, 0), 0)),
            pl.BlockSpec((None, row_tile, w), lambda bi, i: (bi, i, 0)),
            pl.BlockSpec((w, w), lambda bi, i: (0, 0)),
            pl.BlockSpec((1, w), lambda bi, i: (0, 0)),
        ],
        out_specs=pl.BlockSpec((None, row_tile, w), lambda bi, i: (bi, i, 0)),
        out_shape=jax.ShapeDtypeStruct((b, s, w), BF16),
        scratch_shapes=[pltpu.VMEM((POOL_HALO + row_tile, w), F32)],
        compiler_params=_params("parallel", "parallel"),
        name="multiscale_pool",
    )(u, u, w_blockdiag, pool_scale)


def _compress_kernel(x_ref, w1_ref, pos_ref, w2k_ref, w2v_ref, kc_ref, vc_ref):
    n_chunks = x_ref.shape[0]
    both = _dot(x_ref[...], w1_ref[...], HI)
    pre = both[:, 0:LANES] + pltpu.roll(both[:, LANES:2 * LANES], n_chunks - 1, 0) + pos_ref[...]
    hid = _silu(pre)
    kc_ref[...] = _dot(hid, w2k_ref[...], HI).astype(kc_ref.dtype)
    vc_ref[...] = _dot(hid, w2v_ref[...], HI).astype(vc_ref.dtype)


def nsa_compress(cmp_in, w1_packed, pos_term, w2k, w2v):
    b, s, w = cmp_in.shape
    n_chunks = s // CMP_STRIDE
    flat = cmp_in.reshape(b, n_chunks, CMP_STRIDE * w)
    const = lambda shape: pl.BlockSpec(shape, lambda bi: tuple(0 for _ in shape))
    out = jax.ShapeDtypeStruct((b, n_chunks, LANES), BF16)
    return pl.pallas_call(
        _compress_kernel,
        grid=(b,),
        in_specs=[pl.BlockSpec((None, n_chunks, CMP_STRIDE * w), lambda bi: (bi, 0, 0)),
                  const(w1_packed.shape), const(pos_term.shape), const(w2k.shape), const(w2v.shape)],
        out_specs=(pl.BlockSpec((None, n_chunks, LANES), lambda bi: (bi, 0, 0)),) * 2,
        out_shape=(out, out),
        compiler_params=_params("parallel"),
        name="nsa_compress",
    )(flat, w1_packed, pos_term, w2k, w2v)


def pack_compress_weights(cmp_pos, cmp_w1, cmp_w2):
    hd, half = HEAD_DIM, CMP_LEN // 2
    w1 = cmp_w1.reshape(2, 2, half, hd, hd)
    zeros = jnp.zeros((half, hd, hd), F32)
    halves = []
    for part in range(2):
        wk = jnp.concatenate([w1[0, part], zeros], axis=-1)
        wv = jnp.concatenate([zeros, w1[1, part]], axis=-1)
        halves.append(jnp.concatenate([wk, wv], axis=1).reshape(half * 2 * hd, 2 * hd))
    w1_packed = jnp.concatenate(halves, axis=1)
    pos_flat = cmp_pos.reshape(2, 1, CMP_LEN * hd)
    pos_term = jnp.concatenate([jnp.matmul(pos_flat[0], cmp_w1[0], precision=HI),
                                jnp.matmul(pos_flat[1], cmp_w1[1], precision=HI)], axis=1)
    zero2 = jnp.zeros((hd, 2 * hd), F32)
    w2k = jnp.concatenate([jnp.concatenate([cmp_w2[0], cmp_w2[0]], axis=1), zero2], axis=0)
    w2v = jnp.concatenate([zero2, jnp.concatenate([cmp_w2[1], cmp_w2[1]], axis=1)], axis=0)
    return w1_packed, pos_term, w2k, w2v


def _nsa_kernel(q_ref, gate_ref, kc_ref, vc_ref, ksel_ref, vsel_ref, kwin_ref, vwin_ref,
                cover_ref, gexp_ref, expand_ref, o_ref, q4_ref, m_ref, acc_ref, sa_ref, sb_ref, ma_ref, mb_ref):
    i = pl.program_id(1)
    rows = N_HEADS * BLOCK
    q = q_ref[...]
    q4 = jnp.concatenate([_stack_heads(q[:, 0:LANES]), _stack_heads(q[:, LANES:2 * LANES])], axis=0)
    tq = i * BLOCK + (lax.broadcasted_iota(jnp.int32, (rows, 1), 0) & (BLOCK - 1))

    n_cmp = kc_ref.shape[0]
    n_idx = lax.broadcasted_iota(jnp.int32, (1, n_cmp), 1)
    valid_c = (n_idx * CMP_STRIDE + (CMP_LEN - 1) <= tq) & (n_idx < n_cmp - 1)
    s_c = jnp.where(valid_c, _dot_nt(q4, kc_ref[...]), -jnp.inf)
    m_c = jnp.max(s_c, axis=-1, keepdims=True)
    m_c = jnp.where(m_c == -jnp.inf, 0.0, m_c)
    p_c = jnp.exp2(s_c - m_c)
    d_c = jnp.sum(p_c, axis=-1, keepdims=True)
    p_c = p_c / jnp.where(d_c > 0, d_c, 1.0)
    o_c = _dot(p_c.astype(BF16), vc_ref[...])

    head_rows = [slice(h * BLOCK, (h + 1) * BLOCK) for h in range(N_HEADS)]
    value_lane = lax.broadcasted_iota(jnp.int32, (1, LANES), 1) < HEAD_DIM

    def with_ones(v):
        return jnp.where(value_lane, v, jnp.ones_like(v))

    span = NSA_WINDOW + BLOCK
    start_w = pl.multiple_of(jnp.maximum(i * BLOCK - NSA_WINDOW, 0), BLOCK)
    tk = start_w + lax.broadcasted_iota(jnp.int32, (1, span), 1)
    tq_blk = i * BLOCK + lax.broadcasted_iota(jnp.int32, (BLOCK, 1), 0)
    valid_w = (tk <= tq_blk) & (tk > tq_blk - NSA_WINDOW)
    k_w = kwin_ref[pl.ds(start_w, span), :]
    v_w = with_ones(vwin_ref[pl.ds(start_w, span), :])
    acc_w = []

    def window_head(h):
        s_w = jnp.where(valid_w, _dot_nt(q4[head_rows[h]], k_w), -jnp.inf)
        p_w = jnp.exp2(s_w - jnp.max(s_w, axis=-1, keepdims=True))
        acc_w.append(_dot(p_w.astype(BF16), v_w))

    p_heads = p_c[0:BLOCK] + p_c[BLOCK:2 * BLOCK] + p_c[2 * BLOCK:3 * BLOCK] + p_c[3 * BLOCK:4 * BLOCK]
    p_hi, p_lo = _split(p_heads)
    cover = cover_ref[...]
    importance = _dot_nt(cover, p_hi) + _dot_nt(cover, p_lo)
    n_sel = cover_ref.shape[0]
    blk = lax.broadcasted_iota(jnp.int32, (n_sel, BLOCK), 0)
    t_lane = i * BLOCK + lax.broadcasted_iota(jnp.int32, (n_sel, BLOCK), 1)
    cur = t_lane // SEL_BLOCK
    causal = blk * SEL_BLOCK <= t_lane
    forced = (blk == 0) | (blk == cur) | (blk == cur - 1)
    score = jnp.where(causal, jnp.where(forced, NSA_FORCE, importance), -jnp.inf)
    blk_f = blk.astype(F32)
    chosen = jnp.zeros((n_sel, BLOCK), F32)
    for r in range(NSA_TOP_N):
        if r % (NSA_TOP_N // N_HEADS) == 0:
            window_head(r // (NSA_TOP_N // N_HEADS))
        top = jnp.max(score, axis=0, keepdims=True)
        first = jnp.min(jnp.where(score == top, blk_f, float(n_sel)), axis=0, keepdims=True)
        pick = blk_f == first
        score = jnp.where(pick, -jnp.inf, score)
        chosen = jnp.where(pick, 1.0, chosen)
    chosen = jnp.where(causal, chosen, 0.0)
    chosen_q = chosen.T.astype(BF16)

    q4_ref[...] = q4
    m_ref[...] = jnp.full(m_ref.shape, MASKED, F32)
    acc_ref[...] = jnp.zeros(acc_ref.shape, F32)
    n_chunks = (i * BLOCK + BLOCK - 1) // SEL_CHUNK + 1

    def chunk_start(c):
        return pl.multiple_of(jnp.minimum(c, n_chunks - 1) * SEL_CHUNK, SEL_CHUNK)

    def chunk_keep(c):
        start = chunk_start(c)
        key = start + lax.broadcasted_iota(jnp.int32, (1, SEL_CHUNK), 1)
        on_keys = _dot(chosen_q, expand_ref[:, pl.ds(start, SEL_CHUNK)])
        return jnp.where((key <= tq_blk) & (c < n_chunks), on_keys, 0.0) > 0.5

    def stage_scores(h, bufs, c, keep):
        s = jnp.where(keep, _dot_nt(q4_ref[head_rows[h]], ksel_ref[pl.ds(chunk_start(c), SEL_CHUNK), :]), MASKED)
        bufs[0][head_rows[h]] = s
        bufs[1][head_rows[h]] = jnp.max(s, axis=-1, keepdims=True)

    def consume_scores(h, bufs, v_aug):
        m_old = m_ref[head_rows[h]]
        m_new = jnp.maximum(m_old, bufs[1][head_rows[h]])
        m_ref[head_rows[h]] = m_new
        p = jnp.exp2(bufs[0][head_rows[h]] - m_new).astype(BF16)
        acc_ref[head_rows[h]] = jnp.exp2(m_old - m_new) * acc_ref[head_rows[h]] + _dot(p, v_aug)

    def half_step(c, cur_bufs, next_bufs):
        keep_next = chunk_keep(c + 1)
        v_aug = with_ones(vsel_ref[pl.ds(chunk_start(c), SEL_CHUNK), :])
        for h in range(N_HEADS):
            stage_scores(h, next_bufs, c + 1, keep_next)
            consume_scores(h, cur_bufs, v_aug)

    bufs_a = (sa_ref, ma_ref)
    bufs_b = (sb_ref, mb_ref)
    keep0 = chunk_keep(0)
    for h in range(N_HEADS):
        stage_scores(h, bufs_a, 0, keep0)

    def sel_step(t, carry):
        half_step(2 * t, bufs_a, bufs_b)

        @pl.when(2 * t + 1 < n_chunks)
        def _():
            half_step(2 * t + 1, bufs_b, bufs_a)

        return carry

    lax.fori_loop(0, (n_chunks + 1) // 2, sel_step, 0)

    def heads_to_lanes(o):
        return jnp.concatenate([_unstack_heads(o[0:2 * BLOCK], BLOCK),
                                _unstack_heads(o[2 * BLOCK:4 * BLOCK], BLOCK)], axis=1)

    def normalized_heads_to_lanes(acc):
        slabs = []
        for j in range(N_HEADS // 2):
            even = acc[2 * j * BLOCK:(2 * j + 1) * BLOCK]
            odd = acc[(2 * j + 1) * BLOCK:(2 * j + 2) * BLOCK]
            numer = jnp.where(value_lane, even, pltpu.roll(odd, HEAD_DIM, 1))
            denom = jnp.where(value_lane, pltpu.roll(even, HEAD_DIM, 1), odd)
            slabs.append(numer / denom)
        return jnp.concatenate(slabs, axis=1)

    gates = _dot(_sigmoid(gate_ref[...]), gexp_ref[...], HI)
    gw = GROUP_WIDTH
    out = (gates[:, 0:gw] * heads_to_lanes(o_c) + gates[:, gw:2 * gw] * normalized_heads_to_lanes(acc_ref[...])
           + gates[:, 2 * gw:3 * gw] * normalized_heads_to_lanes(jnp.concatenate(acc_w, axis=0)))
    o_ref[...] = out.astype(o_ref.dtype)


def nsa_constants(seq):
    n_cmp_rows = seq // CMP_STRIDE
    n_sel = seq // SEL_BLOCK
    cmp_start = np.arange(n_cmp_rows) * CMP_STRIDE
    sel_start = np.arange(n_sel) * SEL_BLOCK
    cover = np.maximum(np.minimum(cmp_start[None, :] + CMP_LEN, sel_start[:, None] + SEL_BLOCK)
                       - np.maximum(cmp_start[None, :], sel_start[:, None]), 0).astype(np.float32) / CMP_LEN
    gexp = np.zeros((LANES, 3 * GROUP_WIDTH), np.float32)
    for h in range(N_HEADS):
        for br in range(3):
            gexp[h * 3 + br, br * GROUP_WIDTH + h * HEAD_DIM: br * GROUP_WIDTH + (h + 1) * HEAD_DIM] = 1.0
    expand = (np.arange(n_sel)[:, None] == np.arange(seq)[None, :] // SEL_BLOCK).astype(np.float32)
    return cover, gexp, expand


def nsa_attention(q, gate, kc, vc, ksel, vsel, kwin, vwin, cover, gexp, expand):
    b, s, w = q.shape
    assert s >= NSA_WINDOW + BLOCK and s % SEL_CHUNK == 0
    blk = lambda width: pl.BlockSpec((None, BLOCK, width), lambda bi, i: (bi, i, 0))
    per_batch = lambda rows: pl.BlockSpec((None, rows, LANES), lambda bi, i: (bi, 0, 0))
    const = lambda shape: pl.BlockSpec(shape, lambda bi, i: (0, 0))
    rows = N_HEADS * BLOCK
    return pl.pallas_call(
        _nsa_kernel,
        grid=(b, s // BLOCK),
        in_specs=[blk(w), blk(LANES), per_batch(kc.shape[1]), per_batch(vc.shape[1]),
                  per_batch(s), per_batch(s), per_batch(s), per_batch(s),
                  const(cover.shape), const(gexp.shape), const(expand.shape)],
        out_specs=blk(w),
        out_shape=jax.ShapeDtypeStruct((b, s, w), BF16),
        scratch_shapes=[pltpu.VMEM((rows, LANES), BF16), pltpu.VMEM((rows, 1), F32),
                        pltpu.VMEM((rows, LANES), F32), pltpu.VMEM((rows, SEL_CHUNK), F32),
                        pltpu.VMEM((rows, SEL_CHUNK), F32), pltpu.VMEM((rows, 1), F32),
                        pltpu.VMEM((rows, 1), F32)],
        compiler_params=_params("parallel", "arbitrary"),
        name="nsa_attention",
    )(q, gate, kc, vc, ksel, vsel, kwin, vwin, jnp.asarray(cover, BF16), gexp, jnp.asarray(expand, BF16))


CONV_TAIL = 8


GDN_ROWS = 256


def _dot_split_lhs(a, b):
    hi, lo = _split(a)
    return _dot(hi, b) + _dot(lo, b)


def _gdn_kernel(qkv_ref, z_ref, ba_ref, convw_ref, alog_ref, dtb_ref, nw_ref, bexp_ref, bd_ref, tril_ref,
                o_ref, ext_ref, state_ref):
    step = pl.program_id(1)
    cl, gw = DN_CHUNK, GROUP_WIDTH
    rows = qkv_ref.shape[0]

    @pl.when(step == 0)
    def _():
        ext_ref[0:CONV_TAIL, :] = jnp.zeros((CONV_TAIL, 3 * gw), F32)
        state_ref[...] = jnp.zeros(state_ref.shape, F32)

    ext_ref[CONV_TAIL:CONV_TAIL + rows, :] = qkv_ref[...]
    cw = convw_ref[...]
    acc = ext_ref[CONV_TAIL:CONV_TAIL + rows, :] * cw[DN_CONV - 1:DN_CONV, :]
    for j in range(DN_CONV - 1):
        acc = acc + ext_ref[pl.ds(CONV_TAIL - (DN_CONV - 1) + j, rows), :] * cw[j:j + 1, :]
    ext_ref[0:CONV_TAIL, :] = ext_ref[rows:rows + CONV_TAIL, :]
    act = _silu(acc)

    bd = bd_ref[...]
    per_head_sum = lambda t: _dot_split_lhs(t, bd)
    q = act[:, 0:gw]
    k = act[:, gw:2 * gw]
    v = act[:, 2 * gw:3 * gw]
    q = q * lax.rsqrt(per_head_sum(q * q) + 1e-6) * (HEAD_DIM ** -0.5)
    k = k * lax.rsqrt(per_head_sum(k * k) + 1e-6)

    ba = _dot_split_lhs(ba_ref[...], bexp_ref[...])
    beta = _sigmoid(ba[:, 0:gw])
    a_in = ba[:, gw:2 * gw] + dtb_ref[...]
    softplus = jnp.maximum(a_in, 0.0) + jnp.log(1.0 + jnp.exp(-jnp.abs(a_in)))
    g = -jnp.exp(alog_ref[...]) * softplus
    tril = tril_ref[...]
    g_hi = g.astype(BF16)
    g_mid, g_lo = _split(g - g_hi.astype(F32))
    gc_all = _dot(tril, g_hi) + (_dot(tril, g_mid) + _dot(tril, g_lo))

    ri = lax.broadcasted_iota(jnp.int32, (cl, cl), 0)
    ci = lax.broadcasted_iota(jnp.int32, (cl, cl), 1)
    causal = ci <= ri
    strict = ci < ri
    eye = jnp.where(ci == ri, 1.0, 0.0)
    lane = lax.broadcasted_iota(jnp.int32, (1, gw), 1)
    head_lanes = [(lane >= h * HEAD_DIM) & (lane < (h + 1) * HEAD_DIM) for h in range(N_HEADS)]

    n_chunks = rows // cl
    pairs = [(c, h) for c in range(n_chunks) for h in range(N_HEADS)]
    chunk = lambda t, c: t[c * cl:(c + 1) * cl]
    gcs = [chunk(gc_all, c) for c in range(n_chunks)]
    gc_ts = [gc.T for gc in gcs]
    g_lasts = [gc[cl - 1:cl, :] for gc in gcs]
    egs = [jnp.exp(gc) for gc in gcs]
    ks = [chunk(k, c) for c in range(n_chunks)]
    k_bfs = [t.astype(BF16) for t in ks]
    k_betas = [chunk(k, c) * chunk(beta, c) for c in range(n_chunks)]
    v_betas = [(chunk(v, c) * chunk(beta, c)).astype(BF16) for c in range(n_chunks)]
    kbgs = [(k_betas[c] * egs[c]).astype(BF16) for c in range(n_chunks)]
    q_decs = [(chunk(q, c) * egs[c]).astype(BF16) for c in range(n_chunks)]
    k_decs = [(ks[c] * jnp.exp(g_lasts[c] - gcs[c])).astype(BF16) for c in range(n_chunks)]
    head = lambda h: slice(h * HEAD_DIM, (h + 1) * HEAD_DIM)
    decays = [jnp.exp(jnp.where(causal, gcs[c][:, head(h)] - gc_ts[c][head(h), :], -jnp.inf)) for c, h in pairs]
    per_head = lambda t: [jnp.where(head_lanes[h], t, 0.0) for h in range(N_HEADS)]
    kq = [_dot_nt(jnp.concatenate(per_head(k_betas[c]) + per_head(chunk(q, c)), axis=0).astype(BF16), k_bfs[c])
          for c in range(n_chunks)]
    kks = [kq[c][h * cl:(h + 1) * cl] for c, h in pairs]
    qks = [kq[c][(N_HEADS + h) * cl:(N_HEADS + h + 1) * cl] for c, h in pairs]
    intras = [jnp.where(causal, qk * d, 0.0) for qk, d in zip(qks, decays)]
    powers = [jnp.where(strict, -(kk * d), 0.0) for kk, d in zip(kks, decays)]
    t_invs = [eye + p for p in powers]
    for _ in range(5):
        p_bfs = [p.astype(BF16) for p in powers]
        powers = [_dot(p, p) for p in p_bfs]
        t_invs = [t + _dot(t.astype(BF16), p.astype(BF16)) for t, p in zip(t_invs, powers)]
    tv = [_dot(jnp.concatenate(t_invs[c * N_HEADS:(c + 1) * N_HEADS], axis=0).astype(BF16),
               jnp.concatenate([v_betas[c], kbgs[c]], axis=1)) for c in range(n_chunks)]

    def merge_heads(stacked, lanes):
        out = jnp.zeros((cl, gw), F32)
        for h in range(N_HEADS):
            out = jnp.where(head_lanes[h], stacked[h * cl:(h + 1) * cl, lanes], out)
        return out

    us = [merge_heads(tv[c], slice(0, gw)) for c in range(n_chunks)]
    ws = [merge_heads(tv[c], slice(gw, 2 * gw)).astype(BF16) for c in range(n_chunks)]
    bd_f = bd.astype(F32)
    tn = (((0,), (0,)), ((), ()))
    kws = [(bd_f * lax.dot_general(k_decs[c], ws[c], tn, preferred_element_type=F32)).astype(BF16)
           for c in range(n_chunks)]
    kus = [bd_f * lax.dot_general(k_decs[c], us[c].astype(BF16), tn, preferred_element_type=F32)
           for c in range(n_chunks)]
    intra_cat = [jnp.concatenate([intras[c * N_HEADS + h] for h in range(N_HEADS)], axis=1).astype(BF16)
                 for c in range(n_chunks)]

    state = state_ref[...]
    outs = []
    for c in range(n_chunks):
        on_state = _dot(jnp.concatenate([ws[c], q_decs[c], kws[c]], axis=0), state.astype(BF16))
        v_new = us[c] - on_state[0:cl]
        v_stack = jnp.concatenate(per_head(v_new), axis=0).astype(BF16)
        outs.append(on_state[cl:2 * cl] + _dot(intra_cat[c], v_stack))
        state = state * jnp.exp(g_lasts[c]) - on_state[2 * cl:] + kus[c]
    state_ref[...] = state

    o = jnp.concatenate(outs, axis=0)
    o = o * lax.rsqrt(per_head_sum(o * o) * (1.0 / HEAD_DIM) + RMS_EPS) * nw_ref[...]
    o_ref[...] = (o * _silu(z_ref[...])).astype(o_ref.dtype)


def gdn_constants():
    lane_head = np.arange(GROUP_WIDTH) // HEAD_DIM
    bd = (lane_head[:, None] == lane_head[None, :]).astype(np.float32)
    bexp = np.zeros((LANES, 2 * GROUP_WIDTH), np.float32)
    for h in range(N_HEADS):
        bexp[h, h * HEAD_DIM:(h + 1) * HEAD_DIM] = 1.0
        bexp[N_HEADS + h, GROUP_WIDTH + h * HEAD_DIM:GROUP_WIDTH + (h + 1) * HEAD_DIM] = 1.0
    r = np.arange(GDN_ROWS)
    tril = ((r[:, None] // DN_CHUNK == r[None, :] // DN_CHUNK) & (r[None, :] <= r[:, None])).astype(np.float32)
    return bexp, bd, tril


def gated_deltanet(qkv, z, ba, conv_w, a_log_rep, dt_bias_rep, norm_w_rep, bexp, bd, tril):
    b, s, w3 = qkv.shape
    gw = GROUP_WIDTH
    blk = lambda width: pl.BlockSpec((None, GDN_ROWS, width), lambda bi, c: (bi, c, 0))
    const = lambda shape: pl.BlockSpec(shape, lambda bi, c: (0, 0))
    as_bf16 = lambda t: jnp.asarray(t, BF16)
    return pl.pallas_call(
        _gdn_kernel,
        grid=(b, s // GDN_ROWS),
        in_specs=[blk(w3), blk(gw), blk(LANES), const(conv_w.shape), const((1, gw)), const((1, gw)),
                  const((1, gw)), const(bexp.shape), const(bd.shape), const(tril.shape)],
        out_specs=blk(gw),
        out_shape=jax.ShapeDtypeStruct((b, s, gw), BF16),
        scratch_shapes=[pltpu.VMEM((CONV_TAIL + GDN_ROWS, w3), F32), pltpu.VMEM((gw, gw), F32)],
        compiler_params=_params("parallel", "arbitrary"),
        name="gated_deltanet",
    )(qkv, z, ba, conv_w, a_log_rep, dt_bias_rep, norm_w_rep, as_bf16(bexp), as_bf16(bd), as_bf16(tril))


def _first_max(values):
    best = values[0]
    for v in values[1:]:
        best = jnp.maximum(best, v)
    taken = jnp.zeros_like(best)
    hot = []
    for v in values:
        h = jnp.where((v == best) & (taken < 0.5), 1.0, 0.0)
        taken = taken + h
        hot.append(h)
    return best, hot


def _softmax_rows(rows):
    m = rows[0]
    for r in rows[1:]:
        m = jnp.maximum(m, r)
    e = [jnp.exp(r - m) for r in rows]
    z = e[0]
    for t in e[1:]:
        z = z + t
    return [t / z for t in e]


def _out_proj_kernel(ya_ref, yb_ref, yc_ref, yd_ref, wo_ref, x_ref, g1_ref, gain_ref, sc_ref, sh_ref,
                     wr_ref, br_ref, xo_ref, h_ref, combt_ref):
    gw = GROUP_WIDTH
    y = _dot(ya_ref[...], wo_ref[0:gw, :])
    y = y + _dot(yb_ref[...], wo_ref[gw:2 * gw, :])
    y = y + _dot(yc_ref[...], wo_ref[2 * gw:3 * gw, :])
    y = y + _dot(yd_ref[...], wo_ref[3 * gw:4 * gw, :])
    x = x_ref[...] + g1_ref[...] * y
    xo_ref[...] = x
    h = _modulated_norm(x, gain_ref[...], sc_ref[...], sh_ref[...])
    h_hi, h_lo = _split(h)
    h_ref[...] = h_hi

    w_hi, w_lo = _split(wr_ref[...])
    on_h_hi = _dot_nt(jnp.concatenate([w_hi, w_lo], axis=0), h_hi)
    logits = on_h_hi[0:LANES] + (on_h_hi[LANES:2 * LANES] + _dot_nt(w_hi, h_lo)) + br_ref[...]
    ng, ne = N_EXPERT_GROUPS, EXPERTS_PER_GROUP
    p_group = _softmax_rows([logits[r:r + 1, :] for r in range(ng)])
    pg_top, g_hot = _first_max(p_group)
    e_logits = []
    for e in range(ne):
        t = g_hot[0] * logits[ng + e:ng + e + 1, :]
        for gi in range(1, ng):
            t = t + g_hot[gi] * logits[ng + gi * ne + e:ng + gi * ne + e + 1, :]
        e_logits.append(t)
    p_exp = _softmax_rows(e_logits)
    p1, hot1 = _first_max(p_exp)
    rest = [jnp.where(h1 > 0.5, -1.0, p) for p, h1 in zip(p_exp, hot1)]
    p2, hot2 = _first_max(rest)
    total = p1 + p2
    w_exp = [(h1 * (p1 / total) + h2 * (p2 / total)) * pg_top for h1, h2 in zip(hot1, hot2)]
    for gi in range(ng):
        for e in range(ne):
            combt_ref[gi * ne + e:gi * ne + e + 1, :] = g_hot[gi] * w_exp[e]


def out_projection(ys, w_out, x, g1, gain, scale, shift, w_router, b_router, seq, row_tile=512):
    n, d = x.shape
    w_router_t = w_router.T
    tiles_per_seq = seq // row_tile
    row = lambda i: (i, 0)
    per_batch = lambda i: (i // tiles_per_seq, 0, 0)
    const = lambda shape: pl.BlockSpec(shape, lambda i: (0, 0))
    mod = pl.BlockSpec((None, 1, d), per_batch)
    return pl.pallas_call(
        _out_proj_kernel,
        grid=(n // row_tile,),
        in_specs=[pl.BlockSpec((row_tile, GROUP_WIDTH), row)] * 4
        + [const(w_out.shape), pl.BlockSpec((row_tile, d), row), mod, const((1, d)), mod, mod,
           const(w_router_t.shape), const(b_router.shape)],
        out_specs=(pl.BlockSpec((row_tile, d), row), pl.BlockSpec((row_tile, d), row),
                   pl.BlockSpec((N_EXPERTS, row_tile), lambda i: (0, i))),
        out_shape=(jax.ShapeDtypeStruct((n, d), F32), jax.ShapeDtypeStruct((n, d), BF16),
                   jax.ShapeDtypeStruct((N_EXPERTS, n), F32)),
        compiler_params=_params("parallel"),
        name="out_projection",
    )(*ys, w_out, x, g1, gain, scale, shift, w_router_t, b_router)


MOE_ROWS = 160


def _moe_kernel(h_ref, combt_ref, before_ref, wg_ref, wu_ref, wd_ref, x_ref, g2_ref, fin_ref, o_ref,
                acc_ref, rank_ref, *, final_norm):
    e = pl.program_id(1)
    tile = h_ref.shape[0]

    @pl.when(e == 0)
    def _():
        acc_ref[...] = jnp.zeros(acc_ref.shape, F32)
        routed = jnp.where(combt_ref[...] != 0.0, 1.0, 0.0).astype(BF16)
        rank_ref[...] = _dot(routed, before_ref[...])

    weight_row = combt_ref[pl.ds(e, 1), :]
    rank_row = rank_ref[pl.ds(e, 1), :]
    routed_row = weight_row != 0.0
    count = jnp.sum(jnp.where(routed_row, 1, 0))
    h = h_ref[...]

    def gathered_pass(j, carry):
        slot = (j * MOE_ROWS + lax.broadcasted_iota(jnp.int32, (MOE_ROWS, 1), 0)).astype(F32)
        pick = jnp.where((rank_row == slot) & routed_row, 1.0, 0.0)
        pick_bf = pick.astype(BF16)
        rows = _dot(pick_bf, h).astype(BF16)
        weight = jnp.sum(pick * weight_row, axis=-1, keepdims=True)
        hid = _silu(_dot(rows, wg_ref[...])) * _dot(rows, wu_ref[...]) * weight
        y = _dot(hid.astype(BF16), wd_ref[...]).astype(BF16)
        acc_ref[...] += lax.dot_general(pick_bf, y, (((0,), (0,)), ((), ())), preferred_element_type=F32)
        return carry

    lax.fori_loop(0, (count + MOE_ROWS - 1) // MOE_ROWS, gathered_pass, 0)

    @pl.when(e == N_EXPERTS - 1)
    def _():
        x = x_ref[...] + g2_ref[...] * acc_ref[...]
        if final_norm:
            x = x * lax.rsqrt(jnp.mean(x * x, axis=-1, keepdims=True) + RMS_EPS) * fin_ref[...]
        o_ref[...] = x


def moe_experts(h, combt, w_gate, w_up, w_down, x, g2, fin_gain, seq, final_norm, row_tile=1024):
    n, d = x.shape
    f = w_gate.shape[-1]
    tiles_per_seq = seq // row_tile
    row = lambda i, e: (i, 0)
    before = np.triu(np.ones((row_tile, row_tile), np.float32), 1)
    return pl.pallas_call(
        functools.partial(_moe_kernel, final_norm=final_norm),
        grid=(n // row_tile, N_EXPERTS),
        in_specs=[pl.BlockSpec((row_tile, d), row),
                  pl.BlockSpec((N_EXPERTS, row_tile), lambda i, e: (0, i)),
                  pl.BlockSpec((row_tile, row_tile), lambda i, e: (0, 0)),
                  pl.BlockSpec((None, d, f), lambda i, e: (e, 0, 0)),
                  pl.BlockSpec((None, d, f), lambda i, e: (e, 0, 0)),
                  pl.BlockSpec((None, f, d), lambda i, e: (e, 0, 0)),
                  pl.BlockSpec((row_tile, d), row),
                  pl.BlockSpec((None, 1, d), lambda i, e: (i // tiles_per_seq, 0, 0)),
                  pl.BlockSpec((1, d), lambda i, e: (0, 0))],
        out_specs=pl.BlockSpec((row_tile, d), row),
        out_shape=jax.ShapeDtypeStruct((n, d), F32),
        scratch_shapes=[pltpu.VMEM((row_tile, d), F32), pltpu.VMEM((N_EXPERTS, row_tile), F32)],
        compiler_params=_params("parallel", "arbitrary"),
        name="moe_experts",
    )(h, combt, jnp.asarray(before, BF16), w_gate, w_up, w_down, x, g2, fin_gain)


def kernel(x, c, norm_mix, norm_ffn, final_norm, w_ada, b_ada, w_in, w_out, attn_sink, w_pool, pool_scale, cmp_pos, cmp_w1, cmp_w2, conv_w, a_log, dt_bias, dn_norm, w_route_group, b_route_group, w_route_expert, b_route_expert, w_gate, w_up, w_down):
    batch, seq, d = x.shape
    depth = w_in.shape[0]
    n = batch * seq
    cover, gexp, expand = nsa_constants(seq)
    bexp, bd, tril = gdn_constants()
    mod = ada_modulation(c, w_ada, b_ada)
    spread = lambda t: jnp.repeat(t, HEAD_DIM).reshape(1, GROUP_WIDTH)
    seq3 = lambda t: t.reshape(batch, seq, t.shape[-1])
    xf = x.reshape(n, d)
    for l in range(depth):
        sh1, sc1, g1, sh2, sc2, g2 = (mod[l, :, i * d:(i + 1) * d].reshape(batch, 1, d) for i in range(6))
        p = in_projection(xf, norm_mix[l].reshape(1, d), sc1, sh1, pack_in_weights(w_in[l]), seq)
        y_a = swa_attention(seq3(p["a_q"]), seq3(p["a_k"]), seq3(p["a_v"]), attn_sink[l])
        w_pool_bd = jax.scipy.linalg.block_diag(*[w_pool[l, gi] for gi in range(len(POOL_WINDOWS))])
        y_b = multiscale_pool(seq3(p["b_u"]), w_pool_bd.astype(BF16), pool_scale[l].reshape(1, GROUP_WIDTH))
        kc, vc = nsa_compress(seq3(p["c_cmp"]), *pack_compress_weights(cmp_pos[l], cmp_w1[l], cmp_w2[l]))
        y_c = nsa_attention(seq3(p["c_q"]), seq3(p["c_gate"]), kc, vc, seq3(p["c_ksel"]), seq3(p["c_vsel"]),
                            seq3(p["c_kwin"]), seq3(p["c_vwin"]), cover, gexp, expand)
        y_d = gated_deltanet(seq3(p["d_qkv"]), seq3(p["d_z"]), seq3(p["d_ba"]), conv_w[l], spread(a_log[l]),
                             spread(dt_bias[l]), jnp.tile(dn_norm[l], N_HEADS).reshape(1, GROUP_WIDTH), bexp, bd, tril)
        ys = [t.reshape(n, GROUP_WIDTH) for t in (y_a, y_b, y_c, y_d)]
        n_logits = N_EXPERT_GROUPS + N_EXPERTS
        w_router = jnp.pad(jnp.concatenate([w_route_group[l], w_route_expert[l]], axis=1),
                           ((0, 0), (0, LANES - n_logits)))
        b_router = jnp.pad(jnp.concatenate([b_route_group[l], b_route_expert[l]]),
                           (0, LANES - n_logits)).reshape(LANES, 1)
        xf, h2, comb = out_projection(ys, w_out[l].astype(BF16), xf, g1, norm_ffn[l].reshape(1, d), sc2, sh2,
                                      w_router, b_router, seq)
        f = w_gate.shape[-1]
        xf = moe_experts(h2, comb, w_gate[l].reshape(N_EXPERTS, d, f).astype(BF16),
                         w_up[l].reshape(N_EXPERTS, d, f).astype(BF16),
                         w_down[l].reshape(N_EXPERTS, f, d).astype(BF16),
                         xf, g2, final_norm.reshape(1, d), seq, final_norm=(l == depth - 1))
    return xf.reshape(batch, seq, d)
```

```python
import functools

import numpy as np
import jax
import jax.numpy as jnp
from jax import lax
from jax.experimental import pallas as pl
from jax.experimental.pallas import tpu as pltpu

F32 = jnp.float32
BF16 = jnp.bfloat16
HI = lax.Precision.HIGHEST

HEAD_DIM = 64
N_HEADS = 4
GROUP_WIDTH = N_HEADS * HEAD_DIM
BLOCK = 128
RMS_EPS = 1e-6
SWA_WINDOW = 128
SWA_BLOCKS = 4
POOL_WINDOWS = (2, 4, 8, 16)
POOL_HALO = 16
CMP_LEN = 32
CMP_STRIDE = 16
SEL_BLOCK = 64
NSA_TOP_N = 16
NSA_WINDOW = 512
NSA_FORCE = 1e4
SEL_CHUNK = 1024
DN_CONV = 4
DN_CHUNK = 64
N_EXPERT_GROUPS = 4
EXPERTS_PER_GROUP = 4
N_EXPERTS = N_EXPERT_GROUPS * EXPERTS_PER_GROUP
LANES = 128
MASKED = -1e30
QK_SCALE = HEAD_DIM ** -0.5 * float(np.log2(np.e))
VMEM_LIMIT = 56 * 1024 * 1024

SEG_WIDTHS = (
    ("a_q", 256), ("a_k", 256), ("a_v", 256), ("c_q", 256),
    ("c_ksel", 128), ("c_vsel", 128), ("c_kwin", 128), ("c_vwin", 128),
    ("b_u", 256), ("c_cmp", 128), ("c_gate", 128), ("d_qkv", 768), ("d_z", 256), ("d_ba", 128),
)
SEG_BF16 = ("a_q", "a_k", "a_v", "c_q", "c_ksel", "c_vsel", "c_kwin", "c_vwin")
SEG_OFFSETS = {}
_off = 0
for _name, _w in SEG_WIDTHS:
    SEG_OFFSETS[_name] = (_off, _w)
    _off += _w
PACKED_WIDTH = _off


def _sigmoid(x):
    return 1.0 / (1.0 + jnp.exp(-x))


def _silu(x):
    return x * _sigmoid(x)


def _dot(a, b, precision=None):
    return jnp.dot(a, b, precision=precision, preferred_element_type=F32)


def _dot_nt(a, b, precision=None):
    return lax.dot_general(a, b, (((1,), (1,)), ((), ())), precision=precision,
                           preferred_element_type=F32)


def _split(a):
    hi = a.astype(BF16)
    return hi, (a - hi.astype(F32)).astype(BF16)


def _dot_split_lhs(a, b):
    hi, lo = _split(a)
    return _dot(hi, b) + _dot(lo, b)


def _params(*semantics):
    return pltpu.CompilerParams(dimension_semantics=semantics, vmem_limit_bytes=VMEM_LIMIT)


def _ada_kernel(c_ref, w_ref, b_ref, o_ref):
    cond = _silu(c_ref[...])
    o_ref[...] = _dot(cond, w_ref[...], HI) + b_ref[...]


def ada_modulation(c, w_ada, b_ada, col_tile=1536):
    depth, d, width = w_ada.shape
    b = c.shape[0]
    return pl.pallas_call(
        _ada_kernel,
        grid=(depth, width // col_tile),
        in_specs=[
            pl.BlockSpec((b, d), lambda l, j: (0, 0)),
            pl.BlockSpec((None, d, col_tile), lambda l, j: (l, 0, j)),
            pl.BlockSpec((None, 1, col_tile), lambda l, j: (l, 0, j)),
        ],
        out_specs=pl.BlockSpec((None, b, col_tile), lambda l, j: (l, 0, j)),
        out_shape=jax.ShapeDtypeStruct((depth, b, width), F32),
        compiler_params=_params("parallel", "parallel"),
        name="ada_modulation",
    )(c, w_ada, b_ada.reshape(depth, 1, width))


def _modulated_norm(x, gain, scale, shift):
    y = x * lax.rsqrt(jnp.mean(x * x, axis=-1, keepdims=True) + RMS_EPS)
    return y * gain * (1.0 + scale) + shift


def _in_proj_kernel(x_ref, gain_ref, sc_ref, sh_ref, w_ref, *out_refs):
    h = _modulated_norm(x_ref[...], gain_ref[...], sc_ref[...], sh_ref[...]).astype(BF16)
    for (name, _), o_ref in zip(SEG_WIDTHS, out_refs):
        off, width = SEG_OFFSETS[name]
        o_ref[...] = _dot(h, w_ref[:, off:off + width]).astype(o_ref.dtype)


def in_projection(x, gain, scale, shift, w_packed, seq, row_tile=512):
    n, d = x.shape
    tiles_per_seq = seq // row_tile
    row = lambda i: (i, 0)
    per_batch = lambda i: (i // tiles_per_seq, 0, 0)
    out_shape = tuple(
        jax.ShapeDtypeStruct((n, w), BF16 if name in SEG_BF16 else F32) for name, w in SEG_WIDTHS)
    outs = pl.pallas_call(
        _in_proj_kernel,
        grid=(n // row_tile,),
        in_specs=[
            pl.BlockSpec((row_tile, d), row),
            pl.BlockSpec((1, d), lambda i: (0, 0)),
            pl.BlockSpec((None, 1, d), per_batch),
            pl.BlockSpec((None, 1, d), per_batch),
            pl.BlockSpec((d, PACKED_WIDTH), lambda i: (0, 0)),
        ],
        out_specs=tuple(pl.BlockSpec((row_tile, w), row) for _, w in SEG_WIDTHS),
        out_shape=out_shape,
        compiler_params=_params("parallel"),
        name="in_projection",
    )(x, gain, scale, shift, w_packed)
    return dict(zip((name for name, _ in SEG_WIDTHS), outs))


def pack_in_weights(w_in):
    gw, hd = GROUP_WIDTH, HEAD_DIM
    sizes = (gw, 2 * hd, 2 * hd, gw, gw, 6 * hd, 3 * N_HEADS, 3 * gw, gw, N_HEADS, N_HEADS)
    offs = np.concatenate([[0], np.cumsum(sizes)])
    (a_q, a_k, a_v, b_u, c_q, c_kv, c_gate, d_qkv, d_z, d_beta, d_a) = (
        w_in[:, offs[i]:offs[i + 1]] for i in range(len(sizes)))
    d = w_in.shape[0]
    dup = lambda t: jnp.concatenate([t, t], axis=1)
    pad = lambda t: jnp.pad(t, ((0, 0), (0, LANES - t.shape[1])))
    k_cmp, v_cmp, k_sel, v_sel, k_win, v_win = (c_kv[:, i * hd:(i + 1) * hd] for i in range(6))
    a_q = a_q * QK_SCALE
    c_q = c_q * QK_SCALE
    segs = {
        "a_q": a_q,
        "a_k": jnp.concatenate([dup(a_k[:, :hd]), dup(a_k[:, hd:])], axis=1),
        "a_v": jnp.concatenate([dup(a_v[:, :hd]), dup(a_v[:, hd:])], axis=1),
        "c_q": c_q,
        "c_ksel": dup(k_sel), "c_vsel": dup(v_sel), "c_kwin": dup(k_win), "c_vwin": dup(v_win),
        "b_u": b_u,
        "c_cmp": jnp.concatenate([k_cmp, v_cmp], axis=1),
        "c_gate": pad(c_gate),
        "d_qkv": d_qkv, "d_z": d_z,
        "d_ba": pad(jnp.concatenate([d_beta, d_a], axis=1)),
    }
    packed = jnp.concatenate([segs[name] for name, _ in SEG_WIDTHS], axis=1)
    assert packed.shape == (d, PACKED_WIDTH)
    return packed.astype(BF16)


def _stack_heads(slab):
    lane = lax.broadcasted_iota(jnp.int32, slab.shape, 1)
    zero = jnp.zeros_like(slab)
    return jnp.concatenate([jnp.where(lane < HEAD_DIM, slab, zero),
                            jnp.where(lane >= HEAD_DIM, slab, zero)], axis=0)


def _unstack_heads(o, rows):
    lane = lax.broadcasted_iota(jnp.int32, (rows, LANES), 1)
    return jnp.where(lane < HEAD_DIM, o[0:rows], o[rows:2 * rows])


def _swa_kernel(sink_ref, q_ref, kp_ref, kc_ref, vp_ref, vc_ref, o_ref):
    i = pl.program_id(1)
    row = lax.broadcasted_iota(jnp.int32, (2 * BLOCK, 2 * BLOCK), 0)
    col = lax.broadcasted_iota(jnp.int32, (2 * BLOCK, 2 * BLOCK), 1)
    tq = row & (BLOCK - 1)
    tk = col - BLOCK
    banded = (tk <= tq) & (tk > tq - SWA_WINDOW)
    first_valid = banded & ((col >= BLOCK) | (i > 0))
    rowh = lax.broadcasted_iota(jnp.int32, (2 * BLOCK, 1), 0)
    log2e = float(np.log2(np.e))
    sinks = [jnp.where(rowh < BLOCK, sink_ref[2 * j], sink_ref[2 * j + 1]) * log2e for j in range(N_HEADS // 2)]
    pairs = [(sb, j) for sb in range(SWA_BLOCKS) for j in range(N_HEADS // 2)]

    def keys(cur_ref, prev_ref, sb, lanes):
        if sb == 0:
            return jnp.concatenate([prev_ref[:, lanes], cur_ref[0:BLOCK, lanes]], axis=0)
        return cur_ref[(sb - 1) * BLOCK:(sb + 1) * BLOCK, lanes]

    lanes_of = lambda j: slice(j * LANES, (j + 1) * LANES)
    s = [jnp.where(first_valid if sb == 0 else banded,
                   _dot_nt(_stack_heads(q_ref[sb * BLOCK:(sb + 1) * BLOCK, lanes_of(j)]),
                           keys(kc_ref, kp_ref, sb, lanes_of(j))), -jnp.inf) for sb, j in pairs]
    m = [jnp.maximum(jnp.max(t, axis=-1, keepdims=True), sinks[j]) for t, (sb, j) in zip(s, pairs)]
    p = [jnp.exp2(t - mm) for t, mm in zip(s, m)]
    denom = [jnp.sum(t, axis=-1, keepdims=True) + jnp.exp2(sinks[j] - mm) for t, mm, (sb, j) in zip(p, m, pairs)]
    o = [_dot(t.astype(BF16), keys(vc_ref, vp_ref, sb, lanes_of(j))) / d for t, d, (sb, j) in zip(p, denom, pairs)]
    for sb in range(SWA_BLOCKS):
        slabs = [_unstack_heads(o[sb * (N_HEADS // 2) + j], BLOCK) for j in range(N_HEADS // 2)]
        o_ref[sb * BLOCK:(sb + 1) * BLOCK, :] = jnp.concatenate(slabs, axis=1).astype(o_ref.dtype)


def swa_attention(q, k, v, sink):
    b, s, w = q.shape
    rows = SWA_BLOCKS * BLOCK
    cur = pl.BlockSpec((None, rows, w), lambda bi, i: (bi, i, 0))
    prev = pl.BlockSpec((None, BLOCK, w), lambda bi, i: (bi, jnp.maximum(i * SWA_BLOCKS - 1, 0), 0))
    return pl.pallas_call(
        _swa_kernel,
        grid=(b, s // rows),
        in_specs=[pl.BlockSpec(memory_space=pltpu.SMEM), cur, prev, cur, prev, cur],
        out_specs=cur,
        out_shape=jax.ShapeDtypeStruct((b, s, w), BF16),
        compiler_params=_params("parallel", "parallel"),
        name="swa_attention",
    )(sink, q, k, k, v, v)


def _pool_kernel(up_ref, u_ref, w_ref, scale_ref, o_ref, ext_ref):
    i = pl.program_id(1)
    rows = u_ref.shape[0]
    u = u_ref[...]
    halo = up_ref[...]
    ext_ref[0:POOL_HALO, :] = jnp.where(i > 0, halo, jnp.zeros_like(halo))
    ext_ref[POOL_HALO:POOL_HALO + rows, :] = u
    lane = lax.broadcasted_iota(jnp.int32, u.shape, 1)
    pos = i * rows + lax.broadcasted_iota(jnp.int32, u.shape, 0)
    group_ch = GROUP_WIDTH // len(POOL_WINDOWS)
    total = u
    d = jnp.zeros_like(u)
    width = 1
    for gi, w in enumerate(POOL_WINDOWS):
        while width < w:
            total = total + ext_ref[pl.ds(POOL_HALO - width, rows), :]
            width += 1
        cnt = jnp.minimum(pos + 1, w).astype(F32)
        in_group = (lane >= gi * group_ch) & (lane < (gi + 1) * group_ch)
        d = jnp.where(in_group, total / cnt - u, d)
    o_ref[...] = (_dot(d.astype(BF16), w_ref[...]) * scale_ref[...]).astype(o_ref.dtype)


def multiscale_pool(u, w_blockdiag, pool_scale, row_tile=512):
    b, s, w = u.shape
    halo_per_tile = row_tile // POOL_HALO
    return pl.pallas_call(
        _pool_kernel,
        grid=(b, s // row_tile),
        in_specs=[
            pl.BlockSpec((None, POOL_HALO, w), lambda bi, i: (bi, jnp.maximum(i * halo_per_tile - 1, 0), 0)),
            pl.BlockSpec((None, row_tile, w), lambda bi, i: (bi, i, 0)),
            pl.BlockSpec((w, w), lambda bi, i: (0, 0)),
            pl.BlockSpec((1, w), lambda bi, i: (0, 0)),
        ],
        out_specs=pl.BlockSpec((None, row_tile, w), lambda bi, i: (bi, i, 0)),
        out_shape=jax.ShapeDtypeStruct((b, s, w), BF16),
        scratch_shapes=[pltpu.VMEM((POOL_HALO + row_tile, w), F32)],
        compiler_params=_params("parallel", "parallel"),
        name="multiscale_pool",
    )(u, u, w_blockdiag, pool_scale)


def _compress_kernel(x_ref, w1_ref, pos_ref, w2k_ref, w2v_ref, kc_ref, vc_ref):
    n_chunks = x_ref.shape[0]
    both = _dot(x_ref[...], w1_ref[...], HI)
    pre = both[:, 0:LANES] + pltpu.roll(both[:, LANES:2 * LANES], n_chunks - 1, 0) + pos_ref[...]
    hid = _silu(pre)
    kc_ref[...] = _dot(hid, w2k_ref[...], HI).astype(kc_ref.dtype)
    vc_ref[...] = _dot(hid, w2v_ref[...], HI).astype(vc_ref.dtype)


def nsa_compress(cmp_in, w1_packed, pos_term, w2k, w2v):
    b, s, w = cmp_in.shape
    n_chunks = s // CMP_STRIDE
    flat = cmp_in.reshape(b, n_chunks, CMP_STRIDE * w)
    const = lambda shape: pl.BlockSpec(shape, lambda bi: tuple(0 for _ in shape))
    out = jax.ShapeDtypeStruct((b, n_chunks, LANES), BF16)
    return pl.pallas_call(
        _compress_kernel,
        grid=(b,),
        in_specs=[pl.BlockSpec((None, n_chunks, CMP_STRIDE * w), lambda bi: (bi, 0, 0)),
                  const(w1_packed.shape), const(pos_term.shape), const(w2k.shape), const(w2v.shape)],
        out_specs=(pl.BlockSpec((None, n_chunks, LANES), lambda bi: (bi, 0, 0)),) * 2,
        out_shape=(out, out),
        compiler_params=_params("parallel"),
        name="nsa_compress",
    )(flat, w1_packed, pos_term, w2k, w2v)


def pack_compress_weights(cmp_pos, cmp_w1, cmp_w2):
    hd, half = HEAD_DIM, CMP_LEN // 2
    w1 = cmp_w1.reshape(2, 2, half, hd, hd)
    zeros = jnp.zeros((half, hd, hd), F32)
    halves = []
    for part in range(2):
        wk = jnp.concatenate([w1[0, part], zeros], axis=-1)
        wv = jnp.concatenate([zeros, w1[1, part]], axis=-1)
        halves.append(jnp.concatenate([wk, wv], axis=1).reshape(half * 2 * hd, 2 * hd))
    w1_packed = jnp.concatenate(halves, axis=1)
    pos_flat = cmp_pos.reshape(2, 1, CMP_LEN * hd)
    pos_term = jnp.concatenate([jnp.matmul(pos_flat[0], cmp_w1[0], precision=HI),
                                jnp.matmul(pos_flat[1], cmp_w1[1], precision=HI)], axis=1)
    zero2 = jnp.zeros((hd, 2 * hd), F32)
    w2k = jnp.concatenate([jnp.concatenate([cmp_w2[0], cmp_w2[0]], axis=1), zero2], axis=0)
    w2v = jnp.concatenate([zero2, jnp.concatenate([cmp_w2[1], cmp_w2[1]], axis=1)], axis=0)
    return w1_packed, pos_term, w2k, w2v


def _nsa_kernel(q_ref, gate_ref, kc_ref, vc_ref, ksel_ref, vsel_ref, kwin_ref, vwin_ref,
                cover_ref, gexp_ref, expand_ref, o_ref, q4_ref, m_ref, acc_ref, sa_ref, sb_ref, ma_ref, mb_ref):
    i = pl.program_id(1)
    rows = N_HEADS * BLOCK
    q = q_ref[...]
    q4 = jnp.concatenate([_stack_heads(q[:, 0:LANES]), _stack_heads(q[:, LANES:2 * LANES])], axis=0)
    tq = i * BLOCK + (lax.broadcasted_iota(jnp.int32, (rows, 1), 0) & (BLOCK - 1))

    n_cmp = kc_ref.shape[0]
    n_idx = lax.broadcasted_iota(jnp.int32, (1, n_cmp), 1)
    valid_c = (n_idx * CMP_STRIDE + (CMP_LEN - 1) <= tq) & (n_idx < n_cmp - 1)
    s_c = jnp.where(valid_c, _dot_nt(q4, kc_ref[...]), -jnp.inf)
    m_c = jnp.max(s_c, axis=-1, keepdims=True)
    m_c = jnp.where(m_c == -jnp.inf, 0.0, m_c)
    p_c = jnp.exp2(s_c - m_c)
    d_c = jnp.sum(p_c, axis=-1, keepdims=True)
    p_c = p_c / jnp.where(d_c > 0, d_c, 1.0)
    o_c = _dot(p_c.astype(BF16), vc_ref[...])

    head_rows = [slice(h * BLOCK, (h + 1) * BLOCK) for h in range(N_HEADS)]
    value_lane = lax.broadcasted_iota(jnp.int32, (1, LANES), 1) < HEAD_DIM

    def with_ones(v):
        return jnp.where(value_lane, v, jnp.ones_like(v))

    span = NSA_WINDOW + BLOCK
    start_w = pl.multiple_of(jnp.maximum(i * BLOCK - NSA_WINDOW, 0), BLOCK)
    tk = start_w + lax.broadcasted_iota(jnp.int32, (1, span), 1)
    tq_blk = i * BLOCK + lax.broadcasted_iota(jnp.int32, (BLOCK, 1), 0)
    valid_w = (tk <= tq_blk) & (tk > tq_blk - NSA_WINDOW)
    k_w = kwin_ref[pl.ds(start_w, span), :]
    v_w = with_ones(vwin_ref[pl.ds(start_w, span), :])
    acc_w = []

    def window_head(h):
        s_w = jnp.where(valid_w, _dot_nt(q4[head_rows[h]], k_w), -jnp.inf)
        p_w = jnp.exp2(s_w - jnp.max(s_w, axis=-1, keepdims=True))
        acc_w.append(_dot(p_w.astype(BF16), v_w))

    p_heads = p_c[0:BLOCK] + p_c[BLOCK:2 * BLOCK] + p_c[2 * BLOCK:3 * BLOCK] + p_c[3 * BLOCK:4 * BLOCK]
    p_hi, p_lo = _split(p_heads)
    cover = cover_ref[...]
    importance = _dot_nt(cover, p_hi) + _dot_nt(cover, p_lo)
    n_sel = cover_ref.shape[0]
    blk = lax.broadcasted_iota(jnp.int32, (n_sel, BLOCK), 0)
    t_lane = i * BLOCK + lax.broadcasted_iota(jnp.int32, (n_sel, BLOCK), 1)
    cur = t_lane // SEL_BLOCK
    causal = blk * SEL_BLOCK <= t_lane
    forced = (blk == 0) | (blk == cur) | (blk == cur - 1)
    score = jnp.where(causal, jnp.where(forced, NSA_FORCE, importance), -jnp.inf)
    blk_f = blk.astype(F32)
    chosen = jnp.where(forced, 1.0, 0.0)
    score = jnp.where(forced, -jnp.inf, score)
    n_rounds = NSA_TOP_N - 3
    for r in range(n_rounds):
        if r % (n_rounds // N_HEADS) == 0 and r // (n_rounds // N_HEADS) < N_HEADS:
            window_head(r // (n_rounds // N_HEADS))
        top = jnp.max(score, axis=0, keepdims=True)
        first = jnp.min(jnp.where(score == top, blk_f, float(n_sel)), axis=0, keepdims=True)
        pick = blk_f == first
        score = jnp.where(pick, -jnp.inf, score)
        chosen = jnp.where(pick, 1.0, chosen)
    chosen = jnp.where(causal, chosen, 0.0)
    chosen_q = chosen.T.astype(BF16)

    q4_ref[...] = q4
    m_ref[...] = jnp.full(m_ref.shape, MASKED, F32)
    acc_ref[...] = jnp.zeros(acc_ref.shape, F32)
    n_chunks = (i * BLOCK + BLOCK - 1) // SEL_CHUNK + 1

    def chunk_start(c):
        return pl.multiple_of(jnp.minimum(c, n_chunks - 1) * SEL_CHUNK, SEL_CHUNK)

    def chunk_keep(c):
        start = chunk_start(c)
        key = start + lax.broadcasted_iota(jnp.int32, (1, SEL_CHUNK), 1)
        on_keys = _dot(chosen_q, expand_ref[:, pl.ds(start, SEL_CHUNK)])
        return jnp.where((key <= tq_blk) & (c < n_chunks), on_keys, 0.0) > 0.5

    def stage_scores(h, bufs, c, keep):
        s = jnp.where(keep, _dot_nt(q4_ref[head_rows[h]], ksel_ref[pl.ds(chunk_start(c), SEL_CHUNK), :]), MASKED)
        bufs[0][head_rows[h]] = s
        bufs[1][head_rows[h]] = jnp.max(s, axis=-1, keepdims=True)

    def consume_scores(h, bufs, v_aug):
        m_old = m_ref[head_rows[h]]
        m_new = jnp.maximum(m_old, bufs[1][head_rows[h]])
        m_ref[head_rows[h]] = m_new
        p = jnp.exp2(bufs[0][head_rows[h]] - m_new).astype(BF16)
        acc_ref[head_rows[h]] = jnp.exp2(m_old - m_new) * acc_ref[head_rows[h]] + _dot(p, v_aug)

    def half_step(c, cur_bufs, next_bufs):
        keep_next = chunk_keep(c + 1)
        v_aug = with_ones(vsel_ref[pl.ds(chunk_start(c), SEL_CHUNK), :])
        for h in range(N_HEADS):
            stage_scores(h, next_bufs, c + 1, keep_next)
            consume_scores(h, cur_bufs, v_aug)

    bufs_a = (sa_ref, ma_ref)
    bufs_b = (sb_ref, mb_ref)
    keep0 = chunk_keep(0)
    for h in range(N_HEADS):
        stage_scores(h, bufs_a, 0, keep0)

    def sel_step(t, carry):
        half_step(2 * t, bufs_a, bufs_b)

        @pl.when(2 * t + 1 < n_chunks)
        def _():
            half_step(2 * t + 1, bufs_b, bufs_a)

        return carry

    lax.fori_loop(0, (n_chunks + 1) // 2, sel_step, 0)

    def heads_to_lanes(o):
        return jnp.concatenate([_unstack_heads(o[0:2 * BLOCK], BLOCK),
                                _unstack_heads(o[2 * BLOCK:4 * BLOCK], BLOCK)], axis=1)

    def normalized_heads_to_lanes(acc):
        slabs = []
        for j in range(N_HEADS // 2):
            even = acc[2 * j * BLOCK:(2 * j + 1) * BLOCK]
            odd = acc[(2 * j + 1) * BLOCK:(2 * j + 2) * BLOCK]
            numer = jnp.where(value_lane, even, pltpu.roll(odd, HEAD_DIM, 1))
            denom = jnp.where(value_lane, pltpu.roll(even, HEAD_DIM, 1), odd)
            slabs.append(numer / denom)
        return jnp.concatenate(slabs, axis=1)

    gates = _dot_split_lhs(_sigmoid(gate_ref[...]), gexp_ref[...])
    gw = GROUP_WIDTH
    out = (gates[:, 0:gw] * heads_to_lanes(o_c) + gates[:, gw:2 * gw] * normalized_heads_to_lanes(acc_ref[...])
           + gates[:, 2 * gw:3 * gw] * normalized_heads_to_lanes(jnp.concatenate(acc_w, axis=0)))
    o_ref[...] = out.astype(o_ref.dtype)


def nsa_constants(seq):
    n_cmp_rows = seq // CMP_STRIDE
    n_sel = seq // SEL_BLOCK
    cmp_start = np.arange(n_cmp_rows) * CMP_STRIDE
    sel_start = np.arange(n_sel) * SEL_BLOCK
    cover = np.maximum(np.minimum(cmp_start[None, :] + CMP_LEN, sel_start[:, None] + SEL_BLOCK)
                       - np.maximum(cmp_start[None, :], sel_start[:, None]), 0).astype(np.float32) / CMP_LEN
    gexp = np.zeros((LANES, 3 * GROUP_WIDTH), np.float32)
    for h in range(N_HEADS):
        for br in range(3):
            gexp[h * 3 + br, br * GROUP_WIDTH + h * HEAD_DIM: br * GROUP_WIDTH + (h + 1) * HEAD_DIM] = 1.0
    expand = (np.arange(n_sel)[:, None] == np.arange(seq)[None, :] // SEL_BLOCK).astype(np.float32)
    return cover, gexp, expand


def nsa_attention(q, gate, kc, vc, ksel, vsel, kwin, vwin, cover, gexp, expand):
    b, s, w = q.shape
    assert s >= NSA_WINDOW + BLOCK and s % SEL_CHUNK == 0
    blk = lambda width: pl.BlockSpec((None, BLOCK, width), lambda bi, i: (bi, i, 0))
    per_batch = lambda rows: pl.BlockSpec((None, rows, LANES), lambda bi, i: (bi, 0, 0))
    const = lambda shape: pl.BlockSpec(shape, lambda bi, i: (0, 0))
    rows = N_HEADS * BLOCK
    return pl.pallas_call(
        _nsa_kernel,
        grid=(b, s // BLOCK),
        in_specs=[blk(w), blk(LANES), per_batch(kc.shape[1]), per_batch(vc.shape[1]),
                  per_batch(s), per_batch(s), per_batch(s), per_batch(s),
                  const(cover.shape), const(gexp.shape), const(expand.shape)],
        out_specs=blk(w),
        out_shape=jax.ShapeDtypeStruct((b, s, w), BF16),
        scratch_shapes=[pltpu.VMEM((rows, LANES), BF16), pltpu.VMEM((rows, 1), F32),
                        pltpu.VMEM((rows, LANES), F32), pltpu.VMEM((rows, SEL_CHUNK), F32),
                        pltpu.VMEM((rows, SEL_CHUNK), F32), pltpu.VMEM((rows, 1), F32),
                        pltpu.VMEM((rows, 1), F32)],
        compiler_params=_params("parallel", "arbitrary"),
        name="nsa_attention",
    )(q, gate, kc, vc, ksel, vsel, kwin, vwin, jnp.asarray(cover, BF16), jnp.asarray(gexp, BF16),
      jnp.asarray(expand, BF16))


CONV_TAIL = 8


GDN_ROWS = 512


def _gdn_kernel(qkv_ref, z_ref, ba_ref, convw_ref, alog_ref, dtb_ref, nw_ref, bexp_ref, bd_ref, tril_ref,
                o_ref, ext_ref, state_ref):
    step = pl.program_id(1)
    cl, gw = DN_CHUNK, GROUP_WIDTH
    rows = qkv_ref.shape[0]

    @pl.when(step == 0)
    def _():
        ext_ref[0:CONV_TAIL, :] = jnp.zeros((CONV_TAIL, 3 * gw), F32)
        state_ref[...] = jnp.zeros(state_ref.shape, F32)

    ext_ref[CONV_TAIL:CONV_TAIL + rows, :] = qkv_ref[...]
    cw = convw_ref[...]
    acc = ext_ref[CONV_TAIL:CONV_TAIL + rows, :] * cw[DN_CONV - 1:DN_CONV, :]
    for j in range(DN_CONV - 1):
        acc = acc + ext_ref[pl.ds(CONV_TAIL - (DN_CONV - 1) + j, rows), :] * cw[j:j + 1, :]
    ext_ref[0:CONV_TAIL, :] = ext_ref[rows:rows + CONV_TAIL, :]
    act = _silu(acc)

    bd = bd_ref[...]
    per_head_sum = lambda t: _dot_split_lhs(t, bd)
    q = act[:, 0:gw]
    k = act[:, gw:2 * gw]
    v = act[:, 2 * gw:3 * gw]
    q = q * lax.rsqrt(per_head_sum(q * q) + 1e-6) * (HEAD_DIM ** -0.5)
    k = k * lax.rsqrt(per_head_sum(k * k) + 1e-6)

    ba = _dot_split_lhs(ba_ref[...], bexp_ref[...])
    beta = _sigmoid(ba[:, 0:gw])
    a_in = ba[:, gw:2 * gw] + dtb_ref[...]
    softplus = jnp.maximum(a_in, 0.0) + jnp.log(1.0 + jnp.exp(-jnp.abs(a_in)))
    g = -jnp.exp(alog_ref[...]) * softplus
    tril = tril_ref[...]
    g_hi = g.astype(BF16)
    g_mid, g_lo = _split(g - g_hi.astype(F32))
    gc_all = _dot(tril, g_hi) + (_dot(tril, g_mid) + _dot(tril, g_lo))

    ri = lax.broadcasted_iota(jnp.int32, (cl, cl), 0)
    ci = lax.broadcasted_iota(jnp.int32, (cl, cl), 1)
    causal = ci <= ri
    strict = ci < ri
    eye = jnp.where(ci == ri, 1.0, 0.0)
    lane = lax.broadcasted_iota(jnp.int32, (1, gw), 1)
    head_lanes = [(lane >= h * HEAD_DIM) & (lane < (h + 1) * HEAD_DIM) for h in range(N_HEADS)]

    n_chunks = rows // cl
    pairs = [(c, h) for c in range(n_chunks) for h in range(N_HEADS)]
    chunk = lambda t, c: t[c * cl:(c + 1) * cl]
    gcs = [chunk(gc_all, c) for c in range(n_chunks)]
    gc_ts = [gc.T for gc in gcs]
    g_lasts = [gc[cl - 1:cl, :] for gc in gcs]
    egs = [jnp.exp(gc) for gc in gcs]
    ks = [chunk(k, c) for c in range(n_chunks)]
    k_bfs = [t.astype(BF16) for t in ks]
    k_betas = [chunk(k, c) * chunk(beta, c) for c in range(n_chunks)]
    v_betas = [(chunk(v, c) * chunk(beta, c)).astype(BF16) for c in range(n_chunks)]
    kbgs = [(k_betas[c] * egs[c]).astype(BF16) for c in range(n_chunks)]
    q_decs = [(chunk(q, c) * egs[c]).astype(BF16) for c in range(n_chunks)]
    k_decs = [(ks[c] * jnp.exp(g_lasts[c] - gcs[c])).astype(BF16) for c in range(n_chunks)]
    head = lambda h: slice(h * HEAD_DIM, (h + 1) * HEAD_DIM)
    decays = [jnp.exp(jnp.where(causal, gcs[c][:, head(h)] - gc_ts[c][head(h), :], -jnp.inf)) for c, h in pairs]
    per_head = lambda t: [jnp.where(head_lanes[h], t, 0.0) for h in range(N_HEADS)]
    kq = [_dot_nt(jnp.concatenate(per_head(k_betas[c]) + per_head(chunk(q, c)), axis=0).astype(BF16), k_bfs[c])
          for c in range(n_chunks)]
    kks = [kq[c][h * cl:(h + 1) * cl] for c, h in pairs]
    qks = [kq[c][(N_HEADS + h) * cl:(N_HEADS + h + 1) * cl] for c, h in pairs]
    intras = [jnp.where(causal, qk * d, 0.0) for qk, d in zip(qks, decays)]
    powers = [jnp.where(strict, -(kk * d), 0.0) for kk, d in zip(kks, decays)]
    t_invs = [eye + p for p in powers]
    for _ in range(5):
        p_bfs = [p.astype(BF16) for p in powers]
        powers = [_dot(p, p) for p in p_bfs]
        t_invs = [t + _dot(t.astype(BF16), p.astype(BF16)) for t, p in zip(t_invs, powers)]
    tv = [_dot(jnp.concatenate(t_invs[c * N_HEADS:(c + 1) * N_HEADS], axis=0).astype(BF16),
               jnp.concatenate([v_betas[c], kbgs[c]], axis=1)) for c in range(n_chunks)]

    def merge_heads(stacked, lanes):
        out = jnp.zeros((cl, gw), F32)
        for h in range(N_HEADS):
            out = jnp.where(head_lanes[h], stacked[h * cl:(h + 1) * cl, lanes], out)
        return out

    us = [merge_heads(tv[c], slice(0, gw)) for c in range(n_chunks)]
    ws = [merge_heads(tv[c], slice(gw, 2 * gw)).astype(BF16) for c in range(n_chunks)]
    bd_f = bd.astype(F32)
    tn = (((0,), (0,)), ((), ()))
    kws = [(bd_f * lax.dot_general(k_decs[c], ws[c], tn, preferred_element_type=F32)).astype(BF16)
           for c in range(n_chunks)]
    kus = [bd_f * lax.dot_general(k_decs[c], us[c].astype(BF16), tn, preferred_element_type=F32)
           for c in range(n_chunks)]
    intra_cat = [jnp.concatenate([intras[c * N_HEADS + h] for h in range(N_HEADS)], axis=1).astype(BF16)
                 for c in range(n_chunks)]

    state = state_ref[...]
    outs = []
    for c in range(n_chunks):
        on_state = _dot(jnp.concatenate([ws[c], q_decs[c], kws[c]], axis=0), state.astype(BF16))
        v_new = us[c] - on_state[0:cl]
        v_stack = jnp.concatenate(per_head(v_new), axis=0).astype(BF16)
        outs.append(on_state[cl:2 * cl] + _dot(intra_cat[c], v_stack))
        state = state * jnp.exp(g_lasts[c]) - on_state[2 * cl:] + kus[c]
    state_ref[...] = state

    o = jnp.concatenate(outs, axis=0)
    o = o * lax.rsqrt(per_head_sum(o * o) * (1.0 / HEAD_DIM) + RMS_EPS) * nw_ref[...]
    o_ref[...] = (o * _silu(z_ref[...])).astype(o_ref.dtype)


def gdn_constants():
    lane_head = np.arange(GROUP_WIDTH) // HEAD_DIM
    bd = (lane_head[:, None] == lane_head[None, :]).astype(np.float32)
    bexp = np.zeros((LANES, 2 * GROUP_WIDTH), np.float32)
    for h in range(N_HEADS):
        bexp[h, h * HEAD_DIM:(h + 1) * HEAD_DIM] = 1.0
        bexp[N_HEADS + h, GROUP_WIDTH + h * HEAD_DIM:GROUP_WIDTH + (h + 1) * HEAD_DIM] = 1.0
    r = np.arange(GDN_ROWS)
    tril = ((r[:, None] // DN_CHUNK == r[None, :] // DN_CHUNK) & (r[None, :] <= r[:, None])).astype(np.float32)
    return bexp, bd, tril


def gated_deltanet(qkv, z, ba, conv_w, a_log_rep, dt_bias_rep, norm_w_rep, bexp, bd, tril):
    b, s, w3 = qkv.shape
    gw = GROUP_WIDTH
    blk = lambda width: pl.BlockSpec((None, GDN_ROWS, width), lambda bi, c: (bi, c, 0))
    const = lambda shape: pl.BlockSpec(shape, lambda bi, c: (0, 0))
    as_bf16 = lambda t: jnp.asarray(t, BF16)
    return pl.pallas_call(
        _gdn_kernel,
        grid=(b, s // GDN_ROWS),
        in_specs=[blk(w3), blk(gw), blk(LANES), const(conv_w.shape), const((1, gw)), const((1, gw)),
                  const((1, gw)), const(bexp.shape), const(bd.shape), const(tril.shape)],
        out_specs=blk(gw),
        out_shape=jax.ShapeDtypeStruct((b, s, gw), BF16),
        scratch_shapes=[pltpu.VMEM((CONV_TAIL + GDN_ROWS, w3), F32), pltpu.VMEM((gw, gw), F32)],
        compiler_params=_params("parallel", "arbitrary"),
        name="gated_deltanet",
    )(qkv, z, ba, conv_w, a_log_rep, dt_bias_rep, norm_w_rep, as_bf16(bexp), as_bf16(bd), as_bf16(tril))


def _first_max(values):
    best = values[0]
    for v in values[1:]:
        best = jnp.maximum(best, v)
    taken = jnp.zeros_like(best)
    hot = []
    for v in values:
        h = jnp.where((v == best) & (taken < 0.5), 1.0, 0.0)
        taken = taken + h
        hot.append(h)
    return best, hot


def _softmax_rows(rows):
    m = rows[0]
    for r in rows[1:]:
        m = jnp.maximum(m, r)
    e = [jnp.exp(r - m) for r in rows]
    z = e[0]
    for t in e[1:]:
        z = z + t
    return [t / z for t in e]


def _out_proj_kernel(ya_ref, yb_ref, yc_ref, yd_ref, wo_ref, x_ref, g1_ref, gain_ref, sc_ref, sh_ref,
                     wr_ref, br_ref, xo_ref, h_ref, combt_ref):
    gw = GROUP_WIDTH
    y = _dot(ya_ref[...], wo_ref[0:gw, :])
    y = y + _dot(yb_ref[...], wo_ref[gw:2 * gw, :])
    y = y + _dot(yc_ref[...], wo_ref[2 * gw:3 * gw, :])
    y = y + _dot(yd_ref[...], wo_ref[3 * gw:4 * gw, :])
    x = x_ref[...] + g1_ref[...] * y
    xo_ref[...] = x
    h = _modulated_norm(x, gain_ref[...], sc_ref[...], sh_ref[...])
    h_hi, h_lo = _split(h)
    h_ref[...] = h_hi

    w_hi, w_lo = _split(wr_ref[...])
    on_h_hi = _dot_nt(jnp.concatenate([w_hi, w_lo], axis=0), h_hi)
    logits = on_h_hi[0:LANES] + (on_h_hi[LANES:2 * LANES] + _dot_nt(w_hi, h_lo)) + br_ref[...]
    ng, ne = N_EXPERT_GROUPS, EXPERTS_PER_GROUP
    p_group = _softmax_rows([logits[r:r + 1, :] for r in range(ng)])
    pg_top, g_hot = _first_max(p_group)
    e_logits = []
    for e in range(ne):
        t = g_hot[0] * logits[ng + e:ng + e + 1, :]
        for gi in range(1, ng):
            t = t + g_hot[gi] * logits[ng + gi * ne + e:ng + gi * ne + e + 1, :]
        e_logits.append(t)
    p_exp = _softmax_rows(e_logits)
    p1, hot1 = _first_max(p_exp)
    rest = [jnp.where(h1 > 0.5, -1.0, p) for p, h1 in zip(p_exp, hot1)]
    p2, hot2 = _first_max(rest)
    total = p1 + p2
    w_exp = [(h1 * (p1 / total) + h2 * (p2 / total)) * pg_top for h1, h2 in zip(hot1, hot2)]
    for gi in range(ng):
        for e in range(ne):
            combt_ref[gi * ne + e:gi * ne + e + 1, :] = g_hot[gi] * w_exp[e]


def out_projection(ys, w_out, x, g1, gain, scale, shift, w_router, b_router, seq, row_tile=512):
    n, d = x.shape
    w_router_t = w_router.T
    tiles_per_seq = seq // row_tile
    row = lambda i: (i, 0)
    per_batch = lambda i: (i // tiles_per_seq, 0, 0)
    const = lambda shape: pl.BlockSpec(shape, lambda i: (0, 0))
    mod = pl.BlockSpec((None, 1, d), per_batch)
    return pl.pallas_call(
        _out_proj_kernel,
        grid=(n // row_tile,),
        in_specs=[pl.BlockSpec((row_tile, GROUP_WIDTH), row)] * 4
        + [const(w_out.shape), pl.BlockSpec((row_tile, d), row), mod, const((1, d)), mod, mod,
           const(w_router_t.shape), const(b_router.shape)],
        out_specs=(pl.BlockSpec((row_tile, d), row), pl.BlockSpec((row_tile, d), row),
                   pl.BlockSpec((N_EXPERTS, row_tile), lambda i: (0, i))),
        out_shape=(jax.ShapeDtypeStruct((n, d), F32), jax.ShapeDtypeStruct((n, d), BF16),
                   jax.ShapeDtypeStruct((N_EXPERTS, n), F32)),
        compiler_params=_params("parallel"),
        name="out_projection",
    )(*ys, w_out, x, g1, gain, scale, shift, w_router_t, b_router)


MOE_ROWS = 192


def _moe_kernel(h_ref, combt_ref, before_ref, wg_ref, wu_ref, wd_ref, x_ref, g2_ref, fin_ref, o_ref,
                acc_ref, rank_ref, *, final_norm):
    e = pl.program_id(1)
    tile = h_ref.shape[0]

    @pl.when(e == 0)
    def _():
        acc_ref[...] = jnp.zeros(acc_ref.shape, F32)
        routed = jnp.where(combt_ref[...] != 0.0, 1.0, 0.0).astype(BF16)
        rank_ref[...] = _dot(routed, before_ref[...])

    weight_row = combt_ref[pl.ds(e, 1), :]
    rank_row = rank_ref[pl.ds(e, 1), :]
    routed_row = weight_row != 0.0
    count = jnp.sum(jnp.where(routed_row, 1, 0))
    h = h_ref[...]

    def gathered_pass(j, carry):
        slot = (j * MOE_ROWS + lax.broadcasted_iota(jnp.int32, (MOE_ROWS, 1), 0)).astype(F32)
        pick = jnp.where((rank_row == slot) & routed_row, 1.0, 0.0)
        pick_bf = pick.astype(BF16)
        rows = _dot(pick_bf, h).astype(BF16)
        weight = jnp.sum(pick * weight_row, axis=-1, keepdims=True)
        hid = _silu(_dot(rows, wg_ref[...])) * _dot(rows, wu_ref[...]) * weight
        y = _dot(hid.astype(BF16), wd_ref[...]).astype(BF16)
        acc_ref[...] += lax.dot_general(pick_bf, y, (((0,), (0,)), ((), ())), preferred_element_type=F32)
        return carry

    lax.fori_loop(0, (count + MOE_ROWS - 1) // MOE_ROWS, gathered_pass, 0)

    @pl.when(e == N_EXPERTS - 1)
    def _():
        x = x_ref[...] + g2_ref[...] * acc_ref[...]
        if final_norm:
            x = x * lax.rsqrt(jnp.mean(x * x, axis=-1, keepdims=True) + RMS_EPS) * fin_ref[...]
        o_ref[...] = x


def moe_experts(h, combt, w_gate, w_up, w_down, x, g2, fin_gain, seq, final_norm, row_tile=1024):
    n, d = x.shape
    f = w_gate.shape[-1]
    tiles_per_seq = seq // row_tile
    row = lambda i, e: (i, 0)
    before = np.triu(np.ones((row_tile, row_tile), np.float32), 1)
    return pl.pallas_call(
        functools.partial(_moe_kernel, final_norm=final_norm),
        grid=(n // row_tile, N_EXPERTS),
        in_specs=[pl.BlockSpec((row_tile, d), row),
                  pl.BlockSpec((N_EXPERTS, row_tile), lambda i, e: (0, i)),
                  pl.BlockSpec((row_tile, row_tile), lambda i, e: (0, 0)),
                  pl.BlockSpec((None, d, f), lambda i, e: (e, 0, 0)),
                  pl.BlockSpec((None, d, f), lambda i, e: (e, 0, 0)),
                  pl.BlockSpec((None, f, d), lambda i, e: (e, 0, 0)),
                  pl.BlockSpec((row_tile, d), row),
                  pl.BlockSpec((None, 1, d), lambda i, e: (i // tiles_per_seq, 0, 0)),
                  pl.BlockSpec((1, d), lambda i, e: (0, 0))],
        out_specs=pl.BlockSpec((row_tile, d), row),
        out_shape=jax.ShapeDtypeStruct((n, d), F32),
        scratch_shapes=[pltpu.VMEM((row_tile, d), F32), pltpu.VMEM((N_EXPERTS, row_tile), F32)],
        compiler_params=_params("parallel", "arbitrary"),
        name="moe_experts",
    )(h, combt, jnp.asarray(before, BF16), w_gate, w_up, w_down, x, g2, fin_gain)


def kernel(x, c, norm_mix, norm_ffn, final_norm, w_ada, b_ada, w_in, w_out, attn_sink, w_pool, pool_scale, cmp_pos, cmp_w1, cmp_w2, conv_w, a_log, dt_bias, dn_norm, w_route_group, b_route_group, w_route_expert, b_route_expert, w_gate, w_up, w_down):
    batch, seq, d = x.shape
    depth = w_in.shape[0]
    n = batch * seq
    cover, gexp, expand = nsa_constants(seq)
    bexp, bd, tril = gdn_constants()
    mod = ada_modulation(c, w_ada, b_ada)
    spread = lambda t: jnp.repeat(t, HEAD_DIM).reshape(1, GROUP_WIDTH)
    seq3 = lambda t: t.reshape(batch, seq, t.shape[-1])
    xf = x.reshape(n, d)
    for l in range(depth):
        sh1, sc1, g1, sh2, sc2, g2 = (mod[l, :, i * d:(i + 1) * d].reshape(batch, 1, d) for i in range(6))
        p = in_projection(xf, norm_mix[l].reshape(1, d), sc1, sh1, pack_in_weights(w_in[l]), seq)
        y_a = swa_attention(seq3(p["a_q"]), seq3(p["a_k"]), seq3(p["a_v"]), attn_sink[l])
        w_pool_bd = jax.scipy.linalg.block_diag(*[w_pool[l, gi] for gi in range(len(POOL_WINDOWS))])
        y_b = multiscale_pool(seq3(p["b_u"]), w_pool_bd.astype(BF16), pool_scale[l].reshape(1, GROUP_WIDTH))
        kc, vc = nsa_compress(seq3(p["c_cmp"]), *pack_compress_weights(cmp_pos[l], cmp_w1[l], cmp_w2[l]))
        y_c = nsa_attention(seq3(p["c_q"]), seq3(p["c_gate"]), kc, vc, seq3(p["c_ksel"]), seq3(p["c_vsel"]),
                            seq3(p["c_kwin"]), seq3(p["c_vwin"]), cover, gexp, expand)
        y_d = gated_deltanet(seq3(p["d_qkv"]), seq3(p["d_z"]), seq3(p["d_ba"]), conv_w[l], spread(a_log[l]),
                             spread(dt_bias[l]), jnp.tile(dn_norm[l], N_HEADS).reshape(1, GROUP_WIDTH), bexp, bd, tril)
        ys = [t.reshape(n, GROUP_WIDTH) for t in (y_a, y_b, y_c, y_d)]
        n_logits = N_EXPERT_GROUPS + N_EXPERTS
        w_router = jnp.pad(jnp.concatenate([w_route_group[l], w_route_expert[l]], axis=1),
                           ((0, 0), (0, LANES - n_logits)))
        b_router = jnp.pad(jnp.concatenate([b_route_group[l], b_route_expert[l]]),
                           (0, LANES - n_logits)).reshape(LANES, 1)
        xf, h2, comb = out_projection(ys, w_out[l].astype(BF16), xf, g1, norm_ffn[l].reshape(1, d), sc2, sh2,
                                      w_router, b_router, seq)
        f = w_gate.shape[-1]
        xf = moe_experts(h2, comb, w_gate[l].reshape(N_EXPERTS, d, f).astype(BF16),
                         w_up[l].reshape(N_EXPERTS, d, f).astype(BF16),
                         w_down[l].reshape(N_EXPERTS, f, d).astype(BF16),
                         xf, g2, final_norm.reshape(1, d), seq, final_norm=(l == depth - 1))
    return xf.reshape(batch, seq, d)
```

```python
import functools

import numpy as np
import jax
import jax.numpy as jnp
from jax import lax
from jax.experimental import pallas as pl
from jax.experimental.pallas import tpu as pltpu

F32 = jnp.float32
BF16 = jnp.bfloat16
HI = lax.Precision.HIGHEST

HEAD_DIM = 64
N_HEADS = 4
GROUP_WIDTH = N_HEADS * HEAD_DIM
BLOCK = 128
RMS_EPS = 1e-6
SWA_WINDOW = 128
SWA_BLOCKS = 4
POOL_WINDOWS = (2, 4, 8, 16)
POOL_HALO = 16
CMP_LEN = 32
CMP_STRIDE = 16
SEL_BLOCK = 64
NSA_TOP_N = 16
NSA_WINDOW = 512
NSA_FORCE = 1e4
SEL_CHUNK = 1024
DN_CONV = 4
DN_CHUNK = 64
N_EXPERT_GROUPS = 4
EXPERTS_PER_GROUP = 4
N_EXPERTS = N_EXPERT_GROUPS * EXPERTS_PER_GROUP
LANES = 128
MASKED = -1e30
QK_SCALE = HEAD_DIM ** -0.5 * float(np.log2(np.e))
VMEM_LIMIT = 56 * 1024 * 1024

SEG_WIDTHS = (
    ("a_q", 256), ("a_k", 256), ("a_v", 256), ("c_q", 256),
    ("c_ksel", 128), ("c_vsel", 128), ("c_kwin", 128), ("c_vwin", 128),
    ("b_u", 256), ("c_cmp", 128), ("c_gate", 128), ("d_qkv", 768), ("d_z", 256), ("d_ba", 128),
)
SEG_BF16 = ("a_q", "a_k", "a_v", "c_q", "c_ksel", "c_vsel", "c_kwin", "c_vwin")
SEG_OFFSETS = {}
_off = 0
for _name, _w in SEG_WIDTHS:
    SEG_OFFSETS[_name] = (_off, _w)
    _off += _w
PACKED_WIDTH = _off


def _sigmoid(x):
    return 1.0 / (1.0 + jnp.exp(-x))


def _silu(x):
    return x * _sigmoid(x)


def _dot(a, b, precision=None):
    return jnp.dot(a, b, precision=precision, preferred_element_type=F32)


def _dot_nt(a, b, precision=None):
    return lax.dot_general(a, b, (((1,), (1,)), ((), ())), precision=precision,
                           preferred_element_type=F32)


def _split(a):
    hi = a.astype(BF16)
    return hi, (a - hi.astype(F32)).astype(BF16)


def _dot_split_lhs(a, b):
    hi, lo = _split(a)
    return _dot(hi, b) + _dot(lo, b)


def _params(*semantics):
    return pltpu.CompilerParams(dimension_semantics=semantics, vmem_limit_bytes=VMEM_LIMIT)


def _ada_kernel(c_ref, w_ref, b_ref, o_ref):
    cond = _silu(c_ref[...])
    o_ref[...] = _dot(cond, w_ref[...], HI) + b_ref[...]


def ada_modulation(c, w_ada, b_ada, col_tile=1536):
    depth, d, width = w_ada.shape
    b = c.shape[0]
    return pl.pallas_call(
        _ada_kernel,
        grid=(depth, width // col_tile),
        in_specs=[
            pl.BlockSpec((b, d), lambda l, j: (0, 0)),
            pl.BlockSpec((None, d, col_tile), lambda l, j: (l, 0, j)),
            pl.BlockSpec((None, 1, col_tile), lambda l, j: (l, 0, j)),
        ],
        out_specs=pl.BlockSpec((None, b, col_tile), lambda l, j: (l, 0, j)),
        out_shape=jax.ShapeDtypeStruct((depth, b, width), F32),
        compiler_params=_params("parallel", "parallel"),
        name="ada_modulation",
    )(c, w_ada, b_ada.reshape(depth, 1, width))


def _modulated_norm(x, gain, scale, shift):
    y = x * lax.rsqrt(jnp.mean(x * x, axis=-1, keepdims=True) + RMS_EPS)
    return y * gain * (1.0 + scale) + shift


def _in_proj_kernel(x_ref, gain_ref, sc_ref, sh_ref, w_ref, *out_refs):
    h = _modulated_norm(x_ref[...], gain_ref[...], sc_ref[...], sh_ref[...]).astype(BF16)
    for (name, _), o_ref in zip(SEG_WIDTHS, out_refs):
        off, width = SEG_OFFSETS[name]
        o_ref[...] = _dot(h, w_ref[:, off:off + width]).astype(o_ref.dtype)


def in_projection(x, gain, scale, shift, w_packed, seq, row_tile=512):
    n, d = x.shape
    tiles_per_seq = seq // row_tile
    row = lambda i: (i, 0)
    per_batch = lambda i: (i // tiles_per_seq, 0, 0)
    out_shape = tuple(
        jax.ShapeDtypeStruct((n, w), BF16 if name in SEG_BF16 else F32) for name, w in SEG_WIDTHS)
    outs = pl.pallas_call(
        _in_proj_kernel,
        grid=(n // row_tile,),
        in_specs=[
            pl.BlockSpec((row_tile, d), row),
            pl.BlockSpec((1, d), lambda i: (0, 0)),
            pl.BlockSpec((None, 1, d), per_batch),
            pl.BlockSpec((None, 1, d), per_batch),
            pl.BlockSpec((d, PACKED_WIDTH), lambda i: (0, 0)),
        ],
        out_specs=tuple(pl.BlockSpec((row_tile, w), row) for _, w in SEG_WIDTHS),
        out_shape=out_shape,
        compiler_params=_params("parallel"),
        name="in_projection",
    )(x, gain, scale, shift, w_packed)
    return dict(zip((name for name, _ in SEG_WIDTHS), outs))


def pack_in_weights(w_in):
    gw, hd = GROUP_WIDTH, HEAD_DIM
    sizes = (gw, 2 * hd, 2 * hd, gw, gw, 6 * hd, 3 * N_HEADS, 3 * gw, gw, N_HEADS, N_HEADS)
    offs = np.concatenate([[0], np.cumsum(sizes)])
    (a_q, a_k, a_v, b_u, c_q, c_kv, c_gate, d_qkv, d_z, d_beta, d_a) = (
        w_in[:, offs[i]:offs[i + 1]] for i in range(len(sizes)))
    d = w_in.shape[0]
    dup = lambda t: jnp.concatenate([t, t], axis=1)
    pad = lambda t: jnp.pad(t, ((0, 0), (0, LANES - t.shape[1])))
    k_cmp, v_cmp, k_sel, v_sel, k_win, v_win = (c_kv[:, i * hd:(i + 1) * hd] for i in range(6))
    a_q = a_q * QK_SCALE
    c_q = c_q * QK_SCALE
    segs = {
        "a_q": a_q,
        "a_k": jnp.concatenate([dup(a_k[:, :hd]), dup(a_k[:, hd:])], axis=1),
        "a_v": jnp.concatenate([dup(a_v[:, :hd]), dup(a_v[:, hd:])], axis=1),
        "c_q": c_q,
        "c_ksel": dup(k_sel), "c_vsel": dup(v_sel), "c_kwin": dup(k_win), "c_vwin": dup(v_win),
        "b_u": b_u,
        "c_cmp": jnp.concatenate([k_cmp, v_cmp], axis=1),
        "c_gate": pad(c_gate),
        "d_qkv": d_qkv, "d_z": d_z,
        "d_ba": pad(jnp.concatenate([d_beta, d_a], axis=1)),
    }
    packed = jnp.concatenate([segs[name] for name, _ in SEG_WIDTHS], axis=1)
    assert packed.shape == (d, PACKED_WIDTH)
    return packed.astype(BF16)


def _stack_heads(slab):
    lane = lax.broadcasted_iota(jnp.int32, slab.shape, 1)
    zero = jnp.zeros_like(slab)
    return jnp.concatenate([jnp.where(lane < HEAD_DIM, slab, zero),
                            jnp.where(lane >= HEAD_DIM, slab, zero)], axis=0)


def _unstack_heads(o, rows):
    lane = lax.broadcasted_iota(jnp.int32, (rows, LANES), 1)
    return jnp.where(lane < HEAD_DIM, o[0:rows], o[rows:2 * rows])


def _swa_kernel(sink_ref, q_ref, kp_ref, kc_ref, vp_ref, vc_ref, o_ref):
    i = pl.program_id(1)
    row = lax.broadcasted_iota(jnp.int32, (2 * BLOCK, 2 * BLOCK), 0)
    col = lax.broadcasted_iota(jnp.int32, (2 * BLOCK, 2 * BLOCK), 1)
    tq = row & (BLOCK - 1)
    tk = col - BLOCK
    banded = (tk <= tq) & (tk > tq - SWA_WINDOW)
    first_valid = banded & ((col >= BLOCK) | (i > 0))
    rowh = lax.broadcasted_iota(jnp.int32, (2 * BLOCK, 1), 0)
    log2e = float(np.log2(np.e))
    sinks = [jnp.where(rowh < BLOCK, sink_ref[2 * j], sink_ref[2 * j + 1]) * log2e for j in range(N_HEADS // 2)]
    pairs = [(sb, j) for sb in range(SWA_BLOCKS) for j in range(N_HEADS // 2)]

    def keys(cur_ref, prev_ref, sb, lanes):
        if sb == 0:
            return jnp.concatenate([prev_ref[:, lanes], cur_ref[0:BLOCK, lanes]], axis=0)
        return cur_ref[(sb - 1) * BLOCK:(sb + 1) * BLOCK, lanes]

    lanes_of = lambda j: slice(j * LANES, (j + 1) * LANES)
    s = [jnp.where(first_valid if sb == 0 else banded,
                   _dot_nt(_stack_heads(q_ref[sb * BLOCK:(sb + 1) * BLOCK, lanes_of(j)]),
                           keys(kc_ref, kp_ref, sb, lanes_of(j))), -jnp.inf) for sb, j in pairs]
    m = [jnp.maximum(jnp.max(t, axis=-1, keepdims=True), sinks[j]) for t, (sb, j) in zip(s, pairs)]
    p = [jnp.exp2(t - mm) for t, mm in zip(s, m)]
    denom = [jnp.sum(t, axis=-1, keepdims=True) + jnp.exp2(sinks[j] - mm) for t, mm, (sb, j) in zip(p, m, pairs)]
    o = [_dot(t.astype(BF16), keys(vc_ref, vp_ref, sb, lanes_of(j))) / d for t, d, (sb, j) in zip(p, denom, pairs)]
    for sb in range(SWA_BLOCKS):
        slabs = [_unstack_heads(o[sb * (N_HEADS // 2) + j], BLOCK) for j in range(N_HEADS // 2)]
        o_ref[sb * BLOCK:(sb + 1) * BLOCK, :] = jnp.concatenate(slabs, axis=1).astype(o_ref.dtype)


def swa_attention(q, k, v, sink):
    b, s, w = q.shape
    rows = SWA_BLOCKS * BLOCK
    cur = pl.BlockSpec((None, rows, w), lambda bi, i: (bi, i, 0))
    prev = pl.BlockSpec((None, BLOCK, w), lambda bi, i: (bi, jnp.maximum(i * SWA_BLOCKS - 1, 0), 0))
    return pl.pallas_call(
        _swa_kernel,
        grid=(b, s // rows),
        in_specs=[pl.BlockSpec(memory_space=pltpu.SMEM), cur, prev, cur, prev, cur],
        out_specs=cur,
        out_shape=jax.ShapeDtypeStruct((b, s, w), BF16),
        compiler_params=_params("parallel", "parallel"),
        name="swa_attention",
    )(sink, q, k, k, v, v)


def _pool_kernel(up_ref, u_ref, w_ref, scale_ref, o_ref, ext_ref):
    i = pl.program_id(1)
    rows = u_ref.shape[0]
    u = u_ref[...]
    halo = up_ref[...]
    ext_ref[0:POOL_HALO, :] = jnp.where(i > 0, halo, jnp.zeros_like(halo))
    ext_ref[POOL_HALO:POOL_HALO + rows, :] = u
    lane = lax.broadcasted_iota(jnp.int32, u.shape, 1)
    pos = i * rows + lax.broadcasted_iota(jnp.int32, u.shape, 0)
    group_ch = GROUP_WIDTH // len(POOL_WINDOWS)
    total = u
    d = jnp.zeros_like(u)
    width = 1
    for gi, w in enumerate(POOL_WINDOWS):
        while width < w:
            total = total + ext_ref[pl.ds(POOL_HALO - width, rows), :]
            width += 1
        cnt = jnp.minimum(pos + 1, w).astype(F32)
        in_group = (lane >= gi * group_ch) & (lane < (gi + 1) * group_ch)
        d = jnp.where(in_group, total / cnt - u, d)
    o_ref[...] = (_dot(d.astype(BF16), w_ref[...]) * scale_ref[...]).astype(o_ref.dtype)


def multiscale_pool(u, w_blockdiag, pool_scale, row_tile=512):
    b, s, w = u.shape
    halo_per_tile = row_tile // POOL_HALO
    return pl.pallas_call(
        _pool_kernel,
        grid=(b, s // row_tile),
        in_specs=[
            pl.BlockSpec((None, POOL_HALO, w), lambda bi, i: (bi, jnp.maximum(i * halo_per_tile - 1, 0), 0)),
            pl.BlockSpec((None, row_tile, w), lambda bi, i: (bi, i, 0)),
            pl.BlockSpec((w, w), lambda bi, i: (0, 0)),
            pl.BlockSpec((1, w), lambda bi, i: (0, 0)),
        ],
        out_specs=pl.BlockSpec((None, row_tile, w), lambda bi, i: (bi, i, 0)),
        out_shape=jax.ShapeDtypeStruct((b, s, w), BF16),
        scratch_shapes=[pltpu.VMEM((POOL_HALO + row_tile, w), F32)],
        compiler_params=_params("parallel", "parallel"),
        name="multiscale_pool",
    )(u, u, w_blockdiag, pool_scale)


def _compress_kernel(x_ref, w1_ref, pos_ref, w2k_ref, w2v_ref, kc_ref, vc_ref):
    n_chunks = x_ref.shape[0]
    both = _dot(x_ref[...], w1_ref[...], HI)
    pre = both[:, 0:LANES] + pltpu.roll(both[:, LANES:2 * LANES], n_chunks - 1, 0) + pos_ref[...]
    hid = _silu(pre)
    kc_ref[...] = _dot(hid, w2k_ref[...], HI).astype(kc_ref.dtype)
    vc_ref[...] = _dot(hid, w2v_ref[...], HI).astype(vc_ref.dtype)


def nsa_compress(cmp_in, w1_packed, pos_term, w2k, w2v):
    b, s, w = cmp_in.shape
    n_chunks = s // CMP_STRIDE
    flat = cmp_in.reshape(b, n_chunks, CMP_STRIDE * w)
    const = lambda shape: pl.BlockSpec(shape, lambda bi: tuple(0 for _ in shape))
    out = jax.ShapeDtypeStruct((b, n_chunks, LANES), BF16)
    return pl.pallas_call(
        _compress_kernel,
        grid=(b,),
        in_specs=[pl.BlockSpec((None, n_chunks, CMP_STRIDE * w), lambda bi: (bi, 0, 0)),
                  const(w1_packed.shape), const(pos_term.shape), const(w2k.shape), const(w2v.shape)],
        out_specs=(pl.BlockSpec((None, n_chunks, LANES), lambda bi: (bi, 0, 0)),) * 2,
        out_shape=(out, out),
        compiler_params=_params("parallel"),
        name="nsa_compress",
    )(flat, w1_packed, pos_term, w2k, w2v)


def pack_compress_weights(cmp_pos, cmp_w1, cmp_w2):
    hd, half = HEAD_DIM, CMP_LEN // 2
    w1 = cmp_w1.reshape(2, 2, half, hd, hd)
    zeros = jnp.zeros((half, hd, hd), F32)
    halves = []
    for part in range(2):
        wk = jnp.concatenate([w1[0, part], zeros], axis=-1)
        wv = jnp.concatenate([zeros, w1[1, part]], axis=-1)
        halves.append(jnp.concatenate([wk, wv], axis=1).reshape(half * 2 * hd, 2 * hd))
    w1_packed = jnp.concatenate(halves, axis=1)
    pos_flat = cmp_pos.reshape(2, 1, CMP_LEN * hd)
    pos_term = jnp.concatenate([jnp.matmul(pos_flat[0], cmp_w1[0], precision=HI),
                                jnp.matmul(pos_flat[1], cmp_w1[1], precision=HI)], axis=1)
    zero2 = jnp.zeros((hd, 2 * hd), F32)
    w2k = jnp.concatenate([jnp.concatenate([cmp_w2[0], cmp_w2[0]], axis=1), zero2], axis=0)
    w2v = jnp.concatenate([zero2, jnp.concatenate([cmp_w2[1], cmp_w2[1]], axis=1)], axis=0)
    return w1_packed, pos_term, w2k, w2v


def _nsa_kernel(q_ref, gate_ref, kc_ref, vc_ref, ksel_ref, vsel_ref, kwin_ref, vwin_ref,
                cover_ref, gexp_ref, expand_ref, o_ref, q4_ref, m_ref, acc_ref, sa_ref, sb_ref, ma_ref, mb_ref):
    i = pl.program_id(1)
    rows = N_HEADS * BLOCK
    q = q_ref[...]
    q4 = jnp.concatenate([_stack_heads(q[:, 0:LANES]), _stack_heads(q[:, LANES:2 * LANES])], axis=0)

    head_rows = [slice(h * BLOCK, (h + 1) * BLOCK) for h in range(N_HEADS)]
    tq_blk = i * BLOCK + lax.broadcasted_iota(jnp.int32, (BLOCK, 1), 0)
    n_cmp = kc_ref.shape[0]
    n_idx = lax.broadcasted_iota(jnp.int32, (1, n_cmp), 1)
    valid_c = (n_idx * CMP_STRIDE + (CMP_LEN - 1) <= tq_blk) & (n_idx < n_cmp - 1)
    k_c = kc_ref[...]
    v_c = vc_ref[...]
    s_c = [jnp.where(valid_c, _dot_nt(q4[hr], k_c), -jnp.inf) for hr in head_rows]
    m_c = [jnp.max(t, axis=-1, keepdims=True) for t in s_c]
    m_c = [jnp.where(t == -jnp.inf, 0.0, t) for t in m_c]
    p_c = [jnp.exp2(t - mm) for t, mm in zip(s_c, m_c)]
    d_c = [jnp.sum(t, axis=-1, keepdims=True) for t in p_c]
    p_c = [t / jnp.where(dd > 0, dd, 1.0) for t, dd in zip(p_c, d_c)]
    o_c = jnp.concatenate([_dot(t.astype(BF16), v_c) for t in p_c], axis=0)

    value_lane = lax.broadcasted_iota(jnp.int32, (1, LANES), 1) < HEAD_DIM

    def with_ones(v):
        return jnp.where(value_lane, v, jnp.ones_like(v))

    span = NSA_WINDOW + BLOCK
    start_w = pl.multiple_of(jnp.maximum(i * BLOCK - NSA_WINDOW, 0), BLOCK)
    tk = start_w + lax.broadcasted_iota(jnp.int32, (1, span), 1)
    valid_w = (tk <= tq_blk) & (tk > tq_blk - NSA_WINDOW)
    k_w = kwin_ref[pl.ds(start_w, span), :]
    v_w = with_ones(vwin_ref[pl.ds(start_w, span), :])
    acc_w = []

    def window_head(h):
        s_w = jnp.where(valid_w, _dot_nt(q4[head_rows[h]], k_w), -jnp.inf)
        p_w = jnp.exp2(s_w - jnp.max(s_w, axis=-1, keepdims=True))
        acc_w.append(_dot(p_w.astype(BF16), v_w))

    p_heads = (p_c[0] + p_c[1]) + (p_c[2] + p_c[3])
    p_hi, p_lo = _split(p_heads)
    cover = cover_ref[...]
    importance = _dot_nt(cover, p_hi) + _dot_nt(cover, p_lo)
    n_sel = cover_ref.shape[0]
    blk = lax.broadcasted_iota(jnp.int32, (n_sel, BLOCK), 0)
    t_lane = i * BLOCK + lax.broadcasted_iota(jnp.int32, (n_sel, BLOCK), 1)
    cur = t_lane // SEL_BLOCK
    causal = blk * SEL_BLOCK <= t_lane
    forced = (blk == 0) | (blk == cur) | (blk == cur - 1)
    score = jnp.where(causal, jnp.where(forced, NSA_FORCE, importance), -jnp.inf)
    blk_f = blk.astype(F32)
    chosen = jnp.where(forced, 1.0, 0.0)
    score = jnp.where(forced, -jnp.inf, score)
    n_rounds = NSA_TOP_N - 3
    for r in range(n_rounds):
        if r % (n_rounds // N_HEADS) == 0 and r // (n_rounds // N_HEADS) < N_HEADS:
            window_head(r // (n_rounds // N_HEADS))
        top = jnp.max(score, axis=0, keepdims=True)
        first = jnp.min(jnp.where(score == top, blk_f, float(n_sel)), axis=0, keepdims=True)
        pick = blk_f == first
        score = jnp.where(pick, -jnp.inf, score)
        chosen = jnp.where(pick, 1.0, chosen)
    chosen = jnp.where(causal, chosen, 0.0)
    chosen_q = chosen.T.astype(BF16)

    q4_ref[...] = q4
    m_ref[...] = jnp.full(m_ref.shape, MASKED, F32)
    acc_ref[...] = jnp.zeros(acc_ref.shape, F32)
    n_chunks = (i * BLOCK + BLOCK - 1) // SEL_CHUNK + 1

    def chunk_start(c):
        return pl.multiple_of(jnp.minimum(c, n_chunks - 1) * SEL_CHUNK, SEL_CHUNK)

    def chunk_keep(c):
        start = chunk_start(c)
        key = start + lax.broadcasted_iota(jnp.int32, (1, SEL_CHUNK), 1)
        on_keys = _dot(chosen_q, expand_ref[:, pl.ds(start, SEL_CHUNK)])
        return jnp.where((key <= tq_blk) & (c < n_chunks), on_keys, 0.0) > 0.5

    def stage_scores(h, bufs, c, keep):
        s = jnp.where(keep, _dot_nt(q4_ref[head_rows[h]], ksel_ref[pl.ds(chunk_start(c), SEL_CHUNK), :]), MASKED)
        bufs[0][head_rows[h]] = s
        bufs[1][head_rows[h]] = jnp.max(s, axis=-1, keepdims=True)

    def consume_scores(h, bufs, v_aug):
        m_old = m_ref[head_rows[h]]
        m_new = jnp.maximum(m_old, bufs[1][head_rows[h]])
        m_ref[head_rows[h]] = m_new
        p = jnp.exp2(bufs[0][head_rows[h]] - m_new).astype(BF16)
        acc_ref[head_rows[h]] = jnp.exp2(m_old - m_new) * acc_ref[head_rows[h]] + _dot(p, v_aug)

    def half_step(c, cur_bufs, next_bufs):
        keep_next = chunk_keep(c + 1)
        v_aug = with_ones(vsel_ref[pl.ds(chunk_start(c), SEL_CHUNK), :])
        for h in range(N_HEADS):
            stage_scores(h, next_bufs, c + 1, keep_next)
            consume_scores(h, cur_bufs, v_aug)

    bufs_a = (sa_ref, ma_ref)
    bufs_b = (sb_ref, mb_ref)
    keep0 = chunk_keep(0)
    for h in range(N_HEADS):
        stage_scores(h, bufs_a, 0, keep0)

    def sel_step(t, carry):
        half_step(2 * t, bufs_a, bufs_b)

        @pl.when(2 * t + 1 < n_chunks)
        def _():
            half_step(2 * t + 1, bufs_b, bufs_a)

        return carry

    lax.fori_loop(0, (n_chunks + 1) // 2, sel_step, 0)

    def heads_to_lanes(o):
        return jnp.concatenate([_unstack_heads(o[0:2 * BLOCK], BLOCK),
                                _unstack_heads(o[2 * BLOCK:4 * BLOCK], BLOCK)], axis=1)

    def normalized_heads_to_lanes(acc):
        slabs = []
        for j in range(N_HEADS // 2):
            even = acc[2 * j * BLOCK:(2 * j + 1) * BLOCK]
            odd = acc[(2 * j + 1) * BLOCK:(2 * j + 2) * BLOCK]
            numer = jnp.where(value_lane, even, pltpu.roll(odd, HEAD_DIM, 1))
            denom = jnp.where(value_lane, pltpu.roll(even, HEAD_DIM, 1), odd)
            slabs.append(numer / denom)
        return jnp.concatenate(slabs, axis=1)

    gates = _dot_split_lhs(_sigmoid(gate_ref[...]), gexp_ref[...])
    gw = GROUP_WIDTH
    out = (gates[:, 0:gw] * heads_to_lanes(o_c) + gates[:, gw:2 * gw] * normalized_heads_to_lanes(acc_ref[...])
           + gates[:, 2 * gw:3 * gw] * normalized_heads_to_lanes(jnp.concatenate(acc_w, axis=0)))
    o_ref[...] = out.astype(o_ref.dtype)


def nsa_constants(seq):
    n_cmp_rows = seq // CMP_STRIDE
    n_sel = seq // SEL_BLOCK
    cmp_start = np.arange(n_cmp_rows) * CMP_STRIDE
    sel_start = np.arange(n_sel) * SEL_BLOCK
    cover = np.maximum(np.minimum(cmp_start[None, :] + CMP_LEN, sel_start[:, None] + SEL_BLOCK)
                       - np.maximum(cmp_start[None, :], sel_start[:, None]), 0).astype(np.float32) / CMP_LEN
    gexp = np.zeros((LANES, 3 * GROUP_WIDTH), np.float32)
    for h in range(N_HEADS):
        for br in range(3):
            gexp[h * 3 + br, br * GROUP_WIDTH + h * HEAD_DIM: br * GROUP_WIDTH + (h + 1) * HEAD_DIM] = 1.0
    expand = (np.arange(n_sel)[:, None] == np.arange(seq)[None, :] // SEL_BLOCK).astype(np.float32)
    return cover, gexp, expand


def nsa_attention(q, gate, kc, vc, ksel, vsel, kwin, vwin, cover, gexp, expand):
    b, s, w = q.shape
    assert s >= NSA_WINDOW + BLOCK and s % SEL_CHUNK == 0
    blk = lambda width: pl.BlockSpec((None, BLOCK, width), lambda bi, i: (bi, i, 0))
    per_batch = lambda rows: pl.BlockSpec((None, rows, LANES), lambda bi, i: (bi, 0, 0))
    const = lambda shape: pl.BlockSpec(shape, lambda bi, i: (0, 0))
    rows = N_HEADS * BLOCK
    return pl.pallas_call(
        _nsa_kernel,
        grid=(b, s // BLOCK),
        in_specs=[blk(w), blk(LANES), per_batch(kc.shape[1]), per_batch(vc.shape[1]),
                  per_batch(s), per_batch(s), per_batch(s), per_batch(s),
                  const(cover.shape), const(gexp.shape), const(expand.shape)],
        out_specs=blk(w),
        out_shape=jax.ShapeDtypeStruct((b, s, w), BF16),
        scratch_shapes=[pltpu.VMEM((rows, LANES), BF16), pltpu.VMEM((rows, 1), F32),
                        pltpu.VMEM((rows, LANES), F32), pltpu.VMEM((rows, SEL_CHUNK), F32),
                        pltpu.VMEM((rows, SEL_CHUNK), F32), pltpu.VMEM((rows, 1), F32),
                        pltpu.VMEM((rows, 1), F32)],
        compiler_params=_params("parallel", "arbitrary"),
        name="nsa_attention",
    )(q, gate, kc, vc, ksel, vsel, kwin, vwin, jnp.asarray(cover, BF16), jnp.asarray(gexp, BF16),
      jnp.asarray(expand, BF16))


CONV_TAIL = 8


GDN_ROWS = 512


def _gdn_kernel(qkv_ref, z_ref, ba_ref, convw_ref, alog_ref, dtb_ref, nw_ref, bexp_ref, bd_ref, tril_ref,
                o_ref, ext_ref, state_ref):
    step = pl.program_id(1)
    cl, gw = DN_CHUNK, GROUP_WIDTH
    rows = qkv_ref.shape[0]

    @pl.when(step == 0)
    def _():
        ext_ref[0:CONV_TAIL, :] = jnp.zeros((CONV_TAIL, 3 * gw), F32)
        state_ref[...] = jnp.zeros(state_ref.shape, F32)

    ext_ref[CONV_TAIL:CONV_TAIL + rows, :] = qkv_ref[...]
    cw = convw_ref[...]
    acc = ext_ref[CONV_TAIL:CONV_TAIL + rows, :] * cw[DN_CONV - 1:DN_CONV, :]
    for j in range(DN_CONV - 1):
        acc = acc + ext_ref[pl.ds(CONV_TAIL - (DN_CONV - 1) + j, rows), :] * cw[j:j + 1, :]
    ext_ref[0:CONV_TAIL, :] = ext_ref[rows:rows + CONV_TAIL, :]
    act = _silu(acc)

    bd = bd_ref[...]
    per_head_sum = lambda t: _dot_split_lhs(t, bd)
    q = act[:, 0:gw]
    k = act[:, gw:2 * gw]
    v = act[:, 2 * gw:3 * gw]
    q = q * lax.rsqrt(per_head_sum(q * q) + 1e-6) * (HEAD_DIM ** -0.5)
    k = k * lax.rsqrt(per_head_sum(k * k) + 1e-6)

    ba = _dot_split_lhs(ba_ref[...], bexp_ref[...])
    beta = _sigmoid(ba[:, 0:gw])
    a_in = ba[:, gw:2 * gw] + dtb_ref[...]
    softplus = jnp.maximum(a_in, 0.0) + jnp.log(1.0 + jnp.exp(-jnp.abs(a_in)))
    g = -jnp.exp(alog_ref[...]) * softplus
    tril = tril_ref[...]
    g_hi = g.astype(BF16)
    g_mid, g_lo = _split(g - g_hi.astype(F32))
    gc_all = _dot(tril, g_hi) + (_dot(tril, g_mid) + _dot(tril, g_lo))

    ri = lax.broadcasted_iota(jnp.int32, (cl, cl), 0)
    ci = lax.broadcasted_iota(jnp.int32, (cl, cl), 1)
    causal = ci <= ri
    strict = ci < ri
    eye = jnp.where(ci == ri, 1.0, 0.0)
    lane = lax.broadcasted_iota(jnp.int32, (1, gw), 1)
    head_lanes = [(lane >= h * HEAD_DIM) & (lane < (h + 1) * HEAD_DIM) for h in range(N_HEADS)]

    n_chunks = rows // cl
    pairs = [(c, h) for c in range(n_chunks) for h in range(N_HEADS)]
    chunk = lambda t, c: t[c * cl:(c + 1) * cl]
    gcs = [chunk(gc_all, c) for c in range(n_chunks)]
    gc_ts = [gc.T for gc in gcs]
    g_lasts = [gc[cl - 1:cl, :] for gc in gcs]
    egs = [jnp.exp(gc) for gc in gcs]
    ks = [chunk(k, c) for c in range(n_chunks)]
    k_bfs = [t.astype(BF16) for t in ks]
    k_betas = [chunk(k, c) * chunk(beta, c) for c in range(n_chunks)]
    v_betas = [(chunk(v, c) * chunk(beta, c)).astype(BF16) for c in range(n_chunks)]
    kbgs = [(k_betas[c] * egs[c]).astype(BF16) for c in range(n_chunks)]
    q_decs = [(chunk(q, c) * egs[c]).astype(BF16) for c in range(n_chunks)]
    k_decs = [(ks[c] * jnp.exp(g_lasts[c] - gcs[c])).astype(BF16) for c in range(n_chunks)]
    head = lambda h: slice(h * HEAD_DIM, (h + 1) * HEAD_DIM)
    decays = [jnp.exp(jnp.where(causal, gcs[c][:, head(h)] - gc_ts[c][head(h), :], -jnp.inf)) for c, h in pairs]
    per_head = lambda t: [jnp.where(head_lanes[h], t, 0.0) for h in range(N_HEADS)]
    kq = [_dot_nt(jnp.concatenate(per_head(k_betas[c]) + per_head(chunk(q, c)), axis=0).astype(BF16), k_bfs[c])
          for c in range(n_chunks)]
    kks = [kq[c][h * cl:(h + 1) * cl] for c, h in pairs]
    qks = [kq[c][(N_HEADS + h) * cl:(N_HEADS + h + 1) * cl] for c, h in pairs]
    intras = [jnp.where(causal, qk * d, 0.0) for qk, d in zip(qks, decays)]
    powers = [jnp.where(strict, -(kk * d), 0.0) for kk, d in zip(kks, decays)]
    t_invs = [eye + p for p in powers]
    for _ in range(5):
        p_bfs = [p.astype(BF16) for p in powers]
        powers = [_dot(p, p) for p in p_bfs]
        t_invs = [t + _dot(t.astype(BF16), p.astype(BF16)) for t, p in zip(t_invs, powers)]
    tv = [_dot(jnp.concatenate(t_invs[c * N_HEADS:(c + 1) * N_HEADS], axis=0).astype(BF16),
               jnp.concatenate([v_betas[c], kbgs[c]], axis=1)) for c in range(n_chunks)]

    def merge_heads(stacked, lanes):
        out = jnp.zeros((cl, gw), F32)
        for h in range(N_HEADS):
            out = jnp.where(head_lanes[h], stacked[h * cl:(h + 1) * cl, lanes], out)
        return out

    us = [merge_heads(tv[c], slice(0, gw)) for c in range(n_chunks)]
    ws = [merge_heads(tv[c], slice(gw, 2 * gw)).astype(BF16) for c in range(n_chunks)]
    bd_f = bd.astype(F32)
    tn = (((0,), (0,)), ((), ()))
    kws = [(bd_f * lax.dot_general(k_decs[c], ws[c], tn, preferred_element_type=F32)).astype(BF16)
           for c in range(n_chunks)]
    kus = [bd_f * lax.dot_general(k_decs[c], us[c].astype(BF16), tn, preferred_element_type=F32)
           for c in range(n_chunks)]
    intra_cat = [jnp.concatenate([intras[c * N_HEADS + h] for h in range(N_HEADS)], axis=1).astype(BF16)
                 for c in range(n_chunks)]

    state = state_ref[...]
    outs = []
    for c in range(n_chunks):
        on_state = _dot(jnp.concatenate([ws[c], q_decs[c], kws[c]], axis=0), state.astype(BF16))
        v_new = us[c] - on_state[0:cl]
        v_stack = jnp.concatenate(per_head(v_new), axis=0).astype(BF16)
        outs.append(on_state[cl:2 * cl] + _dot(intra_cat[c], v_stack))
        state = state * jnp.exp(g_lasts[c]) - on_state[2 * cl:] + kus[c]
    state_ref[...] = state

    o = jnp.concatenate(outs, axis=0)
    o = o * lax.rsqrt(per_head_sum(o * o) * (1.0 / HEAD_DIM) + RMS_EPS) * nw_ref[...]
    o_ref[...] = (o * _silu(z_ref[...])).astype(o_ref.dtype)


def gdn_constants():
    lane_head = np.arange(GROUP_WIDTH) // HEAD_DIM
    bd = (lane_head[:, None] == lane_head[None, :]).astype(np.float32)
    bexp = np.zeros((LANES, 2 * GROUP_WIDTH), np.float32)
    for h in range(N_HEADS):
        bexp[h, h * HEAD_DIM:(h + 1) * HEAD_DIM] = 1.0
        bexp[N_HEADS + h, GROUP_WIDTH + h * HEAD_DIM:GROUP_WIDTH + (h + 1) * HEAD_DIM] = 1.0
    r = np.arange(GDN_ROWS)
    tril = ((r[:, None] // DN_CHUNK == r[None, :] // DN_CHUNK) & (r[None, :] <= r[:, None])).astype(np.float32)
    return bexp, bd, tril


def gated_deltanet(qkv, z, ba, conv_w, a_log_rep, dt_bias_rep, norm_w_rep, bexp, bd, tril):
    b, s, w3 = qkv.shape
    gw = GROUP_WIDTH
    blk = lambda width: pl.BlockSpec((None, GDN_ROWS, width), lambda bi, c: (bi, c, 0))
    const = lambda shape: pl.BlockSpec(shape, lambda bi, c: (0, 0))
    as_bf16 = lambda t: jnp.asarray(t, BF16)
    return pl.pallas_call(
        _gdn_kernel,
        grid=(b, s // GDN_ROWS),
        in_specs=[blk(w3), blk(gw), blk(LANES), const(conv_w.shape), const((1, gw)), const((1, gw)),
                  const((1, gw)), const(bexp.shape), const(bd.shape), const(tril.shape)],
        out_specs=blk(gw),
        out_shape=jax.ShapeDtypeStruct((b, s, gw), BF16),
        scratch_shapes=[pltpu.VMEM((CONV_TAIL + GDN_ROWS, w3), F32), pltpu.VMEM((gw, gw), F32)],
        compiler_params=_params("parallel", "arbitrary"),
        name="gated_deltanet",
    )(qkv, z, ba, conv_w, a_log_rep, dt_bias_rep, norm_w_rep, as_bf16(bexp), as_bf16(bd), as_bf16(tril))


def _first_max(values):
    best = values[0]
    for v in values[1:]:
        best = jnp.maximum(best, v)
    taken = jnp.zeros_like(best)
    hot = []
    for v in values:
        h = jnp.where((v == best) & (taken < 0.5), 1.0, 0.0)
        taken = taken + h
        hot.append(h)
    return best, hot


def _softmax_rows(rows):
    m = rows[0]
    for r in rows[1:]:
        m = jnp.maximum(m, r)
    e = [jnp.exp(r - m) for r in rows]
    z = e[0]
    for t in e[1:]:
        z = z + t
    return [t / z for t in e]


def _out_proj_kernel(ya_ref, yb_ref, yc_ref, yd_ref, wo_ref, x_ref, g1_ref, gain_ref, sc_ref, sh_ref,
                     wr_ref, br_ref, xo_ref, h_ref, combt_ref):
    gw = GROUP_WIDTH
    y = _dot(ya_ref[...], wo_ref[0:gw, :])
    y = y + _dot(yb_ref[...], wo_ref[gw:2 * gw, :])
    y = y + _dot(yc_ref[...], wo_ref[2 * gw:3 * gw, :])
    y = y + _dot(yd_ref[...], wo_ref[3 * gw:4 * gw, :])
    x = x_ref[...] + g1_ref[...] * y
    xo_ref[...] = x
    h = _modulated_norm(x, gain_ref[...], sc_ref[...], sh_ref[...])
    h_hi, h_lo = _split(h)
    h_ref[...] = h_hi

    w_hi, w_lo = _split(wr_ref[...])
    on_h_hi = _dot_nt(jnp.concatenate([w_hi, w_lo], axis=0), h_hi)
    logits = on_h_hi[0:LANES] + (on_h_hi[LANES:2 * LANES] + _dot_nt(w_hi, h_lo)) + br_ref[...]
    ng, ne = N_EXPERT_GROUPS, EXPERTS_PER_GROUP
    p_group = _softmax_rows([logits[r:r + 1, :] for r in range(ng)])
    pg_top, g_hot = _first_max(p_group)
    e_logits = []
    for e in range(ne):
        t = g_hot[0] * logits[ng + e:ng + e + 1, :]
        for gi in range(1, ng):
            t = t + g_hot[gi] * logits[ng + gi * ne + e:ng + gi * ne + e + 1, :]
        e_logits.append(t)
    p_exp = _softmax_rows(e_logits)
    p1, hot1 = _first_max(p_exp)
    rest = [jnp.where(h1 > 0.5, -1.0, p) for p, h1 in zip(p_exp, hot1)]
    p2, hot2 = _first_max(rest)
    total = p1 + p2
    w_exp = [(h1 * (p1 / total) + h2 * (p2 / total)) * pg_top for h1, h2 in zip(hot1, hot2)]
    for gi in range(ng):
        for e in range(ne):
            combt_ref[gi * ne + e:gi * ne + e + 1, :] = g_hot[gi] * w_exp[e]


def out_projection(ys, w_out, x, g1, gain, scale, shift, w_router, b_router, seq, row_tile=512):
    n, d = x.shape
    w_router_t = w_router.T
    tiles_per_seq = seq // row_tile
    row = lambda i: (i, 0)
    per_batch = lambda i: (i // tiles_per_seq, 0, 0)
    const = lambda shape: pl.BlockSpec(shape, lambda i: (0, 0))
    mod = pl.BlockSpec((None, 1, d), per_batch)
    return pl.pallas_call(
        _out_proj_kernel,
        grid=(n // row_tile,),
        in_specs=[pl.BlockSpec((row_tile, GROUP_WIDTH), row)] * 4
        + [const(w_out.shape), pl.BlockSpec((row_tile, d), row), mod, const((1, d)), mod, mod,
           const(w_router_t.shape), const(b_router.shape)],
        out_specs=(pl.BlockSpec((row_tile, d), row), pl.BlockSpec((row_tile, d), row),
                   pl.BlockSpec((N_EXPERTS, row_tile), lambda i: (0, i))),
        out_shape=(jax.ShapeDtypeStruct((n, d), F32), jax.ShapeDtypeStruct((n, d), BF16),
                   jax.ShapeDtypeStruct((N_EXPERTS, n), F32)),
        compiler_params=_params("parallel"),
        name="out_projection",
    )(*ys, w_out, x, g1, gain, scale, shift, w_router_t, b_router)


MOE_ROWS = 192


def _moe_kernel(h_ref, combt_ref, before_ref, wg_ref, wu_ref, wd_ref, x_ref, g2_ref, fin_ref, o_ref,
                acc_ref, rank_ref, pick_ref, y_ref, *, final_norm):
    e = pl.program_id(1)
    tn = (((0,), (0,)), ((), ()))

    @pl.when(e == 0)
    def _():
        acc_ref[...] = jnp.zeros(acc_ref.shape, F32)
        routed = jnp.where(combt_ref[...] != 0.0, 1.0, 0.0).astype(BF16)
        rank_ref[...] = _dot(routed, before_ref[...])

    weight_row = combt_ref[pl.ds(e, 1), :]
    rank_row = rank_ref[pl.ds(e, 1), :]
    routed_row = weight_row != 0.0
    count = jnp.sum(jnp.where(routed_row, 1, 0))
    h = h_ref[...]

    def gathered_pass(j):
        slot = (j * MOE_ROWS + lax.broadcasted_iota(jnp.int32, (MOE_ROWS, 1), 0)).astype(F32)
        pick = jnp.where((rank_row == slot) & routed_row, 1.0, 0.0)
        pick_bf = pick.astype(BF16)
        rows = _dot(pick_bf, h).astype(BF16)
        weight = jnp.sum(pick * weight_row, axis=-1, keepdims=True)
        hid = _silu(_dot(rows, wg_ref[...])) * _dot(rows, wu_ref[...]) * weight
        return pick_bf, _dot(hid.astype(BF16), wd_ref[...]).astype(BF16)

    first = pl.ds(pl.multiple_of(e * MOE_ROWS, MOE_ROWS), MOE_ROWS)
    pick_ref[first, :], y_ref[first, :] = gathered_pass(0)

    def extra_pass(j, carry):
        pick_bf, y = gathered_pass(j)
        acc_ref[...] += lax.dot_general(pick_bf, y, tn, preferred_element_type=F32)
        return carry

    lax.fori_loop(1, (count + MOE_ROWS - 1) // MOE_ROWS, extra_pass, 0)

    @pl.when(e == N_EXPERTS - 1)
    def _():
        moe = acc_ref[...] + lax.dot_general(pick_ref[...], y_ref[...], tn, preferred_element_type=F32)
        x = x_ref[...] + g2_ref[...] * moe
        if final_norm:
            x = x * lax.rsqrt(jnp.mean(x * x, axis=-1, keepdims=True) + RMS_EPS) * fin_ref[...]
        o_ref[...] = x


def moe_experts(h, combt, w_gate, w_up, w_down, x, g2, fin_gain, seq, final_norm, row_tile=1024):
    n, d = x.shape
    f = w_gate.shape[-1]
    tiles_per_seq = seq // row_tile
    row = lambda i, e: (i, 0)
    before = np.triu(np.ones((row_tile, row_tile), np.float32), 1)
    return pl.pallas_call(
        functools.partial(_moe_kernel, final_norm=final_norm),
        grid=(n // row_tile, N_EXPERTS),
        in_specs=[pl.BlockSpec((row_tile, d), row),
                  pl.BlockSpec((N_EXPERTS, row_tile), lambda i, e: (0, i)),
                  pl.BlockSpec((row_tile, row_tile), lambda i, e: (0, 0)),
                  pl.BlockSpec((None, d, f), lambda i, e: (e, 0, 0)),
                  pl.BlockSpec((None, d, f), lambda i, e: (e, 0, 0)),
                  pl.BlockSpec((None, f, d), lambda i, e: (e, 0, 0)),
                  pl.BlockSpec((row_tile, d), row),
                  pl.BlockSpec((None, 1, d), lambda i, e: (i // tiles_per_seq, 0, 0)),
                  pl.BlockSpec((1, d), lambda i, e: (0, 0))],
        out_specs=pl.BlockSpec((row_tile, d), row),
        out_shape=jax.ShapeDtypeStruct((n, d), F32),
        scratch_shapes=[pltpu.VMEM((row_tile, d), F32), pltpu.VMEM((N_EXPERTS, row_tile), F32),
                        pltpu.VMEM((N_EXPERTS * MOE_ROWS, row_tile), BF16),
                        pltpu.VMEM((N_EXPERTS * MOE_ROWS, d), BF16)],
        compiler_params=_params("parallel", "arbitrary"),
        name="moe_experts",
    )(h, combt, jnp.asarray(before, BF16), w_gate, w_up, w_down, x, g2, fin_gain)


def kernel(x, c, norm_mix, norm_ffn, final_norm, w_ada, b_ada, w_in, w_out, attn_sink, w_pool, pool_scale, cmp_pos, cmp_w1, cmp_w2, conv_w, a_log, dt_bias, dn_norm, w_route_group, b_route_group, w_route_expert, b_route_expert, w_gate, w_up, w_down):
    batch, seq, d = x.shape
    depth = w_in.shape[0]
    n = batch * seq
    cover, gexp, expand = nsa_constants(seq)
    bexp, bd, tril = gdn_constants()
    mod = ada_modulation(c, w_ada, b_ada)
    spread = lambda t: jnp.repeat(t, HEAD_DIM).reshape(1, GROUP_WIDTH)
    seq3 = lambda t: t.reshape(batch, seq, t.shape[-1])
    xf = x.reshape(n, d)
    for l in range(depth):
        sh1, sc1, g1, sh2, sc2, g2 = (mod[l, :, i * d:(i + 1) * d].reshape(batch, 1, d) for i in range(6))
        p = in_projection(xf, norm_mix[l].reshape(1, d), sc1, sh1, pack_in_weights(w_in[l]), seq)
        y_a = swa_attention(seq3(p["a_q"]), seq3(p["a_k"]), seq3(p["a_v"]), attn_sink[l])
        w_pool_bd = jax.scipy.linalg.block_diag(*[w_pool[l, gi] for gi in range(len(POOL_WINDOWS))])
        y_b = multiscale_pool(seq3(p["b_u"]), w_pool_bd.astype(BF16), pool_scale[l].reshape(1, GROUP_WIDTH))
        kc, vc = nsa_compress(seq3(p["c_cmp"]), *pack_compress_weights(cmp_pos[l], cmp_w1[l], cmp_w2[l]))
        y_c = nsa_attention(seq3(p["c_q"]), seq3(p["c_gate"]), kc, vc, seq3(p["c_ksel"]), seq3(p["c_vsel"]),
                            seq3(p["c_kwin"]), seq3(p["c_vwin"]), cover, gexp, expand)
        y_d = gated_deltanet(seq3(p["d_qkv"]), seq3(p["d_z"]), seq3(p["d_ba"]), conv_w[l], spread(a_log[l]),
                             spread(dt_bias[l]), jnp.tile(dn_norm[l], N_HEADS).reshape(1, GROUP_WIDTH), bexp, bd, tril)
        ys = [t.reshape(n, GROUP_WIDTH) for t in (y_a, y_b, y_c, y_d)]
        n_logits = N_EXPERT_GROUPS + N_EXPERTS
        w_router = jnp.pad(jnp.concatenate([w_route_group[l], w_route_expert[l]], axis=1),
                           ((0, 0), (0, LANES - n_logits)))
        b_router = jnp.pad(jnp.concatenate([b_route_group[l], b_route_expert[l]]),
                           (0, LANES - n_logits)).reshape(LANES, 1)
        xf, h2, comb = out_projection(ys, w_out[l].astype(BF16), xf, g1, norm_ffn[l].reshape(1, d), sc2, sh2,
                                      w_router, b_router, seq)
        f = w_gate.shape[-1]
        xf = moe_experts(h2, comb, w_gate[l].reshape(N_EXPERTS, d, f).astype(BF16),
                         w_up[l].reshape(N_EXPERTS, d, f).astype(BF16),
                         w_down[l].reshape(N_EXPERTS, f, d).astype(BF16),
                         xf, g2, final_norm.reshape(1, d), seq, final_norm=(l == depth - 1))
    return xf.reshape(batch, seq, d)
```

```python
import functools

import numpy as np
import jax
import jax.numpy as jnp
from jax import lax
from jax.experimental import pallas as pl
from jax.experimental.pallas import tpu as pltpu

F32 = jnp.float32
BF16 = jnp.bfloat16
HI = lax.Precision.HIGHEST

HEAD_DIM = 64
N_HEADS = 4
GROUP_WIDTH = N_HEADS * HEAD_DIM
BLOCK = 128
RMS_EPS = 1e-6
SWA_WINDOW = 128
SWA_BLOCKS = 4
POOL_WINDOWS = (2, 4, 8, 16)
POOL_HALO = 16
CMP_LEN = 32
CMP_STRIDE = 16
SEL_BLOCK = 64
NSA_TOP_N = 16
NSA_WINDOW = 512
NSA_FORCE = 1e4
SEL_CHUNK = 1024
NSA_BLOCKS = 2
DN_CONV = 4
DN_CHUNK = 64
N_EXPERT_GROUPS = 4
EXPERTS_PER_GROUP = 4
N_EXPERTS = N_EXPERT_GROUPS * EXPERTS_PER_GROUP
LANES = 128
MASKED = -1e30
QK_SCALE = HEAD_DIM ** -0.5 * float(np.log2(np.e))
VMEM_LIMIT = 56 * 1024 * 1024

SEG_WIDTHS = (
    ("a_q", 256), ("a_k", 256), ("a_v", 256), ("c_q", 256),
    ("c_ksel", 128), ("c_vsel", 128), ("c_kwin", 128), ("c_vwin", 128),
    ("b_u", 256), ("c_cmp", 128), ("c_gate", 128), ("d_qkv", 768), ("d_z", 256), ("d_ba", 128),
)
SEG_BF16 = ("a_q", "a_k", "a_v", "c_q", "c_ksel", "c_vsel", "c_kwin", "c_vwin")
SEG_OFFSETS = {}
_off = 0
for _name, _w in SEG_WIDTHS:
    SEG_OFFSETS[_name] = (_off, _w)
    _off += _w
PACKED_WIDTH = _off


def _sigmoid(x):
    return 1.0 / (1.0 + jnp.exp(-x))


def _silu(x):
    return x * _sigmoid(x)


def _dot(a, b, precision=None):
    return jnp.dot(a, b, precision=precision, preferred_element_type=F32)


def _dot_nt(a, b, precision=None):
    return lax.dot_general(a, b, (((1,), (1,)), ((), ())), precision=precision,
                           preferred_element_type=F32)


def _split(a):
    hi = a.astype(BF16)
    return hi, (a - hi.astype(F32)).astype(BF16)


def _dot_split_lhs(a, b):
    hi, lo = _split(a)
    return _dot(hi, b) + _dot(lo, b)


def _params(*semantics):
    return pltpu.CompilerParams(dimension_semantics=semantics, vmem_limit_bytes=VMEM_LIMIT)


def _ada_kernel(c_ref, w_ref, b_ref, o_ref):
    cond = _silu(c_ref[...])
    o_ref[...] = _dot(cond, w_ref[...], HI) + b_ref[...]


def ada_modulation(c, w_ada, b_ada, col_tile=1536):
    depth, d, width = w_ada.shape
    b = c.shape[0]
    return pl.pallas_call(
        _ada_kernel,
        grid=(depth, width // col_tile),
        in_specs=[
            pl.BlockSpec((b, d), lambda l, j: (0, 0)),
            pl.BlockSpec((None, d, col_tile), lambda l, j: (l, 0, j)),
            pl.BlockSpec((None, 1, col_tile), lambda l, j: (l, 0, j)),
        ],
        out_specs=pl.BlockSpec((None, b, col_tile), lambda l, j: (l, 0, j)),
        out_shape=jax.ShapeDtypeStruct((depth, b, width), F32),
        compiler_params=_params("parallel", "parallel"),
        name="ada_modulation",
    )(c, w_ada, b_ada.reshape(depth, 1, width))


def _modulated_norm(x, gain, scale, shift):
    y = x * lax.rsqrt(jnp.mean(x * x, axis=-1, keepdims=True) + RMS_EPS)
    return y * gain * (1.0 + scale) + shift


def _in_proj_kernel(x_ref, gain_ref, sc_ref, sh_ref, w_ref, *out_refs):
    h = _modulated_norm(x_ref[...], gain_ref[...], sc_ref[...], sh_ref[...]).astype(BF16)
    for (name, _), o_ref in zip(SEG_WIDTHS, out_refs):
        off, width = SEG_OFFSETS[name]
        o_ref[...] = _dot(h, w_ref[:, off:off + width]).astype(o_ref.dtype)


def in_projection(x, gain, scale, shift, w_packed, seq, row_tile=512):
    n, d = x.shape
    tiles_per_seq = seq // row_tile
    row = lambda i: (i, 0)
    per_batch = lambda i: (i // tiles_per_seq, 0, 0)
    out_shape = tuple(
        jax.ShapeDtypeStruct((n, w), BF16 if name in SEG_BF16 else F32) for name, w in SEG_WIDTHS)
    outs = pl.pallas_call(
        _in_proj_kernel,
        grid=(n // row_tile,),
        in_specs=[
            pl.BlockSpec((row_tile, d), row),
            pl.BlockSpec((1, d), lambda i: (0, 0)),
            pl.BlockSpec((None, 1, d), per_batch),
            pl.BlockSpec((None, 1, d), per_batch),
            pl.BlockSpec((d, PACKED_WIDTH), lambda i: (0, 0)),
        ],
        out_specs=tuple(pl.BlockSpec((row_tile, w), row) for _, w in SEG_WIDTHS),
        out_shape=out_shape,
        compiler_params=_params("parallel"),
        name="in_projection",
    )(x, gain, scale, shift, w_packed)
    return dict(zip((name for name, _ in SEG_WIDTHS), outs))


def pack_in_weights(w_in):
    gw, hd = GROUP_WIDTH, HEAD_DIM
    sizes = (gw, 2 * hd, 2 * hd, gw, gw, 6 * hd, 3 * N_HEADS, 3 * gw, gw, N_HEADS, N_HEADS)
    offs = np.concatenate([[0], np.cumsum(sizes)])
    (a_q, a_k, a_v, b_u, c_q, c_kv, c_gate, d_qkv, d_z, d_beta, d_a) = (
        w_in[:, offs[i]:offs[i + 1]] for i in range(len(sizes)))
    d = w_in.shape[0]
    dup = lambda t: jnp.concatenate([t, t], axis=1)
    pad = lambda t: jnp.pad(t, ((0, 0), (0, LANES - t.shape[1])))
    k_cmp, v_cmp, k_sel, v_sel, k_win, v_win = (c_kv[:, i * hd:(i + 1) * hd] for i in range(6))
    a_q = a_q * QK_SCALE
    c_q = c_q * QK_SCALE
    segs = {
        "a_q": a_q,
        "a_k": jnp.concatenate([dup(a_k[:, :hd]), dup(a_k[:, hd:])], axis=1),
        "a_v": jnp.concatenate([dup(a_v[:, :hd]), dup(a_v[:, hd:])], axis=1),
        "c_q": c_q,
        "c_ksel": dup(k_sel), "c_vsel": dup(v_sel), "c_kwin": dup(k_win), "c_vwin": dup(v_win),
        "b_u": b_u,
        "c_cmp": jnp.concatenate([k_cmp, v_cmp], axis=1),
        "c_gate": pad(c_gate),
        "d_qkv": d_qkv, "d_z": d_z,
        "d_ba": pad(jnp.concatenate([d_beta, d_a], axis=1)),
    }
    packed = jnp.concatenate([segs[name] for name, _ in SEG_WIDTHS], axis=1)
    assert packed.shape == (d, PACKED_WIDTH)
    return packed.astype(BF16)


def _stack_heads(slab):
    lane = lax.broadcasted_iota(jnp.int32, slab.shape, 1)
    zero = jnp.zeros_like(slab)
    return jnp.concatenate([jnp.where(lane < HEAD_DIM, slab, zero),
                            jnp.where(lane >= HEAD_DIM, slab, zero)], axis=0)


def _unstack_heads(o, rows):
    lane = lax.broadcasted_iota(jnp.int32, (rows, LANES), 1)
    return jnp.where(lane < HEAD_DIM, o[0:rows], o[rows:2 * rows])


def _swa_kernel(sink_ref, q_ref, kp_ref, kc_ref, vp_ref, vc_ref, o_ref):
    i = pl.program_id(1)
    row = lax.broadcasted_iota(jnp.int32, (2 * BLOCK, 2 * BLOCK), 0)
    col = lax.broadcasted_iota(jnp.int32, (2 * BLOCK, 2 * BLOCK), 1)
    tq = row & (BLOCK - 1)
    tk = col - BLOCK
    banded = (tk <= tq) & (tk > tq - SWA_WINDOW)
    first_valid = banded & ((col >= BLOCK) | (i > 0))
    rowh = lax.broadcasted_iota(jnp.int32, (2 * BLOCK, 1), 0)
    log2e = float(np.log2(np.e))
    sinks = [jnp.where(rowh < BLOCK, sink_ref[2 * j], sink_ref[2 * j + 1]) * log2e for j in range(N_HEADS // 2)]
    pairs = [(sb, j) for sb in range(SWA_BLOCKS) for j in range(N_HEADS // 2)]

    def keys(cur_ref, prev_ref, sb, lanes):
        if sb == 0:
            return jnp.concatenate([prev_ref[:, lanes], cur_ref[0:BLOCK, lanes]], axis=0)
        return cur_ref[(sb - 1) * BLOCK:(sb + 1) * BLOCK, lanes]

    lanes_of = lambda j: slice(j * LANES, (j + 1) * LANES)
    s = [jnp.where(first_valid if sb == 0 else banded,
                   _dot_nt(_stack_heads(q_ref[sb * BLOCK:(sb + 1) * BLOCK, lanes_of(j)]),
                           keys(kc_ref, kp_ref, sb, lanes_of(j))), -jnp.inf) for sb, j in pairs]
    m = [jnp.maximum(jnp.max(t, axis=-1, keepdims=True), sinks[j]) for t, (sb, j) in zip(s, pairs)]
    p = [jnp.exp2(t - mm) for t, mm in zip(s, m)]
    denom = [jnp.sum(t, axis=-1, keepdims=True) + jnp.exp2(sinks[j] - mm) for t, mm, (sb, j) in zip(p, m, pairs)]
    o = [_dot(t.astype(BF16), keys(vc_ref, vp_ref, sb, lanes_of(j))) / d for t, d, (sb, j) in zip(p, denom, pairs)]
    for sb in range(SWA_BLOCKS):
        slabs = [_unstack_heads(o[sb * (N_HEADS // 2) + j], BLOCK) for j in range(N_HEADS // 2)]
        o_ref[sb * BLOCK:(sb + 1) * BLOCK, :] = jnp.concatenate(slabs, axis=1).astype(o_ref.dtype)


def swa_attention(q, k, v, sink):
    b, s, w = q.shape
    rows = SWA_BLOCKS * BLOCK
    cur = pl.BlockSpec((None, rows, w), lambda bi, i: (bi, i, 0))
    prev = pl.BlockSpec((None, BLOCK, w), lambda bi, i: (bi, jnp.maximum(i * SWA_BLOCKS - 1, 0), 0))
    return pl.pallas_call(
        _swa_kernel,
        grid=(b, s // rows),
        in_specs=[pl.BlockSpec(memory_space=pltpu.SMEM), cur, prev, cur, prev, cur],
        out_specs=cur,
        out_shape=jax.ShapeDtypeStruct((b, s, w), BF16),
        compiler_params=_params("parallel", "parallel"),
        name="swa_attention",
    )(sink, q, k, k, v, v)


def _pool_kernel(up_ref, u_ref, w_ref, scale_ref, o_ref, ext_ref):
    i = pl.program_id(1)
    rows = u_ref.shape[0]
    u = u_ref[...]
    halo = up_ref[...]
    ext_ref[0:POOL_HALO, :] = jnp.where(i > 0, halo, jnp.zeros_like(halo))
    ext_ref[POOL_HALO:POOL_HALO + rows, :] = u
    lane = lax.broadcasted_iota(jnp.int32, u.shape, 1)
    pos = i * rows + lax.broadcasted_iota(jnp.int32, u.shape, 0)
    group_ch = GROUP_WIDTH // len(POOL_WINDOWS)
    total = u
    d = jnp.zeros_like(u)
    width = 1
    for gi, w in enumerate(POOL_WINDOWS):
        while width < w:
            total = total + ext_ref[pl.ds(POOL_HALO - width, rows), :]
            width += 1
        cnt = jnp.minimum(pos + 1, w).astype(F32)
        in_group = (lane >= gi * group_ch) & (lane < (gi + 1) * group_ch)
        d = jnp.where(in_group, total / cnt - u, d)
    o_ref[...] = (_dot(d.astype(BF16), w_ref[...]) * scale_ref[...]).astype(o_ref.dtype)


def multiscale_pool(u, w_blockdiag, pool_scale, row_tile=512):
    b, s, w = u.shape
    halo_per_tile = row_tile // POOL_HALO
    return pl.pallas_call(
        _pool_kernel,
        grid=(b, s // row_tile),
        in_specs=[
            pl.BlockSpec((None, POOL_HALO, w), lambda bi, i: (bi, jnp.maximum(i * halo_per_tile - 1, 0), 0)),
            pl.BlockSpec((None, row_tile, w), lambda bi, i: (bi, i, 0)),
            pl.BlockSpec((w, w), lambda bi, i: (0, 0)),
            pl.BlockSpec((1, w), lambda bi, i: (0, 0)),
        ],
        out_specs=pl.BlockSpec((None, row_tile, w), lambda bi, i: (bi, i, 0)),
        out_shape=jax.ShapeDtypeStruct((b, s, w), BF16),
        scratch_shapes=[pltpu.VMEM((POOL_HALO + row_tile, w), F32)],
        compiler_params=_params("parallel", "parallel"),
        name="multiscale_pool",
    )(u, u, w_blockdiag, pool_scale)


def _compress_kernel(x_ref, w1_ref, pos_ref, w2k_ref, w2v_ref, kc_ref, vc_ref):
    n_chunks = x_ref.shape[0]
    both = _dot(x_ref[...], w1_ref[...], HI)
    pre = both[:, 0:LANES] + pltpu.roll(both[:, LANES:2 * LANES], n_chunks - 1, 0) + pos_ref[...]
    hid = _silu(pre)
    kc_ref[...] = _dot(hid, w2k_ref[...], HI).astype(kc_ref.dtype)
    vc_ref[...] = _dot(hid, w2v_ref[...], HI).astype(vc_ref.dtype)


def nsa_compress(cmp_in, w1_packed, pos_term, w2k, w2v):
    b, s, w = cmp_in.shape
    n_chunks = s // CMP_STRIDE
    flat = cmp_in.reshape(b, n_chunks, CMP_STRIDE * w)
    const = lambda shape: pl.BlockSpec(shape, lambda bi: tuple(0 for _ in shape))
    out = jax.ShapeDtypeStruct((b, n_chunks, LANES), BF16)
    return pl.pallas_call(
        _compress_kernel,
        grid=(b,),
        in_specs=[pl.BlockSpec((None, n_chunks, CMP_STRIDE * w), lambda bi: (bi, 0, 0)),
                  const(w1_packed.shape), const(pos_term.shape), const(w2k.shape), const(w2v.shape)],
        out_specs=(pl.BlockSpec((None, n_chunks, LANES), lambda bi: (bi, 0, 0)),) * 2,
        out_shape=(out, out),
        compiler_params=_params("parallel"),
        name="nsa_compress",
    )(flat, w1_packed, pos_term, w2k, w2v)


def pack_compress_weights(cmp_pos, cmp_w1, cmp_w2):
    hd, half = HEAD_DIM, CMP_LEN // 2
    w1 = cmp_w1.reshape(2, 2, half, hd, hd)
    zeros = jnp.zeros((half, hd, hd), F32)
    halves = []
    for part in range(2):
        wk = jnp.concatenate([w1[0, part], zeros], axis=-1)
        wv = jnp.concatenate([zeros, w1[1, part]], axis=-1)
        halves.append(jnp.concatenate([wk, wv], axis=1).reshape(half * 2 * hd, 2 * hd))
    w1_packed = jnp.concatenate(halves, axis=1)
    pos_flat = cmp_pos.reshape(2, 1, CMP_LEN * hd)
    pos_term = jnp.concatenate([jnp.matmul(pos_flat[0], cmp_w1[0], precision=HI),
                                jnp.matmul(pos_flat[1], cmp_w1[1], precision=HI)], axis=1)
    zero2 = jnp.zeros((hd, 2 * hd), F32)
    w2k = jnp.concatenate([jnp.concatenate([cmp_w2[0], cmp_w2[0]], axis=1), zero2], axis=0)
    w2v = jnp.concatenate([zero2, jnp.concatenate([cmp_w2[1], cmp_w2[1]], axis=1)], axis=0)
    return w1_packed, pos_term, w2k, w2v


def _nsa_kernel(q_ref, gate_ref, kc_ref, vc_ref, ksel_ref, vsel_ref, kwin_ref, vwin_ref,
                cover_ref, gexp_ref, expand_ref, o_ref, q4_ref, m_ref, acc_ref, sa_ref, sb_ref, ma_ref, mb_ref):
    i = pl.program_id(1)
    nqb = NSA_BLOCKS
    groups = [(b, h) for b in range(nqb) for h in range(N_HEADS)]
    group_rows = [slice(g * BLOCK, (g + 1) * BLOCK) for g in range(len(groups))]
    q4 = jnp.concatenate([_stack_heads(q_ref[b * BLOCK:(b + 1) * BLOCK, s * LANES:(s + 1) * LANES])
                          for b in range(nqb) for s in range(N_HEADS // 2)], axis=0)
    tq_blk = [(i * nqb + b) * BLOCK + lax.broadcasted_iota(jnp.int32, (BLOCK, 1), 0) for b in range(nqb)]

    n_cmp = kc_ref.shape[0]
    n_idx = lax.broadcasted_iota(jnp.int32, (1, n_cmp), 1)
    valid_c = [(n_idx * CMP_STRIDE + (CMP_LEN - 1) <= tq_blk[b]) & (n_idx < n_cmp - 1) for b in range(nqb)]
    k_c = kc_ref[...]
    v_c = vc_ref[...]
    s_c = [jnp.where(valid_c[b], _dot_nt(q4[gr], k_c), -jnp.inf) for (b, h), gr in zip(groups, group_rows)]
    m_c = [jnp.max(t, axis=-1, keepdims=True) for t in s_c]
    m_c = [jnp.where(t == -jnp.inf, 0.0, t) for t in m_c]
    p_c = [jnp.exp2(t - mm) for t, mm in zip(s_c, m_c)]
    d_c = [jnp.sum(t, axis=-1, keepdims=True) for t in p_c]
    p_c = [t / jnp.where(dd > 0, dd, 1.0) for t, dd in zip(p_c, d_c)]
    o_c = [_dot(t.astype(BF16), v_c) for t in p_c]

    value_lane = lax.broadcasted_iota(jnp.int32, (1, LANES), 1) < HEAD_DIM

    def with_ones(v):
        return jnp.where(value_lane, v, jnp.ones_like(v))

    span = NSA_WINDOW + BLOCK
    start_w = [pl.multiple_of(jnp.maximum((i * nqb + b) * BLOCK - NSA_WINDOW, 0), BLOCK) for b in range(nqb)]
    tk = [start_w[b] + lax.broadcasted_iota(jnp.int32, (1, span), 1) for b in range(nqb)]
    valid_w = [(tk[b] <= tq_blk[b]) & (tk[b] > tq_blk[b] - NSA_WINDOW) for b in range(nqb)]
    k_w = [kwin_ref[pl.ds(start_w[b], span), :] for b in range(nqb)]
    v_w = [with_ones(vwin_ref[pl.ds(start_w[b], span), :]) for b in range(nqb)]
    acc_w = []

    def window_group(g):
        b = groups[g][0]
        s_w = jnp.where(valid_w[b], _dot_nt(q4[group_rows[g]], k_w[b]), -jnp.inf)
        p_w = jnp.exp2(s_w - jnp.max(s_w, axis=-1, keepdims=True))
        acc_w.append(_dot(p_w.astype(BF16), v_w[b]))

    cover = cover_ref[...]
    importance = []
    for b in range(nqb):
        p_b = p_c[b * N_HEADS:(b + 1) * N_HEADS]
        p_hi, p_lo = _split((p_b[0] + p_b[1]) + (p_b[2] + p_b[3]))
        importance.append(_dot_nt(cover, p_hi) + _dot_nt(cover, p_lo))
    importance = jnp.concatenate(importance, axis=1)
    n_sel = cover_ref.shape[0]
    n_q = nqb * BLOCK
    blk = lax.broadcasted_iota(jnp.int32, (n_sel, n_q), 0)
    t_lane = i * n_q + lax.broadcasted_iota(jnp.int32, (n_sel, n_q), 1)
    cur = t_lane // SEL_BLOCK
    causal = blk * SEL_BLOCK <= t_lane
    forced = (blk == 0) | (blk == cur) | (blk == cur - 1)
    score = jnp.where(causal, jnp.where(forced, NSA_FORCE, importance), -jnp.inf)
    blk_f = blk.astype(F32)
    chosen = jnp.where(forced, 1.0, 0.0)
    score = jnp.where(forced, -jnp.inf, score)
    n_rounds = NSA_TOP_N - 3
    n_groups = len(groups)
    for r in range(n_rounds):
        for g in range(r * n_groups // n_rounds, (r + 1) * n_groups // n_rounds):
            window_group(g)
        top = jnp.max(score, axis=0, keepdims=True)
        first = jnp.min(jnp.where(score == top, blk_f, float(n_sel)), axis=0, keepdims=True)
        pick = blk_f == first
        score = jnp.where(pick, -jnp.inf, score)
        chosen = jnp.where(pick, 1.0, chosen)
    chosen = jnp.where(causal, chosen, 0.0)
    chosen_q = [chosen[:, b * BLOCK:(b + 1) * BLOCK].T.astype(BF16) for b in range(nqb)]

    q4_ref[...] = q4
    m_ref[...] = jnp.full(m_ref.shape, MASKED, F32)
    acc_ref[...] = jnp.zeros(acc_ref.shape, F32)
    n_chunks = (i * n_q + n_q - 1) // SEL_CHUNK + 1

    def chunk_start(c):
        return pl.multiple_of(jnp.minimum(c, n_chunks - 1) * SEL_CHUNK, SEL_CHUNK)

    def chunk_keep(c):
        start = chunk_start(c)
        key = start + lax.broadcasted_iota(jnp.int32, (1, SEL_CHUNK), 1)
        spread = expand_ref[:, pl.ds(start, SEL_CHUNK)]
        return [jnp.where((key <= tq_blk[b]) & (c < n_chunks), _dot(chosen_q[b], spread), 0.0) > 0.5
                for b in range(nqb)]

    def stage_group(g, bufs, s_all, keep):
        s = jnp.where(keep[groups[g][0]], s_all[group_rows[g]], MASKED)
        bufs[0][group_rows[g]] = s
        bufs[1][group_rows[g]] = jnp.max(s, axis=-1, keepdims=True)

    def half_step(c, cur_bufs, next_bufs):
        keep_next = chunk_keep(c + 1)
        v_aug = with_ones(vsel_ref[pl.ds(chunk_start(c), SEL_CHUNK), :])
        s_next = _dot_nt(q4_ref[...], ksel_ref[pl.ds(chunk_start(c + 1), SEL_CHUNK), :])
        probs, m_olds, m_news = [], [], []
        for g in range(n_groups):
            stage_group(g, next_bufs, s_next, keep_next)
            m_old = m_ref[group_rows[g]]
            m_new = jnp.maximum(m_old, cur_bufs[1][group_rows[g]])
            m_ref[group_rows[g]] = m_new
            probs.append(jnp.exp2(cur_bufs[0][group_rows[g]] - m_new).astype(BF16))
            m_olds.append(m_old)
            m_news.append(m_new)
        pv = _dot(jnp.concatenate(probs, axis=0), v_aug)
        alpha = jnp.exp2(jnp.concatenate(m_olds, axis=0) - jnp.concatenate(m_news, axis=0))
        acc_ref[...] = alpha * acc_ref[...] + pv

    bufs_a = (sa_ref, ma_ref)
    bufs_b = (sb_ref, mb_ref)
    keep0 = chunk_keep(0)
    s_first = _dot_nt(q4, ksel_ref[0:SEL_CHUNK, :])
    for g in range(n_groups):
        stage_group(g, bufs_a, s_first, keep0)

    def sel_step(t, carry):
        half_step(2 * t, bufs_a, bufs_b)

        @pl.when(2 * t + 1 < n_chunks)
        def _():
            half_step(2 * t + 1, bufs_b, bufs_a)

        return carry

    lax.fori_loop(0, (n_chunks + 1) // 2, sel_step, 0)

    def heads_to_lanes(heads):
        return jnp.concatenate([_unstack_heads(jnp.concatenate(heads[2 * j:2 * j + 2], axis=0), BLOCK)
                                for j in range(N_HEADS // 2)], axis=1)

    def normalized_heads_to_lanes(heads):
        slabs = []
        for j in range(N_HEADS // 2):
            even, odd = heads[2 * j], heads[2 * j + 1]
            numer = jnp.where(value_lane, even, pltpu.roll(odd, HEAD_DIM, 1))
            denom = jnp.where(value_lane, pltpu.roll(even, HEAD_DIM, 1), odd)
            slabs.append(numer / denom)
        return jnp.concatenate(slabs, axis=1)

    gates = _dot_split_lhs(_sigmoid(gate_ref[...]), gexp_ref[...])
    gw = GROUP_WIDTH
    for b in range(nqb):
        of_block = slice(b * N_HEADS, (b + 1) * N_HEADS)
        qr = slice(b * BLOCK, (b + 1) * BLOCK)
        acc_sel = [acc_ref[gr] for gr in group_rows[of_block]]
        out = (gates[qr, 0:gw] * heads_to_lanes(o_c[of_block])
               + gates[qr, gw:2 * gw] * normalized_heads_to_lanes(acc_sel)
               + gates[qr, 2 * gw:3 * gw] * normalized_heads_to_lanes(acc_w[of_block]))
        o_ref[qr, :] = out.astype(o_ref.dtype)


def nsa_constants(seq):
    n_cmp_rows = seq // CMP_STRIDE
    n_sel = seq // SEL_BLOCK
    cmp_start = np.arange(n_cmp_rows) * CMP_STRIDE
    sel_start = np.arange(n_sel) * SEL_BLOCK
    cover = np.maximum(np.minimum(cmp_start[None, :] + CMP_LEN, sel_start[:, None] + SEL_BLOCK)
                       - np.maximum(cmp_start[None, :], sel_start[:, None]), 0).astype(np.float32) / CMP_LEN
    gexp = np.zeros((LANES, 3 * GROUP_WIDTH), np.float32)
    for h in range(N_HEADS):
        for br in range(3):
            gexp[h * 3 + br, br * GROUP_WIDTH + h * HEAD_DIM: br * GROUP_WIDTH + (h + 1) * HEAD_DIM] = 1.0
    expand = (np.arange(n_sel)[:, None] == np.arange(seq)[None, :] // SEL_BLOCK).astype(np.float32)
    return cover, gexp, expand


def nsa_attention(q, gate, kc, vc, ksel, vsel, kwin, vwin, cover, gexp, expand):
    b, s, w = q.shape
    assert s >= NSA_WINDOW + BLOCK and s % SEL_CHUNK == 0
    n_q = NSA_BLOCKS * BLOCK
    blk = lambda width: pl.BlockSpec((None, n_q, width), lambda bi, i: (bi, i, 0))
    per_batch = lambda rows: pl.BlockSpec((None, rows, LANES), lambda bi, i: (bi, 0, 0))
    const = lambda shape: pl.BlockSpec(shape, lambda bi, i: (0, 0))
    rows = NSA_BLOCKS * N_HEADS * BLOCK
    return pl.pallas_call(
        _nsa_kernel,
        grid=(b, s // n_q),
        in_specs=[blk(w), blk(LANES), per_batch(kc.shape[1]), per_batch(vc.shape[1]),
                  per_batch(s), per_batch(s), per_batch(s), per_batch(s),
                  const(cover.shape), const(gexp.shape), const(expand.shape)],
        out_specs=blk(w),
        out_shape=jax.ShapeDtypeStruct((b, s, w), BF16),
        scratch_shapes=[pltpu.VMEM((rows, LANES), BF16), pltpu.VMEM((rows, 1), F32),
                        pltpu.VMEM((rows, LANES), F32), pltpu.VMEM((rows, SEL_CHUNK), F32),
                        pltpu.VMEM((rows, SEL_CHUNK), F32), pltpu.VMEM((rows, 1), F32),
                        pltpu.VMEM((rows, 1), F32)],
        compiler_params=_params("parallel", "arbitrary"),
        name="nsa_attention",
    )(q, gate, kc, vc, ksel, vsel, kwin, vwin, jnp.asarray(cover, BF16), jnp.asarray(gexp, BF16),
      jnp.asarray(expand, BF16))


CONV_TAIL = 8


GDN_ROWS = 512


def _gdn_kernel(qkv_ref, z_ref, ba_ref, convw_ref, alog_ref, dtb_ref, nw_ref, bexp_ref, bd_ref, tril_ref,
                o_ref, ext_ref, state_ref):
    step = pl.program_id(1)
    cl, gw = DN_CHUNK, GROUP_WIDTH
    rows = qkv_ref.shape[0]

    @pl.when(step == 0)
    def _():
        ext_ref[0:CONV_TAIL, :] = jnp.zeros((CONV_TAIL, 3 * gw), F32)
        state_ref[...] = jnp.zeros(state_ref.shape, F32)

    ext_ref[CONV_TAIL:CONV_TAIL + rows, :] = qkv_ref[...]
    cw = convw_ref[...]
    acc = ext_ref[CONV_TAIL:CONV_TAIL + rows, :] * cw[DN_CONV - 1:DN_CONV, :]
    for j in range(DN_CONV - 1):
        acc = acc + ext_ref[pl.ds(CONV_TAIL - (DN_CONV - 1) + j, rows), :] * cw[j:j + 1, :]
    ext_ref[0:CONV_TAIL, :] = ext_ref[rows:rows + CONV_TAIL, :]
    act = _silu(acc)

    bd = bd_ref[...]
    per_head_sum = lambda t: _dot_split_lhs(t, bd)
    q = act[:, 0:gw]
    k = act[:, gw:2 * gw]
    v = act[:, 2 * gw:3 * gw]
    q = q * lax.rsqrt(per_head_sum(q * q) + 1e-6) * (HEAD_DIM ** -0.5)
    k = k * lax.rsqrt(per_head_sum(k * k) + 1e-6)

    ba = _dot_split_lhs(ba_ref[...], bexp_ref[...])
    beta = _sigmoid(ba[:, 0:gw])
    a_in = ba[:, gw:2 * gw] + dtb_ref[...]
    softplus = jnp.maximum(a_in, 0.0) + jnp.log(1.0 + jnp.exp(-jnp.abs(a_in)))
    g = -jnp.exp(alog_ref[...]) * softplus
    tril = tril_ref[...]
    g_hi = g.astype(BF16)
    g_mid, g_lo = _split(g - g_hi.astype(F32))
    gc_all = _dot(tril, g_hi) + (_dot(tril, g_mid) + _dot(tril, g_lo))

    ri = lax.broadcasted_iota(jnp.int32, (cl, cl), 0)
    ci = lax.broadcasted_iota(jnp.int32, (cl, cl), 1)
    causal = ci <= ri
    strict = ci < ri
    eye = jnp.where(ci == ri, 1.0, 0.0)
    lane = lax.broadcasted_iota(jnp.int32, (1, gw), 1)
    head_lanes = [(lane >= h * HEAD_DIM) & (lane < (h + 1) * HEAD_DIM) for h in range(N_HEADS)]

    n_chunks = rows // cl
    pairs = [(c, h) for c in range(n_chunks) for h in range(N_HEADS)]
    chunk = lambda t, c: t[c * cl:(c + 1) * cl]
    gcs = [chunk(gc_all, c) for c in range(n_chunks)]
    gc_ts = [gc.T for gc in gcs]
    g_lasts = [gc[cl - 1:cl, :] for gc in gcs]
    egs = [jnp.exp(gc) for gc in gcs]
    ks = [chunk(k, c) for c in range(n_chunks)]
    k_bfs = [t.astype(BF16) for t in ks]
    k_betas = [chunk(k, c) * chunk(beta, c) for c in range(n_chunks)]
    v_betas = [(chunk(v, c) * chunk(beta, c)).astype(BF16) for c in range(n_chunks)]
    kbgs = [(k_betas[c] * egs[c]).astype(BF16) for c in range(n_chunks)]
    q_decs = [(chunk(q, c) * egs[c]).astype(BF16) for c in range(n_chunks)]
    k_decs = [(ks[c] * jnp.exp(g_lasts[c] - gcs[c])).astype(BF16) for c in range(n_chunks)]
    head = lambda h: slice(h * HEAD_DIM, (h + 1) * HEAD_DIM)
    decays = [jnp.exp(jnp.where(causal, gcs[c][:, head(h)] - gc_ts[c][head(h), :], -jnp.inf)) for c, h in pairs]
    per_head = lambda t: [jnp.where(head_lanes[h], t, 0.0) for h in range(N_HEADS)]
    kq = [_dot_nt(jnp.concatenate(per_head(k_betas[c]) + per_head(chunk(q, c)), axis=0).astype(BF16), k_bfs[c])
          for c in range(n_chunks)]
    kks = [kq[c][h * cl:(h + 1) * cl] for c, h in pairs]
    qks = [kq[c][(N_HEADS + h) * cl:(N_HEADS + h + 1) * cl] for c, h in pairs]
    intras = [jnp.where(causal, qk * d, 0.0) for qk, d in zip(qks, decays)]
    powers = [jnp.where(strict, -(kk * d), 0.0) for kk, d in zip(kks, decays)]
    t_invs = [eye + p for p in powers]
    for _ in range(5):
        p_bfs = [p.astype(BF16) for p in powers]
        powers = [_dot(p, p) for p in p_bfs]
        t_invs = [t + _dot(t.astype(BF16), p.astype(BF16)) for t, p in zip(t_invs, powers)]
    tv = [_dot(jnp.concatenate(t_invs[c * N_HEADS:(c + 1) * N_HEADS], axis=0).astype(BF16),
               jnp.concatenate([v_betas[c], kbgs[c]], axis=1)) for c in range(n_chunks)]

    def merge_heads(stacked, lanes):
        out = jnp.zeros((cl, gw), F32)
        for h in range(N_HEADS):
            out = jnp.where(head_lanes[h], stacked[h * cl:(h + 1) * cl, lanes], out)
        return out

    us = [merge_heads(tv[c], slice(0, gw)) for c in range(n_chunks)]
    ws = [merge_heads(tv[c], slice(gw, 2 * gw)).astype(BF16) for c in range(n_chunks)]
    bd_f = bd.astype(F32)
    tn = (((0,), (0,)), ((), ()))
    kws = [(bd_f * lax.dot_general(k_decs[c], ws[c], tn, preferred_element_type=F32)).astype(BF16)
           for c in range(n_chunks)]
    kus = [bd_f * lax.dot_general(k_decs[c], us[c].astype(BF16), tn, preferred_element_type=F32)
           for c in range(n_chunks)]
    intra_cat = [jnp.concatenate([intras[c * N_HEADS + h] for h in range(N_HEADS)], axis=1).astype(BF16)
                 for c in range(n_chunks)]

    state = state_ref[...]
    outs = []
    for c in range(n_chunks):
        on_state = _dot(jnp.concatenate([ws[c], q_decs[c], kws[c]], axis=0), state.astype(BF16))
        v_new = us[c] - on_state[0:cl]
        v_stack = jnp.concatenate(per_head(v_new), axis=0).astype(BF16)
        outs.append(on_state[cl:2 * cl] + _dot(intra_cat[c], v_stack))
        state = state * jnp.exp(g_lasts[c]) - on_state[2 * cl:] + kus[c]
    state_ref[...] = state

    o = jnp.concatenate(outs, axis=0)
    o = o * lax.rsqrt(per_head_sum(o * o) * (1.0 / HEAD_DIM) + RMS_EPS) * nw_ref[...]
    o_ref[...] = (o * _silu(z_ref[...])).astype(o_ref.dtype)


def gdn_constants():
    lane_head = np.arange(GROUP_WIDTH) // HEAD_DIM
    bd = (lane_head[:, None] == lane_head[None, :]).astype(np.float32)
    bexp = np.zeros((LANES, 2 * GROUP_WIDTH), np.float32)
    for h in range(N_HEADS):
        bexp[h, h * HEAD_DIM:(h + 1) * HEAD_DIM] = 1.0
        bexp[N_HEADS + h, GROUP_WIDTH + h * HEAD_DIM:GROUP_WIDTH + (h + 1) * HEAD_DIM] = 1.0
    r = np.arange(GDN_ROWS)
    tril = ((r[:, None] // DN_CHUNK == r[None, :] // DN_CHUNK) & (r[None, :] <= r[:, None])).astype(np.float32)
    return bexp, bd, tril


def gated_deltanet(qkv, z, ba, conv_w, a_log_rep, dt_bias_rep, norm_w_rep, bexp, bd, tril):
    b, s, w3 = qkv.shape
    gw = GROUP_WIDTH
    blk = lambda width: pl.BlockSpec((None, GDN_ROWS, width), lambda bi, c: (bi, c, 0))
    const = lambda shape: pl.BlockSpec(shape, lambda bi, c: (0, 0))
    as_bf16 = lambda t: jnp.asarray(t, BF16)
    return pl.pallas_call(
        _gdn_kernel,
        grid=(b, s // GDN_ROWS),
        in_specs=[blk(w3), blk(gw), blk(LANES), const(conv_w.shape), const((1, gw)), const((1, gw)),
                  const((1, gw)), const(bexp.shape), const(bd.shape), const(tril.shape)],
        out_specs=blk(gw),
        out_shape=jax.ShapeDtypeStruct((b, s, gw), BF16),
        scratch_shapes=[pltpu.VMEM((CONV_TAIL + GDN_ROWS, w3), F32), pltpu.VMEM((gw, gw), F32)],
        compiler_params=_params("parallel", "arbitrary"),
        name="gated_deltanet",
    )(qkv, z, ba, conv_w, a_log_rep, dt_bias_rep, norm_w_rep, as_bf16(bexp), as_bf16(bd), as_bf16(tril))


def _first_max(values):
    best = values[0]
    for v in values[1:]:
        best = jnp.maximum(best, v)
    taken = jnp.zeros_like(best)
    hot = []
    for v in values:
        h = jnp.where((v == best) & (taken < 0.5), 1.0, 0.0)
        taken = taken + h
        hot.append(h)
    return best, hot


def _softmax_rows(rows):
    m = rows[0]
    for r in rows[1:]:
        m = jnp.maximum(m, r)
    e = [jnp.exp(r - m) for r in rows]
    z = e[0]
    for t in e[1:]:
        z = z + t
    return [t / z for t in e]


def _out_proj_kernel(ya_ref, yb_ref, yc_ref, yd_ref, wo_ref, x_ref, g1_ref, gain_ref, sc_ref, sh_ref,
                     wr_ref, br_ref, xo_ref, h_ref, combt_ref):
    gw = GROUP_WIDTH
    y = _dot(ya_ref[...], wo_ref[0:gw, :])
    y = y + _dot(yb_ref[...], wo_ref[gw:2 * gw, :])
    y = y + _dot(yc_ref[...], wo_ref[2 * gw:3 * gw, :])
    y = y + _dot(yd_ref[...], wo_ref[3 * gw:4 * gw, :])
    x = x_ref[...] + g1_ref[...] * y
    xo_ref[...] = x
    h = _modulated_norm(x, gain_ref[...], sc_ref[...], sh_ref[...])
    h_hi, h_lo = _split(h)
    h_ref[...] = h_hi

    w_hi, w_lo = _split(wr_ref[...])
    on_h_hi = _dot_nt(jnp.concatenate([w_hi, w_lo], axis=0), h_hi)
    logits = on_h_hi[0:LANES] + (on_h_hi[LANES:2 * LANES] + _dot_nt(w_hi, h_lo)) + br_ref[...]
    ng, ne = N_EXPERT_GROUPS, EXPERTS_PER_GROUP
    p_group = _softmax_rows([logits[r:r + 1, :] for r in range(ng)])
    pg_top, g_hot = _first_max(p_group)
    e_logits = []
    for e in range(ne):
        t = g_hot[0] * logits[ng + e:ng + e + 1, :]
        for gi in range(1, ng):
            t = t + g_hot[gi] * logits[ng + gi * ne + e:ng + gi * ne + e + 1, :]
        e_logits.append(t)
    p_exp = _softmax_rows(e_logits)
    p1, hot1 = _first_max(p_exp)
    rest = [jnp.where(h1 > 0.5, -1.0, p) for p, h1 in zip(p_exp, hot1)]
    p2, hot2 = _first_max(rest)
    total = p1 + p2
    w_exp = [(h1 * (p1 / total) + h2 * (p2 / total)) * pg_top for h1, h2 in zip(hot1, hot2)]
    for gi in range(ng):
        for e in range(ne):
            combt_ref[gi * ne + e:gi * ne + e + 1, :] = g_hot[gi] * w_exp[e]


def out_projection(ys, w_out, x, g1, gain, scale, shift, w_router, b_router, seq, row_tile=512):
    n, d = x.shape
    w_router_t = w_router.T
    tiles_per_seq = seq // row_tile
    row = lambda i: (i, 0)
    per_batch = lambda i: (i // tiles_per_seq, 0, 0)
    const = lambda shape: pl.BlockSpec(shape, lambda i: (0, 0))
    mod = pl.BlockSpec((None, 1, d), per_batch)
    return pl.pallas_call(
        _out_proj_kernel,
        grid=(n // row_tile,),
        in_specs=[pl.BlockSpec((row_tile, GROUP_WIDTH), row)] * 4
        + [const(w_out.shape), pl.BlockSpec((row_tile, d), row), mod, const((1, d)), mod, mod,
           const(w_router_t.shape), const(b_router.shape)],
        out_specs=(pl.BlockSpec((row_tile, d), row), pl.BlockSpec((row_tile, d), row),
                   pl.BlockSpec((N_EXPERTS, row_tile), lambda i: (0, i))),
        out_shape=(jax.ShapeDtypeStruct((n, d), F32), jax.ShapeDtypeStruct((n, d), BF16),
                   jax.ShapeDtypeStruct((N_EXPERTS, n), F32)),
        compiler_params=_params("parallel"),
        name="out_projection",
    )(*ys, w_out, x, g1, gain, scale, shift, w_router_t, b_router)


MOE_ROWS = 192


def _moe_kernel(h_ref, combt_ref, before_ref, wg_ref, wu_ref, wd_ref, x_ref, g2_ref, fin_ref, o_ref,
                acc_ref, rank_ref, pick_ref, y_ref, *, final_norm):
    e = pl.program_id(1)
    tn = (((0,), (0,)), ((), ()))

    @pl.when(e == 0)
    def _():
        acc_ref[...] = jnp.zeros(acc_ref.shape, F32)
        routed = jnp.where(combt_ref[...] != 0.0, 1.0, 0.0).astype(BF16)
        rank_ref[...] = _dot(routed, before_ref[...])

    weight_row = combt_ref[pl.ds(e, 1), :]
    rank_row = rank_ref[pl.ds(e, 1), :]
    routed_row = weight_row != 0.0
    count = jnp.sum(jnp.where(routed_row, 1, 0))
    h = h_ref[...]

    def gathered_pass(j):
        slot = (j * MOE_ROWS + lax.broadcasted_iota(jnp.int32, (MOE_ROWS, 1), 0)).astype(F32)
        pick = jnp.where((rank_row == slot) & routed_row, 1.0, 0.0)
        pick_bf = pick.astype(BF16)
        rows = _dot(pick_bf, h).astype(BF16)
        weight = jnp.sum(pick * weight_row, axis=-1, keepdims=True)
        hid = _silu(_dot(rows, wg_ref[...])) * _dot(rows, wu_ref[...]) * weight
        return pick_bf, _dot(hid.astype(BF16), wd_ref[...]).astype(BF16)

    first = pl.ds(pl.multiple_of(e * MOE_ROWS, MOE_ROWS), MOE_ROWS)
    pick_ref[first, :], y_ref[first, :] = gathered_pass(0)

    def extra_pass(j, carry):
        pick_bf, y = gathered_pass(j)
        acc_ref[...] += lax.dot_general(pick_bf, y, tn, preferred_element_type=F32)
        return carry

    lax.fori_loop(1, (count + MOE_ROWS - 1) // MOE_ROWS, extra_pass, 0)

    @pl.when(e == N_EXPERTS - 1)
    def _():
        moe = acc_ref[...] + lax.dot_general(pick_ref[...], y_ref[...], tn, preferred_element_type=F32)
        x = x_ref[...] + g2_ref[...] * moe
        if final_norm:
            x = x * lax.rsqrt(jnp.mean(x * x, axis=-1, keepdims=True) + RMS_EPS) * fin_ref[...]
        o_ref[...] = x


def moe_experts(h, combt, w_gate, w_up, w_down, x, g2, fin_gain, seq, final_norm, row_tile=1024):
    n, d = x.shape
    f = w_gate.shape[-1]
    tiles_per_seq = seq // row_tile
    row = lambda i, e: (i, 0)
    before = np.triu(np.ones((row_tile, row_tile), np.float32), 1)
    return pl.pallas_call(
        functools.partial(_moe_kernel, final_norm=final_norm),
        grid=(n // row_tile, N_EXPERTS),
        in_specs=[pl.BlockSpec((row_tile, d), row),
                  pl.BlockSpec((N_EXPERTS, row_tile), lambda i, e: (0, i)),
                  pl.BlockSpec((row_tile, row_tile), lambda i, e: (0, 0)),
                  pl.BlockSpec((None, d, f), lambda i, e: (e, 0, 0)),
                  pl.BlockSpec((None, d, f), lambda i, e: (e, 0, 0)),
                  pl.BlockSpec((None, f, d), lambda i, e: (e, 0, 0)),
                  pl.BlockSpec((row_tile, d), row),
                  pl.BlockSpec((None, 1, d), lambda i, e: (i // tiles_per_seq, 0, 0)),
                  pl.BlockSpec((1, d), lambda i, e: (0, 0))],
        out_specs=pl.BlockSpec((row_tile, d), row),
        out_shape=jax.ShapeDtypeStruct((n, d), F32),
        scratch_shapes=[pltpu.VMEM((row_tile, d), F32), pltpu.VMEM((N_EXPERTS, row_tile), F32),
                        pltpu.VMEM((N_EXPERTS * MOE_ROWS, row_tile), BF16),
                        pltpu.VMEM((N_EXPERTS * MOE_ROWS, d), BF16)],
        compiler_params=_params("parallel", "arbitrary"),
        name="moe_experts",
    )(h, combt, jnp.asarray(before, BF16), w_gate, w_up, w_down, x, g2, fin_gain)


def kernel(x, c, norm_mix, norm_ffn, final_norm, w_ada, b_ada, w_in, w_out, attn_sink, w_pool, pool_scale, cmp_pos, cmp_w1, cmp_w2, conv_w, a_log, dt_bias, dn_norm, w_route_group, b_route_group, w_route_expert, b_route_expert, w_gate, w_up, w_down):
    batch, seq, d = x.shape
    depth = w_in.shape[0]
    n = batch * seq
    cover, gexp, expand = nsa_constants(seq)
    bexp, bd, tril = gdn_constants()
    mod = ada_modulation(c, w_ada, b_ada)
    spread = lambda t: jnp.repeat(t, HEAD_DIM).reshape(1, GROUP_WIDTH)
    seq3 = lambda t: t.reshape(batch, seq, t.shape[-1])
    xf = x.reshape(n, d)
    for l in range(depth):
        sh1, sc1, g1, sh2, sc2, g2 = (mod[l, :, i * d:(i + 1) * d].reshape(batch, 1, d) for i in range(6))
        p = in_projection(xf, norm_mix[l].reshape(1, d), sc1, sh1, pack_in_weights(w_in[l]), seq)
        y_a = swa_attention(seq3(p["a_q"]), seq3(p["a_k"]), seq3(p["a_v"]), attn_sink[l])
        w_pool_bd = jax.scipy.linalg.block_diag(*[w_pool[l, gi] for gi in range(len(POOL_WINDOWS))])
        y_b = multiscale_pool(seq3(p["b_u"]), w_pool_bd.astype(BF16), pool_scale[l].reshape(1, GROUP_WIDTH))
        kc, vc = nsa_compress(seq3(p["c_cmp"]), *pack_compress_weights(cmp_pos[l], cmp_w1[l], cmp_w2[l]))
        y_c = nsa_attention(seq3(p["c_q"]), seq3(p["c_gate"]), kc, vc, seq3(p["c_ksel"]), seq3(p["c_vsel"]),
                            seq3(p["c_kwin"]), seq3(p["c_vwin"]), cover, gexp, expand)
        y_d = gated_deltanet(seq3(p["d_qkv"]), seq3(p["d_z"]), seq3(p["d_ba"]), conv_w[l], spread(a_log[l]),
                             spread(dt_bias[l]), jnp.tile(dn_norm[l], N_HEADS).reshape(1, GROUP_WIDTH), bexp, bd, tril)
        ys = [t.reshape(n, GROUP_WIDTH) for t in (y_a, y_b, y_c, y_d)]
        n_logits = N_EXPERT_GROUPS + N_EXPERTS
        w_router = jnp.pad(jnp.concatenate([w_route_group[l], w_route_expert[l]], axis=1),
                           ((0, 0), (0, LANES - n_logits)))
        b_router = jnp.pad(jnp.concatenate([b_route_group[l], b_route_expert[l]]),
                           (0, LANES - n_logits)).reshape(LANES, 1)
        xf, h2, comb = out_projection(ys, w_out[l].astype(BF16), xf, g1, norm_ffn[l].reshape(1, d), sc2, sh2,
                                      w_router, b_router, seq)
        f = w_gate.shape[-1]
        xf = moe_experts(h2, comb, w_gate[l].reshape(N_EXPERTS, d, f).astype(BF16),
                         w_up[l].reshape(N_EXPERTS, d, f).astype(BF16),
                         w_down[l].reshape(N_EXPERTS, f, d).astype(BF16),
                         xf, g2, final_norm.reshape(1, d), seq, final_norm=(l == depth - 1))
    return xf.reshape(batch, seq, d)
```

```python
import functools

import numpy as np
import jax
import jax.numpy as jnp
from jax import lax
from jax.experimental import pallas as pl
from jax.experimental.pallas import tpu as pltpu

F32 = jnp.float32
BF16 = jnp.bfloat16
HI = lax.Precision.HIGHEST

HEAD_DIM = 64
N_HEADS = 4
GROUP_WIDTH = N_HEADS * HEAD_DIM
BLOCK = 128
RMS_EPS = 1e-6
SWA_WINDOW = 128
SWA_BLOCKS = 4
POOL_WINDOWS = (2, 4, 8, 16)
POOL_HALO = 16
CMP_LEN = 32
CMP_STRIDE = 16
SEL_BLOCK = 64
NSA_TOP_N = 16
NSA_WINDOW = 512
NSA_FORCE = 1e4
SEL_CHUNK = 512
NSA_BLOCKS = 4
DN_CONV = 4
DN_CHUNK = 64
N_EXPERT_GROUPS = 4
EXPERTS_PER_GROUP = 4
N_EXPERTS = N_EXPERT_GROUPS * EXPERTS_PER_GROUP
LANES = 128
MASKED = -1e30
QK_SCALE = HEAD_DIM ** -0.5 * float(np.log2(np.e))
VMEM_LIMIT = 56 * 1024 * 1024

SEG_WIDTHS = (
    ("a_q", 256), ("a_k", 256), ("a_v", 256), ("c_q", 256),
    ("c_ksel", 128), ("c_vsel", 128), ("c_kwin", 128), ("c_vwin", 128),
    ("b_u", 256), ("c_cmp", 128), ("c_gate", 128), ("d_qkv", 768), ("d_z", 256), ("d_ba", 128),
)
SEG_BF16 = ("a_q", "a_k", "a_v", "c_q", "c_ksel", "c_vsel", "c_kwin", "c_vwin")
SEG_OFFSETS = {}
_off = 0
for _name, _w in SEG_WIDTHS:
    SEG_OFFSETS[_name] = (_off, _w)
    _off += _w
PACKED_WIDTH = _off


def _sigmoid(x):
    return 1.0 / (1.0 + jnp.exp(-x))


def _silu(x):
    return x * _sigmoid(x)


def _dot(a, b, precision=None):
    return jnp.dot(a, b, precision=precision, preferred_element_type=F32)


def _dot_nt(a, b, precision=None):
    return lax.dot_general(a, b, (((1,), (1,)), ((), ())), precision=precision,
                           preferred_element_type=F32)


def _split(a):
    hi = a.astype(BF16)
    return hi, (a - hi.astype(F32)).astype(BF16)


def _dot_split_lhs(a, b):
    hi, lo = _split(a)
    return _dot(hi, b) + _dot(lo, b)


def _params(*semantics):
    return pltpu.CompilerParams(dimension_semantics=semantics, vmem_limit_bytes=VMEM_LIMIT)


def _ada_kernel(c_ref, w_ref, b_ref, o_ref):
    cond = _silu(c_ref[...])
    o_ref[...] = _dot(cond, w_ref[...], HI) + b_ref[...]


def ada_modulation(c, w_ada, b_ada, col_tile=1536):
    depth, d, width = w_ada.shape
    b = c.shape[0]
    return pl.pallas_call(
        _ada_kernel,
        grid=(depth, width // col_tile),
        in_specs=[
            pl.BlockSpec((b, d), lambda l, j: (0, 0)),
            pl.BlockSpec((None, d, col_tile), lambda l, j: (l, 0, j)),
            pl.BlockSpec((None, 1, col_tile), lambda l, j: (l, 0, j)),
        ],
        out_specs=pl.BlockSpec((None, b, col_tile), lambda l, j: (l, 0, j)),
        out_shape=jax.ShapeDtypeStruct((depth, b, width), F32),
        compiler_params=_params("parallel", "parallel"),
        name="ada_modulation",
    )(c, w_ada, b_ada.reshape(depth, 1, width))


def _modulated_norm(x, gain, scale, shift):
    y = x * lax.rsqrt(jnp.mean(x * x, axis=-1, keepdims=True) + RMS_EPS)
    return y * gain * (1.0 + scale) + shift


def _in_proj_kernel(x_ref, gain_ref, sc_ref, sh_ref, w_ref, *out_refs):
    h = _modulated_norm(x_ref[...], gain_ref[...], sc_ref[...], sh_ref[...]).astype(BF16)
    y = _dot(h, w_ref[...])
    for (name, _), o_ref in zip(SEG_WIDTHS, out_refs):
        off, width = SEG_OFFSETS[name]
        o_ref[...] = y[:, off:off + width].astype(o_ref.dtype)


def in_projection(x, gain, scale, shift, w_packed, seq, row_tile=512):
    n, d = x.shape
    tiles_per_seq = seq // row_tile
    row = lambda i: (i, 0)
    per_batch = lambda i: (i // tiles_per_seq, 0, 0)
    out_shape = tuple(
        jax.ShapeDtypeStruct((n, w), BF16 if name in SEG_BF16 else F32) for name, w in SEG_WIDTHS)
    outs = pl.pallas_call(
        _in_proj_kernel,
        grid=(n // row_tile,),
        in_specs=[
            pl.BlockSpec((row_tile, d), row),
            pl.BlockSpec((1, d), lambda i: (0, 0)),
            pl.BlockSpec((None, 1, d), per_batch),
            pl.BlockSpec((None, 1, d), per_batch),
            pl.BlockSpec((d, PACKED_WIDTH), lambda i: (0, 0)),
        ],
        out_specs=tuple(pl.BlockSpec((row_tile, w), row) for _, w in SEG_WIDTHS),
        out_shape=out_shape,
        compiler_params=_params("parallel"),
        name="in_projection",
    )(x, gain, scale, shift, w_packed)
    return dict(zip((name for name, _ in SEG_WIDTHS), outs))


def pack_in_weights(w_in):
    gw, hd = GROUP_WIDTH, HEAD_DIM
    sizes = (gw, 2 * hd, 2 * hd, gw, gw, 6 * hd, 3 * N_HEADS, 3 * gw, gw, N_HEADS, N_HEADS)
    offs = np.concatenate([[0], np.cumsum(sizes)])
    (a_q, a_k, a_v, b_u, c_q, c_kv, c_gate, d_qkv, d_z, d_beta, d_a) = (
        w_in[:, offs[i]:offs[i + 1]] for i in range(len(sizes)))
    d = w_in.shape[0]
    dup = lambda t: jnp.concatenate([t, t], axis=1)
    pad = lambda t: jnp.pad(t, ((0, 0), (0, LANES - t.shape[1])))
    k_cmp, v_cmp, k_sel, v_sel, k_win, v_win = (c_kv[:, i * hd:(i + 1) * hd] for i in range(6))
    a_q = a_q * QK_SCALE
    c_q = c_q * QK_SCALE
    segs = {
        "a_q": a_q,
        "a_k": jnp.concatenate([dup(a_k[:, :hd]), dup(a_k[:, hd:])], axis=1),
        "a_v": jnp.concatenate([dup(a_v[:, :hd]), dup(a_v[:, hd:])], axis=1),
        "c_q": c_q,
        "c_ksel": dup(k_sel), "c_vsel": dup(v_sel), "c_kwin": dup(k_win), "c_vwin": dup(v_win),
        "b_u": b_u,
        "c_cmp": jnp.concatenate([k_cmp, v_cmp], axis=1),
        "c_gate": pad(c_gate),
        "d_qkv": d_qkv, "d_z": d_z,
        "d_ba": pad(jnp.concatenate([d_beta, d_a], axis=1)),
    }
    packed = jnp.concatenate([segs[name] for name, _ in SEG_WIDTHS], axis=1)
    assert packed.shape == (d, PACKED_WIDTH)
    return packed.astype(BF16)


def _stack_heads(slab):
    lane = lax.broadcasted_iota(jnp.int32, slab.shape, 1)
    zero = jnp.zeros_like(slab)
    return jnp.concatenate([jnp.where(lane < HEAD_DIM, slab, zero),
                            jnp.where(lane >= HEAD_DIM, slab, zero)], axis=0)


def _unstack_heads(o, rows):
    lane = lax.broadcasted_iota(jnp.int32, (rows, LANES), 1)
    return jnp.where(lane < HEAD_DIM, o[0:rows], o[rows:2 * rows])


def _swa_kernel(sink_ref, q_ref, kp_ref, kc_ref, vp_ref, vc_ref, o_ref):
    i = pl.program_id(1)
    row = lax.broadcasted_iota(jnp.int32, (2 * BLOCK, 2 * BLOCK), 0)
    col = lax.broadcasted_iota(jnp.int32, (2 * BLOCK, 2 * BLOCK), 1)
    tq = row & (BLOCK - 1)
    tk = col - BLOCK
    banded = (tk <= tq) & (tk > tq - SWA_WINDOW)
    first_valid = banded & ((col >= BLOCK) | (i > 0))
    rowh = lax.broadcasted_iota(jnp.int32, (2 * BLOCK, 1), 0)
    log2e = float(np.log2(np.e))
    sinks = [jnp.where(rowh < BLOCK, sink_ref[2 * j], sink_ref[2 * j + 1]) * log2e for j in range(N_HEADS // 2)]
    pairs = [(sb, j) for sb in range(SWA_BLOCKS) for j in range(N_HEADS // 2)]

    def keys(cur_ref, prev_ref, sb, lanes):
        if sb == 0:
            return jnp.concatenate([prev_ref[:, lanes], cur_ref[0:BLOCK, lanes]], axis=0)
        return cur_ref[(sb - 1) * BLOCK:(sb + 1) * BLOCK, lanes]

    lanes_of = lambda j: slice(j * LANES, (j + 1) * LANES)
    s = [jnp.where(first_valid if sb == 0 else banded,
                   _dot_nt(_stack_heads(q_ref[sb * BLOCK:(sb + 1) * BLOCK, lanes_of(j)]),
                           keys(kc_ref, kp_ref, sb, lanes_of(j))), -jnp.inf) for sb, j in pairs]
    m = [jnp.maximum(jnp.max(t, axis=-1, keepdims=True), sinks[j]) for t, (sb, j) in zip(s, pairs)]
    p = [jnp.exp2(t - mm) for t, mm in zip(s, m)]
    denom = [jnp.sum(t, axis=-1, keepdims=True) + jnp.exp2(sinks[j] - mm) for t, mm, (sb, j) in zip(p, m, pairs)]
    o = [_dot(t.astype(BF16), keys(vc_ref, vp_ref, sb, lanes_of(j))) / d for t, d, (sb, j) in zip(p, denom, pairs)]
    for sb in range(SWA_BLOCKS):
        slabs = [_unstack_heads(o[sb * (N_HEADS // 2) + j], BLOCK) for j in range(N_HEADS // 2)]
        o_ref[sb * BLOCK:(sb + 1) * BLOCK, :] = jnp.concatenate(slabs, axis=1).astype(o_ref.dtype)


def swa_attention(q, k, v, sink):
    b, s, w = q.shape
    rows = SWA_BLOCKS * BLOCK
    cur = pl.BlockSpec((None, rows, w), lambda bi, i: (bi, i, 0))
    prev = pl.BlockSpec((None, BLOCK, w), lambda bi, i: (bi, jnp.maximum(i * SWA_BLOCKS - 1, 0), 0))
    return pl.pallas_call(
        _swa_kernel,
        grid=(b, s // rows),
        in_specs=[pl.BlockSpec(memory_space=pltpu.SMEM), cur, prev, cur, prev, cur],
        out_specs=cur,
        out_shape=jax.ShapeDtypeStruct((b, s, w), BF16),
        compiler_params=_params("parallel", "parallel"),
        name="swa_attention",
    )(sink, q, k, k, v, v)


def _pool_kernel(up_ref, u_ref, w_ref, scale_ref, o_ref, ext_ref):
    i = pl.program_id(1)
    rows = u_ref.shape[0]
    u = u_ref[...]
    halo = up_ref[...]
    ext_ref[0:POOL_HALO, :] = jnp.where(i > 0, halo, jnp.zeros_like(halo))
    ext_ref[POOL_HALO:POOL_HALO + rows, :] = u
    lane = lax.broadcasted_iota(jnp.int32, u.shape, 1)
    pos = i * rows + lax.broadcasted_iota(jnp.int32, u.shape, 0)
    group_ch = GROUP_WIDTH // len(POOL_WINDOWS)
    total = u
    d = jnp.zeros_like(u)
    width = 1
    for gi, w in enumerate(POOL_WINDOWS):
        while width < w:
            total = total + ext_ref[pl.ds(POOL_HALO - width, rows), :]
            width += 1
        cnt = jnp.minimum(pos + 1, w).astype(F32)
        in_group = (lane >= gi * group_ch) & (lane < (gi + 1) * group_ch)
        d = jnp.where(in_group, total / cnt - u, d)
    o_ref[...] = (_dot(d.astype(BF16), w_ref[...]) * scale_ref[...]).astype(o_ref.dtype)


def multiscale_pool(u, w_blockdiag, pool_scale, row_tile=512):
    b, s, w = u.shape
    halo_per_tile = row_tile // POOL_HALO
    return pl.pallas_call(
        _pool_kernel,
        grid=(b, s // row_tile),
        in_specs=[
            pl.BlockSpec((None, POOL_HALO, w), lambda bi, i: (bi, jnp.maximum(i * halo_per_tile - 1, 0), 0)),
            pl.BlockSpec((None, row_tile, w), lambda bi, i: (bi, i, 0)),
            pl.BlockSpec((w, w), lambda bi, i: (0, 0)),
            pl.BlockSpec((1, w), lambda bi, i: (0, 0)),
        ],
        out_specs=pl.BlockSpec((None, row_tile, w), lambda bi, i: (bi, i, 0)),
        out_shape=jax.ShapeDtypeStruct((b, s, w), BF16),
        scratch_shapes=[pltpu.VMEM((POOL_HALO + row_tile, w), F32)],
        compiler_params=_params("parallel", "parallel"),
        name="multiscale_pool",
    )(u, u, w_blockdiag, pool_scale)


def _compress_kernel(x_ref, w1_ref, pos_ref, w2k_ref, w2v_ref, kc_ref, vc_ref):
    n_chunks = x_ref.shape[0]
    both = _dot(x_ref[...], w1_ref[...], HI)
    pre = both[:, 0:LANES] + pltpu.roll(both[:, LANES:2 * LANES], n_chunks - 1, 0) + pos_ref[...]
    hid = _silu(pre)
    kc_ref[...] = _dot(hid, w2k_ref[...], HI).astype(kc_ref.dtype)
    vc_ref[...] = _dot(hid, w2v_ref[...], HI).astype(vc_ref.dtype)


def nsa_compress(cmp_in, w1_packed, pos_term, w2k, w2v):
    b, s, w = cmp_in.shape
    n_chunks = s // CMP_STRIDE
    flat = cmp_in.reshape(b, n_chunks, CMP_STRIDE * w)
    const = lambda shape: pl.BlockSpec(shape, lambda bi: tuple(0 for _ in shape))
    out = jax.ShapeDtypeStruct((b, n_chunks, LANES), BF16)
    return pl.pallas_call(
        _compress_kernel,
        grid=(b,),
        in_specs=[pl.BlockSpec((None, n_chunks, CMP_STRIDE * w), lambda bi: (bi, 0, 0)),
                  const(w1_packed.shape), const(pos_term.shape), const(w2k.shape), const(w2v.shape)],
        out_specs=(pl.BlockSpec((None, n_chunks, LANES), lambda bi: (bi, 0, 0)),) * 2,
        out_shape=(out, out),
        compiler_params=_params("parallel"),
        name="nsa_compress",
    )(flat, w1_packed, pos_term, w2k, w2v)


def pack_compress_weights(cmp_pos, cmp_w1, cmp_w2):
    hd, half = HEAD_DIM, CMP_LEN // 2
    w1 = cmp_w1.reshape(2, 2, half, hd, hd)
    zeros = jnp.zeros((half, hd, hd), F32)
    halves = []
    for part in range(2):
        wk = jnp.concatenate([w1[0, part], zeros], axis=-1)
        wv = jnp.concatenate([zeros, w1[1, part]], axis=-1)
        halves.append(jnp.concatenate([wk, wv], axis=1).reshape(half * 2 * hd, 2 * hd))
    w1_packed = jnp.concatenate(halves, axis=1)
    pos_flat = cmp_pos.reshape(2, 1, CMP_LEN * hd)
    pos_term = jnp.concatenate([jnp.matmul(pos_flat[0], cmp_w1[0], precision=HI),
                                jnp.matmul(pos_flat[1], cmp_w1[1], precision=HI)], axis=1)
    zero2 = jnp.zeros((hd, 2 * hd), F32)
    w2k = jnp.concatenate([jnp.concatenate([cmp_w2[0], cmp_w2[0]], axis=1), zero2], axis=0)
    w2v = jnp.concatenate([zero2, jnp.concatenate([cmp_w2[1], cmp_w2[1]], axis=1)], axis=0)
    return w1_packed, pos_term, w2k, w2v


def _nsa_kernel(q_ref, gate_ref, kc_ref, vc_ref, ksel_ref, vsel_ref, kwin_ref, vwin_ref,
                cover_ref, gexp_ref, expand_ref, o_ref, q4_ref, m_ref, acc_ref, sa_ref, sb_ref, ma_ref, mb_ref):
    i = pl.program_id(1)
    nqb = NSA_BLOCKS
    groups = [(b, h) for b in range(nqb) for h in range(N_HEADS)]
    group_rows = [slice(g * BLOCK, (g + 1) * BLOCK) for g in range(len(groups))]
    q4 = jnp.concatenate([_stack_heads(q_ref[b * BLOCK:(b + 1) * BLOCK, s * LANES:(s + 1) * LANES])
                          for b in range(nqb) for s in range(N_HEADS // 2)], axis=0)
    tq_blk = [(i * nqb + b) * BLOCK + lax.broadcasted_iota(jnp.int32, (BLOCK, 1), 0) for b in range(nqb)]

    n_cmp = kc_ref.shape[0]
    n_idx = lax.broadcasted_iota(jnp.int32, (1, n_cmp), 1)
    valid_c = [(n_idx * CMP_STRIDE + (CMP_LEN - 1) <= tq_blk[b]) & (n_idx < n_cmp - 1) for b in range(nqb)]
    k_c = kc_ref[...]
    v_c = vc_ref[...]
    s_c_all = _dot_nt(q4, k_c)
    s_c = [jnp.where(valid_c[b], s_c_all[gr], -jnp.inf) for (b, h), gr in zip(groups, group_rows)]
    m_c = [jnp.max(t, axis=-1, keepdims=True) for t in s_c]
    m_c = [jnp.where(t == -jnp.inf, 0.0, t) for t in m_c]
    p_c = [jnp.exp2(t - mm) for t, mm in zip(s_c, m_c)]
    d_c = [jnp.sum(t, axis=-1, keepdims=True) for t in p_c]
    p_c = [t / jnp.where(dd > 0, dd, 1.0) for t, dd in zip(p_c, d_c)]
    o_c = [_dot(t.astype(BF16), v_c) for t in p_c]

    value_lane = lax.broadcasted_iota(jnp.int32, (1, LANES), 1) < HEAD_DIM

    def with_ones(v):
        return jnp.where(value_lane, v, jnp.ones_like(v))

    span = NSA_WINDOW + BLOCK
    start_w = [pl.multiple_of(jnp.maximum((i * nqb + b) * BLOCK - NSA_WINDOW, 0), BLOCK) for b in range(nqb)]
    tk = [start_w[b] + lax.broadcasted_iota(jnp.int32, (1, span), 1) for b in range(nqb)]
    valid_w = [(tk[b] <= tq_blk[b]) & (tk[b] > tq_blk[b] - NSA_WINDOW) for b in range(nqb)]
    k_w = [kwin_ref[pl.ds(start_w[b], span), :] for b in range(nqb)]
    v_w = [with_ones(vwin_ref[pl.ds(start_w[b], span), :]) for b in range(nqb)]
    acc_w = []

    def window_group(g):
        b = groups[g][0]
        s_w = jnp.where(valid_w[b], _dot_nt(q4[group_rows[g]], k_w[b]), -jnp.inf)
        p_w = jnp.exp2(s_w - jnp.max(s_w, axis=-1, keepdims=True))
        acc_w.append(_dot(p_w.astype(BF16), v_w[b]))

    cover = cover_ref[...]
    importance = []
    for b in range(nqb):
        p_b = p_c[b * N_HEADS:(b + 1) * N_HEADS]
        p_hi, p_lo = _split((p_b[0] + p_b[1]) + (p_b[2] + p_b[3]))
        importance.append(_dot_nt(cover, p_hi) + _dot_nt(cover, p_lo))
    importance = jnp.concatenate(importance, axis=1)
    n_sel = cover_ref.shape[0]
    n_q = nqb * BLOCK
    blk = lax.broadcasted_iota(jnp.int32, (n_sel, n_q), 0)
    t_lane = i * n_q + lax.broadcasted_iota(jnp.int32, (n_sel, n_q), 1)
    cur = t_lane // SEL_BLOCK
    causal = blk * SEL_BLOCK <= t_lane
    forced = (blk == 0) | (blk == cur) | (blk == cur - 1)
    score = jnp.where(causal, jnp.where(forced, NSA_FORCE, importance), -jnp.inf)
    blk_f = blk.astype(F32)
    chosen = jnp.where(forced, 1.0, 0.0)
    score = jnp.where(forced, -jnp.inf, score)
    n_rounds = NSA_TOP_N - 3
    n_groups = len(groups)
    for r in range(n_rounds):
        for g in range(r * n_groups // n_rounds, (r + 1) * n_groups // n_rounds):
            window_group(g)
        top = jnp.max(score, axis=0, keepdims=True)
        first = jnp.min(jnp.where(score == top, blk_f, float(n_sel)), axis=0, keepdims=True)
        pick = blk_f == first
        score = jnp.where(pick, -jnp.inf, score)
        chosen = jnp.where(pick, 1.0, chosen)
    chosen = jnp.where(causal, chosen, 0.0)
    chosen_q = [chosen[:, b * BLOCK:(b + 1) * BLOCK].T.astype(BF16) for b in range(nqb)]

    q4_ref[...] = q4
    m_ref[...] = jnp.full(m_ref.shape, MASKED, F32)
    acc_ref[...] = jnp.zeros(acc_ref.shape, F32)
    n_chunks = (i * n_q + n_q - 1) // SEL_CHUNK + 1

    def chunk_start(c):
        return pl.multiple_of(jnp.minimum(c, n_chunks - 1) * SEL_CHUNK, SEL_CHUNK)

    def chunk_keep(c):
        start = chunk_start(c)
        key = start + lax.broadcasted_iota(jnp.int32, (1, SEL_CHUNK), 1)
        spread = expand_ref[:, pl.ds(start, SEL_CHUNK)]
        return [jnp.where((key <= tq_blk[b]) & (c < n_chunks), _dot(chosen_q[b], spread), 0.0) > 0.5
                for b in range(nqb)]

    def stage_group(g, bufs, s_all, keep):
        s = jnp.where(keep[groups[g][0]], s_all[group_rows[g]], MASKED)
        bufs[0][group_rows[g]] = s
        bufs[1][group_rows[g]] = jnp.max(s, axis=-1, keepdims=True)

    def half_step(c, cur_bufs, next_bufs):
        keep_next = chunk_keep(c + 1)
        v_aug = with_ones(vsel_ref[pl.ds(chunk_start(c), SEL_CHUNK), :])
        s_next = _dot_nt(q4_ref[...], ksel_ref[pl.ds(chunk_start(c + 1), SEL_CHUNK), :])
        probs, m_olds, m_news = [], [], []
        for g in range(n_groups):
            stage_group(g, next_bufs, s_next, keep_next)
            m_old = m_ref[group_rows[g]]
            m_new = jnp.maximum(m_old, cur_bufs[1][group_rows[g]])
            m_ref[group_rows[g]] = m_new
            probs.append(jnp.exp2(cur_bufs[0][group_rows[g]] - m_new).astype(BF16))
            m_olds.append(m_old)
            m_news.append(m_new)
        pv = _dot(jnp.concatenate(probs, axis=0), v_aug)
        alpha = jnp.exp2(jnp.concatenate(m_olds, axis=0) - jnp.concatenate(m_news, axis=0))
        acc_ref[...] = alpha * acc_ref[...] + pv

    bufs_a = (sa_ref, ma_ref)
    bufs_b = (sb_ref, mb_ref)
    keep0 = chunk_keep(0)
    s_first = _dot_nt(q4, ksel_ref[0:SEL_CHUNK, :])
    for g in range(n_groups):
        stage_group(g, bufs_a, s_first, keep0)

    def sel_step(t, carry):
        half_step(2 * t, bufs_a, bufs_b)

        @pl.when(2 * t + 1 < n_chunks)
        def _():
            half_step(2 * t + 1, bufs_b, bufs_a)

        return carry

    lax.fori_loop(0, (n_chunks + 1) // 2, sel_step, 0)

    def heads_to_lanes(heads):
        return jnp.concatenate([_unstack_heads(jnp.concatenate(heads[2 * j:2 * j + 2], axis=0), BLOCK)
                                for j in range(N_HEADS // 2)], axis=1)

    def normalized_heads_to_lanes(heads):
        slabs = []
        for j in range(N_HEADS // 2):
            even, odd = heads[2 * j], heads[2 * j + 1]
            numer = jnp.where(value_lane, even, pltpu.roll(odd, HEAD_DIM, 1))
            denom = jnp.where(value_lane, pltpu.roll(even, HEAD_DIM, 1), odd)
            slabs.append(numer / denom)
        return jnp.concatenate(slabs, axis=1)

    gates = _dot_split_lhs(_sigmoid(gate_ref[...]), gexp_ref[...])
    gw = GROUP_WIDTH
    for b in range(nqb):
        of_block = slice(b * N_HEADS, (b + 1) * N_HEADS)
        qr = slice(b * BLOCK, (b + 1) * BLOCK)
        acc_sel = [acc_ref[gr] for gr in group_rows[of_block]]
        out = (gates[qr, 0:gw] * heads_to_lanes(o_c[of_block])
               + gates[qr, gw:2 * gw] * normalized_heads_to_lanes(acc_sel)
               + gates[qr, 2 * gw:3 * gw] * normalized_heads_to_lanes(acc_w[of_block]))
        o_ref[qr, :] = out.astype(o_ref.dtype)


def nsa_constants(seq):
    n_cmp_rows = seq // CMP_STRIDE
    n_sel = seq // SEL_BLOCK
    cmp_start = np.arange(n_cmp_rows) * CMP_STRIDE
    sel_start = np.arange(n_sel) * SEL_BLOCK
    cover = np.maximum(np.minimum(cmp_start[None, :] + CMP_LEN, sel_start[:, None] + SEL_BLOCK)
                       - np.maximum(cmp_start[None, :], sel_start[:, None]), 0).astype(np.float32) / CMP_LEN
    gexp = np.zeros((LANES, 3 * GROUP_WIDTH), np.float32)
    for h in range(N_HEADS):
        for br in range(3):
            gexp[h * 3 + br, br * GROUP_WIDTH + h * HEAD_DIM: br * GROUP_WIDTH + (h + 1) * HEAD_DIM] = 1.0
    expand = (np.arange(n_sel)[:, None] == np.arange(seq)[None, :] // SEL_BLOCK).astype(np.float32)
    return cover, gexp, expand


def nsa_attention(q, gate, kc, vc, ksel, vsel, kwin, vwin, cover, gexp, expand):
    b, s, w = q.shape
    assert s >= NSA_WINDOW + BLOCK and s % SEL_CHUNK == 0
    n_q = NSA_BLOCKS * BLOCK
    blk = lambda width: pl.BlockSpec((None, n_q, width), lambda bi, i: (bi, i, 0))
    per_batch = lambda rows: pl.BlockSpec((None, rows, LANES), lambda bi, i: (bi, 0, 0))
    const = lambda shape: pl.BlockSpec(shape, lambda bi, i: (0, 0))
    rows = NSA_BLOCKS * N_HEADS * BLOCK
    return pl.pallas_call(
        _nsa_kernel,
        grid=(b, s // n_q),
        in_specs=[blk(w), blk(LANES), per_batch(kc.shape[1]), per_batch(vc.shape[1]),
                  per_batch(s), per_batch(s), per_batch(s), per_batch(s),
                  const(cover.shape), const(gexp.shape), const(expand.shape)],
        out_specs=blk(w),
        out_shape=jax.ShapeDtypeStruct((b, s, w), BF16),
        scratch_shapes=[pltpu.VMEM((rows, LANES), BF16), pltpu.VMEM((rows, 1), F32),
                        pltpu.VMEM((rows, LANES), F32), pltpu.VMEM((rows, SEL_CHUNK), F32),
                        pltpu.VMEM((rows, SEL_CHUNK), F32), pltpu.VMEM((rows, 1), F32),
                        pltpu.VMEM((rows, 1), F32)],
        compiler_params=_params("parallel", "arbitrary"),
        name="nsa_attention",
    )(q, gate, kc, vc, ksel, vsel, kwin, vwin, jnp.asarray(cover, BF16), jnp.asarray(gexp, BF16),
      jnp.asarray(expand, BF16))


CONV_TAIL = 8


GDN_ROWS = 512


def _gdn_kernel(qkv_ref, z_ref, ba_ref, convw_ref, alog_ref, dtb_ref, nw_ref, bexp_ref, bd_ref, tril_ref,
                o_ref, ext_ref, state_ref):
    step = pl.program_id(1)
    cl, gw = DN_CHUNK, GROUP_WIDTH
    rows = qkv_ref.shape[0]

    @pl.when(step == 0)
    def _():
        ext_ref[0:CONV_TAIL, :] = jnp.zeros((CONV_TAIL, 3 * gw), F32)
        state_ref[...] = jnp.zeros(state_ref.shape, F32)

    ext_ref[CONV_TAIL:CONV_TAIL + rows, :] = qkv_ref[...]
    cw = convw_ref[...]
    acc = ext_ref[CONV_TAIL:CONV_TAIL + rows, :] * cw[DN_CONV - 1:DN_CONV, :]
    for j in range(DN_CONV - 1):
        acc = acc + ext_ref[pl.ds(CONV_TAIL - (DN_CONV - 1) + j, rows), :] * cw[j:j + 1, :]
    ext_ref[0:CONV_TAIL, :] = ext_ref[rows:rows + CONV_TAIL, :]
    act = _silu(acc)

    bd = bd_ref[...]
    per_head_sum = lambda t: _dot_split_lhs(t, bd)
    q = act[:, 0:gw]
    k = act[:, gw:2 * gw]
    v = act[:, 2 * gw:3 * gw]
    q = q * lax.rsqrt(per_head_sum(q * q) + 1e-6) * (HEAD_DIM ** -0.5)
    k = k * lax.rsqrt(per_head_sum(k * k) + 1e-6)

    ba = _dot_split_lhs(ba_ref[...], bexp_ref[...])
    beta = _sigmoid(ba[:, 0:gw])
    a_in = ba[:, gw:2 * gw] + dtb_ref[...]
    softplus = jnp.maximum(a_in, 0.0) + jnp.log(1.0 + jnp.exp(-jnp.abs(a_in)))
    g = -jnp.exp(alog_ref[...]) * softplus
    tril = tril_ref[...]
    g_hi = g.astype(BF16)
    g_mid, g_lo = _split(g - g_hi.astype(F32))
    gc_all = _dot(tril, g_hi) + (_dot(tril, g_mid) + _dot(tril, g_lo))

    ri = lax.broadcasted_iota(jnp.int32, (cl, cl), 0)
    ci = lax.broadcasted_iota(jnp.int32, (cl, cl), 1)
    causal = ci <= ri
    strict = ci < ri
    eye = jnp.where(ci == ri, 1.0, 0.0)
    lane = lax.broadcasted_iota(jnp.int32, (1, gw), 1)
    head_lanes = [(lane >= h * HEAD_DIM) & (lane < (h + 1) * HEAD_DIM) for h in range(N_HEADS)]

    n_chunks = rows // cl
    pairs = [(c, h) for c in range(n_chunks) for h in range(N_HEADS)]
    chunk = lambda t, c: t[c * cl:(c + 1) * cl]
    gcs = [chunk(gc_all, c) for c in range(n_chunks)]
    gc_ts = [gc.T for gc in gcs]
    g_lasts = [gc[cl - 1:cl, :] for gc in gcs]
    egs = [jnp.exp(gc) for gc in gcs]
    ks = [chunk(k, c) for c in range(n_chunks)]
    k_bfs = [t.astype(BF16) for t in ks]
    k_betas = [chunk(k, c) * chunk(beta, c) for c in range(n_chunks)]
    v_betas = [(chunk(v, c) * chunk(beta, c)).astype(BF16) for c in range(n_chunks)]
    kbgs = [(k_betas[c] * egs[c]).astype(BF16) for c in range(n_chunks)]
    q_decs = [(chunk(q, c) * egs[c]).astype(BF16) for c in range(n_chunks)]
    k_decs = [(ks[c] * jnp.exp(g_lasts[c] - gcs[c])).astype(BF16) for c in range(n_chunks)]
    head = lambda h: slice(h * HEAD_DIM, (h + 1) * HEAD_DIM)
    decays = [jnp.exp(jnp.where(causal, gcs[c][:, head(h)] - gc_ts[c][head(h), :], -jnp.inf)) for c, h in pairs]
    per_head = lambda t: [jnp.where(head_lanes[h], t, 0.0) for h in range(N_HEADS)]
    kq = [_dot_nt(jnp.concatenate(per_head(k_betas[c]) + per_head(chunk(q, c)), axis=0).astype(BF16), k_bfs[c])
          for c in range(n_chunks)]
    kks = [kq[c][h * cl:(h + 1) * cl] for c, h in pairs]
    qks = [kq[c][(N_HEADS + h) * cl:(N_HEADS + h + 1) * cl] for c, h in pairs]
    intras = [jnp.where(causal, qk * d, 0.0) for qk, d in zip(qks, decays)]
    powers = [jnp.where(strict, -(kk * d), 0.0) for kk, d in zip(kks, decays)]
    t_invs = [eye + p for p in powers]
    for _ in range(5):
        p_bfs = [p.astype(BF16) for p in powers]
        powers = [_dot(p, p) for p in p_bfs]
        t_invs = [t + _dot(t.astype(BF16), p.astype(BF16)) for t, p in zip(t_invs, powers)]
    tv = [_dot(jnp.concatenate(t_invs[c * N_HEADS:(c + 1) * N_HEADS], axis=0).astype(BF16),
               jnp.concatenate([v_betas[c], kbgs[c]], axis=1)) for c in range(n_chunks)]

    def merge_heads(stacked, lanes):
        out = jnp.zeros((cl, gw), F32)
        for h in range(N_HEADS):
            out = jnp.where(head_lanes[h], stacked[h * cl:(h + 1) * cl, lanes], out)
        return out

    us = [merge_heads(tv[c], slice(0, gw)) for c in range(n_chunks)]
    ws = [merge_heads(tv[c], slice(gw, 2 * gw)).astype(BF16) for c in range(n_chunks)]
    bd_f = bd.astype(F32)
    tn = (((0,), (0,)), ((), ()))
    kws = [(bd_f * lax.dot_general(k_decs[c], ws[c], tn, preferred_element_type=F32)).astype(BF16)
           for c in range(n_chunks)]
    kus = [bd_f * lax.dot_general(k_decs[c], us[c].astype(BF16), tn, preferred_element_type=F32)
           for c in range(n_chunks)]
    intra_cat = [jnp.concatenate([intras[c * N_HEADS + h] for h in range(N_HEADS)], axis=1).astype(BF16)
                 for c in range(n_chunks)]

    state = state_ref[...]
    outs = []
    for c in range(n_chunks):
        on_state = _dot(jnp.concatenate([ws[c], q_decs[c], kws[c]], axis=0), state.astype(BF16))
        v_new = us[c] - on_state[0:cl]
        v_stack = jnp.concatenate(per_head(v_new), axis=0).astype(BF16)
        outs.append(on_state[cl:2 * cl] + _dot(intra_cat[c], v_stack))
        state = state * jnp.exp(g_lasts[c]) - on_state[2 * cl:] + kus[c]
    state_ref[...] = state

    o = jnp.concatenate(outs, axis=0)
    o = o * lax.rsqrt(per_head_sum(o * o) * (1.0 / HEAD_DIM) + RMS_EPS) * nw_ref[...]
    o_ref[...] = (o * _silu(z_ref[...])).astype(o_ref.dtype)


def gdn_constants():
    lane_head = np.arange(GROUP_WIDTH) // HEAD_DIM
    bd = (lane_head[:, None] == lane_head[None, :]).astype(np.float32)
    bexp = np.zeros((LANES, 2 * GROUP_WIDTH), np.float32)
    for h in range(N_HEADS):
        bexp[h, h * HEAD_DIM:(h + 1) * HEAD_DIM] = 1.0
        bexp[N_HEADS + h, GROUP_WIDTH + h * HEAD_DIM:GROUP_WIDTH + (h + 1) * HEAD_DIM] = 1.0
    r = np.arange(GDN_ROWS)
    tril = ((r[:, None] // DN_CHUNK == r[None, :] // DN_CHUNK) & (r[None, :] <= r[:, None])).astype(np.float32)
    return bexp, bd, tril


def gated_deltanet(qkv, z, ba, conv_w, a_log_rep, dt_bias_rep, norm_w_rep, bexp, bd, tril):
    b, s, w3 = qkv.shape
    gw = GROUP_WIDTH
    blk = lambda width: pl.BlockSpec((None, GDN_ROWS, width), lambda bi, c: (bi, c, 0))
    const = lambda shape: pl.BlockSpec(shape, lambda bi, c: (0, 0))
    as_bf16 = lambda t: jnp.asarray(t, BF16)
    return pl.pallas_call(
        _gdn_kernel,
        grid=(b, s // GDN_ROWS),
        in_specs=[blk(w3), blk(gw), blk(LANES), const(conv_w.shape), const((1, gw)), const((1, gw)),
                  const((1, gw)), const(bexp.shape), const(bd.shape), const(tril.shape)],
        out_specs=blk(gw),
        out_shape=jax.ShapeDtypeStruct((b, s, gw), BF16),
        scratch_shapes=[pltpu.VMEM((CONV_TAIL + GDN_ROWS, w3), F32), pltpu.VMEM((gw, gw), F32)],
        compiler_params=_params("parallel", "arbitrary"),
        name="gated_deltanet",
    )(qkv, z, ba, conv_w, a_log_rep, dt_bias_rep, norm_w_rep, as_bf16(bexp), as_bf16(bd), as_bf16(tril))


def _first_max(values):
    best = values[0]
    for v in values[1:]:
        best = jnp.maximum(best, v)
    taken = jnp.zeros_like(best)
    hot = []
    for v in values:
        h = jnp.where((v == best) & (taken < 0.5), 1.0, 0.0)
        taken = taken + h
        hot.append(h)
    return best, hot


def _softmax_rows(rows):
    m = rows[0]
    for r in rows[1:]:
        m = jnp.maximum(m, r)
    e = [jnp.exp(r - m) for r in rows]
    z = e[0]
    for t in e[1:]:
        z = z + t
    return [t / z for t in e]


def _out_proj_kernel(ya_ref, yb_ref, yc_ref, yd_ref, wo_ref, x_ref, g1_ref, gain_ref, sc_ref, sh_ref,
                     wr_ref, br_ref, xo_ref, h_ref, combt_ref):
    mixed = jnp.concatenate([ya_ref[...], yb_ref[...], yc_ref[...], yd_ref[...]], axis=1)
    x = x_ref[...] + g1_ref[...] * _dot(mixed, wo_ref[...])
    xo_ref[...] = x
    h = _modulated_norm(x, gain_ref[...], sc_ref[...], sh_ref[...])
    h_hi, h_lo = _split(h)
    h_ref[...] = h_hi

    w_hi, w_lo = _split(wr_ref[...])
    on_h_hi = _dot_nt(jnp.concatenate([w_hi, w_lo], axis=0), h_hi)
    logits = on_h_hi[0:LANES] + (on_h_hi[LANES:2 * LANES] + _dot_nt(w_hi, h_lo)) + br_ref[...]
    ng, ne = N_EXPERT_GROUPS, EXPERTS_PER_GROUP
    p_group = _softmax_rows([logits[r:r + 1, :] for r in range(ng)])
    pg_top, g_hot = _first_max(p_group)
    e_logits = []
    for e in range(ne):
        t = g_hot[0] * logits[ng + e:ng + e + 1, :]
        for gi in range(1, ng):
            t = t + g_hot[gi] * logits[ng + gi * ne + e:ng + gi * ne + e + 1, :]
        e_logits.append(t)
    p_exp = _softmax_rows(e_logits)
    p1, hot1 = _first_max(p_exp)
    rest = [jnp.where(h1 > 0.5, -1.0, p) for p, h1 in zip(p_exp, hot1)]
    p2, hot2 = _first_max(rest)
    total = p1 + p2
    w_exp = [(h1 * (p1 / total) + h2 * (p2 / total)) * pg_top for h1, h2 in zip(hot1, hot2)]
    for gi in range(ng):
        for e in range(ne):
            combt_ref[gi * ne + e:gi * ne + e + 1, :] = g_hot[gi] * w_exp[e]


def out_projection(ys, w_out, x, g1, gain, scale, shift, w_router, b_router, seq, row_tile=512):
    n, d = x.shape
    w_router_t = w_router.T
    tiles_per_seq = seq // row_tile
    row = lambda i: (i, 0)
    per_batch = lambda i: (i // tiles_per_seq, 0, 0)
    const = lambda shape: pl.BlockSpec(shape, lambda i: (0, 0))
    mod = pl.BlockSpec((None, 1, d), per_batch)
    return pl.pallas_call(
        _out_proj_kernel,
        grid=(n // row_tile,),
        in_specs=[pl.BlockSpec((row_tile, GROUP_WIDTH), row)] * 4
        + [const(w_out.shape), pl.BlockSpec((row_tile, d), row), mod, const((1, d)), mod, mod,
           const(w_router_t.shape), const(b_router.shape)],
        out_specs=(pl.BlockSpec((row_tile, d), row), pl.BlockSpec((row_tile, d), row),
                   pl.BlockSpec((N_EXPERTS, row_tile), lambda i: (0, i))),
        out_shape=(jax.ShapeDtypeStruct((n, d), F32), jax.ShapeDtypeStruct((n, d), BF16),
                   jax.ShapeDtypeStruct((N_EXPERTS, n), F32)),
        compiler_params=_params("parallel"),
        name="out_projection",
    )(*ys, w_out, x, g1, gain, scale, shift, w_router_t, b_router)


MOE_ROWS = 192


def _moe_kernel(h_ref, combt_ref, before_ref, wg_ref, wu_ref, wd_ref, x_ref, g2_ref, fin_ref, o_ref,
                acc_ref, rank_ref, pick_ref, y_ref, *, final_norm):
    e = pl.program_id(1)
    tn = (((0,), (0,)), ((), ()))

    @pl.when(e == 0)
    def _():
        acc_ref[...] = jnp.zeros(acc_ref.shape, F32)
        routed = jnp.where(combt_ref[...] != 0.0, 1.0, 0.0).astype(BF16)
        rank_ref[...] = _dot(routed, before_ref[...])

    weight_row = combt_ref[pl.ds(e, 1), :]
    rank_row = rank_ref[pl.ds(e, 1), :]
    routed_row = weight_row != 0.0
    count = jnp.sum(jnp.where(routed_row, 1, 0))
    h = h_ref[...]

    def gathered_pass(j):
        slot = (j * MOE_ROWS + lax.broadcasted_iota(jnp.int32, (MOE_ROWS, 1), 0)).astype(F32)
        pick = jnp.where((rank_row == slot) & routed_row, 1.0, 0.0)
        pick_bf = pick.astype(BF16)
        rows = _dot(pick_bf, h).astype(BF16)
        weight = jnp.sum(pick * weight_row, axis=-1, keepdims=True)
        hid = _silu(_dot(rows, wg_ref[...])) * _dot(rows, wu_ref[...]) * weight
        return pick_bf, _dot(hid.astype(BF16), wd_ref[...]).astype(BF16)

    first = pl.ds(pl.multiple_of(e * MOE_ROWS, MOE_ROWS), MOE_ROWS)
    pick_ref[first, :], y_ref[first, :] = gathered_pass(0)

    def extra_pass(j, carry):
        pick_bf, y = gathered_pass(j)
        acc_ref[...] += lax.dot_general(pick_bf, y, tn, preferred_element_type=F32)
        return carry

    lax.fori_loop(1, (count + MOE_ROWS - 1) // MOE_ROWS, extra_pass, 0)

    @pl.when(e == N_EXPERTS - 1)
    def _():
        moe = acc_ref[...] + lax.dot_general(pick_ref[...], y_ref[...], tn, preferred_element_type=F32)
        x = x_ref[...] + g2_ref[...] * moe
        if final_norm:
            x = x * lax.rsqrt(jnp.mean(x * x, axis=-1, keepdims=True) + RMS_EPS) * fin_ref[...]
        o_ref[...] = x


def moe_experts(h, combt, w_gate, w_up, w_down, x, g2, fin_gain, seq, final_norm, row_tile=1024):
    n, d = x.shape
    f = w_gate.shape[-1]
    tiles_per_seq = seq // row_tile
    row = lambda i, e: (i, 0)
    before = np.triu(np.ones((row_tile, row_tile), np.float32), 1)
    return pl.pallas_call(
        functools.partial(_moe_kernel, final_norm=final_norm),
        grid=(n // row_tile, N_EXPERTS),
        in_specs=[pl.BlockSpec((row_tile, d), row),
                  pl.BlockSpec((N_EXPERTS, row_tile), lambda i, e: (0, i)),
                  pl.BlockSpec((row_tile, row_tile), lambda i, e: (0, 0)),
                  pl.BlockSpec((None, d, f), lambda i, e: (e, 0, 0)),
                  pl.BlockSpec((None, d, f), lambda i, e: (e, 0, 0)),
                  pl.BlockSpec((None, f, d), lambda i, e: (e, 0, 0)),
                  pl.BlockSpec((row_tile, d), row),
                  pl.BlockSpec((None, 1, d), lambda i, e: (i // tiles_per_seq, 0, 0)),
                  pl.BlockSpec((1, d), lambda i, e: (0, 0))],
        out_specs=pl.BlockSpec((row_tile, d), row),
        out_shape=jax.ShapeDtypeStruct((n, d), F32),
        scratch_shapes=[pltpu.VMEM((row_tile, d), F32), pltpu.VMEM((N_EXPERTS, row_tile), F32),
                        pltpu.VMEM((N_EXPERTS * MOE_ROWS, row_tile), BF16),
                        pltpu.VMEM((N_EXPERTS * MOE_ROWS, d), BF16)],
        compiler_params=_params("parallel", "arbitrary"),
        name="moe_experts",
    )(h, combt, jnp.asarray(before, BF16), w_gate, w_up, w_down, x, g2, fin_gain)


def kernel(x, c, norm_mix, norm_ffn, final_norm, w_ada, b_ada, w_in, w_out, attn_sink, w_pool, pool_scale, cmp_pos, cmp_w1, cmp_w2, conv_w, a_log, dt_bias, dn_norm, w_route_group, b_route_group, w_route_expert, b_route_expert, w_gate, w_up, w_down):
    batch, seq, d = x.shape
    depth = w_in.shape[0]
    n = batch * seq
    cover, gexp, expand = nsa_constants(seq)
    bexp, bd, tril = gdn_constants()
    mod = ada_modulation(c, w_ada, b_ada)
    spread = lambda t: jnp.repeat(t, HEAD_DIM).reshape(1, GROUP_WIDTH)
    seq3 = lambda t: t.reshape(batch, seq, t.shape[-1])
    xf = x.reshape(n, d)
    for l in range(depth):
        sh1, sc1, g1, sh2, sc2, g2 = (mod[l, :, i * d:(i + 1) * d].reshape(batch, 1, d) for i in range(6))
        p = in_projection(xf, norm_mix[l].reshape(1, d), sc1, sh1, pack_in_weights(w_in[l]), seq)
        y_a = swa_attention(seq3(p["a_q"]), seq3(p["a_k"]), seq3(p["a_v"]), attn_sink[l])
        w_pool_bd = jax.scipy.linalg.block_diag(*[w_pool[l, gi] for gi in range(len(POOL_WINDOWS))])
        y_b = multiscale_pool(seq3(p["b_u"]), w_pool_bd.astype(BF16), pool_scale[l].reshape(1, GROUP_WIDTH))
        kc, vc = nsa_compress(seq3(p["c_cmp"]), *pack_compress_weights(cmp_pos[l], cmp_w1[l], cmp_w2[l]))
        y_c = nsa_attention(seq3(p["c_q"]), seq3(p["c_gate"]), kc, vc, seq3(p["c_ksel"]), seq3(p["c_vsel"]),
                            seq3(p["c_kwin"]), seq3(p["c_vwin"]), cover, gexp, expand)
        y_d = gated_deltanet(seq3(p["d_qkv"]), seq3(p["d_z"]), seq3(p["d_ba"]), conv_w[l], spread(a_log[l]),
                             spread(dt_bias[l]), jnp.tile(dn_norm[l], N_HEADS).reshape(1, GROUP_WIDTH), bexp, bd, tril)
        ys = [t.reshape(n, GROUP_WIDTH) for t in (y_a, y_b, y_c, y_d)]
        n_logits = N_EXPERT_GROUPS + N_EXPERTS
        w_router = jnp.pad(jnp.concatenate([w_route_group[l], w_route_expert[l]], axis=1),
                           ((0, 0), (0, LANES - n_logits)))
        b_router = jnp.pad(jnp.concatenate([b_route_group[l], b_route_expert[l]]),
                           (0, LANES - n_logits)).reshape(LANES, 1)
        xf, h2, comb = out_projection(ys, w_out[l].astype(BF16), xf, g1, norm_ffn[l].reshape(1, d), sc2, sh2,
                                      w_router, b_router, seq)
        f = w_gate.shape[-1]
        xf = moe_experts(h2, comb, w_gate[l].reshape(N_EXPERTS, d, f).astype(BF16),
                         w_up[l].reshape(N_EXPERTS, d, f).astype(BF16),
                         w_down[l].reshape(N_EXPERTS, f, d).astype(BF16),
                         xf, g2, final_norm.reshape(1, d), seq, final_norm=(l == depth - 1))
    return xf.reshape(batch, seq, d)
```

```python
import functools

import numpy as np
import jax
import jax.numpy as jnp
from jax import lax
from jax.experimental import pallas as pl
from jax.experimental.pallas import tpu as pltpu

F32 = jnp.float32
BF16 = jnp.bfloat16
HI = lax.Precision.HIGHEST

HEAD_DIM = 64
N_HEADS = 4
GROUP_WIDTH = N_HEADS * HEAD_DIM
BLOCK = 128
RMS_EPS = 1e-6
SWA_WINDOW = 128
SWA_BLOCKS = 4
POOL_WINDOWS = (2, 4, 8, 16)
POOL_HALO = 16
CMP_LEN = 32
CMP_STRIDE = 16
SEL_BLOCK = 64
NSA_TOP_N = 16
NSA_WINDOW = 512
NSA_FORCE = 1e4
SEL_CHUNK = 512
NSA_BLOCKS = 4
DN_CONV = 4
DN_CHUNK = 64
N_EXPERT_GROUPS = 4
EXPERTS_PER_GROUP = 4
N_EXPERTS = N_EXPERT_GROUPS * EXPERTS_PER_GROUP
LANES = 128
MASKED = -1e30
QK_SCALE = HEAD_DIM ** -0.5 * float(np.log2(np.e))
VMEM_LIMIT = 56 * 1024 * 1024

SEG_WIDTHS = (
    ("a_q", 256), ("a_k", 256), ("a_v", 256), ("c_q", 256),
    ("c_ksel", 128), ("c_vsel", 128), ("c_kwin", 128), ("c_vwin", 128),
    ("b_u", 256), ("c_cmp", 128), ("c_gate", 128), ("d_qkv", 768), ("d_z", 256), ("d_ba", 128),
)
SEG_BF16 = ("a_q", "a_k", "a_v", "c_q", "c_ksel", "c_vsel", "c_kwin", "c_vwin")
SEG_OFFSETS = {}
_off = 0
for _name, _w in SEG_WIDTHS:
    SEG_OFFSETS[_name] = (_off, _w)
    _off += _w
PACKED_WIDTH = _off


def _sigmoid(x):
    return 1.0 / (1.0 + jnp.exp(-x))


def _silu(x):
    return x * _sigmoid(x)


def _dot(a, b, precision=None):
    return jnp.dot(a, b, precision=precision, preferred_element_type=F32)


def _dot_nt(a, b, precision=None):
    return lax.dot_general(a, b, (((1,), (1,)), ((), ())), precision=precision,
                           preferred_element_type=F32)


def _split(a):
    hi = a.astype(BF16)
    return hi, (a - hi.astype(F32)).astype(BF16)


def _dot_split_lhs(a, b):
    hi, lo = _split(a)
    return _dot(hi, b) + _dot(lo, b)


def _params(*semantics):
    return pltpu.CompilerParams(dimension_semantics=semantics, vmem_limit_bytes=VMEM_LIMIT)


def _ada_kernel(c_ref, w_ref, b_ref, o_ref):
    cond = _silu(c_ref[...])
    o_ref[...] = _dot(cond, w_ref[...], HI) + b_ref[...]


def ada_modulation(c, w_ada, b_ada, col_tile=1536):
    depth, d, width = w_ada.shape
    b = c.shape[0]
    return pl.pallas_call(
        _ada_kernel,
        grid=(depth, width // col_tile),
        in_specs=[
            pl.BlockSpec((b, d), lambda l, j: (0, 0)),
            pl.BlockSpec((None, d, col_tile), lambda l, j: (l, 0, j)),
            pl.BlockSpec((None, 1, col_tile), lambda l, j: (l, 0, j)),
        ],
        out_specs=pl.BlockSpec((None, b, col_tile), lambda l, j: (l, 0, j)),
        out_shape=jax.ShapeDtypeStruct((depth, b, width), F32),
        compiler_params=_params("parallel", "parallel"),
        name="ada_modulation",
    )(c, w_ada, b_ada.reshape(depth, 1, width))


def _modulated_norm(x, gain, scale, shift):
    y = x * lax.rsqrt(jnp.mean(x * x, axis=-1, keepdims=True) + RMS_EPS)
    return y * gain * (1.0 + scale) + shift


def _in_proj_kernel(x_ref, gain_ref, sc_ref, sh_ref, w_ref, *out_refs):
    h = _modulated_norm(x_ref[...], gain_ref[...], sc_ref[...], sh_ref[...]).astype(BF16)
    y = _dot(h, w_ref[...])
    for (name, _), o_ref in zip(SEG_WIDTHS, out_refs):
        off, width = SEG_OFFSETS[name]
        o_ref[...] = y[:, off:off + width].astype(o_ref.dtype)


def in_projection(x, gain, scale, shift, w_packed, seq, row_tile=512):
    n, d = x.shape
    tiles_per_seq = seq // row_tile
    row = lambda i: (i, 0)
    per_batch = lambda i: (i // tiles_per_seq, 0, 0)
    out_shape = tuple(
        jax.ShapeDtypeStruct((n, w), BF16 if name in SEG_BF16 else F32) for name, w in SEG_WIDTHS)
    outs = pl.pallas_call(
        _in_proj_kernel,
        grid=(n // row_tile,),
        in_specs=[
            pl.BlockSpec((row_tile, d), row),
            pl.BlockSpec((1, d), lambda i: (0, 0)),
            pl.BlockSpec((None, 1, d), per_batch),
            pl.BlockSpec((None, 1, d), per_batch),
            pl.BlockSpec((d, PACKED_WIDTH), lambda i: (0, 0)),
        ],
        out_specs=tuple(pl.BlockSpec((row_tile, w), row) for _, w in SEG_WIDTHS),
        out_shape=out_shape,
        compiler_params=_params("parallel"),
        name="in_projection",
    )(x, gain, scale, shift, w_packed)
    return dict(zip((name for name, _ in SEG_WIDTHS), outs))


def pack_in_weights(w_in):
    gw, hd = GROUP_WIDTH, HEAD_DIM
    sizes = (gw, 2 * hd, 2 * hd, gw, gw, 6 * hd, 3 * N_HEADS, 3 * gw, gw, N_HEADS, N_HEADS)
    offs = np.concatenate([[0], np.cumsum(sizes)])
    (a_q, a_k, a_v, b_u, c_q, c_kv, c_gate, d_qkv, d_z, d_beta, d_a) = (
        w_in[:, offs[i]:offs[i + 1]] for i in range(len(sizes)))
    d = w_in.shape[0]
    dup = lambda t: jnp.concatenate([t, t], axis=1)
    pad = lambda t: jnp.pad(t, ((0, 0), (0, LANES - t.shape[1])))
    k_cmp, v_cmp, k_sel, v_sel, k_win, v_win = (c_kv[:, i * hd:(i + 1) * hd] for i in range(6))
    a_q = a_q * QK_SCALE
    c_q = c_q * QK_SCALE
    segs = {
        "a_q": a_q,
        "a_k": jnp.concatenate([dup(a_k[:, :hd]), dup(a_k[:, hd:])], axis=1),
        "a_v": jnp.concatenate([dup(a_v[:, :hd]), dup(a_v[:, hd:])], axis=1),
        "c_q": c_q,
        "c_ksel": dup(k_sel), "c_vsel": dup(v_sel), "c_kwin": dup(k_win), "c_vwin": dup(v_win),
        "b_u": b_u,
        "c_cmp": jnp.concatenate([k_cmp, v_cmp], axis=1),
        "c_gate": pad(c_gate),
        "d_qkv": d_qkv, "d_z": d_z,
        "d_ba": pad(jnp.concatenate([d_beta, d_a], axis=1)),
    }
    packed = jnp.concatenate([segs[name] for name, _ in SEG_WIDTHS], axis=1)
    assert packed.shape == (d, PACKED_WIDTH)
    return packed.astype(BF16)


def _stack_heads(slab):
    lane = lax.broadcasted_iota(jnp.int32, slab.shape, 1)
    zero = jnp.zeros_like(slab)
    return jnp.concatenate([jnp.where(lane < HEAD_DIM, slab, zero),
                            jnp.where(lane >= HEAD_DIM, slab, zero)], axis=0)


def _unstack_heads(o, rows):
    lane = lax.broadcasted_iota(jnp.int32, (rows, LANES), 1)
    return jnp.where(lane < HEAD_DIM, o[0:rows], o[rows:2 * rows])


def _swa_kernel(sink_ref, q_ref, kp_ref, kc_ref, vp_ref, vc_ref, o_ref):
    i = pl.program_id(1)
    row = lax.broadcasted_iota(jnp.int32, (2 * BLOCK, 2 * BLOCK), 0)
    col = lax.broadcasted_iota(jnp.int32, (2 * BLOCK, 2 * BLOCK), 1)
    tq = row & (BLOCK - 1)
    tk = col - BLOCK
    banded = (tk <= tq) & (tk > tq - SWA_WINDOW)
    first_valid = banded & ((col >= BLOCK) | (i > 0))
    rowh = lax.broadcasted_iota(jnp.int32, (2 * BLOCK, 1), 0)
    log2e = float(np.log2(np.e))
    sinks = [jnp.where(rowh < BLOCK, sink_ref[2 * j], sink_ref[2 * j + 1]) * log2e for j in range(N_HEADS // 2)]
    pairs = [(sb, j) for sb in range(SWA_BLOCKS) for j in range(N_HEADS // 2)]

    def keys(cur_ref, prev_ref, sb, lanes):
        if sb == 0:
            return jnp.concatenate([prev_ref[:, lanes], cur_ref[0:BLOCK, lanes]], axis=0)
        return cur_ref[(sb - 1) * BLOCK:(sb + 1) * BLOCK, lanes]

    lanes_of = lambda j: slice(j * LANES, (j + 1) * LANES)
    s = [jnp.where(first_valid if sb == 0 else banded,
                   _dot_nt(_stack_heads(q_ref[sb * BLOCK:(sb + 1) * BLOCK, lanes_of(j)]),
                           keys(kc_ref, kp_ref, sb, lanes_of(j))), -jnp.inf) for sb, j in pairs]
    m = [jnp.maximum(jnp.max(t, axis=-1, keepdims=True), sinks[j]) for t, (sb, j) in zip(s, pairs)]
    p = [jnp.exp2(t - mm) for t, mm in zip(s, m)]
    denom = [jnp.sum(t, axis=-1, keepdims=True) + jnp.exp2(sinks[j] - mm) for t, mm, (sb, j) in zip(p, m, pairs)]
    o = [_dot(t.astype(BF16), keys(vc_ref, vp_ref, sb, lanes_of(j))) / d for t, d, (sb, j) in zip(p, denom, pairs)]
    for sb in range(SWA_BLOCKS):
        slabs = [_unstack_heads(o[sb * (N_HEADS // 2) + j], BLOCK) for j in range(N_HEADS // 2)]
        o_ref[sb * BLOCK:(sb + 1) * BLOCK, :] = jnp.concatenate(slabs, axis=1).astype(o_ref.dtype)


def swa_attention(q, k, v, sink):
    b, s, w = q.shape
    rows = SWA_BLOCKS * BLOCK
    cur = pl.BlockSpec((None, rows, w), lambda bi, i: (bi, i, 0))
    prev = pl.BlockSpec((None, BLOCK, w), lambda bi, i: (bi, jnp.maximum(i * SWA_BLOCKS - 1, 0), 0))
    return pl.pallas_call(
        _swa_kernel,
        grid=(b, s // rows),
        in_specs=[pl.BlockSpec(memory_space=pltpu.SMEM), cur, prev, cur, prev, cur],
        out_specs=cur,
        out_shape=jax.ShapeDtypeStruct((b, s, w), BF16),
        compiler_params=_params("parallel", "parallel"),
        name="swa_attention",
    )(sink, q, k, k, v, v)


def _pool_kernel(up_ref, u_ref, w_ref, scale_ref, o_ref, ext_ref):
    i = pl.program_id(1)
    rows = u_ref.shape[0]
    u = u_ref[...]
    halo = up_ref[...]
    ext_ref[0:POOL_HALO, :] = jnp.where(i > 0, halo, jnp.zeros_like(halo))
    ext_ref[POOL_HALO:POOL_HALO + rows, :] = u
    lane = lax.broadcasted_iota(jnp.int32, u.shape, 1)
    pos = i * rows + lax.broadcasted_iota(jnp.int32, u.shape, 0)
    group_ch = GROUP_WIDTH // len(POOL_WINDOWS)
    total = u
    d = jnp.zeros_like(u)
    width = 1
    for gi, w in enumerate(POOL_WINDOWS):
        while width < w:
            total = total + ext_ref[pl.ds(POOL_HALO - width, rows), :]
            width += 1
        cnt = jnp.minimum(pos + 1, w).astype(F32)
        in_group = (lane >= gi * group_ch) & (lane < (gi + 1) * group_ch)
        d = jnp.where(in_group, total / cnt - u, d)
    o_ref[...] = (_dot(d.astype(BF16), w_ref[...]) * scale_ref[...]).astype(o_ref.dtype)


def multiscale_pool(u, w_blockdiag, pool_scale, row_tile=512):
    b, s, w = u.shape
    halo_per_tile = row_tile // POOL_HALO
    return pl.pallas_call(
        _pool_kernel,
        grid=(b, s // row_tile),
        in_specs=[
            pl.BlockSpec((None, POOL_HALO, w), lambda bi, i: (bi, jnp.maximum(i * halo_per_tile - 1, 0), 0)),
            pl.BlockSpec((None, row_tile, w), lambda bi, i: (bi, i, 0)),
            pl.BlockSpec((w, w), lambda bi, i: (0, 0)),
            pl.BlockSpec((1, w), lambda bi, i: (0, 0)),
        ],
        out_specs=pl.BlockSpec((None, row_tile, w), lambda bi, i: (bi, i, 0)),
        out_shape=jax.ShapeDtypeStruct((b, s, w), BF16),
        scratch_shapes=[pltpu.VMEM((POOL_HALO + row_tile, w), F32)],
        compiler_params=_params("parallel", "parallel"),
        name="multiscale_pool",
    )(u, u, w_blockdiag, pool_scale)


def _compress_kernel(x_ref, w1_ref, pos_ref, w2k_ref, w2v_ref, kc_ref, vc_ref):
    n_chunks = x_ref.shape[0]
    both = _dot(x_ref[...], w1_ref[...], HI)
    pre = both[:, 0:LANES] + pltpu.roll(both[:, LANES:2 * LANES], n_chunks - 1, 0) + pos_ref[...]
    hid = _silu(pre)
    kc_ref[...] = _dot(hid, w2k_ref[...], HI).astype(kc_ref.dtype)
    vc_ref[...] = _dot(hid, w2v_ref[...], HI).astype(vc_ref.dtype)


def nsa_compress(cmp_in, w1_packed, pos_term, w2k, w2v):
    b, s, w = cmp_in.shape
    n_chunks = s // CMP_STRIDE
    flat = cmp_in.reshape(b, n_chunks, CMP_STRIDE * w)
    const = lambda shape: pl.BlockSpec(shape, lambda bi: tuple(0 for _ in shape))
    out = jax.ShapeDtypeStruct((b, n_chunks, LANES), BF16)
    return pl.pallas_call(
        _compress_kernel,
        grid=(b,),
        in_specs=[pl.BlockSpec((None, n_chunks, CMP_STRIDE * w), lambda bi: (bi, 0, 0)),
                  const(w1_packed.shape), const(pos_term.shape), const(w2k.shape), const(w2v.shape)],
        out_specs=(pl.BlockSpec((None, n_chunks, LANES), lambda bi: (bi, 0, 0)),) * 2,
        out_shape=(out, out),
        compiler_params=_params("parallel"),
        name="nsa_compress",
    )(flat, w1_packed, pos_term, w2k, w2v)


def pack_compress_weights(cmp_pos, cmp_w1, cmp_w2):
    hd, half = HEAD_DIM, CMP_LEN // 2
    w1 = cmp_w1.reshape(2, 2, half, hd, hd)
    zeros = jnp.zeros((half, hd, hd), F32)
    halves = []
    for part in range(2):
        wk = jnp.concatenate([w1[0, part], zeros], axis=-1)
        wv = jnp.concatenate([zeros, w1[1, part]], axis=-1)
        halves.append(jnp.concatenate([wk, wv], axis=1).reshape(half * 2 * hd, 2 * hd))
    w1_packed = jnp.concatenate(halves, axis=1)
    pos_flat = cmp_pos.reshape(2, 1, CMP_LEN * hd)
    pos_term = jnp.concatenate([jnp.matmul(pos_flat[0], cmp_w1[0], precision=HI),
                                jnp.matmul(pos_flat[1], cmp_w1[1], precision=HI)], axis=1)
    zero2 = jnp.zeros((hd, 2 * hd), F32)
    w2k = jnp.concatenate([jnp.concatenate([cmp_w2[0], cmp_w2[0]], axis=1), zero2], axis=0)
    w2v = jnp.concatenate([zero2, jnp.concatenate([cmp_w2[1], cmp_w2[1]], axis=1)], axis=0)
    return w1_packed, pos_term, w2k, w2v


def _nsa_kernel(q_ref, gate_ref, kc_ref, vc_ref, ksel_ref, vsel_ref, kwin_ref, vwin_ref,
                cover_ref, gexp_ref, expand_ref, o_ref, q4_ref, m_ref, acc_ref, sa_ref, sb_ref, ma_ref, mb_ref):
    i = pl.program_id(1)
    nqb = NSA_BLOCKS
    groups = [(b, h) for b in range(nqb) for h in range(N_HEADS)]
    group_rows = [slice(g * BLOCK, (g + 1) * BLOCK) for g in range(len(groups))]
    q4 = jnp.concatenate([_stack_heads(q_ref[b * BLOCK:(b + 1) * BLOCK, s * LANES:(s + 1) * LANES])
                          for b in range(nqb) for s in range(N_HEADS // 2)], axis=0)
    tq_blk = [(i * nqb + b) * BLOCK + lax.broadcasted_iota(jnp.int32, (BLOCK, 1), 0) for b in range(nqb)]

    n_cmp = kc_ref.shape[0]
    n_idx = lax.broadcasted_iota(jnp.int32, (1, n_cmp), 1)
    valid_c = [(n_idx * CMP_STRIDE + (CMP_LEN - 1) <= tq_blk[b]) & (n_idx < n_cmp - 1) for b in range(nqb)]
    k_c = kc_ref[...]
    v_c = vc_ref[...]
    s_c_all = _dot_nt(q4, k_c)
    s_c = [jnp.where(valid_c[b], s_c_all[gr], -jnp.inf) for (b, h), gr in zip(groups, group_rows)]
    m_c = [jnp.max(t, axis=-1, keepdims=True) for t in s_c]
    m_c = [jnp.where(t == -jnp.inf, 0.0, t) for t in m_c]
    p_c = [jnp.exp2(t - mm) for t, mm in zip(s_c, m_c)]
    d_c = [jnp.sum(t, axis=-1, keepdims=True) for t in p_c]
    p_c = [t / jnp.where(dd > 0, dd, 1.0) for t, dd in zip(p_c, d_c)]
    o_c = [_dot(t.astype(BF16), v_c) for t in p_c]

    value_lane = lax.broadcasted_iota(jnp.int32, (1, LANES), 1) < HEAD_DIM

    def with_ones(v):
        return jnp.where(value_lane, v, jnp.ones_like(v))

    span = NSA_WINDOW + BLOCK
    start_w = [pl.multiple_of(jnp.maximum((i * nqb + b) * BLOCK - NSA_WINDOW, 0), BLOCK) for b in range(nqb)]
    tk = [start_w[b] + lax.broadcasted_iota(jnp.int32, (1, span), 1) for b in range(nqb)]
    valid_w = [(tk[b] <= tq_blk[b]) & (tk[b] > tq_blk[b] - NSA_WINDOW) for b in range(nqb)]
    k_w = [kwin_ref[pl.ds(start_w[b], span), :] for b in range(nqb)]
    v_w = [with_ones(vwin_ref[pl.ds(start_w[b], span), :]) for b in range(nqb)]
    acc_w = []

    def window_group(g):
        b = groups[g][0]
        s_w = jnp.where(valid_w[b], _dot_nt(q4[group_rows[g]], k_w[b]), -jnp.inf)
        p_w = jnp.exp2(s_w - jnp.max(s_w, axis=-1, keepdims=True))
        acc_w.append(_dot(p_w.astype(BF16), v_w[b]))

    cover = cover_ref[...]
    importance = []
    for b in range(nqb):
        p_b = p_c[b * N_HEADS:(b + 1) * N_HEADS]
        p_hi, p_lo = _split((p_b[0] + p_b[1]) + (p_b[2] + p_b[3]))
        importance.append(_dot_nt(cover, p_hi) + _dot_nt(cover, p_lo))
    importance = jnp.concatenate(importance, axis=1)
    n_sel = cover_ref.shape[0]
    n_q = nqb * BLOCK
    blk = lax.broadcasted_iota(jnp.int32, (n_sel, n_q), 0)
    t_lane = i * n_q + lax.broadcasted_iota(jnp.int32, (n_sel, n_q), 1)
    cur = t_lane // SEL_BLOCK
    causal = blk * SEL_BLOCK <= t_lane
    forced = (blk == 0) | (blk == cur) | (blk == cur - 1)
    score = jnp.where(causal, jnp.where(forced, NSA_FORCE, importance), -jnp.inf)
    blk_f = blk.astype(F32)
    chosen = jnp.where(forced, 1.0, 0.0)
    score = jnp.where(forced, -jnp.inf, score)
    n_rounds = NSA_TOP_N - 3
    n_groups = len(groups)
    for r in range(n_rounds):
        for g in range(r * n_groups // n_rounds, (r + 1) * n_groups // n_rounds):
            window_group(g)
        top = jnp.max(score, axis=0, keepdims=True)
        first = jnp.min(jnp.where(score == top, blk_f, float(n_sel)), axis=0, keepdims=True)
        pick = blk_f == first
        score = jnp.where(pick, -jnp.inf, score)
        chosen = jnp.where(pick, 1.0, chosen)
    chosen = jnp.where(causal, chosen, 0.0)
    chosen_q = [chosen[:, b * BLOCK:(b + 1) * BLOCK].T.astype(BF16) for b in range(nqb)]

    q4_ref[...] = q4
    m_ref[...] = jnp.full(m_ref.shape, MASKED, F32)
    acc_ref[...] = jnp.zeros(acc_ref.shape, F32)
    n_chunks = (i * n_q + n_q - 1) // SEL_CHUNK + 1

    def chunk_start(c):
        return pl.multiple_of(jnp.minimum(c, n_chunks - 1) * SEL_CHUNK, SEL_CHUNK)

    def chunk_keep(c):
        start = chunk_start(c)
        key = start + lax.broadcasted_iota(jnp.int32, (1, SEL_CHUNK), 1)
        spread = expand_ref[:, pl.ds(start, SEL_CHUNK)]
        return [jnp.where((key <= tq_blk[b]) & (c < n_chunks), _dot(chosen_q[b], spread), 0.0) > 0.5
                for b in range(nqb)]

    def stage_group(g, bufs, s_all, keep):
        s = jnp.where(keep[groups[g][0]], s_all[group_rows[g]], MASKED)
        bufs[0][group_rows[g]] = s
        bufs[1][group_rows[g]] = jnp.max(s, axis=-1, keepdims=True)

    def half_step(c, cur_bufs, next_bufs):
        keep_next = chunk_keep(c + 1)
        v_aug = with_ones(vsel_ref[pl.ds(chunk_start(c), SEL_CHUNK), :])
        s_next = _dot_nt(q4_ref[...], ksel_ref[pl.ds(chunk_start(c + 1), SEL_CHUNK), :])
        probs, m_olds, m_news = [], [], []
        for g in range(n_groups):
            stage_group(g, next_bufs, s_next, keep_next)
            m_old = m_ref[group_rows[g]]
            m_new = jnp.maximum(m_old, cur_bufs[1][group_rows[g]])
            m_ref[group_rows[g]] = m_new
            probs.append(jnp.exp2(cur_bufs[0][group_rows[g]] - m_new).astype(BF16))
            m_olds.append(m_old)
            m_news.append(m_new)
        pv = _dot(jnp.concatenate(probs, axis=0), v_aug)
        alpha = jnp.exp2(jnp.concatenate(m_olds, axis=0) - jnp.concatenate(m_news, axis=0))
        acc_ref[...] = alpha * acc_ref[...] + pv

    bufs_a = (sa_ref, ma_ref)
    bufs_b = (sb_ref, mb_ref)
    keep0 = chunk_keep(0)
    s_first = _dot_nt(q4, ksel_ref[0:SEL_CHUNK, :])
    for g in range(n_groups):
        stage_group(g, bufs_a, s_first, keep0)

    def sel_step(t, carry):
        half_step(2 * t, bufs_a, bufs_b)

        @pl.when(2 * t + 1 < n_chunks)
        def _():
            half_step(2 * t + 1, bufs_b, bufs_a)

        return carry

    lax.fori_loop(0, (n_chunks + 1) // 2, sel_step, 0)

    def heads_to_lanes(heads):
        return jnp.concatenate([_unstack_heads(jnp.concatenate(heads[2 * j:2 * j + 2], axis=0), BLOCK)
                                for j in range(N_HEADS // 2)], axis=1)

    def normalized_heads_to_lanes(heads):
        slabs = []
        for j in range(N_HEADS // 2):
            even, odd = heads[2 * j], heads[2 * j + 1]
            numer = jnp.where(value_lane, even, pltpu.roll(odd, HEAD_DIM, 1))
            denom = jnp.where(value_lane, pltpu.roll(even, HEAD_DIM, 1), odd)
            slabs.append(numer / denom)
        return jnp.concatenate(slabs, axis=1)

    gates = _dot_split_lhs(_sigmoid(gate_ref[...]), gexp_ref[...])
    gw = GROUP_WIDTH
    for b in range(nqb):
        of_block = slice(b * N_HEADS, (b + 1) * N_HEADS)
        qr = slice(b * BLOCK, (b + 1) * BLOCK)
        acc_sel = [acc_ref[gr] for gr in group_rows[of_block]]
        out = (gates[qr, 0:gw] * heads_to_lanes(o_c[of_block])
               + gates[qr, gw:2 * gw] * normalized_heads_to_lanes(acc_sel)
               + gates[qr, 2 * gw:3 * gw] * normalized_heads_to_lanes(acc_w[of_block]))
        o_ref[qr, :] = out.astype(o_ref.dtype)


def nsa_constants(seq):
    n_cmp_rows = seq // CMP_STRIDE
    n_sel = seq // SEL_BLOCK
    cmp_start = np.arange(n_cmp_rows) * CMP_STRIDE
    sel_start = np.arange(n_sel) * SEL_BLOCK
    cover = np.maximum(np.minimum(cmp_start[None, :] + CMP_LEN, sel_start[:, None] + SEL_BLOCK)
                       - np.maximum(cmp_start[None, :], sel_start[:, None]), 0).astype(np.float32) / CMP_LEN
    gexp = np.zeros((LANES, 3 * GROUP_WIDTH), np.float32)
    for h in range(N_HEADS):
        for br in range(3):
            gexp[h * 3 + br, br * GROUP_WIDTH + h * HEAD_DIM: br * GROUP_WIDTH + (h + 1) * HEAD_DIM] = 1.0
    expand = (np.arange(n_sel)[:, None] == np.arange(seq)[None, :] // SEL_BLOCK).astype(np.float32)
    return cover, gexp, expand


def nsa_attention(q, gate, kc, vc, ksel, vsel, kwin, vwin, cover, gexp, expand):
    b, s, w = q.shape
    assert s >= NSA_WINDOW + BLOCK and s % SEL_CHUNK == 0
    n_q = NSA_BLOCKS * BLOCK
    blk = lambda width: pl.BlockSpec((None, n_q, width), lambda bi, i: (bi, i, 0))
    per_batch = lambda rows: pl.BlockSpec((None, rows, LANES), lambda bi, i: (bi, 0, 0))
    const = lambda shape: pl.BlockSpec(shape, lambda bi, i: (0, 0))
    rows = NSA_BLOCKS * N_HEADS * BLOCK
    return pl.pallas_call(
        _nsa_kernel,
        grid=(b, s // n_q),
        in_specs=[blk(w), blk(LANES), per_batch(kc.shape[1]), per_batch(vc.shape[1]),
                  per_batch(s), per_batch(s), per_batch(s), per_batch(s),
                  const(cover.shape), const(gexp.shape), const(expand.shape)],
        out_specs=blk(w),
        out_shape=jax.ShapeDtypeStruct((b, s, w), BF16),
        scratch_shapes=[pltpu.VMEM((rows, LANES), BF16), pltpu.VMEM((rows, 1), F32),
                        pltpu.VMEM((rows, LANES), F32), pltpu.VMEM((rows, SEL_CHUNK), F32),
                        pltpu.VMEM((rows, SEL_CHUNK), F32), pltpu.VMEM((rows, 1), F32),
                        pltpu.VMEM((rows, 1), F32)],
        compiler_params=_params("parallel", "arbitrary"),
        name="nsa_attention",
    )(q, gate, kc, vc, ksel, vsel, kwin, vwin, jnp.asarray(cover, BF16), jnp.asarray(gexp, BF16),
      jnp.asarray(expand, BF16))


CONV_TAIL = 8


GDN_ROWS = 512


def _gdn_kernel(qkv_ref, z_ref, ba_ref, convw_ref, alog_ref, dtb_ref, nw_ref, bexp_ref, bd_ref, tril_ref,
                o_ref, ext_ref, state_ref):
    step = pl.program_id(1)
    cl, gw = DN_CHUNK, GROUP_WIDTH
    rows = qkv_ref.shape[0]

    @pl.when(step == 0)
    def _():
        ext_ref[0:CONV_TAIL, :] = jnp.zeros((CONV_TAIL, 3 * gw), F32)
        state_ref[...] = jnp.zeros(state_ref.shape, F32)

    ext_ref[CONV_TAIL:CONV_TAIL + rows, :] = qkv_ref[...]
    cw = convw_ref[...]
    acc = ext_ref[CONV_TAIL:CONV_TAIL + rows, :] * cw[DN_CONV - 1:DN_CONV, :]
    for j in range(DN_CONV - 1):
        acc = acc + ext_ref[pl.ds(CONV_TAIL - (DN_CONV - 1) + j, rows), :] * cw[j:j + 1, :]
    ext_ref[0:CONV_TAIL, :] = ext_ref[rows:rows + CONV_TAIL, :]
    act = _silu(acc)

    bd = bd_ref[...]
    per_head_sum = lambda t: _dot_split_lhs(t, bd)
    q = act[:, 0:gw]
    k = act[:, gw:2 * gw]
    v = act[:, 2 * gw:3 * gw]
    q = q * lax.rsqrt(per_head_sum(q * q) + 1e-6) * (HEAD_DIM ** -0.5)
    k = k * lax.rsqrt(per_head_sum(k * k) + 1e-6)

    ba = _dot_split_lhs(ba_ref[...], bexp_ref[...])
    beta = _sigmoid(ba[:, 0:gw])
    a_in = ba[:, gw:2 * gw] + dtb_ref[...]
    softplus = jnp.maximum(a_in, 0.0) + jnp.log(1.0 + jnp.exp(-jnp.abs(a_in)))
    g = -jnp.exp(alog_ref[...]) * softplus
    tril = tril_ref[...]
    g_hi = g.astype(BF16)
    g_mid, g_lo = _split(g - g_hi.astype(F32))
    gc_all = _dot(tril, g_hi) + (_dot(tril, g_mid) + _dot(tril, g_lo))

    ri = lax.broadcasted_iota(jnp.int32, (cl, cl), 0)
    ci = lax.broadcasted_iota(jnp.int32, (cl, cl), 1)
    causal = ci <= ri
    strict = ci < ri
    eye = jnp.where(ci == ri, 1.0, 0.0)
    lane = lax.broadcasted_iota(jnp.int32, (1, gw), 1)
    head_lanes = [(lane >= h * HEAD_DIM) & (lane < (h + 1) * HEAD_DIM) for h in range(N_HEADS)]

    n_chunks = rows // cl
    pairs = [(c, h) for c in range(n_chunks) for h in range(N_HEADS)]
    chunk = lambda t, c: t[c * cl:(c + 1) * cl]
    gcs = [chunk(gc_all, c) for c in range(n_chunks)]
    gc_ts = [gc.T for gc in gcs]
    g_lasts = [gc[cl - 1:cl, :] for gc in gcs]
    egs = [jnp.exp(gc) for gc in gcs]
    ks = [chunk(k, c) for c in range(n_chunks)]
    k_bfs = [t.astype(BF16) for t in ks]
    k_betas = [chunk(k, c) * chunk(beta, c) for c in range(n_chunks)]
    v_betas = [(chunk(v, c) * chunk(beta, c)).astype(BF16) for c in range(n_chunks)]
    kbgs = [(k_betas[c] * egs[c]).astype(BF16) for c in range(n_chunks)]
    q_decs = [(chunk(q, c) * egs[c]).astype(BF16) for c in range(n_chunks)]
    k_decs = [(ks[c] * jnp.exp(g_lasts[c] - gcs[c])).astype(BF16) for c in range(n_chunks)]
    head = lambda h: slice(h * HEAD_DIM, (h + 1) * HEAD_DIM)
    decays = [jnp.exp(jnp.where(causal, gcs[c][:, head(h)] - gc_ts[c][head(h), :], -jnp.inf)) for c, h in pairs]
    per_head = lambda t: [jnp.where(head_lanes[h], t, 0.0) for h in range(N_HEADS)]
    kq = [_dot_nt(jnp.concatenate(per_head(k_betas[c]) + per_head(chunk(q, c)), axis=0).astype(BF16), k_bfs[c])
          for c in range(n_chunks)]
    kks = [kq[c][h * cl:(h + 1) * cl] for c, h in pairs]
    qks = [kq[c][(N_HEADS + h) * cl:(N_HEADS + h + 1) * cl] for c, h in pairs]
    intras = [jnp.where(causal, qk * d, 0.0) for qk, d in zip(qks, decays)]
    powers = [jnp.where(strict, -(kk * d), 0.0) for kk, d in zip(kks, decays)]
    t_invs = [eye + p for p in powers]
    for _ in range(5):
        p_bfs = [p.astype(BF16) for p in powers]
        powers = [_dot(p, p) for p in p_bfs]
        t_invs = [t + _dot(t.astype(BF16), p.astype(BF16)) for t, p in zip(t_invs, powers)]
    tv = [_dot(jnp.concatenate(t_invs[c * N_HEADS:(c + 1) * N_HEADS], axis=0).astype(BF16),
               jnp.concatenate([v_betas[c], kbgs[c]], axis=1)) for c in range(n_chunks)]

    def merge_heads(stacked, lanes):
        out = jnp.zeros((cl, gw), F32)
        for h in range(N_HEADS):
            out = jnp.where(head_lanes[h], stacked[h * cl:(h + 1) * cl, lanes], out)
        return out

    us = [merge_heads(tv[c], slice(0, gw)) for c in range(n_chunks)]
    ws = [merge_heads(tv[c], slice(gw, 2 * gw)).astype(BF16) for c in range(n_chunks)]
    bd_f = bd.astype(F32)
    tn = (((0,), (0,)), ((), ()))
    kws = [(bd_f * lax.dot_general(k_decs[c], ws[c], tn, preferred_element_type=F32)).astype(BF16)
           for c in range(n_chunks)]
    kus = [bd_f * lax.dot_general(k_decs[c], us[c].astype(BF16), tn, preferred_element_type=F32)
           for c in range(n_chunks)]
    intra_cat = [jnp.concatenate([intras[c * N_HEADS + h] for h in range(N_HEADS)], axis=1).astype(BF16)
                 for c in range(n_chunks)]

    state = state_ref[...]
    outs = []
    for c in range(n_chunks):
        on_state = _dot(jnp.concatenate([ws[c], q_decs[c], kws[c]], axis=0), state.astype(BF16))
        v_new = us[c] - on_state[0:cl]
        v_stack = jnp.concatenate(per_head(v_new), axis=0).astype(BF16)
        outs.append(on_state[cl:2 * cl] + _dot(intra_cat[c], v_stack))
        state = state * jnp.exp(g_lasts[c]) - on_state[2 * cl:] + kus[c]
    state_ref[...] = state

    o = jnp.concatenate(outs, axis=0)
    o = o * lax.rsqrt(per_head_sum(o * o) * (1.0 / HEAD_DIM) + RMS_EPS) * nw_ref[...]
    o_ref[...] = (o * _silu(z_ref[...])).astype(o_ref.dtype)


def gdn_constants():
    lane_head = np.arange(GROUP_WIDTH) // HEAD_DIM
    bd = (lane_head[:, None] == lane_head[None, :]).astype(np.float32)
    bexp = np.zeros((LANES, 2 * GROUP_WIDTH), np.float32)
    for h in range(N_HEADS):
        bexp[h, h * HEAD_DIM:(h + 1) * HEAD_DIM] = 1.0
        bexp[N_HEADS + h, GROUP_WIDTH + h * HEAD_DIM:GROUP_WIDTH + (h + 1) * HEAD_DIM] = 1.0
    r = np.arange(GDN_ROWS)
    tril = ((r[:, None] // DN_CHUNK == r[None, :] // DN_CHUNK) & (r[None, :] <= r[:, None])).astype(np.float32)
    return bexp, bd, tril


def gated_deltanet(qkv, z, ba, conv_w, a_log_rep, dt_bias_rep, norm_w_rep, bexp, bd, tril):
    b, s, w3 = qkv.shape
    gw = GROUP_WIDTH
    blk = lambda width: pl.BlockSpec((None, GDN_ROWS, width), lambda bi, c: (bi, c, 0))
    const = lambda shape: pl.BlockSpec(shape, lambda bi, c: (0, 0))
    as_bf16 = lambda t: jnp.asarray(t, BF16)
    return pl.pallas_call(
        _gdn_kernel,
        grid=(b, s // GDN_ROWS),
        in_specs=[blk(w3), blk(gw), blk(LANES), const(conv_w.shape), const((1, gw)), const((1, gw)),
                  const((1, gw)), const(bexp.shape), const(bd.shape), const(tril.shape)],
        out_specs=blk(gw),
        out_shape=jax.ShapeDtypeStruct((b, s, gw), BF16),
        scratch_shapes=[pltpu.VMEM((CONV_TAIL + GDN_ROWS, w3), F32), pltpu.VMEM((gw, gw), F32)],
        compiler_params=_params("parallel", "arbitrary"),
        name="gated_deltanet",
    )(qkv, z, ba, conv_w, a_log_rep, dt_bias_rep, norm_w_rep, as_bf16(bexp), as_bf16(bd), as_bf16(tril))


def _first_max(values):
    best = values[0]
    for v in values[1:]:
        best = jnp.maximum(best, v)
    taken = jnp.zeros_like(best)
    hot = []
    for v in values:
        h = jnp.where((v == best) & (taken < 0.5), 1.0, 0.0)
        taken = taken + h
        hot.append(h)
    return best, hot


def _softmax_rows(rows):
    m = rows[0]
    for r in rows[1:]:
        m = jnp.maximum(m, r)
    e = [jnp.exp(r - m) for r in rows]
    z = e[0]
    for t in e[1:]:
        z = z + t
    return [t / z for t in e]


def _out_proj_kernel(ya_ref, yb_ref, yc_ref, yd_ref, wo_ref, x_ref, g1_ref, gain_ref, sc_ref, sh_ref,
                     wr_ref, br_ref, xo_ref, h_ref, combt_ref):
    mixed = jnp.concatenate([ya_ref[...], yb_ref[...], yc_ref[...], yd_ref[...]], axis=1)
    x = x_ref[...] + g1_ref[...] * _dot(mixed, wo_ref[...])
    xo_ref[...] = x
    h = _modulated_norm(x, gain_ref[...], sc_ref[...], sh_ref[...])
    h_hi, h_lo = _split(h)
    h_ref[...] = h_hi

    w_hi, w_lo = _split(wr_ref[...])
    on_h_hi = _dot_nt(jnp.concatenate([w_hi, w_lo], axis=0), h_hi)
    logits = on_h_hi[0:LANES] + (on_h_hi[LANES:2 * LANES] + _dot_nt(w_hi, h_lo)) + br_ref[...]
    ng, ne = N_EXPERT_GROUPS, EXPERTS_PER_GROUP
    p_group = _softmax_rows([logits[r:r + 1, :] for r in range(ng)])
    pg_top, g_hot = _first_max(p_group)
    e_logits = []
    for e in range(ne):
        t = g_hot[0] * logits[ng + e:ng + e + 1, :]
        for gi in range(1, ng):
            t = t + g_hot[gi] * logits[ng + gi * ne + e:ng + gi * ne + e + 1, :]
        e_logits.append(t)
    p_exp = _softmax_rows(e_logits)
    p1, hot1 = _first_max(p_exp)
    rest = [jnp.where(h1 > 0.5, -1.0, p) for p, h1 in zip(p_exp, hot1)]
    p2, hot2 = _first_max(rest)
    total = p1 + p2
    w_exp = [(h1 * (p1 / total) + h2 * (p2 / total)) * pg_top for h1, h2 in zip(hot1, hot2)]
    for gi in range(ng):
        for e in range(ne):
            combt_ref[gi * ne + e:gi * ne + e + 1, :] = g_hot[gi] * w_exp[e]


def out_projection(ys, w_out, x, g1, gain, scale, shift, w_router, b_router, seq, row_tile=512):
    n, d = x.shape
    w_router_t = w_router.T
    tiles_per_seq = seq // row_tile
    row = lambda i: (i, 0)
    per_batch = lambda i: (i // tiles_per_seq, 0, 0)
    const = lambda shape: pl.BlockSpec(shape, lambda i: (0, 0))
    mod = pl.BlockSpec((None, 1, d), per_batch)
    return pl.pallas_call(
        _out_proj_kernel,
        grid=(n // row_tile,),
        in_specs=[pl.BlockSpec((row_tile, GROUP_WIDTH), row)] * 4
        + [const(w_out.shape), pl.BlockSpec((row_tile, d), row), mod, const((1, d)), mod, mod,
           const(w_router_t.shape), const(b_router.shape)],
        out_specs=(pl.BlockSpec((row_tile, d), row), pl.BlockSpec((row_tile, d), row),
                   pl.BlockSpec((N_EXPERTS, row_tile), lambda i: (0, i))),
        out_shape=(jax.ShapeDtypeStruct((n, d), F32), jax.ShapeDtypeStruct((n, d), BF16),
                   jax.ShapeDtypeStruct((N_EXPERTS, n), F32)),
        compiler_params=_params("parallel"),
        name="out_projection",
    )(*ys, w_out, x, g1, gain, scale, shift, w_router_t, b_router)


MOE_PARTS = 2
MOE_ROWS = 96


def _moe_kernel(h_ref, combt_ref, before_ref, wg_ref, wu_ref, wd_ref, x_ref, g2_ref, fin_ref, o_ref,
                acc_ref, rank_ref, pick_ref, y_ref, *, final_norm):
    e = pl.program_id(1)
    part = h_ref.shape[0] // MOE_PARTS
    tn = (((0,), (0,)), ((), ()))
    part_tokens = [slice(p * part, (p + 1) * part) for p in range(MOE_PARTS)]
    part_slots = [slice(p * MOE_ROWS, (p + 1) * MOE_ROWS) for p in range(MOE_PARTS)]

    @pl.when(e == 0)
    def _():
        acc_ref[...] = jnp.zeros(acc_ref.shape, F32)
        routed = jnp.where(combt_ref[...] != 0.0, 1.0, 0.0).astype(BF16)
        rank_ref[...] = _dot(routed, before_ref[...])

    weight_row = combt_ref[pl.ds(e, 1), :]
    rank_row = rank_ref[pl.ds(e, 1), :]
    routed_row = weight_row != 0.0
    counts = [jnp.sum(jnp.where(routed_row[:, tok], 1, 0)) for tok in part_tokens]
    n_passes = (functools.reduce(jnp.maximum, counts) + MOE_ROWS - 1) // MOE_ROWS

    def gathered_pass(j):
        slot = (j * MOE_ROWS + lax.broadcasted_iota(jnp.int32, (MOE_ROWS, 1), 0)).astype(F32)
        picks = [jnp.where((rank_row[:, tok] == slot) & routed_row[:, tok], 1.0, 0.0)
                 for tok in part_tokens]
        picks_bf = [p.astype(BF16) for p in picks]
        rows = jnp.concatenate([_dot(p, h_ref[tok, :]) for p, tok in zip(picks_bf, part_tokens)],
                               axis=0).astype(BF16)
        weight = jnp.concatenate([jnp.sum(p * weight_row[:, tok], axis=-1, keepdims=True)
                                  for p, tok in zip(picks, part_tokens)], axis=0)
        hid = _silu(_dot(rows, wg_ref[...])) * _dot(rows, wu_ref[...]) * weight
        return picks_bf, _dot(hid.astype(BF16), wd_ref[...]).astype(BF16)

    picks0, y0 = gathered_pass(0)
    first = pl.ds(pl.multiple_of(e * MOE_ROWS, MOE_ROWS), MOE_ROWS)
    for p in range(MOE_PARTS):
        pick_ref[p, first, :] = picks0[p]
        y_ref[p, first, :] = y0[part_slots[p]]

    def extra_pass(j, carry):
        picks_bf, y = gathered_pass(j)
        for p in range(MOE_PARTS):
            acc_ref[part_tokens[p], :] += lax.dot_general(picks_bf[p], y[part_slots[p]], tn,
                                                          preferred_element_type=F32)
        return carry

    lax.fori_loop(1, n_passes, extra_pass, 0)

    @pl.when(e == N_EXPERTS - 1)
    def _():
        moe = jnp.concatenate([lax.dot_general(pick_ref[p], y_ref[p], tn, preferred_element_type=F32)
                               for p in range(MOE_PARTS)], axis=0) + acc_ref[...]
        x = x_ref[...] + g2_ref[...] * moe
        if final_norm:
            x = x * lax.rsqrt(jnp.mean(x * x, axis=-1, keepdims=True) + RMS_EPS) * fin_ref[...]
        o_ref[...] = x


def moe_experts(h, combt, w_gate, w_up, w_down, x, g2, fin_gain, seq, final_norm, row_tile=1024):
    n, d = x.shape
    f = w_gate.shape[-1]
    tiles_per_seq = seq // row_tile
    row = lambda i, e: (i, 0)
    tok = np.arange(row_tile)
    part = row_tile // MOE_PARTS
    before = ((tok[:, None] < tok[None, :]) & (tok[:, None] // part == tok[None, :] // part)).astype(np.float32)
    return pl.pallas_call(
        functools.partial(_moe_kernel, final_norm=final_norm),
        grid=(n // row_tile, N_EXPERTS),
        in_specs=[pl.BlockSpec((row_tile, d), row),
                  pl.BlockSpec((N_EXPERTS, row_tile), lambda i, e: (0, i)),
                  pl.BlockSpec((row_tile, row_tile), lambda i, e: (0, 0)),
                  pl.BlockSpec((None, d, f), lambda i, e: (e, 0, 0)),
                  pl.BlockSpec((None, d, f), lambda i, e: (e, 0, 0)),
                  pl.BlockSpec((None, f, d), lambda i, e: (e, 0, 0)),
                  pl.BlockSpec((row_tile, d), row),
                  pl.BlockSpec((None, 1, d), lambda i, e: (i // tiles_per_seq, 0, 0)),
                  pl.BlockSpec((1, d), lambda i, e: (0, 0))],
        out_specs=pl.BlockSpec((row_tile, d), row),
        out_shape=jax.ShapeDtypeStruct((n, d), F32),
        scratch_shapes=[pltpu.VMEM((row_tile, d), F32), pltpu.VMEM((N_EXPERTS, row_tile), F32),
                        pltpu.VMEM((MOE_PARTS, N_EXPERTS * MOE_ROWS, part), BF16),
                        pltpu.VMEM((MOE_PARTS, N_EXPERTS * MOE_ROWS, d), BF16)],
        compiler_params=_params("parallel", "arbitrary"),
        name="moe_experts",
    )(h, combt, jnp.asarray(before, BF16), w_gate, w_up, w_down, x, g2, fin_gain)


def kernel(x, c, norm_mix, norm_ffn, final_norm, w_ada, b_ada, w_in, w_out, attn_sink, w_pool, pool_scale, cmp_pos, cmp_w1, cmp_w2, conv_w, a_log, dt_bias, dn_norm, w_route_group, b_route_group, w_route_expert, b_route_expert, w_gate, w_up, w_down):
    batch, seq, d = x.shape
    depth = w_in.shape[0]
    n = batch * seq
    cover, gexp, expand = nsa_constants(seq)
    bexp, bd, tril = gdn_constants()
    mod = ada_modulation(c, w_ada, b_ada)
    spread = lambda t: jnp.repeat(t, HEAD_DIM).reshape(1, GROUP_WIDTH)
    seq3 = lambda t: t.reshape(batch, seq, t.shape[-1])
    xf = x.reshape(n, d)
    for l in range(depth):
        sh1, sc1, g1, sh2, sc2, g2 = (mod[l, :, i * d:(i + 1) * d].reshape(batch, 1, d) for i in range(6))
        p = in_projection(xf, norm_mix[l].reshape(1, d), sc1, sh1, pack_in_weights(w_in[l]), seq)
        y_a = swa_attention(seq3(p["a_q"]), seq3(p["a_k"]), seq3(p["a_v"]), attn_sink[l])
        w_pool_bd = jax.scipy.linalg.block_diag(*[w_pool[l, gi] for gi in range(len(POOL_WINDOWS))])
        y_b = multiscale_pool(seq3(p["b_u"]), w_pool_bd.astype(BF16), pool_scale[l].reshape(1, GROUP_WIDTH))
        kc, vc = nsa_compress(seq3(p["c_cmp"]), *pack_compress_weights(cmp_pos[l], cmp_w1[l], cmp_w2[l]))
        y_c = nsa_attention(seq3(p["c_q"]), seq3(p["c_gate"]), kc, vc, seq3(p["c_ksel"]), seq3(p["c_vsel"]),
                            seq3(p["c_kwin"]), seq3(p["c_vwin"]), cover, gexp, expand)
        y_d = gated_deltanet(seq3(p["d_qkv"]), seq3(p["d_z"]), seq3(p["d_ba"]), conv_w[l], spread(a_log[l]),
                             spread(dt_bias[l]), jnp.tile(dn_norm[l], N_HEADS).reshape(1, GROUP_WIDTH), bexp, bd, tril)
        ys = [t.reshape(n, GROUP_WIDTH) for t in (y_a, y_b, y_c, y_d)]
        n_logits = N_EXPERT_GROUPS + N_EXPERTS
        w_router = jnp.pad(jnp.concatenate([w_route_group[l], w_route_expert[l]], axis=1),
                           ((0, 0), (0, LANES - n_logits)))
        b_router = jnp.pad(jnp.concatenate([b_route_group[l], b_route_expert[l]]),
                           (0, LANES - n_logits)).reshape(LANES, 1)
        xf, h2, comb = out_projection(ys, w_out[l].astype(BF16), xf, g1, norm_ffn[l].reshape(1, d), sc2, sh2,
                                      w_router, b_router, seq)
        f = w_gate.shape[-1]
        xf = moe_experts(h2, comb, w_gate[l].reshape(N_EXPERTS, d, f).astype(BF16),
                         w_up[l].reshape(N_EXPERTS, d, f).astype(BF16),
                         w_down[l].reshape(N_EXPERTS, f, d).astype(BF16),
                         xf, g2, final_norm.reshape(1, d), seq, final_norm=(l == depth - 1))
    return xf.reshape(batch, seq, d)
```

```python
import functools

import numpy as np
import jax
import jax.numpy as jnp
from jax import lax
from jax.experimental import pallas as pl
from jax.experimental.pallas import tpu as pltpu

F32 = jnp.float32
BF16 = jnp.bfloat16
HI = lax.Precision.HIGHEST

HEAD_DIM = 64
N_HEADS = 4
GROUP_WIDTH = N_HEADS * HEAD_DIM
BLOCK = 128
RMS_EPS = 1e-6
SWA_WINDOW = 128
SWA_BLOCKS = 4
POOL_WINDOWS = (2, 4, 8, 16)
POOL_HALO = 16
CMP_LEN = 32
CMP_STRIDE = 16
SEL_BLOCK = 64
NSA_TOP_N = 16
NSA_WINDOW = 512
NSA_FORCE = 1e4
SEL_CHUNK = 512
NSA_BLOCKS = 4
DN_CONV = 4
DN_CHUNK = 64
N_EXPERT_GROUPS = 4
EXPERTS_PER_GROUP = 4
N_EXPERTS = N_EXPERT_GROUPS * EXPERTS_PER_GROUP
LANES = 128
MASKED = -1e30
QK_SCALE = HEAD_DIM ** -0.5 * float(np.log2(np.e))
VMEM_LIMIT = 56 * 1024 * 1024

SEG_WIDTHS = (
    ("a_q", 256), ("a_k", 256), ("a_v", 256), ("c_q", 256),
    ("c_ksel", 128), ("c_vsel", 128), ("c_kwin", 128), ("c_vwin", 128),
    ("b_u", 256), ("c_cmp", 128), ("c_gate", 128), ("d_qkv", 768), ("d_z", 256), ("d_ba", 128),
)
SEG_BF16 = ("a_q", "a_k", "a_v", "c_q", "c_ksel", "c_vsel", "c_kwin", "c_vwin")
SEG_OFFSETS = {}
_off = 0
for _name, _w in SEG_WIDTHS:
    SEG_OFFSETS[_name] = (_off, _w)
    _off += _w
PACKED_WIDTH = _off


def _sigmoid(x):
    return 1.0 / (1.0 + jnp.exp(-x))


def _silu(x):
    return x * _sigmoid(x)


def _dot(a, b, precision=None):
    return jnp.dot(a, b, precision=precision, preferred_element_type=F32)


def _dot_nt(a, b, precision=None):
    return lax.dot_general(a, b, (((1,), (1,)), ((), ())), precision=precision,
                           preferred_element_type=F32)


def _split(a):
    hi = a.astype(BF16)
    return hi, (a - hi.astype(F32)).astype(BF16)


def _dot_split_lhs(a, b):
    hi, lo = _split(a)
    return _dot(hi, b) + _dot(lo, b)


def _params(*semantics):
    return pltpu.CompilerParams(dimension_semantics=semantics, vmem_limit_bytes=VMEM_LIMIT)


def _ada_kernel(c_ref, w_ref, b_ref, o_ref):
    cond = _silu(c_ref[...])
    o_ref[...] = _dot(cond, w_ref[...], HI) + b_ref[...]


def ada_modulation(c, w_ada, b_ada, col_tile=1536):
    depth, d, width = w_ada.shape
    b = c.shape[0]
    return pl.pallas_call(
        _ada_kernel,
        grid=(depth, width // col_tile),
        in_specs=[
            pl.BlockSpec((b, d), lambda l, j: (0, 0)),
            pl.BlockSpec((None, d, col_tile), lambda l, j: (l, 0, j)),
            pl.BlockSpec((None, 1, col_tile), lambda l, j: (l, 0, j)),
        ],
        out_specs=pl.BlockSpec((None, b, col_tile), lambda l, j: (l, 0, j)),
        out_shape=jax.ShapeDtypeStruct((depth, b, width), F32),
        compiler_params=_params("parallel", "parallel"),
        name="ada_modulation",
    )(c, w_ada, b_ada.reshape(depth, 1, width))


def _modulated_norm(x, gain, scale, shift):
    y = x * lax.rsqrt(jnp.mean(x * x, axis=-1, keepdims=True) + RMS_EPS)
    return y * gain * (1.0 + scale) + shift


def _in_proj_kernel(x_ref, gain_ref, sc_ref, sh_ref, w_ref, *out_refs):
    h = _modulated_norm(x_ref[...], gain_ref[...], sc_ref[...], sh_ref[...]).astype(BF16)
    y = _dot(h, w_ref[...])
    for (name, _), o_ref in zip(SEG_WIDTHS, out_refs):
        off, width = SEG_OFFSETS[name]
        o_ref[...] = y[:, off:off + width].astype(o_ref.dtype)


def in_projection(x, gain, scale, shift, w_packed, seq, row_tile=512):
    n, d = x.shape
    tiles_per_seq = seq // row_tile
    row = lambda i: (i, 0)
    per_batch = lambda i: (i // tiles_per_seq, 0, 0)
    out_shape = tuple(
        jax.ShapeDtypeStruct((n, w), BF16 if name in SEG_BF16 else F32) for name, w in SEG_WIDTHS)
    outs = pl.pallas_call(
        _in_proj_kernel,
        grid=(n // row_tile,),
        in_specs=[
            pl.BlockSpec((row_tile, d), row),
            pl.BlockSpec((1, d), lambda i: (0, 0)),
            pl.BlockSpec((None, 1, d), per_batch),
            pl.BlockSpec((None, 1, d), per_batch),
            pl.BlockSpec((d, PACKED_WIDTH), lambda i: (0, 0)),
        ],
        out_specs=tuple(pl.BlockSpec((row_tile, w), row) for _, w in SEG_WIDTHS),
        out_shape=out_shape,
        compiler_params=_params("parallel"),
        name="in_projection",
    )(x, gain, scale, shift, w_packed)
    return dict(zip((name for name, _ in SEG_WIDTHS), outs))


def pack_in_weights(w_in):
    gw, hd = GROUP_WIDTH, HEAD_DIM
    sizes = (gw, 2 * hd, 2 * hd, gw, gw, 6 * hd, 3 * N_HEADS, 3 * gw, gw, N_HEADS, N_HEADS)
    offs = np.concatenate([[0], np.cumsum(sizes)])
    (a_q, a_k, a_v, b_u, c_q, c_kv, c_gate, d_qkv, d_z, d_beta, d_a) = (
        w_in[:, offs[i]:offs[i + 1]] for i in range(len(sizes)))
    d = w_in.shape[0]
    dup = lambda t: jnp.concatenate([t, t], axis=1)
    pad = lambda t: jnp.pad(t, ((0, 0), (0, LANES - t.shape[1])))
    k_cmp, v_cmp, k_sel, v_sel, k_win, v_win = (c_kv[:, i * hd:(i + 1) * hd] for i in range(6))
    a_q = a_q * QK_SCALE
    c_q = c_q * QK_SCALE
    segs = {
        "a_q": a_q,
        "a_k": jnp.concatenate([dup(a_k[:, :hd]), dup(a_k[:, hd:])], axis=1),
        "a_v": jnp.concatenate([dup(a_v[:, :hd]), dup(a_v[:, hd:])], axis=1),
        "c_q": c_q,
        "c_ksel": dup(k_sel), "c_vsel": dup(v_sel), "c_kwin": dup(k_win), "c_vwin": dup(v_win),
        "b_u": b_u,
        "c_cmp": jnp.concatenate([k_cmp, v_cmp], axis=1),
        "c_gate": pad(c_gate),
        "d_qkv": d_qkv, "d_z": d_z,
        "d_ba": pad(jnp.concatenate([d_beta, d_a], axis=1)),
    }
    packed = jnp.concatenate([segs[name] for name, _ in SEG_WIDTHS], axis=1)
    assert packed.shape == (d, PACKED_WIDTH)
    return packed.astype(BF16)


def _stack_heads(slab):
    lane = lax.broadcasted_iota(jnp.int32, slab.shape, 1)
    zero = jnp.zeros_like(slab)
    return jnp.concatenate([jnp.where(lane < HEAD_DIM, slab, zero),
                            jnp.where(lane >= HEAD_DIM, slab, zero)], axis=0)


def _unstack_heads(o, rows):
    lane = lax.broadcasted_iota(jnp.int32, (rows, LANES), 1)
    return jnp.where(lane < HEAD_DIM, o[0:rows], o[rows:2 * rows])


def _swa_kernel(sink_ref, q_ref, kp_ref, kc_ref, vp_ref, vc_ref, o_ref):
    i = pl.program_id(1)
    row = lax.broadcasted_iota(jnp.int32, (2 * BLOCK, 2 * BLOCK), 0)
    col = lax.broadcasted_iota(jnp.int32, (2 * BLOCK, 2 * BLOCK), 1)
    tq = row & (BLOCK - 1)
    tk = col - BLOCK
    banded = (tk <= tq) & (tk > tq - SWA_WINDOW)
    first_valid = banded & ((col >= BLOCK) | (i > 0))
    rowh = lax.broadcasted_iota(jnp.int32, (2 * BLOCK, 1), 0)
    log2e = float(np.log2(np.e))
    sinks = [jnp.where(rowh < BLOCK, sink_ref[2 * j], sink_ref[2 * j + 1]) * log2e for j in range(N_HEADS // 2)]
    pairs = [(sb, j) for sb in range(SWA_BLOCKS) for j in range(N_HEADS // 2)]

    def keys(cur_ref, prev_ref, sb, lanes):
        if sb == 0:
            return jnp.concatenate([prev_ref[:, lanes], cur_ref[0:BLOCK, lanes]], axis=0)
        return cur_ref[(sb - 1) * BLOCK:(sb + 1) * BLOCK, lanes]

    lanes_of = lambda j: slice(j * LANES, (j + 1) * LANES)
    s = [jnp.where(first_valid if sb == 0 else banded,
                   _dot_nt(_stack_heads(q_ref[sb * BLOCK:(sb + 1) * BLOCK, lanes_of(j)]),
                           keys(kc_ref, kp_ref, sb, lanes_of(j))), -jnp.inf) for sb, j in pairs]
    m = [jnp.maximum(jnp.max(t, axis=-1, keepdims=True), sinks[j]) for t, (sb, j) in zip(s, pairs)]
    p = [jnp.exp2(t - mm) for t, mm in zip(s, m)]
    denom = [jnp.sum(t, axis=-1, keepdims=True) + jnp.exp2(sinks[j] - mm) for t, mm, (sb, j) in zip(p, m, pairs)]
    o = [_dot(t.astype(BF16), keys(vc_ref, vp_ref, sb, lanes_of(j))) / d for t, d, (sb, j) in zip(p, denom, pairs)]
    for sb in range(SWA_BLOCKS):
        slabs = [_unstack_heads(o[sb * (N_HEADS // 2) + j], BLOCK) for j in range(N_HEADS // 2)]
        o_ref[sb * BLOCK:(sb + 1) * BLOCK, :] = jnp.concatenate(slabs, axis=1).astype(o_ref.dtype)


def swa_attention(q, k, v, sink):
    b, s, w = q.shape
    rows = SWA_BLOCKS * BLOCK
    cur = pl.BlockSpec((None, rows, w), lambda bi, i: (bi, i, 0))
    prev = pl.BlockSpec((None, BLOCK, w), lambda bi, i: (bi, jnp.maximum(i * SWA_BLOCKS - 1, 0), 0))
    return pl.pallas_call(
        _swa_kernel,
        grid=(b, s // rows),
        in_specs=[pl.BlockSpec(memory_space=pltpu.SMEM), cur, prev, cur, prev, cur],
        out_specs=cur,
        out_shape=jax.ShapeDtypeStruct((b, s, w), BF16),
        compiler_params=_params("parallel", "parallel"),
        name="swa_attention",
    )(sink, q, k, k, v, v)


def _pool_kernel(up_ref, u_ref, w_ref, scale_ref, o_ref, ext_ref):
    i = pl.program_id(1)
    rows = u_ref.shape[0]
    u = u_ref[...]
    halo = up_ref[...]
    ext_ref[0:POOL_HALO, :] = jnp.where(i > 0, halo, jnp.zeros_like(halo))
    ext_ref[POOL_HALO:POOL_HALO + rows, :] = u
    lane = lax.broadcasted_iota(jnp.int32, u.shape, 1)
    pos = i * rows + lax.broadcasted_iota(jnp.int32, u.shape, 0)
    group_ch = GROUP_WIDTH // len(POOL_WINDOWS)
    total = u
    d = jnp.zeros_like(u)
    width = 1
    for gi, w in enumerate(POOL_WINDOWS):
        while width < w:
            total = total + ext_ref[pl.ds(POOL_HALO - width, rows), :]
            width += 1
        cnt = jnp.minimum(pos + 1, w).astype(F32)
        in_group = (lane >= gi * group_ch) & (lane < (gi + 1) * group_ch)
        d = jnp.where(in_group, total / cnt - u, d)
    o_ref[...] = (_dot(d.astype(BF16), w_ref[...]) * scale_ref[...]).astype(o_ref.dtype)


def multiscale_pool(u, w_blockdiag, pool_scale, row_tile=512):
    b, s, w = u.shape
    halo_per_tile = row_tile // POOL_HALO
    return pl.pallas_call(
        _pool_kernel,
        grid=(b, s // row_tile),
        in_specs=[
            pl.BlockSpec((None, POOL_HALO, w), lambda bi, i: (bi, jnp.maximum(i * halo_per_tile - 1, 0), 0)),
            pl.BlockSpec((None, row_tile, w), lambda bi, i: (bi, i, 0)),
            pl.BlockSpec((w, w), lambda bi, i: (0, 0)),
            pl.BlockSpec((1, w), lambda bi, i: (0, 0)),
        ],
        out_specs=pl.BlockSpec((None, row_tile, w), lambda bi, i: (bi, i, 0)),
        out_shape=jax.ShapeDtypeStruct((b, s, w), BF16),
        scratch_shapes=[pltpu.VMEM((POOL_HALO + row_tile, w), F32)],
        compiler_params=_params("parallel", "parallel"),
        name="multiscale_pool",
    )(u, u, w_blockdiag, pool_scale)


def _compress_kernel(x_ref, w1_ref, pos_ref, w2k_ref, w2v_ref, kc_ref, vc_ref):
    n_chunks = x_ref.shape[0]
    both = _dot(x_ref[...], w1_ref[...], HI)
    pre = both[:, 0:LANES] + pltpu.roll(both[:, LANES:2 * LANES], n_chunks - 1, 0) + pos_ref[...]
    hid = _silu(pre)
    kc_ref[...] = _dot(hid, w2k_ref[...], HI).astype(kc_ref.dtype)
    vc_ref[...] = _dot(hid, w2v_ref[...], HI).astype(vc_ref.dtype)


def nsa_compress(cmp_in, w1_packed, pos_term, w2k, w2v):
    b, s, w = cmp_in.shape
    n_chunks = s // CMP_STRIDE
    flat = cmp_in.reshape(b, n_chunks, CMP_STRIDE * w)
    const = lambda shape: pl.BlockSpec(shape, lambda bi: tuple(0 for _ in shape))
    out = jax.ShapeDtypeStruct((b, n_chunks, LANES), BF16)
    return pl.pallas_call(
        _compress_kernel,
        grid=(b,),
        in_specs=[pl.BlockSpec((None, n_chunks, CMP_STRIDE * w), lambda bi: (bi, 0, 0)),
                  const(w1_packed.shape), const(pos_term.shape), const(w2k.shape), const(w2v.shape)],
        out_specs=(pl.BlockSpec((None, n_chunks, LANES), lambda bi: (bi, 0, 0)),) * 2,
        out_shape=(out, out),
        compiler_params=_params("parallel"),
        name="nsa_compress",
    )(flat, w1_packed, pos_term, w2k, w2v)


def pack_compress_weights(cmp_pos, cmp_w1, cmp_w2):
    hd, half = HEAD_DIM, CMP_LEN // 2
    w1 = cmp_w1.reshape(2, 2, half, hd, hd)
    zeros = jnp.zeros((half, hd, hd), F32)
    halves = []
    for part in range(2):
        wk = jnp.concatenate([w1[0, part], zeros], axis=-1)
        wv = jnp.concatenate([zeros, w1[1, part]], axis=-1)
        halves.append(jnp.concatenate([wk, wv], axis=1).reshape(half * 2 * hd, 2 * hd))
    w1_packed = jnp.concatenate(halves, axis=1)
    pos_flat = cmp_pos.reshape(2, 1, CMP_LEN * hd)
    pos_term = jnp.concatenate([jnp.matmul(pos_flat[0], cmp_w1[0], precision=HI),
                                jnp.matmul(pos_flat[1], cmp_w1[1], precision=HI)], axis=1)
    zero2 = jnp.zeros((hd, 2 * hd), F32)
    w2k = jnp.concatenate([jnp.concatenate([cmp_w2[0], cmp_w2[0]], axis=1), zero2], axis=0)
    w2v = jnp.concatenate([zero2, jnp.concatenate([cmp_w2[1], cmp_w2[1]], axis=1)], axis=0)
    return w1_packed, pos_term, w2k, w2v


def _nsa_kernel(q_ref, gate_ref, kc_ref, vc_ref, ksel_ref, vsel_ref, kwin_ref, vwin_ref,
                cover_ref, gexp_ref, expand_ref, o_ref, q4_ref, m_ref, acc_ref, sa_ref, sb_ref, ma_ref, mb_ref):
    i = pl.program_id(1)
    nqb = NSA_BLOCKS
    groups = [(b, h) for b in range(nqb) for h in range(N_HEADS)]
    group_rows = [slice(g * BLOCK, (g + 1) * BLOCK) for g in range(len(groups))]
    q4 = jnp.concatenate([_stack_heads(q_ref[b * BLOCK:(b + 1) * BLOCK, s * LANES:(s + 1) * LANES])
                          for b in range(nqb) for s in range(N_HEADS // 2)], axis=0)
    tq_blk = [(i * nqb + b) * BLOCK + lax.broadcasted_iota(jnp.int32, (BLOCK, 1), 0) for b in range(nqb)]

    n_cmp = kc_ref.shape[0]
    n_idx = lax.broadcasted_iota(jnp.int32, (1, n_cmp), 1)
    valid_c = [(n_idx * CMP_STRIDE + (CMP_LEN - 1) <= tq_blk[b]) & (n_idx < n_cmp - 1) for b in range(nqb)]
    k_c = kc_ref[...]
    v_c = vc_ref[...]
    s_c_all = _dot_nt(q4, k_c)
    s_c = [jnp.where(valid_c[b], s_c_all[gr], -jnp.inf) for (b, h), gr in zip(groups, group_rows)]
    m_c = [jnp.max(t, axis=-1, keepdims=True) for t in s_c]
    m_c = [jnp.where(t == -jnp.inf, 0.0, t) for t in m_c]
    p_c = [jnp.exp2(t - mm) for t, mm in zip(s_c, m_c)]
    d_c = [jnp.sum(t, axis=-1, keepdims=True) for t in p_c]
    p_c = [t / jnp.where(dd > 0, dd, 1.0) for t, dd in zip(p_c, d_c)]
    o_c = [_dot(t.astype(BF16), v_c) for t in p_c]

    value_lane = lax.broadcasted_iota(jnp.int32, (1, LANES), 1) < HEAD_DIM

    def with_ones(v):
        return jnp.where(value_lane, v, jnp.ones_like(v))

    span = NSA_WINDOW + BLOCK
    start_w = [pl.multiple_of(jnp.maximum((i * nqb + b) * BLOCK - NSA_WINDOW, 0), BLOCK) for b in range(nqb)]
    tk = [start_w[b] + lax.broadcasted_iota(jnp.int32, (1, span), 1) for b in range(nqb)]
    valid_w = [(tk[b] <= tq_blk[b]) & (tk[b] > tq_blk[b] - NSA_WINDOW) for b in range(nqb)]
    k_w = [kwin_ref[pl.ds(start_w[b], span), :] for b in range(nqb)]
    v_w = [with_ones(vwin_ref[pl.ds(start_w[b], span), :]) for b in range(nqb)]
    acc_w = []

    def window_group(g):
        b = groups[g][0]
        s_w = jnp.where(valid_w[b], _dot_nt(q4[group_rows[g]], k_w[b]), -jnp.inf)
        p_w = jnp.exp2(s_w - jnp.max(s_w, axis=-1, keepdims=True))
        acc_w.append(_dot(p_w.astype(BF16), v_w[b]))

    cover = cover_ref[...]
    importance = []
    for b in range(nqb):
        p_b = p_c[b * N_HEADS:(b + 1) * N_HEADS]
        p_hi, p_lo = _split((p_b[0] + p_b[1]) + (p_b[2] + p_b[3]))
        importance.append(_dot_nt(cover, p_hi) + _dot_nt(cover, p_lo))
    importance = jnp.concatenate(importance, axis=1)
    n_sel = cover_ref.shape[0]
    n_q = nqb * BLOCK
    blk = lax.broadcasted_iota(jnp.int32, (n_sel, n_q), 0)
    t_lane = i * n_q + lax.broadcasted_iota(jnp.int32, (n_sel, n_q), 1)
    cur = t_lane // SEL_BLOCK
    causal = blk * SEL_BLOCK <= t_lane
    forced = (blk == 0) | (blk == cur) | (blk == cur - 1)
    score = jnp.where(causal, jnp.where(forced, NSA_FORCE, importance), -jnp.inf)
    blk_f = blk.astype(F32)
    chosen = jnp.where(forced, 1.0, 0.0)
    score = jnp.where(forced, -jnp.inf, score)
    n_rounds = NSA_TOP_N - 3
    n_groups = len(groups)
    for r in range(n_rounds):
        for g in range(r * n_groups // n_rounds, (r + 1) * n_groups // n_rounds):
            window_group(g)
        top = jnp.max(score, axis=0, keepdims=True)
        first = jnp.min(jnp.where(score == top, blk_f, float(n_sel)), axis=0, keepdims=True)
        pick = blk_f == first
        score = jnp.where(pick, -jnp.inf, score)
        chosen = jnp.where(pick, 1.0, chosen)
    chosen = jnp.where(causal, chosen, 0.0)
    chosen_q = [chosen[:, b * BLOCK:(b + 1) * BLOCK].T.astype(BF16) for b in range(nqb)]

    q4_ref[...] = q4
    m_ref[...] = jnp.full(m_ref.shape, MASKED, F32)
    acc_ref[...] = jnp.zeros(acc_ref.shape, F32)
    n_chunks = (i * n_q + n_q - 1) // SEL_CHUNK + 1

    def chunk_start(c):
        return pl.multiple_of(jnp.minimum(c, n_chunks - 1) * SEL_CHUNK, SEL_CHUNK)

    def chunk_keep(c):
        start = chunk_start(c)
        key = start + lax.broadcasted_iota(jnp.int32, (1, SEL_CHUNK), 1)
        spread = expand_ref[:, pl.ds(start, SEL_CHUNK)]
        return [jnp.where((key <= tq_blk[b]) & (c < n_chunks), _dot(chosen_q[b], spread), 0.0) > 0.5
                for b in range(nqb)]

    def stage_group(g, bufs, s_all, keep):
        s = jnp.where(keep[groups[g][0]], s_all[group_rows[g]], MASKED)
        bufs[0][group_rows[g]] = s
        bufs[1][group_rows[g]] = jnp.max(s, axis=-1, keepdims=True)

    def half_step(c, cur_bufs, next_bufs):
        keep_next = chunk_keep(c + 1)
        v_aug = with_ones(vsel_ref[pl.ds(chunk_start(c), SEL_CHUNK), :])
        s_next = _dot_nt(q4_ref[...], ksel_ref[pl.ds(chunk_start(c + 1), SEL_CHUNK), :])
        probs, m_olds, m_news = [], [], []
        for g in range(n_groups):
            stage_group(g, next_bufs, s_next, keep_next)
            m_old = m_ref[group_rows[g]]
            m_new = jnp.maximum(m_old, cur_bufs[1][group_rows[g]])
            m_ref[group_rows[g]] = m_new
            probs.append(jnp.exp2(cur_bufs[0][group_rows[g]] - m_new).astype(BF16))
            m_olds.append(m_old)
            m_news.append(m_new)
        pv = _dot(jnp.concatenate(probs, axis=0), v_aug)
        alpha = jnp.exp2(jnp.concatenate(m_olds, axis=0) - jnp.concatenate(m_news, axis=0))
        acc_ref[...] = alpha * acc_ref[...] + pv

    bufs_a = (sa_ref, ma_ref)
    bufs_b = (sb_ref, mb_ref)
    keep0 = chunk_keep(0)
    s_first = _dot_nt(q4, ksel_ref[0:SEL_CHUNK, :])
    for g in range(n_groups):
        stage_group(g, bufs_a, s_first, keep0)

    def sel_step(t, carry):
        half_step(2 * t, bufs_a, bufs_b)

        @pl.when(2 * t + 1 < n_chunks)
        def _():
            half_step(2 * t + 1, bufs_b, bufs_a)

        return carry

    lax.fori_loop(0, (n_chunks + 1) // 2, sel_step, 0)

    def heads_to_lanes(heads):
        return jnp.concatenate([_unstack_heads(jnp.concatenate(heads[2 * j:2 * j + 2], axis=0), BLOCK)
                                for j in range(N_HEADS // 2)], axis=1)

    def normalized_heads_to_lanes(heads):
        slabs = []
        for j in range(N_HEADS // 2):
            even, odd = heads[2 * j], heads[2 * j + 1]
            numer = jnp.where(value_lane, even, pltpu.roll(odd, HEAD_DIM, 1))
            denom = jnp.where(value_lane, pltpu.roll(even, HEAD_DIM, 1), odd)
            slabs.append(numer / denom)
        return jnp.concatenate(slabs, axis=1)

    gates = _dot_split_lhs(_sigmoid(gate_ref[...]), gexp_ref[...])
    gw = GROUP_WIDTH
    for b in range(nqb):
        of_block = slice(b * N_HEADS, (b + 1) * N_HEADS)
        qr = slice(b * BLOCK, (b + 1) * BLOCK)
        acc_sel = [acc_ref[gr] for gr in group_rows[of_block]]
        out = (gates[qr, 0:gw] * heads_to_lanes(o_c[of_block])
               + gates[qr, gw:2 * gw] * normalized_heads_to_lanes(acc_sel)
               + gates[qr, 2 * gw:3 * gw] * normalized_heads_to_lanes(acc_w[of_block]))
        o_ref[qr, :] = out.astype(o_ref.dtype)


def nsa_constants(seq):
    n_cmp_rows = seq // CMP_STRIDE
    n_sel = seq // SEL_BLOCK
    cmp_start = np.arange(n_cmp_rows) * CMP_STRIDE
    sel_start = np.arange(n_sel) * SEL_BLOCK
    cover = np.maximum(np.minimum(cmp_start[None, :] + CMP_LEN, sel_start[:, None] + SEL_BLOCK)
                       - np.maximum(cmp_start[None, :], sel_start[:, None]), 0).astype(np.float32) / CMP_LEN
    gexp = np.zeros((LANES, 3 * GROUP_WIDTH), np.float32)
    for h in range(N_HEADS):
        for br in range(3):
            gexp[h * 3 + br, br * GROUP_WIDTH + h * HEAD_DIM: br * GROUP_WIDTH + (h + 1) * HEAD_DIM] = 1.0
    expand = (np.arange(n_sel)[:, None] == np.arange(seq)[None, :] // SEL_BLOCK).astype(np.float32)
    return cover, gexp, expand


def nsa_attention(q, gate, kc, vc, ksel, vsel, kwin, vwin, cover, gexp, expand):
    b, s, w = q.shape
    assert s >= NSA_WINDOW + BLOCK and s % SEL_CHUNK == 0
    n_q = NSA_BLOCKS * BLOCK
    blk = lambda width: pl.BlockSpec((None, n_q, width), lambda bi, i: (bi, i, 0))
    per_batch = lambda rows: pl.BlockSpec((None, rows, LANES), lambda bi, i: (bi, 0, 0))
    const = lambda shape: pl.BlockSpec(shape, lambda bi, i: (0, 0))
    rows = NSA_BLOCKS * N_HEADS * BLOCK
    return pl.pallas_call(
        _nsa_kernel,
        grid=(b, s // n_q),
        in_specs=[blk(w), blk(LANES), per_batch(kc.shape[1]), per_batch(vc.shape[1]),
                  per_batch(s), per_batch(s), per_batch(s), per_batch(s),
                  const(cover.shape), const(gexp.shape), const(expand.shape)],
        out_specs=blk(w),
        out_shape=jax.ShapeDtypeStruct((b, s, w), BF16),
        scratch_shapes=[pltpu.VMEM((rows, LANES), BF16), pltpu.VMEM((rows, 1), F32),
                        pltpu.VMEM((rows, LANES), F32), pltpu.VMEM((rows, SEL_CHUNK), F32),
                        pltpu.VMEM((rows, SEL_CHUNK), F32), pltpu.VMEM((rows, 1), F32),
                        pltpu.VMEM((rows, 1), F32)],
        compiler_params=_params("parallel", "arbitrary"),
        name="nsa_attention",
    )(q, gate, kc, vc, ksel, vsel, kwin, vwin, jnp.asarray(cover, BF16), jnp.asarray(gexp, BF16),
      jnp.asarray(expand, BF16))


CONV_TAIL = 8


GDN_ROWS = 512


def _gdn_kernel(qkv_ref, z_ref, ba_ref, convw_ref, alog_ref, dtb_ref, nw_ref, bexp_ref, bd_ref, tril_ref,
                o_ref, ext_ref, state_ref):
    step = pl.program_id(1)
    cl, gw = DN_CHUNK, GROUP_WIDTH
    rows = qkv_ref.shape[0]

    @pl.when(step == 0)
    def _():
        ext_ref[0:CONV_TAIL, :] = jnp.zeros((CONV_TAIL, 3 * gw), F32)
        state_ref[...] = jnp.zeros(state_ref.shape, F32)

    ext_ref[CONV_TAIL:CONV_TAIL + rows, :] = qkv_ref[...]
    cw = convw_ref[...]
    acc = ext_ref[CONV_TAIL:CONV_TAIL + rows, :] * cw[DN_CONV - 1:DN_CONV, :]
    for j in range(DN_CONV - 1):
        acc = acc + ext_ref[pl.ds(CONV_TAIL - (DN_CONV - 1) + j, rows), :] * cw[j:j + 1, :]
    ext_ref[0:CONV_TAIL, :] = ext_ref[rows:rows + CONV_TAIL, :]
    act = _silu(acc)

    bd = bd_ref[...]
    per_head_sum = lambda t: _dot_split_lhs(t, bd)
    q = act[:, 0:gw]
    k = act[:, gw:2 * gw]
    v = act[:, 2 * gw:3 * gw]
    q = q * lax.rsqrt(per_head_sum(q * q) + 1e-6) * (HEAD_DIM ** -0.5)
    k = k * lax.rsqrt(per_head_sum(k * k) + 1e-6)

    ba = _dot_split_lhs(ba_ref[...], bexp_ref[...])
    beta = _sigmoid(ba[:, 0:gw])
    a_in = ba[:, gw:2 * gw] + dtb_ref[...]
    softplus = jnp.maximum(a_in, 0.0) + jnp.log(1.0 + jnp.exp(-jnp.abs(a_in)))
    g = -jnp.exp(alog_ref[...]) * softplus
    tril = tril_ref[...]
    g_hi = g.astype(BF16)
    g_mid, g_lo = _split(g - g_hi.astype(F32))
    gc_all = _dot(tril, g_hi) + (_dot(tril, g_mid) + _dot(tril, g_lo))

    ri = lax.broadcasted_iota(jnp.int32, (cl, cl), 0)
    ci = lax.broadcasted_iota(jnp.int32, (cl, cl), 1)
    causal = ci <= ri
    strict = ci < ri
    eye = jnp.where(ci == ri, 1.0, 0.0)
    lane = lax.broadcasted_iota(jnp.int32, (1, gw), 1)
    head_lanes = [(lane >= h * HEAD_DIM) & (lane < (h + 1) * HEAD_DIM) for h in range(N_HEADS)]

    n_chunks = rows // cl
    pairs = [(c, h) for c in range(n_chunks) for h in range(N_HEADS)]
    chunk = lambda t, c: t[c * cl:(c + 1) * cl]
    gcs = [chunk(gc_all, c) for c in range(n_chunks)]
    gc_ts = [gc.T for gc in gcs]
    g_lasts = [gc[cl - 1:cl, :] for gc in gcs]
    egs = [jnp.exp(gc) for gc in gcs]
    ks = [chunk(k, c) for c in range(n_chunks)]
    k_bfs = [t.astype(BF16) for t in ks]
    k_betas = [chunk(k, c) * chunk(beta, c) for c in range(n_chunks)]
    v_betas = [(chunk(v, c) * chunk(beta, c)).astype(BF16) for c in range(n_chunks)]
    kbgs = [(k_betas[c] * egs[c]).astype(BF16) for c in range(n_chunks)]
    q_decs = [(chunk(q, c) * egs[c]).astype(BF16) for c in range(n_chunks)]
    k_decs = [(ks[c] * jnp.exp(g_lasts[c] - gcs[c])).astype(BF16) for c in range(n_chunks)]
    head = lambda h: slice(h * HEAD_DIM, (h + 1) * HEAD_DIM)
    decays = [jnp.exp(jnp.where(causal, gcs[c][:, head(h)] - gc_ts[c][head(h), :], -jnp.inf)) for c, h in pairs]
    per_head = lambda t: [jnp.where(head_lanes[h], t, 0.0) for h in range(N_HEADS)]
    kq = [_dot_nt(jnp.concatenate(per_head(k_betas[c]) + per_head(chunk(q, c)), axis=0).astype(BF16), k_bfs[c])
          for c in range(n_chunks)]
    kks = [kq[c][h * cl:(h + 1) * cl] for c, h in pairs]
    qks = [kq[c][(N_HEADS + h) * cl:(N_HEADS + h + 1) * cl] for c, h in pairs]
    intras = [jnp.where(causal, qk * d, 0.0) for qk, d in zip(qks, decays)]
    powers = [jnp.where(strict, -(kk * d), 0.0) for kk, d in zip(kks, decays)]
    t_invs = [eye + p for p in powers]
    for _ in range(5):
        p_bfs = [p.astype(BF16) for p in powers]
        powers = [_dot(p, p) for p in p_bfs]
        t_invs = [t + _dot(t.astype(BF16), p.astype(BF16)) for t, p in zip(t_invs, powers)]
    tv = [_dot(jnp.concatenate(t_invs[c * N_HEADS:(c + 1) * N_HEADS], axis=0).astype(BF16),
               jnp.concatenate([v_betas[c], kbgs[c]], axis=1)) for c in range(n_chunks)]

    def merge_heads(stacked, lanes):
        out = jnp.zeros((cl, gw), F32)
        for h in range(N_HEADS):
            out = jnp.where(head_lanes[h], stacked[h * cl:(h + 1) * cl, lanes], out)
        return out

    us = [merge_heads(tv[c], slice(0, gw)) for c in range(n_chunks)]
    ws = [merge_heads(tv[c], slice(gw, 2 * gw)).astype(BF16) for c in range(n_chunks)]
    bd_f = bd.astype(F32)
    tn = (((0,), (0,)), ((), ()))
    kws = [(bd_f * lax.dot_general(k_decs[c], ws[c], tn, preferred_element_type=F32)).astype(BF16)
           for c in range(n_chunks)]
    kus = [bd_f * lax.dot_general(k_decs[c], us[c].astype(BF16), tn, preferred_element_type=F32)
           for c in range(n_chunks)]
    intra_cat = [jnp.concatenate([intras[c * N_HEADS + h] for h in range(N_HEADS)], axis=1).astype(BF16)
                 for c in range(n_chunks)]

    state = state_ref[...]
    outs = []
    for c in range(n_chunks):
        on_state = _dot(jnp.concatenate([ws[c], q_decs[c], kws[c]], axis=0), state.astype(BF16))
        v_new = us[c] - on_state[0:cl]
        v_stack = jnp.concatenate(per_head(v_new), axis=0).astype(BF16)
        outs.append(on_state[cl:2 * cl] + _dot(intra_cat[c], v_stack))
        state = state * jnp.exp(g_lasts[c]) - on_state[2 * cl:] + kus[c]
    state_ref[...] = state

    o = jnp.concatenate(outs, axis=0)
    o = o * lax.rsqrt(per_head_sum(o * o) * (1.0 / HEAD_DIM) + RMS_EPS) * nw_ref[...]
    o_ref[...] = (o * _silu(z_ref[...])).astype(o_ref.dtype)


def gdn_constants():
    lane_head = np.arange(GROUP_WIDTH) // HEAD_DIM
    bd = (lane_head[:, None] == lane_head[None, :]).astype(np.float32)
    bexp = np.zeros((LANES, 2 * GROUP_WIDTH), np.float32)
    for h in range(N_HEADS):
        bexp[h, h * HEAD_DIM:(h + 1) * HEAD_DIM] = 1.0
        bexp[N_HEADS + h, GROUP_WIDTH + h * HEAD_DIM:GROUP_WIDTH + (h + 1) * HEAD_DIM] = 1.0
    r = np.arange(GDN_ROWS)
    tril = ((r[:, None] // DN_CHUNK == r[None, :] // DN_CHUNK) & (r[None, :] <= r[:, None])).astype(np.float32)
    return bexp, bd, tril


def gated_deltanet(qkv, z, ba, conv_w, a_log_rep, dt_bias_rep, norm_w_rep, bexp, bd, tril):
    b, s, w3 = qkv.shape
    gw = GROUP_WIDTH
    blk = lambda width: pl.BlockSpec((None, GDN_ROWS, width), lambda bi, c: (bi, c, 0))
    const = lambda shape: pl.BlockSpec(shape, lambda bi, c: (0, 0))
    as_bf16 = lambda t: jnp.asarray(t, BF16)
    return pl.pallas_call(
        _gdn_kernel,
        grid=(b, s // GDN_ROWS),
        in_specs=[blk(w3), blk(gw), blk(LANES), const(conv_w.shape), const((1, gw)), const((1, gw)),
                  const((1, gw)), const(bexp.shape), const(bd.shape), const(tril.shape)],
        out_specs=blk(gw),
        out_shape=jax.ShapeDtypeStruct((b, s, gw), BF16),
        scratch_shapes=[pltpu.VMEM((CONV_TAIL + GDN_ROWS, w3), F32), pltpu.VMEM((gw, gw), F32)],
        compiler_params=_params("parallel", "arbitrary"),
        name="gated_deltanet",
    )(qkv, z, ba, conv_w, a_log_rep, dt_bias_rep, norm_w_rep, as_bf16(bexp), as_bf16(bd), as_bf16(tril))


def _first_max(values):
    best = values[0]
    for v in values[1:]:
        best = jnp.maximum(best, v)
    taken = jnp.zeros_like(best)
    hot = []
    for v in values:
        h = jnp.where((v == best) & (taken < 0.5), 1.0, 0.0)
        taken = taken + h
        hot.append(h)
    return best, hot


def _softmax_rows(rows):
    m = rows[0]
    for r in rows[1:]:
        m = jnp.maximum(m, r)
    e = [jnp.exp(r - m) for r in rows]
    z = e[0]
    for t in e[1:]:
        z = z + t
    return [t / z for t in e]


def _out_proj_kernel(ya_ref, yb_ref, yc_ref, yd_ref, wo_ref, x_ref, g1_ref, gain_ref, sc_ref, sh_ref,
                     wr_ref, br_ref, xo_ref, h_ref, combt_ref):
    mixed = jnp.concatenate([ya_ref[...], yb_ref[...], yc_ref[...], yd_ref[...]], axis=1)
    x = x_ref[...] + g1_ref[...] * _dot(mixed, wo_ref[...])
    xo_ref[...] = x
    h = _modulated_norm(x, gain_ref[...], sc_ref[...], sh_ref[...])
    h_hi, h_lo = _split(h)
    h_ref[...] = h_hi

    w_hi, w_lo = _split(wr_ref[...])
    on_h_hi = _dot_nt(jnp.concatenate([w_hi, w_lo], axis=0), h_hi)
    logits = on_h_hi[0:LANES] + (on_h_hi[LANES:2 * LANES] + _dot_nt(w_hi, h_lo)) + br_ref[...]
    ng, ne = N_EXPERT_GROUPS, EXPERTS_PER_GROUP
    p_group = _softmax_rows([logits[r:r + 1, :] for r in range(ng)])
    pg_top, g_hot = _first_max(p_group)
    e_logits = []
    for e in range(ne):
        t = g_hot[0] * logits[ng + e:ng + e + 1, :]
        for gi in range(1, ng):
            t = t + g_hot[gi] * logits[ng + gi * ne + e:ng + gi * ne + e + 1, :]
        e_logits.append(t)
    p_exp = _softmax_rows(e_logits)
    p1, hot1 = _first_max(p_exp)
    rest = [jnp.where(h1 > 0.5, -1.0, p) for p, h1 in zip(p_exp, hot1)]
    p2, hot2 = _first_max(rest)
    total = p1 + p2
    w_exp = [(h1 * (p1 / total) + h2 * (p2 / total)) * pg_top for h1, h2 in zip(hot1, hot2)]
    for gi in range(ng):
        for e in range(ne):
            combt_ref[gi * ne + e:gi * ne + e + 1, :] = g_hot[gi] * w_exp[e]


def out_projection(ys, w_out, x, g1, gain, scale, shift, w_router, b_router, seq, row_tile=512):
    n, d = x.shape
    w_router_t = w_router.T
    tiles_per_seq = seq // row_tile
    row = lambda i: (i, 0)
    per_batch = lambda i: (i // tiles_per_seq, 0, 0)
    const = lambda shape: pl.BlockSpec(shape, lambda i: (0, 0))
    mod = pl.BlockSpec((None, 1, d), per_batch)
    return pl.pallas_call(
        _out_proj_kernel,
        grid=(n // row_tile,),
        in_specs=[pl.BlockSpec((row_tile, GROUP_WIDTH), row)] * 4
        + [const(w_out.shape), pl.BlockSpec((row_tile, d), row), mod, const((1, d)), mod, mod,
           const(w_router_t.shape), const(b_router.shape)],
        out_specs=(pl.BlockSpec((row_tile, d), row), pl.BlockSpec((row_tile, d), row),
                   pl.BlockSpec((N_EXPERTS, row_tile), lambda i: (0, i))),
        out_shape=(jax.ShapeDtypeStruct((n, d), F32), jax.ShapeDtypeStruct((n, d), BF16),
                   jax.ShapeDtypeStruct((N_EXPERTS, n), F32)),
        compiler_params=_params("parallel"),
        name="out_projection",
    )(*ys, w_out, x, g1, gain, scale, shift, w_router_t, b_router)


MOE_PARTS = 4
MOE_ROWS = 64


def _moe_kernel(h_ref, combt_ref, before_ref, wg_ref, wu_ref, wd_ref, x_ref, g2_ref, fin_ref, o_ref,
                acc_ref, rank_ref, pick_ref, y_ref, *, final_norm):
    e = pl.program_id(1)
    part = h_ref.shape[0] // MOE_PARTS
    tn = (((0,), (0,)), ((), ()))
    part_tokens = [slice(p * part, (p + 1) * part) for p in range(MOE_PARTS)]
    part_slots = [slice(p * MOE_ROWS, (p + 1) * MOE_ROWS) for p in range(MOE_PARTS)]

    @pl.when(e == 0)
    def _():
        acc_ref[...] = jnp.zeros(acc_ref.shape, F32)
        routed = jnp.where(combt_ref[...] != 0.0, 1.0, 0.0).astype(BF16)
        rank_ref[...] = _dot(routed, before_ref[...])

    weight_row = combt_ref[pl.ds(e, 1), :]
    rank_row = rank_ref[pl.ds(e, 1), :]
    routed_row = weight_row != 0.0
    counts = [jnp.sum(jnp.where(routed_row[:, tok], 1, 0)) for tok in part_tokens]
    n_passes = (functools.reduce(jnp.maximum, counts) + MOE_ROWS - 1) // MOE_ROWS

    def gathered_pass(j):
        slot = (j * MOE_ROWS + lax.broadcasted_iota(jnp.int32, (MOE_ROWS, 1), 0)).astype(F32)
        picks = [jnp.where((rank_row[:, tok] == slot) & routed_row[:, tok], 1.0, 0.0)
                 for tok in part_tokens]
        picks_bf = [p.astype(BF16) for p in picks]
        rows = jnp.concatenate([_dot(p, h_ref[tok, :]) for p, tok in zip(picks_bf, part_tokens)],
                               axis=0).astype(BF16)
        weight = jnp.concatenate([jnp.sum(p * weight_row[:, tok], axis=-1, keepdims=True)
                                  for p, tok in zip(picks, part_tokens)], axis=0)
        hid = _silu(_dot(rows, wg_ref[...])) * _dot(rows, wu_ref[...]) * weight
        return picks_bf, _dot(hid.astype(BF16), wd_ref[...]).astype(BF16)

    picks0, y0 = gathered_pass(0)
    first = pl.ds(pl.multiple_of(e * MOE_ROWS, MOE_ROWS), MOE_ROWS)
    for p in range(MOE_PARTS):
        pick_ref[p, first, :] = picks0[p]
        y_ref[p, first, :] = y0[part_slots[p]]

    def extra_pass(j, carry):
        picks_bf, y = gathered_pass(j)
        for p in range(MOE_PARTS):
            acc_ref[part_tokens[p], :] += lax.dot_general(picks_bf[p], y[part_slots[p]], tn,
                                                          preferred_element_type=F32)
        return carry

    lax.fori_loop(1, n_passes, extra_pass, 0)

    @pl.when(e == N_EXPERTS - 1)
    def _():
        moe = jnp.concatenate([lax.dot_general(pick_ref[p], y_ref[p], tn, preferred_element_type=F32)
                               for p in range(MOE_PARTS)], axis=0) + acc_ref[...]
        x = x_ref[...] + g2_ref[...] * moe
        if final_norm:
            x = x * lax.rsqrt(jnp.mean(x * x, axis=-1, keepdims=True) + RMS_EPS) * fin_ref[...]
        o_ref[...] = x


def moe_experts(h, combt, w_gate, w_up, w_down, x, g2, fin_gain, seq, final_norm, row_tile=1024):
    n, d = x.shape
    f = w_gate.shape[-1]
    tiles_per_seq = seq // row_tile
    row = lambda i, e: (i, 0)
    tok = np.arange(row_tile)
    part = row_tile // MOE_PARTS
    before = ((tok[:, None] < tok[None, :]) & (tok[:, None] // part == tok[None, :] // part)).astype(np.float32)
    return pl.pallas_call(
        functools.partial(_moe_kernel, final_norm=final_norm),
        grid=(n // row_tile, N_EXPERTS),
        in_specs=[pl.BlockSpec((row_tile, d), row),
                  pl.BlockSpec((N_EXPERTS, row_tile), lambda i, e: (0, i)),
                  pl.BlockSpec((row_tile, row_tile), lambda i, e: (0, 0)),
                  pl.BlockSpec((None, d, f), lambda i, e: (e, 0, 0)),
                  pl.BlockSpec((None, d, f), lambda i, e: (e, 0, 0)),
                  pl.BlockSpec((None, f, d), lambda i, e: (e, 0, 0)),
                  pl.BlockSpec((row_tile, d), row),
                  pl.BlockSpec((None, 1, d), lambda i, e: (i // tiles_per_seq, 0, 0)),
                  pl.BlockSpec((1, d), lambda i, e: (0, 0))],
        out_specs=pl.BlockSpec((row_tile, d), row),
        out_shape=jax.ShapeDtypeStruct((n, d), F32),
        scratch_shapes=[pltpu.VMEM((row_tile, d), F32), pltpu.VMEM((N_EXPERTS, row_tile), F32),
                        pltpu.VMEM((MOE_PARTS, N_EXPERTS * MOE_ROWS, part), BF16),
                        pltpu.VMEM((MOE_PARTS, N_EXPERTS * MOE_ROWS, d), BF16)],
        compiler_params=_params("parallel", "arbitrary"),
        name="moe_experts",
    )(h, combt, jnp.asarray(before, BF16), w_gate, w_up, w_down, x, g2, fin_gain)


def kernel(x, c, norm_mix, norm_ffn, final_norm, w_ada, b_ada, w_in, w_out, attn_sink, w_pool, pool_scale, cmp_pos, cmp_w1, cmp_w2, conv_w, a_log, dt_bias, dn_norm, w_route_group, b_route_group, w_route_expert, b_route_expert, w_gate, w_up, w_down):
    batch, seq, d = x.shape
    depth = w_in.shape[0]
    n = batch * seq
    cover, gexp, expand = nsa_constants(seq)
    bexp, bd, tril = gdn_constants()
    mod = ada_modulation(c, w_ada, b_ada)
    spread = lambda t: jnp.repeat(t, HEAD_DIM).reshape(1, GROUP_WIDTH)
    seq3 = lambda t: t.reshape(batch, seq, t.shape[-1])
    xf = x.reshape(n, d)
    for l in range(depth):
        sh1, sc1, g1, sh2, sc2, g2 = (mod[l, :, i * d:(i + 1) * d].reshape(batch, 1, d) for i in range(6))
        p = in_projection(xf, norm_mix[l].reshape(1, d), sc1, sh1, pack_in_weights(w_in[l]), seq)
        y_a = swa_attention(seq3(p["a_q"]), seq3(p["a_k"]), seq3(p["a_v"]), attn_sink[l])
        w_pool_bd = jax.scipy.linalg.block_diag(*[w_pool[l, gi] for gi in range(len(POOL_WINDOWS))])
        y_b = multiscale_pool(seq3(p["b_u"]), w_pool_bd.astype(BF16), pool_scale[l].reshape(1, GROUP_WIDTH))
        kc, vc = nsa_compress(seq3(p["c_cmp"]), *pack_compress_weights(cmp_pos[l], cmp_w1[l], cmp_w2[l]))
        y_c = nsa_attention(seq3(p["c_q"]), seq3(p["c_gate"]), kc, vc, seq3(p["c_ksel"]), seq3(p["c_vsel"]),
                            seq3(p["c_kwin"]), seq3(p["c_vwin"]), cover, gexp, expand)
        y_d = gated_deltanet(seq3(p["d_qkv"]), seq3(p["d_z"]), seq3(p["d_ba"]), conv_w[l], spread(a_log[l]),
                             spread(dt_bias[l]), jnp.tile(dn_norm[l], N_HEADS).reshape(1, GROUP_WIDTH), bexp, bd, tril)
        ys = [t.reshape(n, GROUP_WIDTH) for t in (y_a, y_b, y_c, y_d)]
        n_logits = N_EXPERT_GROUPS + N_EXPERTS
        w_router = jnp.pad(jnp.concatenate([w_route_group[l], w_route_expert[l]], axis=1),
                           ((0, 0), (0, LANES - n_logits)))
        b_router = jnp.pad(jnp.concatenate([b_route_group[l], b_route_expert[l]]),
                           (0, LANES - n_logits)).reshape(LANES, 1)
        xf, h2, comb = out_projection(ys, w_out[l].astype(BF16), xf, g1, norm_ffn[l].reshape(1, d), sc2, sh2,
                                      w_router, b_router, seq)
        f = w_gate.shape[-1]
        xf = moe_experts(h2, comb, w_gate[l].reshape(N_EXPERTS, d, f).astype(BF16),
                         w_up[l].reshape(N_EXPERTS, d, f).astype(BF16),
                         w_down[l].reshape(N_EXPERTS, f, d).astype(BF16),
                         xf, g2, final_norm.reshape(1, d), seq, final_norm=(l == depth - 1))
    return xf.reshape(batch, seq, d)
```

```python
import functools

import numpy as np
import jax
import jax.numpy as jnp
from jax import lax
from jax.experimental import pallas as pl
from jax.experimental.pallas import tpu as pltpu

F32 = jnp.float32
BF16 = jnp.bfloat16
HI = lax.Precision.HIGHEST

HEAD_DIM = 64
N_HEADS = 4
GROUP_WIDTH = N_HEADS * HEAD_DIM
BLOCK = 128
RMS_EPS = 1e-6
SWA_WINDOW = 128
SWA_BLOCKS = 4
POOL_WINDOWS = (2, 4, 8, 16)
POOL_HALO = 16
CMP_LEN = 32
CMP_STRIDE = 16
SEL_BLOCK = 64
NSA_TOP_N = 16
NSA_WINDOW = 512
NSA_FORCE = 1e4
SEL_CHUNK = 512
NSA_BLOCKS = 4
DN_CONV = 4
DN_CHUNK = 64
N_EXPERT_GROUPS = 4
EXPERTS_PER_GROUP = 4
N_EXPERTS = N_EXPERT_GROUPS * EXPERTS_PER_GROUP
LANES = 128
MASKED = -1e30
QK_SCALE = HEAD_DIM ** -0.5 * float(np.log2(np.e))
VMEM_LIMIT = 56 * 1024 * 1024

SEG_WIDTHS = (
    ("a_q", 256), ("a_k", 256), ("a_v", 256), ("c_q", 256),
    ("c_ksel", 128), ("c_vsel", 128), ("c_kwin", 128), ("c_vwin", 128),
    ("b_u", 256), ("c_cmp", 128), ("c_gate", 128), ("d_qkv", 768), ("d_z", 256), ("d_ba", 128),
)
SEG_BF16 = ("a_q", "a_k", "a_v", "c_q", "c_ksel", "c_vsel", "c_kwin", "c_vwin")
SEG_OFFSETS = {}
_off = 0
for _name, _w in SEG_WIDTHS:
    SEG_OFFSETS[_name] = (_off, _w)
    _off += _w
PACKED_WIDTH = _off


def _sigmoid(x):
    return 1.0 / (1.0 + jnp.exp(-x))


def _silu(x):
    return x * _sigmoid(x)


def _dot(a, b, precision=None):
    return jnp.dot(a, b, precision=precision, preferred_element_type=F32)


def _dot_nt(a, b, precision=None):
    return lax.dot_general(a, b, (((1,), (1,)), ((), ())), precision=precision,
                           preferred_element_type=F32)


def _split(a):
    hi = a.astype(BF16)
    return hi, (a - hi.astype(F32)).astype(BF16)


def _dot_split_lhs(a, b):
    hi, lo = _split(a)
    return _dot(hi, b) + _dot(lo, b)


def _params(*semantics):
    return pltpu.CompilerParams(dimension_semantics=semantics, vmem_limit_bytes=VMEM_LIMIT)


def _ada_kernel(c_ref, w_ref, b_ref, o_ref):
    cond = _silu(c_ref[...])
    o_ref[...] = _dot(cond, w_ref[...], HI) + b_ref[...]


def ada_modulation(c, w_ada, b_ada, col_tile=1536):
    depth, d, width = w_ada.shape
    b = c.shape[0]
    return pl.pallas_call(
        _ada_kernel,
        grid=(depth, width // col_tile),
        in_specs=[
            pl.BlockSpec((b, d), lambda l, j: (0, 0)),
            pl.BlockSpec((None, d, col_tile), lambda l, j: (l, 0, j)),
            pl.BlockSpec((None, 1, col_tile), lambda l, j: (l, 0, j)),
        ],
        out_specs=pl.BlockSpec((None, b, col_tile), lambda l, j: (l, 0, j)),
        out_shape=jax.ShapeDtypeStruct((depth, b, width), F32),
        compiler_params=_params("parallel", "parallel"),
        name="ada_modulation",
    )(c, w_ada, b_ada.reshape(depth, 1, width))


def _modulated_norm(x, gain, scale, shift):
    y = x * lax.rsqrt(jnp.mean(x * x, axis=-1, keepdims=True) + RMS_EPS)
    return y * gain * (1.0 + scale) + shift


def _in_proj_kernel(x_ref, gain_ref, sc_ref, sh_ref, w_ref, *out_refs):
    h = _modulated_norm(x_ref[...], gain_ref[...], sc_ref[...], sh_ref[...]).astype(BF16)
    y = _dot(h, w_ref[...])
    for (name, _), o_ref in zip(SEG_WIDTHS, out_refs):
        off, width = SEG_OFFSETS[name]
        o_ref[...] = y[:, off:off + width].astype(o_ref.dtype)


def in_projection(x, gain, scale, shift, w_packed, seq, row_tile=1024):
    n, d = x.shape
    tiles_per_seq = seq // row_tile
    row = lambda i: (i, 0)
    per_batch = lambda i: (i // tiles_per_seq, 0, 0)
    out_shape = tuple(
        jax.ShapeDtypeStruct((n, w), BF16 if name in SEG_BF16 else F32) for name, w in SEG_WIDTHS)
    outs = pl.pallas_call(
        _in_proj_kernel,
        grid=(n // row_tile,),
        in_specs=[
            pl.BlockSpec((row_tile, d), row),
            pl.BlockSpec((1, d), lambda i: (0, 0)),
            pl.BlockSpec((None, 1, d), per_batch),
            pl.BlockSpec((None, 1, d), per_batch),
            pl.BlockSpec((d, PACKED_WIDTH), lambda i: (0, 0)),
        ],
        out_specs=tuple(pl.BlockSpec((row_tile, w), row) for _, w in SEG_WIDTHS),
        out_shape=out_shape,
        compiler_params=_params("parallel"),
        name="in_projection",
    )(x, gain, scale, shift, w_packed)
    return dict(zip((name for name, _ in SEG_WIDTHS), outs))


def pack_in_weights(w_in):
    gw, hd = GROUP_WIDTH, HEAD_DIM
    sizes = (gw, 2 * hd, 2 * hd, gw, gw, 6 * hd, 3 * N_HEADS, 3 * gw, gw, N_HEADS, N_HEADS)
    offs = np.concatenate([[0], np.cumsum(sizes)])
    (a_q, a_k, a_v, b_u, c_q, c_kv, c_gate, d_qkv, d_z, d_beta, d_a) = (
        w_in[:, offs[i]:offs[i + 1]] for i in range(len(sizes)))
    d = w_in.shape[0]
    dup = lambda t: jnp.concatenate([t, t], axis=1)
    pad = lambda t: jnp.pad(t, ((0, 0), (0, LANES - t.shape[1])))
    k_cmp, v_cmp, k_sel, v_sel, k_win, v_win = (c_kv[:, i * hd:(i + 1) * hd] for i in range(6))
    a_q = a_q * QK_SCALE
    c_q = c_q * QK_SCALE
    segs = {
        "a_q": a_q,
        "a_k": jnp.concatenate([dup(a_k[:, :hd]), dup(a_k[:, hd:])], axis=1),
        "a_v": jnp.concatenate([dup(a_v[:, :hd]), dup(a_v[:, hd:])], axis=1),
        "c_q": c_q,
        "c_ksel": dup(k_sel), "c_vsel": dup(v_sel), "c_kwin": dup(k_win), "c_vwin": dup(v_win),
        "b_u": b_u,
        "c_cmp": jnp.concatenate([k_cmp, v_cmp], axis=1),
        "c_gate": pad(c_gate),
        "d_qkv": d_qkv, "d_z": d_z,
        "d_ba": pad(jnp.concatenate([d_beta, d_a], axis=1)),
    }
    packed = jnp.concatenate([segs[name] for name, _ in SEG_WIDTHS], axis=1)
    assert packed.shape == (d, PACKED_WIDTH)
    return packed.astype(BF16)


def _stack_heads(slab):
    lane = lax.broadcasted_iota(jnp.int32, slab.shape, 1)
    zero = jnp.zeros_like(slab)
    return jnp.concatenate([jnp.where(lane < HEAD_DIM, slab, zero),
                            jnp.where(lane >= HEAD_DIM, slab, zero)], axis=0)


def _unstack_heads(o, rows):
    lane = lax.broadcasted_iota(jnp.int32, (rows, LANES), 1)
    return jnp.where(lane < HEAD_DIM, o[0:rows], o[rows:2 * rows])


def _swa_kernel(sink_ref, q_ref, kp_ref, kc_ref, vp_ref, vc_ref, o_ref):
    i = pl.program_id(1)
    row = lax.broadcasted_iota(jnp.int32, (2 * BLOCK, 2 * BLOCK), 0)
    col = lax.broadcasted_iota(jnp.int32, (2 * BLOCK, 2 * BLOCK), 1)
    tq = row & (BLOCK - 1)
    tk = col - BLOCK
    banded = (tk <= tq) & (tk > tq - SWA_WINDOW)
    first_valid = banded & ((col >= BLOCK) | (i > 0))
    rowh = lax.broadcasted_iota(jnp.int32, (2 * BLOCK, 1), 0)
    log2e = float(np.log2(np.e))
    sinks = [jnp.where(rowh < BLOCK, sink_ref[2 * j], sink_ref[2 * j + 1]) * log2e for j in range(N_HEADS // 2)]
    pairs = [(sb, j) for sb in range(SWA_BLOCKS) for j in range(N_HEADS // 2)]

    def keys(cur_ref, prev_ref, sb, lanes):
        if sb == 0:
            return jnp.concatenate([prev_ref[:, lanes], cur_ref[0:BLOCK, lanes]], axis=0)
        return cur_ref[(sb - 1) * BLOCK:(sb + 1) * BLOCK, lanes]

    lanes_of = lambda j: slice(j * LANES, (j + 1) * LANES)
    s = [jnp.where(first_valid if sb == 0 else banded,
                   _dot_nt(_stack_heads(q_ref[sb * BLOCK:(sb + 1) * BLOCK, lanes_of(j)]),
                           keys(kc_ref, kp_ref, sb, lanes_of(j))), -jnp.inf) for sb, j in pairs]
    m = [jnp.maximum(jnp.max(t, axis=-1, keepdims=True), sinks[j]) for t, (sb, j) in zip(s, pairs)]
    p = [jnp.exp2(t - mm) for t, mm in zip(s, m)]
    denom = [jnp.sum(t, axis=-1, keepdims=True) + jnp.exp2(sinks[j] - mm) for t, mm, (sb, j) in zip(p, m, pairs)]
    o = [_dot(t.astype(BF16), keys(vc_ref, vp_ref, sb, lanes_of(j))) / d for t, d, (sb, j) in zip(p, denom, pairs)]
    for sb in range(SWA_BLOCKS):
        slabs = [_unstack_heads(o[sb * (N_HEADS // 2) + j], BLOCK) for j in range(N_HEADS // 2)]
        o_ref[sb * BLOCK:(sb + 1) * BLOCK, :] = jnp.concatenate(slabs, axis=1).astype(o_ref.dtype)


def swa_attention(q, k, v, sink):
    b, s, w = q.shape
    rows = SWA_BLOCKS * BLOCK
    cur = pl.BlockSpec((None, rows, w), lambda bi, i: (bi, i, 0))
    prev = pl.BlockSpec((None, BLOCK, w), lambda bi, i: (bi, jnp.maximum(i * SWA_BLOCKS - 1, 0), 0))
    return pl.pallas_call(
        _swa_kernel,
        grid=(b, s // rows),
        in_specs=[pl.BlockSpec(memory_space=pltpu.SMEM), cur, prev, cur, prev, cur],
        out_specs=cur,
        out_shape=jax.ShapeDtypeStruct((b, s, w), BF16),
        compiler_params=_params("parallel", "parallel"),
        name="swa_attention",
    )(sink, q, k, k, v, v)


def _pool_kernel(up_ref, u_ref, w_ref, scale_ref, o_ref, ext_ref):
    i = pl.program_id(1)
    rows = u_ref.shape[0]
    u = u_ref[...]
    halo = up_ref[...]
    ext_ref[0:POOL_HALO, :] = jnp.where(i > 0, halo, jnp.zeros_like(halo))
    ext_ref[POOL_HALO:POOL_HALO + rows, :] = u
    lane = lax.broadcasted_iota(jnp.int32, u.shape, 1)
    pos = i * rows + lax.broadcasted_iota(jnp.int32, u.shape, 0)
    group_ch = GROUP_WIDTH // len(POOL_WINDOWS)
    total = u
    d = jnp.zeros_like(u)
    width = 1
    for gi, w in enumerate(POOL_WINDOWS):
        while width < w:
            total = total + ext_ref[pl.ds(POOL_HALO - width, rows), :]
            width += 1
        cnt = jnp.minimum(pos + 1, w).astype(F32)
        in_group = (lane >= gi * group_ch) & (lane < (gi + 1) * group_ch)
        d = jnp.where(in_group, total / cnt - u, d)
    o_ref[...] = (_dot(d.astype(BF16), w_ref[...]) * scale_ref[...]).astype(o_ref.dtype)


def multiscale_pool(u, w_blockdiag, pool_scale, row_tile=512):
    b, s, w = u.shape
    halo_per_tile = row_tile // POOL_HALO
    return pl.pallas_call(
        _pool_kernel,
        grid=(b, s // row_tile),
        in_specs=[
            pl.BlockSpec((None, POOL_HALO, w), lambda bi, i: (bi, jnp.maximum(i * halo_per_tile - 1, 0), 0)),
            pl.BlockSpec((None, row_tile, w), lambda bi, i: (bi, i, 0)),
            pl.BlockSpec((w, w), lambda bi, i: (0, 0)),
            pl.BlockSpec((1, w), lambda bi, i: (0, 0)),
        ],
        out_specs=pl.BlockSpec((None, row_tile, w), lambda bi, i: (bi, i, 0)),
        out_shape=jax.ShapeDtypeStruct((b, s, w), BF16),
        scratch_shapes=[pltpu.VMEM((POOL_HALO + row_tile, w), F32)],
        compiler_params=_params("parallel", "parallel"),
        name="multiscale_pool",
    )(u, u, w_blockdiag, pool_scale)


def _compress_kernel(x_ref, w1_ref, pos_ref, w2k_ref, w2v_ref, kc_ref, vc_ref):
    n_chunks = x_ref.shape[0]
    both = _dot(x_ref[...], w1_ref[...], HI)
    pre = both[:, 0:LANES] + pltpu.roll(both[:, LANES:2 * LANES], n_chunks - 1, 0) + pos_ref[...]
    hid = _silu(pre)
    kc_ref[...] = _dot(hid, w2k_ref[...], HI).astype(kc_ref.dtype)
    vc_ref[...] = _dot(hid, w2v_ref[...], HI).astype(vc_ref.dtype)


def nsa_compress(cmp_in, w1_packed, pos_term, w2k, w2v):
    b, s, w = cmp_in.shape
    n_chunks = s // CMP_STRIDE
    flat = cmp_in.reshape(b, n_chunks, CMP_STRIDE * w)
    const = lambda shape: pl.BlockSpec(shape, lambda bi: tuple(0 for _ in shape))
    out = jax.ShapeDtypeStruct((b, n_chunks, LANES), BF16)
    return pl.pallas_call(
        _compress_kernel,
        grid=(b,),
        in_specs=[pl.BlockSpec((None, n_chunks, CMP_STRIDE * w), lambda bi: (bi, 0, 0)),
                  const(w1_packed.shape), const(pos_term.shape), const(w2k.shape), const(w2v.shape)],
        out_specs=(pl.BlockSpec((None, n_chunks, LANES), lambda bi: (bi, 0, 0)),) * 2,
        out_shape=(out, out),
        compiler_params=_params("parallel"),
        name="nsa_compress",
    )(flat, w1_packed, pos_term, w2k, w2v)


def pack_compress_weights(cmp_pos, cmp_w1, cmp_w2):
    hd, half = HEAD_DIM, CMP_LEN // 2
    w1 = cmp_w1.reshape(2, 2, half, hd, hd)
    zeros = jnp.zeros((half, hd, hd), F32)
    halves = []
    for part in range(2):
        wk = jnp.concatenate([w1[0, part], zeros], axis=-1)
        wv = jnp.concatenate([zeros, w1[1, part]], axis=-1)
        halves.append(jnp.concatenate([wk, wv], axis=1).reshape(half * 2 * hd, 2 * hd))
    w1_packed = jnp.concatenate(halves, axis=1)
    pos_flat = cmp_pos.reshape(2, 1, CMP_LEN * hd)
    pos_term = jnp.concatenate([jnp.matmul(pos_flat[0], cmp_w1[0], precision=HI),
                                jnp.matmul(pos_flat[1], cmp_w1[1], precision=HI)], axis=1)
    zero2 = jnp.zeros((hd, 2 * hd), F32)
    w2k = jnp.concatenate([jnp.concatenate([cmp_w2[0], cmp_w2[0]], axis=1), zero2], axis=0)
    w2v = jnp.concatenate([zero2, jnp.concatenate([cmp_w2[1], cmp_w2[1]], axis=1)], axis=0)
    return w1_packed, pos_term, w2k, w2v


def _nsa_kernel(q_ref, gate_ref, kc_ref, vc_ref, ksel_ref, vsel_ref, kwin_ref, vwin_ref,
                cover_ref, gexp_ref, expand_ref, o_ref, q4_ref, m_ref, acc_ref, sa_ref, sb_ref, ma_ref, mb_ref):
    i = pl.program_id(1)
    nqb = NSA_BLOCKS
    groups = [(b, h) for b in range(nqb) for h in range(N_HEADS)]
    group_rows = [slice(g * BLOCK, (g + 1) * BLOCK) for g in range(len(groups))]
    q4 = jnp.concatenate([_stack_heads(q_ref[b * BLOCK:(b + 1) * BLOCK, s * LANES:(s + 1) * LANES])
                          for b in range(nqb) for s in range(N_HEADS // 2)], axis=0)
    tq_blk = [(i * nqb + b) * BLOCK + lax.broadcasted_iota(jnp.int32, (BLOCK, 1), 0) for b in range(nqb)]

    n_cmp = kc_ref.shape[0]
    n_idx = lax.broadcasted_iota(jnp.int32, (1, n_cmp), 1)
    valid_c = [(n_idx * CMP_STRIDE + (CMP_LEN - 1) <= tq_blk[b]) & (n_idx < n_cmp - 1) for b in range(nqb)]
    k_c = kc_ref[...]
    v_c = vc_ref[...]
    s_c_all = _dot_nt(q4, k_c)
    s_c = [jnp.where(valid_c[b], s_c_all[gr], -jnp.inf) for (b, h), gr in zip(groups, group_rows)]
    m_c = [jnp.max(t, axis=-1, keepdims=True) for t in s_c]
    m_c = [jnp.where(t == -jnp.inf, 0.0, t) for t in m_c]
    p_c = [jnp.exp2(t - mm) for t, mm in zip(s_c, m_c)]
    d_c = [jnp.sum(t, axis=-1, keepdims=True) for t in p_c]
    p_c = [t / jnp.where(dd > 0, dd, 1.0) for t, dd in zip(p_c, d_c)]
    o_c = [_dot(t.astype(BF16), v_c) for t in p_c]

    value_lane = lax.broadcasted_iota(jnp.int32, (1, LANES), 1) < HEAD_DIM

    def with_ones(v):
        return jnp.where(value_lane, v, jnp.ones_like(v))

    span = NSA_WINDOW + BLOCK
    start_w = [pl.multiple_of(jnp.maximum((i * nqb + b) * BLOCK - NSA_WINDOW, 0), BLOCK) for b in range(nqb)]
    tk = [start_w[b] + lax.broadcasted_iota(jnp.int32, (1, span), 1) for b in range(nqb)]
    valid_w = [(tk[b] <= tq_blk[b]) & (tk[b] > tq_blk[b] - NSA_WINDOW) for b in range(nqb)]
    k_w = [kwin_ref[pl.ds(start_w[b], span), :] for b in range(nqb)]
    v_w = [with_ones(vwin_ref[pl.ds(start_w[b], span), :]) for b in range(nqb)]
    acc_w = []

    def window_group(g):
        b = groups[g][0]
        s_w = jnp.where(valid_w[b], _dot_nt(q4[group_rows[g]], k_w[b]), -jnp.inf)
        p_w = jnp.exp2(s_w - jnp.max(s_w, axis=-1, keepdims=True))
        acc_w.append(_dot(p_w.astype(BF16), v_w[b]))

    cover = cover_ref[...]
    importance = []
    for b in range(nqb):
        p_b = p_c[b * N_HEADS:(b + 1) * N_HEADS]
        p_hi, p_lo = _split((p_b[0] + p_b[1]) + (p_b[2] + p_b[3]))
        importance.append(_dot_nt(cover, p_hi) + _dot_nt(cover, p_lo))
    importance = jnp.concatenate(importance, axis=1)
    n_sel = cover_ref.shape[0]
    n_q = nqb * BLOCK
    blk = lax.broadcasted_iota(jnp.int32, (n_sel, n_q), 0)
    t_lane = i * n_q + lax.broadcasted_iota(jnp.int32, (n_sel, n_q), 1)
    cur = t_lane // SEL_BLOCK
    causal = blk * SEL_BLOCK <= t_lane
    forced = (blk == 0) | (blk == cur) | (blk == cur - 1)
    score = jnp.where(causal, jnp.where(forced, NSA_FORCE, importance), -jnp.inf)
    blk_f = blk.astype(F32)
    chosen = jnp.where(forced, 1.0, 0.0)
    score = jnp.where(forced, -jnp.inf, score)
    n_rounds = NSA_TOP_N - 3
    n_groups = len(groups)
    for r in range(n_rounds):
        for g in range(r * n_groups // n_rounds, (r + 1) * n_groups // n_rounds):
            window_group(g)
        top = jnp.max(score, axis=0, keepdims=True)
        first = jnp.min(jnp.where(score == top, blk_f, float(n_sel)), axis=0, keepdims=True)
        pick = blk_f == first
        score = jnp.where(pick, -jnp.inf, score)
        chosen = jnp.where(pick, 1.0, chosen)
    chosen = jnp.where(causal, chosen, 0.0)
    chosen_q = [chosen[:, b * BLOCK:(b + 1) * BLOCK].T.astype(BF16) for b in range(nqb)]

    q4_ref[...] = q4
    m_ref[...] = jnp.full(m_ref.shape, MASKED, F32)
    acc_ref[...] = jnp.zeros(acc_ref.shape, F32)
    n_chunks = (i * n_q + n_q - 1) // SEL_CHUNK + 1

    def chunk_start(c):
        return pl.multiple_of(jnp.minimum(c, n_chunks - 1) * SEL_CHUNK, SEL_CHUNK)

    def chunk_keep(c):
        start = chunk_start(c)
        key = start + lax.broadcasted_iota(jnp.int32, (1, SEL_CHUNK), 1)
        spread = expand_ref[:, pl.ds(start, SEL_CHUNK)]
        return [jnp.where((key <= tq_blk[b]) & (c < n_chunks), _dot(chosen_q[b], spread), 0.0) > 0.5
                for b in range(nqb)]

    def stage_group(g, bufs, s_all, keep):
        s = jnp.where(keep[groups[g][0]], s_all[group_rows[g]], MASKED)
        bufs[0][group_rows[g]] = s
        bufs[1][group_rows[g]] = jnp.max(s, axis=-1, keepdims=True)

    def half_step(c, cur_bufs, next_bufs):
        keep_next = chunk_keep(c + 1)
        v_aug = with_ones(vsel_ref[pl.ds(chunk_start(c), SEL_CHUNK), :])
        s_next = _dot_nt(q4_ref[...], ksel_ref[pl.ds(chunk_start(c + 1), SEL_CHUNK), :])
        probs, m_olds, m_news = [], [], []
        for g in range(n_groups):
            stage_group(g, next_bufs, s_next, keep_next)
            m_old = m_ref[group_rows[g]]
            m_new = jnp.maximum(m_old, cur_bufs[1][group_rows[g]])
            m_ref[group_rows[g]] = m_new
            probs.append(jnp.exp2(cur_bufs[0][group_rows[g]] - m_new).astype(BF16))
            m_olds.append(m_old)
            m_news.append(m_new)
        pv = _dot(jnp.concatenate(probs, axis=0), v_aug)
        alpha = jnp.exp2(jnp.concatenate(m_olds, axis=0) - jnp.concatenate(m_news, axis=0))
        acc_ref[...] = alpha * acc_ref[...] + pv

    bufs_a = (sa_ref, ma_ref)
    bufs_b = (sb_ref, mb_ref)
    keep0 = chunk_keep(0)
    s_first = _dot_nt(q4, ksel_ref[0:SEL_CHUNK, :])
    for g in range(n_groups):
        stage_group(g, bufs_a, s_first, keep0)

    def sel_step(t, carry):
        half_step(2 * t, bufs_a, bufs_b)

        @pl.when(2 * t + 1 < n_chunks)
        def _():
            half_step(2 * t + 1, bufs_b, bufs_a)

        return carry

    lax.fori_loop(0, (n_chunks + 1) // 2, sel_step, 0)

    def heads_to_lanes(heads):
        return jnp.concatenate([_unstack_heads(jnp.concatenate(heads[2 * j:2 * j + 2], axis=0), BLOCK)
                                for j in range(N_HEADS // 2)], axis=1)

    def normalized_heads_to_lanes(heads):
        slabs = []
        for j in range(N_HEADS // 2):
            even, odd = heads[2 * j], heads[2 * j + 1]
            numer = jnp.where(value_lane, even, pltpu.roll(odd, HEAD_DIM, 1))
            denom = jnp.where(value_lane, pltpu.roll(even, HEAD_DIM, 1), odd)
            slabs.append(numer / denom)
        return jnp.concatenate(slabs, axis=1)

    gates = _dot_split_lhs(_sigmoid(gate_ref[...]), gexp_ref[...])
    gw = GROUP_WIDTH
    for b in range(nqb):
        of_block = slice(b * N_HEADS, (b + 1) * N_HEADS)
        qr = slice(b * BLOCK, (b + 1) * BLOCK)
        acc_sel = [acc_ref[gr] for gr in group_rows[of_block]]
        out = (gates[qr, 0:gw] * heads_to_lanes(o_c[of_block])
               + gates[qr, gw:2 * gw] * normalized_heads_to_lanes(acc_sel)
               + gates[qr, 2 * gw:3 * gw] * normalized_heads_to_lanes(acc_w[of_block]))
        o_ref[qr, :] = out.astype(o_ref.dtype)


def nsa_constants(seq):
    n_cmp_rows = seq // CMP_STRIDE
    n_sel = seq // SEL_BLOCK
    cmp_start = np.arange(n_cmp_rows) * CMP_STRIDE
    sel_start = np.arange(n_sel) * SEL_BLOCK
    cover = np.maximum(np.minimum(cmp_start[None, :] + CMP_LEN, sel_start[:, None] + SEL_BLOCK)
                       - np.maximum(cmp_start[None, :], sel_start[:, None]), 0).astype(np.float32) / CMP_LEN
    gexp = np.zeros((LANES, 3 * GROUP_WIDTH), np.float32)
    for h in range(N_HEADS):
        for br in range(3):
            gexp[h * 3 + br, br * GROUP_WIDTH + h * HEAD_DIM: br * GROUP_WIDTH + (h + 1) * HEAD_DIM] = 1.0
    expand = (np.arange(n_sel)[:, None] == np.arange(seq)[None, :] // SEL_BLOCK).astype(np.float32)
    return cover, gexp, expand


def nsa_attention(q, gate, kc, vc, ksel, vsel, kwin, vwin, cover, gexp, expand):
    b, s, w = q.shape
    assert s >= NSA_WINDOW + BLOCK and s % SEL_CHUNK == 0
    n_q = NSA_BLOCKS * BLOCK
    blk = lambda width: pl.BlockSpec((None, n_q, width), lambda bi, i: (bi, i, 0))
    per_batch = lambda rows: pl.BlockSpec((None, rows, LANES), lambda bi, i: (bi, 0, 0))
    const = lambda shape: pl.BlockSpec(shape, lambda bi, i: (0, 0))
    rows = NSA_BLOCKS * N_HEADS * BLOCK
    return pl.pallas_call(
        _nsa_kernel,
        grid=(b, s // n_q),
        in_specs=[blk(w), blk(LANES), per_batch(kc.shape[1]), per_batch(vc.shape[1]),
                  per_batch(s), per_batch(s), per_batch(s), per_batch(s),
                  const(cover.shape), const(gexp.shape), const(expand.shape)],
        out_specs=blk(w),
        out_shape=jax.ShapeDtypeStruct((b, s, w), BF16),
        scratch_shapes=[pltpu.VMEM((rows, LANES), BF16), pltpu.VMEM((rows, 1), F32),
                        pltpu.VMEM((rows, LANES), F32), pltpu.VMEM((rows, SEL_CHUNK), F32),
                        pltpu.VMEM((rows, SEL_CHUNK), F32), pltpu.VMEM((rows, 1), F32),
                        pltpu.VMEM((rows, 1), F32)],
        compiler_params=_params("parallel", "arbitrary"),
        name="nsa_attention",
    )(q, gate, kc, vc, ksel, vsel, kwin, vwin, jnp.asarray(cover, BF16), jnp.asarray(gexp, BF16),
      jnp.asarray(expand, BF16))


CONV_TAIL = 8


GDN_ROWS = 512


def _gdn_kernel(qkv_ref, z_ref, ba_ref, convw_ref, alog_ref, dtb_ref, nw_ref, bexp_ref, bd_ref, tril_ref,
                o_ref, ext_ref, state_ref):
    step = pl.program_id(1)
    cl, gw = DN_CHUNK, GROUP_WIDTH
    rows = qkv_ref.shape[0]

    @pl.when(step == 0)
    def _():
        ext_ref[0:CONV_TAIL, :] = jnp.zeros((CONV_TAIL, 3 * gw), F32)
        state_ref[...] = jnp.zeros(state_ref.shape, F32)

    ext_ref[CONV_TAIL:CONV_TAIL + rows, :] = qkv_ref[...]
    cw = convw_ref[...]
    acc = ext_ref[CONV_TAIL:CONV_TAIL + rows, :] * cw[DN_CONV - 1:DN_CONV, :]
    for j in range(DN_CONV - 1):
        acc = acc + ext_ref[pl.ds(CONV_TAIL - (DN_CONV - 1) + j, rows), :] * cw[j:j + 1, :]
    ext_ref[0:CONV_TAIL, :] = ext_ref[rows:rows + CONV_TAIL, :]
    act = _silu(acc)

    bd = bd_ref[...]
    per_head_sum = lambda t: _dot_split_lhs(t, bd)
    q = act[:, 0:gw]
    k = act[:, gw:2 * gw]
    v = act[:, 2 * gw:3 * gw]
    q = q * lax.rsqrt(per_head_sum(q * q) + 1e-6) * (HEAD_DIM ** -0.5)
    k = k * lax.rsqrt(per_head_sum(k * k) + 1e-6)

    ba = _dot_split_lhs(ba_ref[...], bexp_ref[...])
    beta = _sigmoid(ba[:, 0:gw])
    a_in = ba[:, gw:2 * gw] + dtb_ref[...]
    softplus = jnp.maximum(a_in, 0.0) + jnp.log(1.0 + jnp.exp(-jnp.abs(a_in)))
    g = -jnp.exp(alog_ref[...]) * softplus
    tril = tril_ref[...]
    g_hi = g.astype(BF16)
    g_mid, g_lo = _split(g - g_hi.astype(F32))
    gc_all = _dot(tril, g_hi) + (_dot(tril, g_mid) + _dot(tril, g_lo))

    ri = lax.broadcasted_iota(jnp.int32, (cl, cl), 0)
    ci = lax.broadcasted_iota(jnp.int32, (cl, cl), 1)
    causal = ci <= ri
    strict = ci < ri
    eye = jnp.where(ci == ri, 1.0, 0.0)
    lane = lax.broadcasted_iota(jnp.int32, (1, gw), 1)
    head_lanes = [(lane >= h * HEAD_DIM) & (lane < (h + 1) * HEAD_DIM) for h in range(N_HEADS)]

    n_chunks = rows // cl
    pairs = [(c, h) for c in range(n_chunks) for h in range(N_HEADS)]
    chunk = lambda t, c: t[c * cl:(c + 1) * cl]
    gcs = [chunk(gc_all, c) for c in range(n_chunks)]
    gc_ts = [gc.T for gc in gcs]
    g_lasts = [gc[cl - 1:cl, :] for gc in gcs]
    egs = [jnp.exp(gc) for gc in gcs]
    ks = [chunk(k, c) for c in range(n_chunks)]
    k_bfs = [t.astype(BF16) for t in ks]
    k_betas = [chunk(k, c) * chunk(beta, c) for c in range(n_chunks)]
    v_betas = [(chunk(v, c) * chunk(beta, c)).astype(BF16) for c in range(n_chunks)]
    kbgs = [(k_betas[c] * egs[c]).astype(BF16) for c in range(n_chunks)]
    q_decs = [(chunk(q, c) * egs[c]).astype(BF16) for c in range(n_chunks)]
    k_decs = [(ks[c] * jnp.exp(g_lasts[c] - gcs[c])).astype(BF16) for c in range(n_chunks)]
    head = lambda h: slice(h * HEAD_DIM, (h + 1) * HEAD_DIM)
    decays = [jnp.exp(jnp.where(causal, gcs[c][:, head(h)] - gc_ts[c][head(h), :], -jnp.inf)) for c, h in pairs]
    per_head = lambda t: [jnp.where(head_lanes[h], t, 0.0) for h in range(N_HEADS)]
    kq = [_dot_nt(jnp.concatenate(per_head(k_betas[c]) + per_head(chunk(q, c)), axis=0).astype(BF16), k_bfs[c])
          for c in range(n_chunks)]
    kks = [kq[c][h * cl:(h + 1) * cl] for c, h in pairs]
    qks = [kq[c][(N_HEADS + h) * cl:(N_HEADS + h + 1) * cl] for c, h in pairs]
    intras = [jnp.where(causal, qk * d, 0.0) for qk, d in zip(qks, decays)]
    powers = [jnp.where(strict, -(kk * d), 0.0) for kk, d in zip(kks, decays)]
    t_invs = [eye + p for p in powers]
    for _ in range(5):
        p_bfs = [p.astype(BF16) for p in powers]
        powers = [_dot(p, p) for p in p_bfs]
        t_invs = [t + _dot(t.astype(BF16), p.astype(BF16)) for t, p in zip(t_invs, powers)]
    tv = [_dot(jnp.concatenate(t_invs[c * N_HEADS:(c + 1) * N_HEADS], axis=0).astype(BF16),
               jnp.concatenate([v_betas[c], kbgs[c]], axis=1)) for c in range(n_chunks)]

    def merge_heads(stacked, lanes):
        out = jnp.zeros((cl, gw), F32)
        for h in range(N_HEADS):
            out = jnp.where(head_lanes[h], stacked[h * cl:(h + 1) * cl, lanes], out)
        return out

    us = [merge_heads(tv[c], slice(0, gw)) for c in range(n_chunks)]
    ws = [merge_heads(tv[c], slice(gw, 2 * gw)).astype(BF16) for c in range(n_chunks)]
    bd_f = bd.astype(F32)
    tn = (((0,), (0,)), ((), ()))
    kws = [(bd_f * lax.dot_general(k_decs[c], ws[c], tn, preferred_element_type=F32)).astype(BF16)
           for c in range(n_chunks)]
    kus = [bd_f * lax.dot_general(k_decs[c], us[c].astype(BF16), tn, preferred_element_type=F32)
           for c in range(n_chunks)]
    intra_cat = [jnp.concatenate([intras[c * N_HEADS + h] for h in range(N_HEADS)], axis=1).astype(BF16)
                 for c in range(n_chunks)]

    state = state_ref[...]
    outs = []
    for c in range(n_chunks):
        on_state = _dot(jnp.concatenate([ws[c], q_decs[c], kws[c]], axis=0), state.astype(BF16))
        v_new = us[c] - on_state[0:cl]
        v_stack = jnp.concatenate(per_head(v_new), axis=0).astype(BF16)
        outs.append(on_state[cl:2 * cl] + _dot(intra_cat[c], v_stack))
        state = state * jnp.exp(g_lasts[c]) - on_state[2 * cl:] + kus[c]
    state_ref[...] = state

    o = jnp.concatenate(outs, axis=0)
    o = o * lax.rsqrt(per_head_sum(o * o) * (1.0 / HEAD_DIM) + RMS_EPS) * nw_ref[...]
    o_ref[...] = (o * _silu(z_ref[...])).astype(o_ref.dtype)


def gdn_constants():
    lane_head = np.arange(GROUP_WIDTH) // HEAD_DIM
    bd = (lane_head[:, None] == lane_head[None, :]).astype(np.float32)
    bexp = np.zeros((LANES, 2 * GROUP_WIDTH), np.float32)
    for h in range(N_HEADS):
        bexp[h, h * HEAD_DIM:(h + 1) * HEAD_DIM] = 1.0
        bexp[N_HEADS + h, GROUP_WIDTH + h * HEAD_DIM:GROUP_WIDTH + (h + 1) * HEAD_DIM] = 1.0
    r = np.arange(GDN_ROWS)
    tril = ((r[:, None] // DN_CHUNK == r[None, :] // DN_CHUNK) & (r[None, :] <= r[:, None])).astype(np.float32)
    return bexp, bd, tril


def gated_deltanet(qkv, z, ba, conv_w, a_log_rep, dt_bias_rep, norm_w_rep, bexp, bd, tril):
    b, s, w3 = qkv.shape
    gw = GROUP_WIDTH
    blk = lambda width: pl.BlockSpec((None, GDN_ROWS, width), lambda bi, c: (bi, c, 0))
    const = lambda shape: pl.BlockSpec(shape, lambda bi, c: (0, 0))
    as_bf16 = lambda t: jnp.asarray(t, BF16)
    return pl.pallas_call(
        _gdn_kernel,
        grid=(b, s // GDN_ROWS),
        in_specs=[blk(w3), blk(gw), blk(LANES), const(conv_w.shape), const((1, gw)), const((1, gw)),
                  const((1, gw)), const(bexp.shape), const(bd.shape), const(tril.shape)],
        out_specs=blk(gw),
        out_shape=jax.ShapeDtypeStruct((b, s, gw), BF16),
        scratch_shapes=[pltpu.VMEM((CONV_TAIL + GDN_ROWS, w3), F32), pltpu.VMEM((gw, gw), F32)],
        compiler_params=_params("parallel", "arbitrary"),
        name="gated_deltanet",
    )(qkv, z, ba, conv_w, a_log_rep, dt_bias_rep, norm_w_rep, as_bf16(bexp), as_bf16(bd), as_bf16(tril))


def _first_max(values):
    best = values[0]
    for v in values[1:]:
        best = jnp.maximum(best, v)
    taken = jnp.zeros_like(best)
    hot = []
    for v in values:
        h = jnp.where((v == best) & (taken < 0.5), 1.0, 0.0)
        taken = taken + h
        hot.append(h)
    return best, hot


def _softmax_rows(rows):
    m = rows[0]
    for r in rows[1:]:
        m = jnp.maximum(m, r)
    e = [jnp.exp(r - m) for r in rows]
    z = e[0]
    for t in e[1:]:
        z = z + t
    return [t / z for t in e]


def _out_proj_kernel(ya_ref, yb_ref, yc_ref, yd_ref, wo_ref, x_ref, g1_ref, gain_ref, sc_ref, sh_ref,
                     wr_ref, br_ref, xo_ref, h_ref, combt_ref):
    mixed = jnp.concatenate([ya_ref[...], yb_ref[...], yc_ref[...], yd_ref[...]], axis=1)
    x = x_ref[...] + g1_ref[...] * _dot(mixed, wo_ref[...])
    xo_ref[...] = x
    h = _modulated_norm(x, gain_ref[...], sc_ref[...], sh_ref[...])
    h_hi, h_lo = _split(h)
    h_ref[...] = h_hi

    w_hi, w_lo = _split(wr_ref[...])
    on_h_hi = _dot_nt(jnp.concatenate([w_hi, w_lo], axis=0), h_hi)
    logits = on_h_hi[0:LANES] + (on_h_hi[LANES:2 * LANES] + _dot_nt(w_hi, h_lo)) + br_ref[...]
    ng, ne = N_EXPERT_GROUPS, EXPERTS_PER_GROUP
    p_group = _softmax_rows([logits[r:r + 1, :] for r in range(ng)])
    pg_top, g_hot = _first_max(p_group)
    e_logits = []
    for e in range(ne):
        t = g_hot[0] * logits[ng + e:ng + e + 1, :]
        for gi in range(1, ng):
            t = t + g_hot[gi] * logits[ng + gi * ne + e:ng + gi * ne + e + 1, :]
        e_logits.append(t)
    p_exp = _softmax_rows(e_logits)
    p1, hot1 = _first_max(p_exp)
    rest = [jnp.where(h1 > 0.5, -1.0, p) for p, h1 in zip(p_exp, hot1)]
    p2, hot2 = _first_max(rest)
    total = p1 + p2
    w_exp = [(h1 * (p1 / total) + h2 * (p2 / total)) * pg_top for h1, h2 in zip(hot1, hot2)]
    for gi in range(ng):
        for e in range(ne):
            combt_ref[gi * ne + e:gi * ne + e + 1, :] = g_hot[gi] * w_exp[e]


def out_projection(ys, w_out, x, g1, gain, scale, shift, w_router, b_router, seq, row_tile=1024):
    n, d = x.shape
    w_router_t = w_router.T
    tiles_per_seq = seq // row_tile
    row = lambda i: (i, 0)
    per_batch = lambda i: (i // tiles_per_seq, 0, 0)
    const = lambda shape: pl.BlockSpec(shape, lambda i: (0, 0))
    mod = pl.BlockSpec((None, 1, d), per_batch)
    return pl.pallas_call(
        _out_proj_kernel,
        grid=(n // row_tile,),
        in_specs=[pl.BlockSpec((row_tile, GROUP_WIDTH), row)] * 4
        + [const(w_out.shape), pl.BlockSpec((row_tile, d), row), mod, const((1, d)), mod, mod,
           const(w_router_t.shape), const(b_router.shape)],
        out_specs=(pl.BlockSpec((row_tile, d), row), pl.BlockSpec((row_tile, d), row),
                   pl.BlockSpec((N_EXPERTS, row_tile), lambda i: (0, i))),
        out_shape=(jax.ShapeDtypeStruct((n, d), F32), jax.ShapeDtypeStruct((n, d), BF16),
                   jax.ShapeDtypeStruct((N_EXPERTS, n), F32)),
        compiler_params=_params("parallel"),
        name="out_projection",
    )(*ys, w_out, x, g1, gain, scale, shift, w_router_t, b_router)


MOE_PARTS = 4
MOE_ROWS = 64


def _moe_kernel(h_ref, combt_ref, before_ref, wg_ref, wu_ref, wd_ref, x_ref, g2_ref, fin_ref, o_ref,
                acc_ref, rank_ref, pick_ref, y_ref, *, final_norm):
    e = pl.program_id(1)
    part = h_ref.shape[0] // MOE_PARTS
    tn = (((0,), (0,)), ((), ()))
    part_tokens = [slice(p * part, (p + 1) * part) for p in range(MOE_PARTS)]
    part_slots = [slice(p * MOE_ROWS, (p + 1) * MOE_ROWS) for p in range(MOE_PARTS)]

    @pl.when(e == 0)
    def _():
        acc_ref[...] = jnp.zeros(acc_ref.shape, F32)
        routed = jnp.where(combt_ref[...] != 0.0, 1.0, 0.0).astype(BF16)
        rank_ref[...] = _dot(routed, before_ref[...])

    weight_row = combt_ref[pl.ds(e, 1), :]
    rank_row = rank_ref[pl.ds(e, 1), :]
    routed_row = weight_row != 0.0
    counts = [jnp.sum(jnp.where(routed_row[:, tok], 1, 0)) for tok in part_tokens]
    n_passes = (functools.reduce(jnp.maximum, counts) + MOE_ROWS - 1) // MOE_ROWS

    def gathered_pass(j):
        slot = (j * MOE_ROWS + lax.broadcasted_iota(jnp.int32, (MOE_ROWS, 1), 0)).astype(F32)
        picks = [jnp.where((rank_row[:, tok] == slot) & routed_row[:, tok], 1.0, 0.0)
                 for tok in part_tokens]
        picks_bf = [p.astype(BF16) for p in picks]
        rows = jnp.concatenate([_dot(p, h_ref[tok, :]) for p, tok in zip(picks_bf, part_tokens)],
                               axis=0).astype(BF16)
        weight = jnp.concatenate([jnp.sum(p * weight_row[:, tok], axis=-1, keepdims=True)
                                  for p, tok in zip(picks, part_tokens)], axis=0)
        hid = _silu(_dot(rows, wg_ref[...])) * _dot(rows, wu_ref[...]) * weight
        return picks_bf, _dot(hid.astype(BF16), wd_ref[...]).astype(BF16)

    picks0, y0 = gathered_pass(0)
    first = pl.ds(pl.multiple_of(e * MOE_ROWS, MOE_ROWS), MOE_ROWS)
    for p in range(MOE_PARTS):
        pick_ref[p, first, :] = picks0[p]
        y_ref[p, first, :] = y0[part_slots[p]]

    def extra_pass(j, carry):
        picks_bf, y = gathered_pass(j)
        for p in range(MOE_PARTS):
            acc_ref[part_tokens[p], :] += lax.dot_general(picks_bf[p], y[part_slots[p]], tn,
                                                          preferred_element_type=F32)
        return carry

    lax.fori_loop(1, n_passes, extra_pass, 0)

    @pl.when(e == N_EXPERTS - 1)
    def _():
        moe = jnp.concatenate([lax.dot_general(pick_ref[p], y_ref[p], tn, preferred_element_type=F32)
                               for p in range(MOE_PARTS)], axis=0) + acc_ref[...]
        x = x_ref[...] + g2_ref[...] * moe
        if final_norm:
            x = x * lax.rsqrt(jnp.mean(x * x, axis=-1, keepdims=True) + RMS_EPS) * fin_ref[...]
        o_ref[...] = x


def moe_experts(h, combt, w_gate, w_up, w_down, x, g2, fin_gain, seq, final_norm, row_tile=1024):
    n, d = x.shape
    f = w_gate.shape[-1]
    tiles_per_seq = seq // row_tile
    row = lambda i, e: (i, 0)
    tok = np.arange(row_tile)
    part = row_tile // MOE_PARTS
    before = ((tok[:, None] < tok[None, :]) & (tok[:, None] // part == tok[None, :] // part)).astype(np.float32)
    return pl.pallas_call(
        functools.partial(_moe_kernel, final_norm=final_norm),
        grid=(n // row_tile, N_EXPERTS),
        in_specs=[pl.BlockSpec((row_tile, d), row),
                  pl.BlockSpec((N_EXPERTS, row_tile), lambda i, e: (0, i)),
                  pl.BlockSpec((row_tile, row_tile), lambda i, e: (0, 0)),
                  pl.BlockSpec((None, d, f), lambda i, e: (e, 0, 0)),
                  pl.BlockSpec((None, d, f), lambda i, e: (e, 0, 0)),
                  pl.BlockSpec((None, f, d), lambda i, e: (e, 0, 0)),
                  pl.BlockSpec((row_tile, d), row),
                  pl.BlockSpec((None, 1, d), lambda i, e: (i // tiles_per_seq, 0, 0)),
                  pl.BlockSpec((1, d), lambda i, e: (0, 0))],
        out_specs=pl.BlockSpec((row_tile, d), row),
        out_shape=jax.ShapeDtypeStruct((n, d), F32),
        scratch_shapes=[pltpu.VMEM((row_tile, d), F32), pltpu.VMEM((N_EXPERTS, row_tile), F32),
                        pltpu.VMEM((MOE_PARTS, N_EXPERTS * MOE_ROWS, part), BF16),
                        pltpu.VMEM((MOE_PARTS, N_EXPERTS * MOE_ROWS, d), BF16)],
        compiler_params=_params("parallel", "arbitrary"),
        name="moe_experts",
    )(h, combt, jnp.asarray(before, BF16), w_gate, w_up, w_down, x, g2, fin_gain)


def kernel(x, c, norm_mix, norm_ffn, final_norm, w_ada, b_ada, w_in, w_out, attn_sink, w_pool, pool_scale, cmp_pos, cmp_w1, cmp_w2, conv_w, a_log, dt_bias, dn_norm, w_route_group, b_route_group, w_route_expert, b_route_expert, w_gate, w_up, w_down):
    batch, seq, d = x.shape
    depth = w_in.shape[0]
    n = batch * seq
    cover, gexp, expand = nsa_constants(seq)
    bexp, bd, tril = gdn_constants()
    mod = ada_modulation(c, w_ada, b_ada)
    spread = lambda t: jnp.repeat(t, HEAD_DIM).reshape(1, GROUP_WIDTH)
    seq3 = lambda t: t.reshape(batch, seq, t.shape[-1])
    xf = x.reshape(n, d)
    for l in range(depth):
        sh1, sc1, g1, sh2, sc2, g2 = (mod[l, :, i * d:(i + 1) * d].reshape(batch, 1, d) for i in range(6))
        p = in_projection(xf, norm_mix[l].reshape(1, d), sc1, sh1, pack_in_weights(w_in[l]), seq)
        y_a = swa_attention(seq3(p["a_q"]), seq3(p["a_k"]), seq3(p["a_v"]), attn_sink[l])
        w_pool_bd = jax.scipy.linalg.block_diag(*[w_pool[l, gi] for gi in range(len(POOL_WINDOWS))])
        y_b = multiscale_pool(seq3(p["b_u"]), w_pool_bd.astype(BF16), pool_scale[l].reshape(1, GROUP_WIDTH))
        kc, vc = nsa_compress(seq3(p["c_cmp"]), *pack_compress_weights(cmp_pos[l], cmp_w1[l], cmp_w2[l]))
        y_c = nsa_attention(seq3(p["c_q"]), seq3(p["c_gate"]), kc, vc, seq3(p["c_ksel"]), seq3(p["c_vsel"]),
                            seq3(p["c_kwin"]), seq3(p["c_vwin"]), cover, gexp, expand)
        y_d = gated_deltanet(seq3(p["d_qkv"]), seq3(p["d_z"]), seq3(p["d_ba"]), conv_w[l], spread(a_log[l]),
                             spread(dt_bias[l]), jnp.tile(dn_norm[l], N_HEADS).reshape(1, GROUP_WIDTH), bexp, bd, tril)
        ys = [t.reshape(n, GROUP_WIDTH) for t in (y_a, y_b, y_c, y_d)]
        n_logits = N_EXPERT_GROUPS + N_EXPERTS
        w_router = jnp.pad(jnp.concatenate([w_route_group[l], w_route_expert[l]], axis=1),
                           ((0, 0), (0, LANES - n_logits)))
        b_router = jnp.pad(jnp.concatenate([b_route_group[l], b_route_expert[l]]),
                           (0, LANES - n_logits)).reshape(LANES, 1)
        xf, h2, comb = out_projection(ys, w_out[l].astype(BF16), xf, g1, norm_ffn[l].reshape(1, d), sc2, sh2,
                                      w_router, b_router, seq)
        f = w_gate.shape[-1]
        xf = moe_experts(h2, comb, w_gate[l].reshape(N_EXPERTS, d, f).astype(BF16),
                         w_up[l].reshape(N_EXPERTS, d, f).astype(BF16),
                         w_down[l].reshape(N_EXPERTS, f, d).astype(BF16),
                         xf, g2, final_norm.reshape(1, d), seq, final_norm=(l == depth - 1))
    return xf.reshape(batch, seq, d)
```

```python
import functools

import numpy as np
import jax
import jax.numpy as jnp
from jax import lax
from jax.experimental import pallas as pl
from jax.experimental.pallas import tpu as pltpu

F32 = jnp.float32
BF16 = jnp.bfloat16
HI = lax.Precision.HIGHEST

HEAD_DIM = 64
N_HEADS = 4
GROUP_WIDTH = N_HEADS * HEAD_DIM
BLOCK = 128
RMS_EPS = 1e-6
SWA_WINDOW = 128
SWA_BLOCKS = 4
POOL_WINDOWS = (2, 4, 8, 16)
POOL_HALO = 16
CMP_LEN = 32
CMP_STRIDE = 16
SEL_BLOCK = 64
NSA_TOP_N = 16
NSA_WINDOW = 512
NSA_FORCE = 1e4
SEL_CHUNK = 512
NSA_BLOCKS = 4
DN_CONV = 4
DN_CHUNK = 64
N_EXPERT_GROUPS = 4
EXPERTS_PER_GROUP = 4
N_EXPERTS = N_EXPERT_GROUPS * EXPERTS_PER_GROUP
LANES = 128
MASKED = -1e30
QK_SCALE = HEAD_DIM ** -0.5 * float(np.log2(np.e))
VMEM_LIMIT = 56 * 1024 * 1024

SEG_WIDTHS = (
    ("a_q", 256), ("a_k", 256), ("a_v", 256), ("c_q", 256),
    ("c_ksel", 128), ("c_vsel", 128), ("c_kwin", 128), ("c_vwin", 128),
    ("b_u", 256), ("c_cmp", 128), ("c_gate", 128), ("d_qkv", 768), ("d_z", 256), ("d_ba", 128),
)
SEG_BF16 = ("a_q", "a_k", "a_v", "c_q", "c_ksel", "c_vsel", "c_kwin", "c_vwin")
SEG_OFFSETS = {}
_off = 0
for _name, _w in SEG_WIDTHS:
    SEG_OFFSETS[_name] = (_off, _w)
    _off += _w
PACKED_WIDTH = _off


def _sigmoid(x):
    return 1.0 / (1.0 + jnp.exp(-x))


def _silu(x):
    return x * _sigmoid(x)


def _dot(a, b, precision=None):
    return jnp.dot(a, b, precision=precision, preferred_element_type=F32)


def _dot_nt(a, b, precision=None):
    return lax.dot_general(a, b, (((1,), (1,)), ((), ())), precision=precision,
                           preferred_element_type=F32)


def _split(a):
    hi = a.astype(BF16)
    return hi, (a - hi.astype(F32)).astype(BF16)


def _dot_split_lhs(a, b):
    hi, lo = _split(a)
    return _dot(hi, b) + _dot(lo, b)


def _params(*semantics):
    return pltpu.CompilerParams(dimension_semantics=semantics, vmem_limit_bytes=VMEM_LIMIT)


def _ada_kernel(c_ref, w_ref, b_ref, o_ref):
    cond = _silu(c_ref[...])
    o_ref[...] = _dot(cond, w_ref[...], HI) + b_ref[...]


def ada_modulation(c, w_ada, b_ada, col_tile=1536):
    depth, d, width = w_ada.shape
    b = c.shape[0]
    return pl.pallas_call(
        _ada_kernel,
        grid=(depth, width // col_tile),
        in_specs=[
            pl.BlockSpec((b, d), lambda l, j: (0, 0)),
            pl.BlockSpec((None, d, col_tile), lambda l, j: (l, 0, j)),
            pl.BlockSpec((None, 1, col_tile), lambda l, j: (l, 0, j)),
        ],
        out_specs=pl.BlockSpec((None, b, col_tile), lambda l, j: (l, 0, j)),
        out_shape=jax.ShapeDtypeStruct((depth, b, width), F32),
        compiler_params=_params("parallel", "parallel"),
        name="ada_modulation",
    )(c, w_ada, b_ada.reshape(depth, 1, width))


def _modulated_norm(x, gain, scale, shift):
    y = x * lax.rsqrt(jnp.mean(x * x, axis=-1, keepdims=True) + RMS_EPS)
    return y * gain * (1.0 + scale) + shift


def _in_proj_kernel(x_ref, gain_ref, sc_ref, sh_ref, w_ref, *out_refs):
    h = _modulated_norm(x_ref[...], gain_ref[...], sc_ref[...], sh_ref[...]).astype(BF16)
    y = _dot(h, w_ref[...])
    for (name, _), o_ref in zip(SEG_WIDTHS, out_refs):
        off, width = SEG_OFFSETS[name]
        o_ref[...] = y[:, off:off + width].astype(o_ref.dtype)


def in_projection(x, gain, scale, shift, w_packed, seq, row_tile=1024):
    n, d = x.shape
    tiles_per_seq = seq // row_tile
    row = lambda i: (i, 0)
    per_batch = lambda i: (i // tiles_per_seq, 0, 0)
    out_shape = tuple(
        jax.ShapeDtypeStruct((n, w), BF16 if name in SEG_BF16 else F32) for name, w in SEG_WIDTHS)
    outs = pl.pallas_call(
        _in_proj_kernel,
        grid=(n // row_tile,),
        in_specs=[
            pl.BlockSpec((row_tile, d), row),
            pl.BlockSpec((1, d), lambda i: (0, 0)),
            pl.BlockSpec((None, 1, d), per_batch),
            pl.BlockSpec((None, 1, d), per_batch),
            pl.BlockSpec((d, PACKED_WIDTH), lambda i: (0, 0)),
        ],
        out_specs=tuple(pl.BlockSpec((row_tile, w), row) for _, w in SEG_WIDTHS),
        out_shape=out_shape,
        compiler_params=_params("parallel"),
        name="in_projection",
    )(x, gain, scale, shift, w_packed)
    return dict(zip((name for name, _ in SEG_WIDTHS), outs))


def pack_in_weights(w_in):
    gw, hd = GROUP_WIDTH, HEAD_DIM
    sizes = (gw, 2 * hd, 2 * hd, gw, gw, 6 * hd, 3 * N_HEADS, 3 * gw, gw, N_HEADS, N_HEADS)
    offs = np.concatenate([[0], np.cumsum(sizes)])
    (a_q, a_k, a_v, b_u, c_q, c_kv, c_gate, d_qkv, d_z, d_beta, d_a) = (
        w_in[:, offs[i]:offs[i + 1]] for i in range(len(sizes)))
    d = w_in.shape[0]
    dup = lambda t: jnp.concatenate([t, t], axis=1)
    pad = lambda t: jnp.pad(t, ((0, 0), (0, LANES - t.shape[1])))
    k_cmp, v_cmp, k_sel, v_sel, k_win, v_win = (c_kv[:, i * hd:(i + 1) * hd] for i in range(6))
    a_q = a_q * QK_SCALE
    c_q = c_q * QK_SCALE
    segs = {
        "a_q": a_q,
        "a_k": jnp.concatenate([dup(a_k[:, :hd]), dup(a_k[:, hd:])], axis=1),
        "a_v": jnp.concatenate([dup(a_v[:, :hd]), dup(a_v[:, hd:])], axis=1),
        "c_q": c_q,
        "c_ksel": dup(k_sel), "c_vsel": dup(v_sel), "c_kwin": dup(k_win), "c_vwin": dup(v_win),
        "b_u": b_u,
        "c_cmp": jnp.concatenate([k_cmp, v_cmp], axis=1),
        "c_gate": pad(c_gate),
        "d_qkv": d_qkv, "d_z": d_z,
        "d_ba": pad(jnp.concatenate([d_beta, d_a], axis=1)),
    }
    packed = jnp.concatenate([segs[name] for name, _ in SEG_WIDTHS], axis=1)
    assert packed.shape == (d, PACKED_WIDTH)
    return packed.astype(BF16)


def _stack_heads(slab):
    lane = lax.broadcasted_iota(jnp.int32, slab.shape, 1)
    zero = jnp.zeros_like(slab)
    return jnp.concatenate([jnp.where(lane < HEAD_DIM, slab, zero),
                            jnp.where(lane >= HEAD_DIM, slab, zero)], axis=0)


def _unstack_heads(o, rows):
    lane = lax.broadcasted_iota(jnp.int32, (rows, LANES), 1)
    return jnp.where(lane < HEAD_DIM, o[0:rows], o[rows:2 * rows])


def _swa_kernel(sink_ref, q_ref, kp_ref, kc_ref, vp_ref, vc_ref, o_ref):
    i = pl.program_id(1)
    row = lax.broadcasted_iota(jnp.int32, (2 * BLOCK, 2 * BLOCK), 0)
    col = lax.broadcasted_iota(jnp.int32, (2 * BLOCK, 2 * BLOCK), 1)
    tq = row & (BLOCK - 1)
    tk = col - BLOCK
    banded = (tk <= tq) & (tk > tq - SWA_WINDOW)
    first_valid = banded & ((col >= BLOCK) | (i > 0))
    rowh = lax.broadcasted_iota(jnp.int32, (2 * BLOCK, 1), 0)
    log2e = float(np.log2(np.e))
    sinks = [jnp.where(rowh < BLOCK, sink_ref[2 * j], sink_ref[2 * j + 1]) * log2e for j in range(N_HEADS // 2)]
    pairs = [(sb, j) for sb in range(SWA_BLOCKS) for j in range(N_HEADS // 2)]

    def keys(cur_ref, prev_ref, sb, lanes):
        if sb == 0:
            return jnp.concatenate([prev_ref[:, lanes], cur_ref[0:BLOCK, lanes]], axis=0)
        return cur_ref[(sb - 1) * BLOCK:(sb + 1) * BLOCK, lanes]

    lanes_of = lambda j: slice(j * LANES, (j + 1) * LANES)
    s = [jnp.where(first_valid if sb == 0 else banded,
                   _dot_nt(_stack_heads(q_ref[sb * BLOCK:(sb + 1) * BLOCK, lanes_of(j)]),
                           keys(kc_ref, kp_ref, sb, lanes_of(j))), -jnp.inf) for sb, j in pairs]
    m = [jnp.maximum(jnp.max(t, axis=-1, keepdims=True), sinks[j]) for t, (sb, j) in zip(s, pairs)]
    p = [jnp.exp2(t - mm) for t, mm in zip(s, m)]
    denom = [jnp.sum(t, axis=-1, keepdims=True) + jnp.exp2(sinks[j] - mm) for t, mm, (sb, j) in zip(p, m, pairs)]
    o = [_dot(t.astype(BF16), keys(vc_ref, vp_ref, sb, lanes_of(j))) / d for t, d, (sb, j) in zip(p, denom, pairs)]
    for sb in range(SWA_BLOCKS):
        slabs = [_unstack_heads(o[sb * (N_HEADS // 2) + j], BLOCK) for j in range(N_HEADS // 2)]
        o_ref[sb * BLOCK:(sb + 1) * BLOCK, :] = jnp.concatenate(slabs, axis=1).astype(o_ref.dtype)


def swa_attention(q, k, v, sink):
    b, s, w = q.shape
    rows = SWA_BLOCKS * BLOCK
    cur = pl.BlockSpec((None, rows, w), lambda bi, i: (bi, i, 0))
    prev = pl.BlockSpec((None, BLOCK, w), lambda bi, i: (bi, jnp.maximum(i * SWA_BLOCKS - 1, 0), 0))
    return pl.pallas_call(
        _swa_kernel,
        grid=(b, s // rows),
        in_specs=[pl.BlockSpec(memory_space=pltpu.SMEM), cur, prev, cur, prev, cur],
        out_specs=cur,
        out_shape=jax.ShapeDtypeStruct((b, s, w), BF16),
        compiler_params=_params("parallel", "parallel"),
        name="swa_attention",
    )(sink, q, k, k, v, v)


def _pool_kernel(up_ref, u_ref, w_ref, scale_ref, o_ref, ext_ref):
    i = pl.program_id(1)
    rows = u_ref.shape[0]
    u = u_ref[...]
    halo = up_ref[...]
    ext_ref[0:POOL_HALO, :] = jnp.where(i > 0, halo, jnp.zeros_like(halo))
    ext_ref[POOL_HALO:POOL_HALO + rows, :] = u
    lane = lax.broadcasted_iota(jnp.int32, u.shape, 1)
    pos = i * rows + lax.broadcasted_iota(jnp.int32, u.shape, 0)
    group_ch = GROUP_WIDTH // len(POOL_WINDOWS)
    total = u
    d = jnp.zeros_like(u)
    width = 1
    for gi, w in enumerate(POOL_WINDOWS):
        while width < w:
            total = total + ext_ref[pl.ds(POOL_HALO - width, rows), :]
            width += 1
        cnt = jnp.minimum(pos + 1, w).astype(F32)
        in_group = (lane >= gi * group_ch) & (lane < (gi + 1) * group_ch)
        d = jnp.where(in_group, total / cnt - u, d)
    o_ref[...] = (_dot(d.astype(BF16), w_ref[...]) * scale_ref[...]).astype(o_ref.dtype)


def multiscale_pool(u, w_blockdiag, pool_scale, row_tile=512):
    b, s, w = u.shape
    halo_per_tile = row_tile // POOL_HALO
    return pl.pallas_call(
        _pool_kernel,
        grid=(b, s // row_tile),
        in_specs=[
            pl.BlockSpec((None, POOL_HALO, w), lambda bi, i: (bi, jnp.maximum(i * halo_per_tile - 1, 0), 0)),
            pl.BlockSpec((None, row_tile, w), lambda bi, i: (bi, i, 0)),
            pl.BlockSpec((w, w), lambda bi, i: (0, 0)),
            pl.BlockSpec((1, w), lambda bi, i: (0, 0)),
        ],
        out_specs=pl.BlockSpec((None, row_tile, w), lambda bi, i: (bi, i, 0)),
        out_shape=jax.ShapeDtypeStruct((b, s, w), BF16),
        scratch_shapes=[pltpu.VMEM((POOL_HALO + row_tile, w), F32)],
        compiler_params=_params("parallel", "parallel"),
        name="multiscale_pool",
    )(u, u, w_blockdiag, pool_scale)


def _compress_kernel(x_ref, w1_ref, pos_ref, w2k_ref, w2v_ref, kc_ref, vc_ref):
    n_chunks = x_ref.shape[0]
    both = _dot(x_ref[...], w1_ref[...], HI)
    pre = both[:, 0:LANES] + pltpu.roll(both[:, LANES:2 * LANES], n_chunks - 1, 0) + pos_ref[...]
    hid = _silu(pre)
    kc_ref[...] = _dot(hid, w2k_ref[...], HI).astype(kc_ref.dtype)
    vc_ref[...] = _dot(hid, w2v_ref[...], HI).astype(vc_ref.dtype)


def nsa_compress(cmp_in, w1_packed, pos_term, w2k, w2v):
    b, s, w = cmp_in.shape
    n_chunks = s // CMP_STRIDE
    flat = cmp_in.reshape(b, n_chunks, CMP_STRIDE * w)
    const = lambda shape: pl.BlockSpec(shape, lambda bi: tuple(0 for _ in shape))
    out = jax.ShapeDtypeStruct((b, n_chunks, LANES), BF16)
    return pl.pallas_call(
        _compress_kernel,
        grid=(b,),
        in_specs=[pl.BlockSpec((None, n_chunks, CMP_STRIDE * w), lambda bi: (bi, 0, 0)),
                  const(w1_packed.shape), const(pos_term.shape), const(w2k.shape), const(w2v.shape)],
        out_specs=(pl.BlockSpec((None, n_chunks, LANES), lambda bi: (bi, 0, 0)),) * 2,
        out_shape=(out, out),
        compiler_params=_params("parallel"),
        name="nsa_compress",
    )(flat, w1_packed, pos_term, w2k, w2v)


def pack_compress_weights(cmp_pos, cmp_w1, cmp_w2):
    hd, half = HEAD_DIM, CMP_LEN // 2
    w1 = cmp_w1.reshape(2, 2, half, hd, hd)
    zeros = jnp.zeros((half, hd, hd), F32)
    halves = []
    for part in range(2):
        wk = jnp.concatenate([w1[0, part], zeros], axis=-1)
        wv = jnp.concatenate([zeros, w1[1, part]], axis=-1)
        halves.append(jnp.concatenate([wk, wv], axis=1).reshape(half * 2 * hd, 2 * hd))
    w1_packed = jnp.concatenate(halves, axis=1)
    pos_flat = cmp_pos.reshape(2, 1, CMP_LEN * hd)
    pos_term = jnp.concatenate([jnp.matmul(pos_flat[0], cmp_w1[0], precision=HI),
                                jnp.matmul(pos_flat[1], cmp_w1[1], precision=HI)], axis=1)
    zero2 = jnp.zeros((hd, 2 * hd), F32)
    w2k = jnp.concatenate([jnp.concatenate([cmp_w2[0], cmp_w2[0]], axis=1), zero2], axis=0)
    w2v = jnp.concatenate([zero2, jnp.concatenate([cmp_w2[1], cmp_w2[1]], axis=1)], axis=0)
    return w1_packed, pos_term, w2k, w2v


def _nsa_kernel(q_ref, gate_ref, kc_ref, vc_ref, ksel_ref, vsel_ref, kwin_ref, vwin_ref,
                cover_ref, gexp_ref, expand_ref, o_ref, q4_ref, m_ref, acc_ref, sa_ref, sb_ref, ma_ref, mb_ref):
    i = pl.program_id(1)
    nqb = NSA_BLOCKS
    groups = [(b, h) for b in range(nqb) for h in range(N_HEADS)]
    group_rows = [slice(g * BLOCK, (g + 1) * BLOCK) for g in range(len(groups))]
    q4 = jnp.concatenate([_stack_heads(q_ref[b * BLOCK:(b + 1) * BLOCK, s * LANES:(s + 1) * LANES])
                          for b in range(nqb) for s in range(N_HEADS // 2)], axis=0)
    tq_blk = [(i * nqb + b) * BLOCK + lax.broadcasted_iota(jnp.int32, (BLOCK, 1), 0) for b in range(nqb)]

    n_cmp = kc_ref.shape[0]
    n_idx = lax.broadcasted_iota(jnp.int32, (1, n_cmp), 1)
    valid_c = [(n_idx * CMP_STRIDE + (CMP_LEN - 1) <= tq_blk[b]) & (n_idx < n_cmp - 1) for b in range(nqb)]
    k_c = kc_ref[...]
    v_c = vc_ref[...]
    s_c_all = _dot_nt(q4, k_c)
    s_c = [jnp.where(valid_c[b], s_c_all[gr], -jnp.inf) for (b, h), gr in zip(groups, group_rows)]
    m_c = [jnp.max(t, axis=-1, keepdims=True) for t in s_c]
    m_c = [jnp.where(t == -jnp.inf, 0.0, t) for t in m_c]
    p_c = [jnp.exp2(t - mm) for t, mm in zip(s_c, m_c)]
    d_c = [jnp.sum(t, axis=-1, keepdims=True) for t in p_c]
    p_c = [t / jnp.where(dd > 0, dd, 1.0) for t, dd in zip(p_c, d_c)]
    o_c = [_dot(t.astype(BF16), v_c) for t in p_c]

    value_lane = lax.broadcasted_iota(jnp.int32, (1, LANES), 1) < HEAD_DIM

    def with_ones(v):
        return jnp.where(value_lane, v, jnp.ones_like(v))

    span = NSA_WINDOW + BLOCK
    start_w = [pl.multiple_of(jnp.maximum((i * nqb + b) * BLOCK - NSA_WINDOW, 0), BLOCK) for b in range(nqb)]
    tk = [start_w[b] + lax.broadcasted_iota(jnp.int32, (1, span), 1) for b in range(nqb)]
    valid_w = [(tk[b] <= tq_blk[b]) & (tk[b] > tq_blk[b] - NSA_WINDOW) for b in range(nqb)]
    k_w = [kwin_ref[pl.ds(start_w[b], span), :] for b in range(nqb)]
    v_w = [with_ones(vwin_ref[pl.ds(start_w[b], span), :]) for b in range(nqb)]
    acc_w = []

    def window_group(g):
        b = groups[g][0]
        s_w = jnp.where(valid_w[b], _dot_nt(q4[group_rows[g]], k_w[b]), -jnp.inf)
        p_w = jnp.exp2(s_w - jnp.max(s_w, axis=-1, keepdims=True))
        acc_w.append(_dot(p_w.astype(BF16), v_w[b]))

    cover = cover_ref[...]
    importance = []
    for b in range(nqb):
        p_b = p_c[b * N_HEADS:(b + 1) * N_HEADS]
        p_hi, p_lo = _split((p_b[0] + p_b[1]) + (p_b[2] + p_b[3]))
        importance.append(_dot_nt(cover, p_hi) + _dot_nt(cover, p_lo))
    importance = jnp.concatenate(importance, axis=1)
    n_sel = cover_ref.shape[0]
    n_q = nqb * BLOCK
    blk = lax.broadcasted_iota(jnp.int32, (n_sel, n_q), 0)
    t_lane = i * n_q + lax.broadcasted_iota(jnp.int32, (n_sel, n_q), 1)
    cur = t_lane // SEL_BLOCK
    causal = blk * SEL_BLOCK <= t_lane
    forced = (blk == 0) | (blk == cur) | (blk == cur - 1)
    score = jnp.where(causal, jnp.where(forced, NSA_FORCE, importance), -jnp.inf)
    blk_f = blk.astype(F32)
    chosen = jnp.where(forced, 1.0, 0.0)
    score = jnp.where(forced, -jnp.inf, score)
    n_rounds = NSA_TOP_N - 3
    n_groups = len(groups)
    for r in range(n_rounds):
        for g in range(r * n_groups // n_rounds, (r + 1) * n_groups // n_rounds):
            window_group(g)
        top = jnp.max(score, axis=0, keepdims=True)
        first = jnp.min(jnp.where(score == top, blk_f, float(n_sel)), axis=0, keepdims=True)
        pick = blk_f == first
        score = jnp.where(pick, -jnp.inf, score)
        chosen = jnp.where(pick, 1.0, chosen)
    chosen = jnp.where(causal, chosen, 0.0)
    chosen_q = [chosen[:, b * BLOCK:(b + 1) * BLOCK].T.astype(BF16) for b in range(nqb)]

    q4_ref[...] = q4
    m_ref[...] = jnp.full(m_ref.shape, MASKED, F32)
    acc_ref[...] = jnp.zeros(acc_ref.shape, F32)
    n_chunks = (i * n_q + n_q - 1) // SEL_CHUNK + 1

    def chunk_start(c):
        return pl.multiple_of(jnp.minimum(c, n_chunks - 1) * SEL_CHUNK, SEL_CHUNK)

    def chunk_keep(c):
        start = chunk_start(c)
        key = start + lax.broadcasted_iota(jnp.int32, (1, SEL_CHUNK), 1)
        spread = expand_ref[:, pl.ds(start, SEL_CHUNK)]
        return [jnp.where((key <= tq_blk[b]) & (c < n_chunks), _dot(chosen_q[b], spread), 0.0) > 0.5
                for b in range(nqb)]

    def stage_group(g, bufs, s_all, keep):
        s = jnp.where(keep[groups[g][0]], s_all[group_rows[g]], MASKED)
        bufs[0][group_rows[g]] = s
        bufs[1][group_rows[g]] = jnp.max(s, axis=-1, keepdims=True)

    def half_step(c, cur_bufs, next_bufs):
        keep_next = chunk_keep(c + 1)
        v_aug = with_ones(vsel_ref[pl.ds(chunk_start(c), SEL_CHUNK), :])
        s_next = _dot_nt(q4_ref[...], ksel_ref[pl.ds(chunk_start(c + 1), SEL_CHUNK), :])
        probs, m_olds, m_news = [], [], []
        for g in range(n_groups):
            stage_group(g, next_bufs, s_next, keep_next)
            m_old = m_ref[group_rows[g]]
            m_new = jnp.maximum(m_old, cur_bufs[1][group_rows[g]])
            m_ref[group_rows[g]] = m_new
            probs.append(jnp.exp2(cur_bufs[0][group_rows[g]] - m_new).astype(BF16))
            m_olds.append(m_old)
            m_news.append(m_new)
        pv = _dot(jnp.concatenate(probs, axis=0), v_aug)
        alpha = jnp.exp2(jnp.concatenate(m_olds, axis=0) - jnp.concatenate(m_news, axis=0))
        acc_ref[...] = alpha * acc_ref[...] + pv

    bufs_a = (sa_ref, ma_ref)
    bufs_b = (sb_ref, mb_ref)
    keep0 = chunk_keep(0)
    s_first = _dot_nt(q4, ksel_ref[0:SEL_CHUNK, :])
    for g in range(n_groups):
        stage_group(g, bufs_a, s_first, keep0)

    def sel_step(t, carry):
        half_step(2 * t, bufs_a, bufs_b)

        @pl.when(2 * t + 1 < n_chunks)
        def _():
            half_step(2 * t + 1, bufs_b, bufs_a)

        return carry

    lax.fori_loop(0, (n_chunks + 1) // 2, sel_step, 0)

    def heads_to_lanes(heads):
        return jnp.concatenate([_unstack_heads(jnp.concatenate(heads[2 * j:2 * j + 2], axis=0), BLOCK)
                                for j in range(N_HEADS // 2)], axis=1)

    def normalized_heads_to_lanes(heads):
        slabs = []
        for j in range(N_HEADS // 2):
            even, odd = heads[2 * j], heads[2 * j + 1]
            numer = jnp.where(value_lane, even, pltpu.roll(odd, HEAD_DIM, 1))
            denom = jnp.where(value_lane, pltpu.roll(even, HEAD_DIM, 1), odd)
            slabs.append(numer / denom)
        return jnp.concatenate(slabs, axis=1)

    gates = _dot_split_lhs(_sigmoid(gate_ref[...]), gexp_ref[...])
    gw = GROUP_WIDTH
    for b in range(nqb):
        of_block = slice(b * N_HEADS, (b + 1) * N_HEADS)
        qr = slice(b * BLOCK, (b + 1) * BLOCK)
        acc_sel = [acc_ref[gr] for gr in group_rows[of_block]]
        out = (gates[qr, 0:gw] * heads_to_lanes(o_c[of_block])
               + gates[qr, gw:2 * gw] * normalized_heads_to_lanes(acc_sel)
               + gates[qr, 2 * gw:3 * gw] * normalized_heads_to_lanes(acc_w[of_block]))
        o_ref[qr, :] = out.astype(o_ref.dtype)


def nsa_constants(seq):
    n_cmp_rows = seq // CMP_STRIDE
    n_sel = seq // SEL_BLOCK
    cmp_start = np.arange(n_cmp_rows) * CMP_STRIDE
    sel_start = np.arange(n_sel) * SEL_BLOCK
    cover = np.maximum(np.minimum(cmp_start[None, :] + CMP_LEN, sel_start[:, None] + SEL_BLOCK)
                       - np.maximum(cmp_start[None, :], sel_start[:, None]), 0).astype(np.float32) / CMP_LEN
    gexp = np.zeros((LANES, 3 * GROUP_WIDTH), np.float32)
    for h in range(N_HEADS):
        for br in range(3):
            gexp[h * 3 + br, br * GROUP_WIDTH + h * HEAD_DIM: br * GROUP_WIDTH + (h + 1) * HEAD_DIM] = 1.0
    expand = (np.arange(n_sel)[:, None] == np.arange(seq)[None, :] // SEL_BLOCK).astype(np.float32)
    return cover, gexp, expand


def nsa_attention(q, gate, kc, vc, ksel, vsel, kwin, vwin, cover, gexp, expand):
    b, s, w = q.shape
    assert s >= NSA_WINDOW + BLOCK and s % SEL_CHUNK == 0
    n_q = NSA_BLOCKS * BLOCK
    blk = lambda width: pl.BlockSpec((None, n_q, width), lambda bi, i: (bi, i, 0))
    per_batch = lambda rows: pl.BlockSpec((None, rows, LANES), lambda bi, i: (bi, 0, 0))
    const = lambda shape: pl.BlockSpec(shape, lambda bi, i: (0, 0))
    rows = NSA_BLOCKS * N_HEADS * BLOCK
    return pl.pallas_call(
        _nsa_kernel,
        grid=(b, s // n_q),
        in_specs=[blk(w), blk(LANES), per_batch(kc.shape[1]), per_batch(vc.shape[1]),
                  per_batch(s), per_batch(s), per_batch(s), per_batch(s),
                  const(cover.shape), const(gexp.shape), const(expand.shape)],
        out_specs=blk(w),
        out_shape=jax.ShapeDtypeStruct((b, s, w), BF16),
        scratch_shapes=[pltpu.VMEM((rows, LANES), BF16), pltpu.VMEM((rows, 1), F32),
                        pltpu.VMEM((rows, LANES), F32), pltpu.VMEM((rows, SEL_CHUNK), F32),
                        pltpu.VMEM((rows, SEL_CHUNK), F32), pltpu.VMEM((rows, 1), F32),
                        pltpu.VMEM((rows, 1), F32)],
        compiler_params=_params("parallel", "arbitrary"),
        name="nsa_attention",
    )(q, gate, kc, vc, ksel, vsel, kwin, vwin, jnp.asarray(cover, BF16), jnp.asarray(gexp, BF16),
      jnp.asarray(expand, BF16))


CONV_TAIL = 8


GDN_ROWS = 512


def _gdn_kernel(qkv_ref, z_ref, ba_ref, convw_ref, alog_ref, dtb_ref, nw_ref, bexp_ref, bd_ref, tril_ref,
                o_ref, ext_ref, state_ref):
    step = pl.program_id(1)
    cl, gw = DN_CHUNK, GROUP_WIDTH
    rows = qkv_ref.shape[0]

    @pl.when(step == 0)
    def _():
        ext_ref[0:CONV_TAIL, :] = jnp.zeros((CONV_TAIL, 3 * gw), F32)
        state_ref[...] = jnp.zeros(state_ref.shape, F32)

    ext_ref[CONV_TAIL:CONV_TAIL + rows, :] = qkv_ref[...]
    cw = convw_ref[...]
    acc = ext_ref[CONV_TAIL:CONV_TAIL + rows, :] * cw[DN_CONV - 1:DN_CONV, :]
    for j in range(DN_CONV - 1):
        acc = acc + ext_ref[pl.ds(CONV_TAIL - (DN_CONV - 1) + j, rows), :] * cw[j:j + 1, :]
    ext_ref[0:CONV_TAIL, :] = ext_ref[rows:rows + CONV_TAIL, :]
    act = _silu(acc)

    bd = bd_ref[...]
    per_head_sum = lambda t: _dot_split_lhs(t, bd)
    q = act[:, 0:gw]
    k = act[:, gw:2 * gw]
    v = act[:, 2 * gw:3 * gw]
    q = q * lax.rsqrt(per_head_sum(q * q) + 1e-6) * (HEAD_DIM ** -0.5)
    k = k * lax.rsqrt(per_head_sum(k * k) + 1e-6)

    ba = _dot_split_lhs(ba_ref[...], bexp_ref[...])
    beta = _sigmoid(ba[:, 0:gw])
    a_in = ba[:, gw:2 * gw] + dtb_ref[...]
    softplus = jnp.maximum(a_in, 0.0) + jnp.log(1.0 + jnp.exp(-jnp.abs(a_in)))
    g = -jnp.exp(alog_ref[...]) * softplus
    tril = tril_ref[...]
    g_hi = g.astype(BF16)
    g_mid, g_lo = _split(g - g_hi.astype(F32))
    gc_all = _dot(tril, g_hi) + (_dot(tril, g_mid) + _dot(tril, g_lo))

    ri = lax.broadcasted_iota(jnp.int32, (cl, cl), 0)
    ci = lax.broadcasted_iota(jnp.int32, (cl, cl), 1)
    causal = ci <= ri
    strict = ci < ri
    eye = jnp.where(ci == ri, 1.0, 0.0)
    lane = lax.broadcasted_iota(jnp.int32, (1, gw), 1)
    head_lanes = [(lane >= h * HEAD_DIM) & (lane < (h + 1) * HEAD_DIM) for h in range(N_HEADS)]

    n_chunks = rows // cl
    pairs = [(c, h) for c in range(n_chunks) for h in range(N_HEADS)]
    chunk = lambda t, c: t[c * cl:(c + 1) * cl]
    gcs = [chunk(gc_all, c) for c in range(n_chunks)]
    gc_ts = [gc.T for gc in gcs]
    g_lasts = [gc[cl - 1:cl, :] for gc in gcs]
    egs = [jnp.exp(gc) for gc in gcs]
    ks = [chunk(k, c) for c in range(n_chunks)]
    k_bfs = [t.astype(BF16) for t in ks]
    k_betas = [chunk(k, c) * chunk(beta, c) for c in range(n_chunks)]
    v_betas = [(chunk(v, c) * chunk(beta, c)).astype(BF16) for c in range(n_chunks)]
    kbgs = [(k_betas[c] * egs[c]).astype(BF16) for c in range(n_chunks)]
    q_decs = [(chunk(q, c) * egs[c]).astype(BF16) for c in range(n_chunks)]
    k_decs = [(ks[c] * jnp.exp(g_lasts[c] - gcs[c])).astype(BF16) for c in range(n_chunks)]
    head = lambda h: slice(h * HEAD_DIM, (h + 1) * HEAD_DIM)
    decays = [jnp.exp(jnp.where(causal, gcs[c][:, head(h)] - gc_ts[c][head(h), :], -jnp.inf)) for c, h in pairs]
    per_head = lambda t: [jnp.where(head_lanes[h], t, 0.0) for h in range(N_HEADS)]
    kq = [_dot_nt(jnp.concatenate(per_head(k_betas[c]) + per_head(chunk(q, c)), axis=0).astype(BF16), k_bfs[c])
          for c in range(n_chunks)]
    kks = [kq[c][h * cl:(h + 1) * cl] for c, h in pairs]
    qks = [kq[c][(N_HEADS + h) * cl:(N_HEADS + h + 1) * cl] for c, h in pairs]
    intras = [jnp.where(causal, qk * d, 0.0) for qk, d in zip(qks, decays)]
    powers = [jnp.where(strict, -(kk * d), 0.0) for kk, d in zip(kks, decays)]
    t_invs = [eye + p for p in powers]
    for _ in range(5):
        p_bfs = [p.astype(BF16) for p in powers]
        powers = [_dot(p, p) for p in p_bfs]
        t_invs = [t + _dot(t.astype(BF16), p.astype(BF16)) for t, p in zip(t_invs, powers)]
    tv = [_dot(jnp.concatenate(t_invs[c * N_HEADS:(c + 1) * N_HEADS], axis=0).astype(BF16),
               jnp.concatenate([v_betas[c], kbgs[c]], axis=1)) for c in range(n_chunks)]

    def merge_heads(stacked, lanes):
        out = jnp.zeros((cl, gw), F32)
        for h in range(N_HEADS):
            out = jnp.where(head_lanes[h], stacked[h * cl:(h + 1) * cl, lanes], out)
        return out

    us = [merge_heads(tv[c], slice(0, gw)) for c in range(n_chunks)]
    ws = [merge_heads(tv[c], slice(gw, 2 * gw)).astype(BF16) for c in range(n_chunks)]
    bd_f = bd.astype(F32)
    tn = (((0,), (0,)), ((), ()))
    kws = [(bd_f * lax.dot_general(k_decs[c], ws[c], tn, preferred_element_type=F32)).astype(BF16)
           for c in range(n_chunks)]
    kus = [bd_f * lax.dot_general(k_decs[c], us[c].astype(BF16), tn, preferred_element_type=F32)
           for c in range(n_chunks)]
    intra_cat = [jnp.concatenate([intras[c * N_HEADS + h] for h in range(N_HEADS)], axis=1).astype(BF16)
                 for c in range(n_chunks)]

    state = state_ref[...]
    outs = []
    for c in range(n_chunks):
        on_state = _dot(jnp.concatenate([ws[c], q_decs[c], kws[c]], axis=0), state.astype(BF16))
        v_new = us[c] - on_state[0:cl]
        v_stack = jnp.concatenate(per_head(v_new), axis=0).astype(BF16)
        outs.append(on_state[cl:2 * cl] + _dot(intra_cat[c], v_stack))
        state = state * jnp.exp(g_lasts[c]) - on_state[2 * cl:] + kus[c]
    state_ref[...] = state

    o = jnp.concatenate(outs, axis=0)
    o = o * lax.rsqrt(per_head_sum(o * o) * (1.0 / HEAD_DIM) + RMS_EPS) * nw_ref[...]
    o_ref[...] = (o * _silu(z_ref[...])).astype(o_ref.dtype)


def gdn_constants():
    lane_head = np.arange(GROUP_WIDTH) // HEAD_DIM
    bd = (lane_head[:, None] == lane_head[None, :]).astype(np.float32)
    bexp = np.zeros((LANES, 2 * GROUP_WIDTH), np.float32)
    for h in range(N_HEADS):
        bexp[h, h * HEAD_DIM:(h + 1) * HEAD_DIM] = 1.0
        bexp[N_HEADS + h, GROUP_WIDTH + h * HEAD_DIM:GROUP_WIDTH + (h + 1) * HEAD_DIM] = 1.0
    r = np.arange(GDN_ROWS)
    tril = ((r[:, None] // DN_CHUNK == r[None, :] // DN_CHUNK) & (r[None, :] <= r[:, None])).astype(np.float32)
    return bexp, bd, tril


def gated_deltanet(qkv, z, ba, conv_w, a_log_rep, dt_bias_rep, norm_w_rep, bexp, bd, tril):
    b, s, w3 = qkv.shape
    gw = GROUP_WIDTH
    blk = lambda width: pl.BlockSpec((None, GDN_ROWS, width), lambda bi, c: (bi, c, 0))
    const = lambda shape: pl.BlockSpec(shape, lambda bi, c: (0, 0))
    as_bf16 = lambda t: jnp.asarray(t, BF16)
    return pl.pallas_call(
        _gdn_kernel,
        grid=(b, s // GDN_ROWS),
        in_specs=[blk(w3), blk(gw), blk(LANES), const(conv_w.shape), const((1, gw)), const((1, gw)),
                  const((1, gw)), const(bexp.shape), const(bd.shape), const(tril.shape)],
        out_specs=blk(gw),
        out_shape=jax.ShapeDtypeStruct((b, s, gw), BF16),
        scratch_shapes=[pltpu.VMEM((CONV_TAIL + GDN_ROWS, w3), F32), pltpu.VMEM((gw, gw), F32)],
        compiler_params=_params("parallel", "arbitrary"),
        name="gated_deltanet",
    )(qkv, z, ba, conv_w, a_log_rep, dt_bias_rep, norm_w_rep, as_bf16(bexp), as_bf16(bd), as_bf16(tril))


def _first_max(values):
    best = values[0]
    for v in values[1:]:
        best = jnp.maximum(best, v)
    taken = jnp.zeros_like(best)
    hot = []
    for v in values:
        h = jnp.where((v == best) & (taken < 0.5), 1.0, 0.0)
        taken = taken + h
        hot.append(h)
    return best, hot


def _softmax_rows(rows):
    m = rows[0]
    for r in rows[1:]:
        m = jnp.maximum(m, r)
    e = [jnp.exp(r - m) for r in rows]
    z = e[0]
    for t in e[1:]:
        z = z + t
    return [t / z for t in e]


def _out_proj_kernel(ya_ref, yb_ref, yc_ref, yd_ref, wo_ref, x_ref, g1_ref, gain_ref, sc_ref, sh_ref,
                     wr_ref, br_ref, xo_ref, h_ref, combt_ref):
    mixed = jnp.concatenate([ya_ref[...], yb_ref[...], yc_ref[...], yd_ref[...]], axis=1)
    x = x_ref[...] + g1_ref[...] * _dot(mixed, wo_ref[...])
    xo_ref[...] = x
    h = _modulated_norm(x, gain_ref[...], sc_ref[...], sh_ref[...])
    h_hi, h_lo = _split(h)
    h_ref[...] = h_hi

    w_hi, w_lo = _split(wr_ref[...])
    on_h_hi = _dot_nt(jnp.concatenate([w_hi, w_lo], axis=0), h_hi)
    logits = on_h_hi[0:LANES] + (on_h_hi[LANES:2 * LANES] + _dot_nt(w_hi, h_lo)) + br_ref[...]
    ng, ne = N_EXPERT_GROUPS, EXPERTS_PER_GROUP
    p_group = _softmax_rows([logits[r:r + 1, :] for r in range(ng)])
    pg_top, g_hot = _first_max(p_group)
    e_logits = []
    for e in range(ne):
        t = g_hot[0] * logits[ng + e:ng + e + 1, :]
        for gi in range(1, ng):
            t = t + g_hot[gi] * logits[ng + gi * ne + e:ng + gi * ne + e + 1, :]
        e_logits.append(t)
    p_exp = _softmax_rows(e_logits)
    p1, hot1 = _first_max(p_exp)
    rest = [jnp.where(h1 > 0.5, -1.0, p) for p, h1 in zip(p_exp, hot1)]
    p2, hot2 = _first_max(rest)
    total = p1 + p2
    w_exp = [(h1 * (p1 / total) + h2 * (p2 / total)) * pg_top for h1, h2 in zip(hot1, hot2)]
    for gi in range(ng):
        for e in range(ne):
            combt_ref[gi * ne + e:gi * ne + e + 1, :] = g_hot[gi] * w_exp[e]


def out_projection(ys, w_out, x, g1, gain, scale, shift, w_router, b_router, seq, row_tile=1024):
    n, d = x.shape
    w_router_t = w_router.T
    tiles_per_seq = seq // row_tile
    row = lambda i: (i, 0)
    per_batch = lambda i: (i // tiles_per_seq, 0, 0)
    const = lambda shape: pl.BlockSpec(shape, lambda i: (0, 0))
    mod = pl.BlockSpec((None, 1, d), per_batch)
    return pl.pallas_call(
        _out_proj_kernel,
        grid=(n // row_tile,),
        in_specs=[pl.BlockSpec((row_tile, GROUP_WIDTH), row)] * 4
        + [const(w_out.shape), pl.BlockSpec((row_tile, d), row), mod, const((1, d)), mod, mod,
           const(w_router_t.shape), const(b_router.shape)],
        out_specs=(pl.BlockSpec((row_tile, d), row), pl.BlockSpec((row_tile, d), row),
                   pl.BlockSpec((N_EXPERTS, row_tile), lambda i: (0, i))),
        out_shape=(jax.ShapeDtypeStruct((n, d), F32), jax.ShapeDtypeStruct((n, d), BF16),
                   jax.ShapeDtypeStruct((N_EXPERTS, n), F32)),
        compiler_params=_params("parallel"),
        name="out_projection",
    )(*ys, w_out, x, g1, gain, scale, shift, w_router_t, b_router)


MOE_PARTS = 8
MOE_ROWS = 32


def _moe_kernel(h_ref, combt_ref, before_ref, wg_ref, wu_ref, wd_ref, x_ref, g2_ref, fin_ref, o_ref,
                acc_ref, rank_ref, pick_ref, y_ref, *, final_norm):
    e = pl.program_id(1)
    part = h_ref.shape[0] // MOE_PARTS
    tn = (((0,), (0,)), ((), ()))
    part_tokens = [slice(p * part, (p + 1) * part) for p in range(MOE_PARTS)]
    part_slots = [slice(p * MOE_ROWS, (p + 1) * MOE_ROWS) for p in range(MOE_PARTS)]

    @pl.when(e == 0)
    def _():
        acc_ref[...] = jnp.zeros(acc_ref.shape, F32)
        routed = jnp.where(combt_ref[...] != 0.0, 1.0, 0.0).astype(BF16)
        rank_ref[...] = _dot(routed, before_ref[...])

    weight_row = combt_ref[pl.ds(e, 1), :]
    rank_row = rank_ref[pl.ds(e, 1), :]
    routed_row = weight_row != 0.0
    counts = [jnp.sum(jnp.where(routed_row[:, tok], 1, 0)) for tok in part_tokens]
    n_passes = (functools.reduce(jnp.maximum, counts) + MOE_ROWS - 1) // MOE_ROWS

    def gathered_pass(j):
        slot = (j * MOE_ROWS + lax.broadcasted_iota(jnp.int32, (MOE_ROWS, 1), 0)).astype(F32)
        picks = [jnp.where((rank_row[:, tok] == slot) & routed_row[:, tok], 1.0, 0.0)
                 for tok in part_tokens]
        picks_bf = [p.astype(BF16) for p in picks]
        rows = jnp.concatenate([_dot(p, h_ref[tok, :]) for p, tok in zip(picks_bf, part_tokens)],
                               axis=0).astype(BF16)
        weight = jnp.concatenate([jnp.sum(p * weight_row[:, tok], axis=-1, keepdims=True)
                                  for p, tok in zip(picks, part_tokens)], axis=0)
        hid = _silu(_dot(rows, wg_ref[...])) * _dot(rows, wu_ref[...]) * weight
        return picks_bf, _dot(hid.astype(BF16), wd_ref[...]).astype(BF16)

    picks0, y0 = gathered_pass(0)
    first = pl.ds(pl.multiple_of(e * MOE_ROWS, MOE_ROWS), MOE_ROWS)
    for p in range(MOE_PARTS):
        pick_ref[p, first, :] = picks0[p]
        y_ref[p, first, :] = y0[part_slots[p]]

    def extra_pass(j, carry):
        picks_bf, y = gathered_pass(j)
        for p in range(MOE_PARTS):
            acc_ref[part_tokens[p], :] += lax.dot_general(picks_bf[p], y[part_slots[p]], tn,
                                                          preferred_element_type=F32)
        return carry

    lax.fori_loop(1, n_passes, extra_pass, 0)

    @pl.when(e == N_EXPERTS - 1)
    def _():
        moe = jnp.concatenate([lax.dot_general(pick_ref[p], y_ref[p], tn, preferred_element_type=F32)
                               for p in range(MOE_PARTS)], axis=0) + acc_ref[...]
        x = x_ref[...] + g2_ref[...] * moe
        if final_norm:
            x = x * lax.rsqrt(jnp.mean(x * x, axis=-1, keepdims=True) + RMS_EPS) * fin_ref[...]
        o_ref[...] = x


def moe_experts(h, combt, w_gate, w_up, w_down, x, g2, fin_gain, seq, final_norm, row_tile=1024):
    n, d = x.shape
    f = w_gate.shape[-1]
    tiles_per_seq = seq // row_tile
    row = lambda i, e: (i, 0)
    tok = np.arange(row_tile)
    part = row_tile // MOE_PARTS
    before = ((tok[:, None] < tok[None, :]) & (tok[:, None] // part == tok[None, :] // part)).astype(np.float32)
    return pl.pallas_call(
        functools.partial(_moe_kernel, final_norm=final_norm),
        grid=(n // row_tile, N_EXPERTS),
        in_specs=[pl.BlockSpec((row_tile, d), row),
                  pl.BlockSpec((N_EXPERTS, row_tile), lambda i, e: (0, i)),
                  pl.BlockSpec((row_tile, row_tile), lambda i, e: (0, 0)),
                  pl.BlockSpec((None, d, f), lambda i, e: (e, 0, 0)),
                  pl.BlockSpec((None, d, f), lambda i, e: (e, 0, 0)),
                  pl.BlockSpec((None, f, d), lambda i, e: (e, 0, 0)),
                  pl.BlockSpec((row_tile, d), row),
                  pl.BlockSpec((None, 1, d), lambda i, e: (i // tiles_per_seq, 0, 0)),
                  pl.BlockSpec((1, d), lambda i, e: (0, 0))],
        out_specs=pl.BlockSpec((row_tile, d), row),
        out_shape=jax.ShapeDtypeStruct((n, d), F32),
        scratch_shapes=[pltpu.VMEM((row_tile, d), F32), pltpu.VMEM((N_EXPERTS, row_tile), F32),
                        pltpu.VMEM((MOE_PARTS, N_EXPERTS * MOE_ROWS, part), BF16),
                        pltpu.VMEM((MOE_PARTS, N_EXPERTS * MOE_ROWS, d), BF16)],
        compiler_params=_params("parallel", "arbitrary"),
        name="moe_experts",
    )(h, combt, jnp.asarray(before, BF16), w_gate, w_up, w_down, x, g2, fin_gain)


def kernel(x, c, norm_mix, norm_ffn, final_norm, w_ada, b_ada, w_in, w_out, attn_sink, w_pool, pool_scale, cmp_pos, cmp_w1, cmp_w2, conv_w, a_log, dt_bias, dn_norm, w_route_group, b_route_group, w_route_expert, b_route_expert, w_gate, w_up, w_down):
    batch, seq, d = x.shape
    depth = w_in.shape[0]
    n = batch * seq
    cover, gexp, expand = nsa_constants(seq)
    bexp, bd, tril = gdn_constants()
    mod = ada_modulation(c, w_ada, b_ada)
    spread = lambda t: jnp.repeat(t, HEAD_DIM).reshape(1, GROUP_WIDTH)
    seq3 = lambda t: t.reshape(batch, seq, t.shape[-1])
    xf = x.reshape(n, d)
    for l in range(depth):
        sh1, sc1, g1, sh2, sc2, g2 = (mod[l, :, i * d:(i + 1) * d].reshape(batch, 1, d) for i in range(6))
        p = in_projection(xf, norm_mix[l].reshape(1, d), sc1, sh1, pack_in_weights(w_in[l]), seq)
        y_a = swa_attention(seq3(p["a_q"]), seq3(p["a_k"]), seq3(p["a_v"]), attn_sink[l])
        w_pool_bd = jax.scipy.linalg.block_diag(*[w_pool[l, gi] for gi in range(len(POOL_WINDOWS))])
        y_b = multiscale_pool(seq3(p["b_u"]), w_pool_bd.astype(BF16), pool_scale[l].reshape(1, GROUP_WIDTH))
        kc, vc = nsa_compress(seq3(p["c_cmp"]), *pack_compress_weights(cmp_pos[l], cmp_w1[l], cmp_w2[l]))
        y_c = nsa_attention(seq3(p["c_q"]), seq3(p["c_gate"]), kc, vc, seq3(p["c_ksel"]), seq3(p["c_vsel"]),
                            seq3(p["c_kwin"]), seq3(p["c_vwin"]), cover, gexp, expand)
        y_d = gated_deltanet(seq3(p["d_qkv"]), seq3(p["d_z"]), seq3(p["d_ba"]), conv_w[l], spread(a_log[l]),
                             spread(dt_bias[l]), jnp.tile(dn_norm[l], N_HEADS).reshape(1, GROUP_WIDTH), bexp, bd, tril)
        ys = [t.reshape(n, GROUP_WIDTH) for t in (y_a, y_b, y_c, y_d)]
        n_logits = N_EXPERT_GROUPS + N_EXPERTS
        w_router = jnp.pad(jnp.concatenate([w_route_group[l], w_route_expert[l]], axis=1),
                           ((0, 0), (0, LANES - n_logits)))
        b_router = jnp.pad(jnp.concatenate([b_route_group[l], b_route_expert[l]]),
                           (0, LANES - n_logits)).reshape(LANES, 1)
        xf, h2, comb = out_projection(ys, w_out[l].astype(BF16), xf, g1, norm_ffn[l].reshape(1, d), sc2, sh2,
                                      w_router, b_router, seq)
        f = w_gate.shape[-1]
        xf = moe_experts(h2, comb, w_gate[l].reshape(N_EXPERTS, d, f).astype(BF16),
                         w_up[l].reshape(N_EXPERTS, d, f).astype(BF16),
                         w_down[l].reshape(N_EXPERTS, f, d).astype(BF16),
                         xf, g2, final_norm.reshape(1, d), seq, final_norm=(l == depth - 1))
    return xf.reshape(batch, seq, d)
```
